```python
import jax, jax.numpy as jnp
from jax import lax
import numpy as np

D_MODEL = 1024
BATCH = 4
SEQ = 8192
DEPTH = 2

CHUNK = 64
ATTN_HEADS = 16
ATTN_KV_HEADS = 4
ATTN_HEAD_DIM = 64
ATTN_GROUP = ATTN_HEADS // ATTN_KV_HEADS
WINDOW = 128
WIN_CHUNKS = WINDOW // CHUNK
BAND = (WIN_CHUNKS + 1) * CHUNK
ATTN_WIDTH = ATTN_HEADS * ATTN_HEAD_DIM
KV_WIDTH = ATTN_KV_HEADS * ATTN_HEAD_DIM
REC_EXPAND = 128
REC_HEADS = D_MODEL // REC_EXPAND
REC_KEY_DIM = REC_EXPAND
REC_VAL_DIM = D_MODEL // REC_HEADS
FORGET_DIM = REC_HEADS * REC_KEY_DIM
INPUT_DIM = REC_HEADS * REC_VAL_DIM
SPLIT_SIZES = (ATTN_WIDTH, KV_WIDTH, KV_WIDTH, FORGET_DIM, FORGET_DIM, INPUT_DIM, INPUT_DIM, D_MODEL, D_MODEL)
IN_WIDTH = sum(SPLIT_SIZES)
N_EXPERTS = 256
TOP_K = 8
N_GROUPS = 8
TOPK_GROUPS = 4
EXPERT_FF = 256
SHARED_FF = 256
ROUTED_SCALE = 2.5
EXPERT_BLOCK = 128
DEEPNORM_ALPHA = (2 * DEPTH) ** 0.25
DEEPNORM_BETA = (8 * DEPTH) ** -0.25
LN_EPS = 1e-5
RMS_EPS = 1e-5

kernel_name = "hybrid_swa_hgrn2_moe_deepnorm"


def layer_norm(x, g, b):
    xf = x.astype(jnp.float32)
    mu = jnp.mean(xf, axis=-1, keepdims=True)
    var = jnp.mean(jnp.square(xf - mu), axis=-1, keepdims=True)
    return ((xf - mu) * lax.rsqrt(var + LN_EPS) * g + b).astype(x.dtype)


def swiglu(x, w_gu, w_down):
    g, u = jnp.split(x @ w_gu, 2, axis=-1)
    return (jax.nn.silu(g) * u) @ w_down


def sliding_window_sink_attention(q, k, v, sinks):
    B, T = q.shape[0], q.shape[1]
    NC = T // CHUNK
    qc = q.reshape(B, NC, CHUNK, ATTN_KV_HEADS, ATTN_GROUP, ATTN_HEAD_DIM)

    def band(a):
        ac = a.reshape(B, NC, CHUNK, ATTN_KV_HEADS, ATTN_HEAD_DIM)
        ap = jnp.pad(ac, ((0, 0), (WIN_CHUNKS, 0), (0, 0), (0, 0), (0, 0)))
        return jnp.concatenate([ap[:, m:m + NC] for m in range(WIN_CHUNKS + 1)], axis=2)

    kb, vb = band(k), band(v)
    s = jnp.einsum('bnqhgd,bnshd->bhgnqs', qc, kb).astype(jnp.float32) * (ATTN_HEAD_DIM ** -0.5)
    key_chunk = jnp.arange(NC)[:, None] - WIN_CHUNKS + jnp.arange(BAND)[None, :] // CHUNK
    s = jnp.where((key_chunk >= 0)[:, None, :], s, -jnp.inf)
    sink = sinks.astype(jnp.float32).reshape(ATTN_KV_HEADS, ATTN_GROUP)[None, :, :, None, None, None]
    m = jnp.maximum(jnp.max(s, axis=-1, keepdims=True), sink)
    e = jnp.exp(s - m)
    p = e / (jnp.sum(e, axis=-1, keepdims=True) + jnp.exp(sink - m))
    o = jnp.einsum('bhgnqs,bnshd->bnqhgd', p.astype(v.dtype), vb)
    return o.reshape(B, T, ATTN_WIDTH)


def hgrn2_branch(q, f_logit, i, g, lower_bound, norm_g):
    B, T = q.shape[0], q.shape[1]
    NC = T // CHUNK
    f32 = jnp.float32
    lb = lower_bound.astype(f32).reshape(REC_HEADS, REC_KEY_DIM)
    fl = f_logit.astype(f32).reshape(B, T, REC_HEADS, REC_KEY_DIM)
    log_f = jnp.logaddexp(jnp.log(lb), jnp.log1p(-lb) + jax.nn.log_sigmoid(fl))
    k = -jnp.expm1(log_f)
    qh = jax.nn.silu(q.astype(f32)).reshape(B, T, REC_HEADS, REC_KEY_DIM)
    vh = i.astype(f32).reshape(B, T, REC_HEADS, REC_VAL_DIM)

    def chunks(a):
        return a.reshape(B, NC, CHUNK, REC_HEADS, a.shape[-1]).transpose(1, 0, 3, 2, 4)

    causal = jnp.tril(jnp.ones((CHUNK, CHUNK), dtype=bool))[:, :, None]

    def step(S, blk):
        qc, kc, vc, gc = blk
        b = jnp.cumsum(gc, axis=2)
        o_inter = jnp.einsum('bhtk,bhkv->bhtv', qc * jnp.exp(b), S)
        decay = jnp.exp(jnp.where(causal, b[:, :, :, None, :] - b[:, :, None, :, :], -jnp.inf))
        A = jnp.einsum('bhtk,bhtsk,bhsk->bhts', qc, decay, kc)
        o = o_inter + jnp.einsum('bhts,bhsv->bhtv', A, vc)
        b_end = b[:, :, -1:, :]
        S = jnp.exp(b_end[:, :, 0, :])[..., None] * S + jnp.einsum('bhsk,bhsv->bhkv', kc * jnp.exp(b_end - b), vc)
        return S, o

    S0 = jnp.zeros((B, REC_HEADS, REC_KEY_DIM, REC_VAL_DIM), f32)
    _, o = lax.scan(step, S0, (chunks(qh), chunks(k), chunks(vh), chunks(log_f)))
    o = o.transpose(1, 0, 3, 2, 4).reshape(B, T, REC_HEADS, REC_VAL_DIM)
    o = o * lax.rsqrt(jnp.mean(jnp.square(o), axis=-1, keepdims=True) + RMS_EPS) * norm_g
    o = o * jax.nn.silu(g.astype(f32).reshape(B, T, REC_HEADS, REC_VAL_DIM))
    return o.reshape(B, T, INPUT_DIM).astype(q.dtype)


def hybrid_mixer(h, w_in, b_in, sinks, rec_norm_g, lower_bound, w_proj_attn, w_proj_rec, w_out):
    B, T, _ = h.shape
    proj = jnp.einsum('btd,de->bte', h, w_in) + b_in
    offsets = np.cumsum(SPLIT_SIZES)[:-1].tolist()
    q_a, k_a, v_a, q_r, f_r, i_r, g_r, gate_a, gate_r = jnp.split(proj, offsets, axis=-1)
    attn = sliding_window_sink_attention(
        q_a.reshape(B, T, ATTN_HEADS, ATTN_HEAD_DIM),
        k_a.reshape(B, T, ATTN_KV_HEADS, ATTN_HEAD_DIM),
        v_a.reshape(B, T, ATTN_KV_HEADS, ATTN_HEAD_DIM), sinks)
    rec = hgrn2_branch(q_r, f_r, i_r, g_r, lower_bound, rec_norm_g)
    merged = jax.nn.sigmoid(gate_a) * (attn @ w_proj_attn) + jax.nn.sigmoid(gate_r) * (rec @ w_proj_rec)
    return merged @ w_out


def moe_ffn(h, router_w, router_bias, w_gu, w_down, sh_gu, sh_down):
    B, T, D = h.shape
    N = B * T
    f32 = jnp.float32
    hf = h.reshape(N, D)
    scores = jax.nn.sigmoid((hf @ router_w).astype(f32))
    sel = scores + router_bias.astype(f32)
    grp_score = lax.top_k(sel.reshape(N, N_GROUPS, N_EXPERTS // N_GROUPS), 2)[0].sum(-1)
    _, gidx = lax.top_k(grp_score, TOPK_GROUPS)
    gmask = jnp.any(gidx[:, :, None] == jnp.arange(N_GROUPS)[None, None, :], axis=1)
    sel = jnp.where(jnp.repeat(gmask, N_EXPERTS // N_GROUPS, axis=1), sel, -jnp.inf)
    _, idx = lax.top_k(sel, TOP_K)
    gate = jnp.take_along_axis(scores, idx, axis=1)
    gate = gate / jnp.sum(gate, axis=-1, keepdims=True) * ROUTED_SCALE
    R = N * TOP_K
    e_flat = idx.reshape(R)
    order = jnp.argsort(e_flat)
    e_sorted = e_flat[order]
    tok_sorted = (order // TOP_K).astype(jnp.int32)
    gate_sorted = gate.reshape(R)[order]
    counts = jnp.bincount(e_flat, length=N_EXPERTS)
    padded = (counts + EXPERT_BLOCK - 1) // EXPERT_BLOCK * EXPERT_BLOCK
    pad_end = jnp.cumsum(padded)
    pad_start = pad_end - padded
    start = jnp.cumsum(counts) - counts
    pos = pad_start[e_sorted] + jnp.arange(R) - start[e_sorted]
    n_blocks = -(-R // EXPERT_BLOCK) + N_EXPERTS
    rows = n_blocks * EXPERT_BLOCK
    row_tok = jnp.full((rows,), N, jnp.int32).at[pos].set(tok_sorted)
    row_gate = jnp.zeros((rows,), f32).at[pos].set(gate_sorted)
    blk_expert = jnp.minimum(jnp.searchsorted(pad_end, jnp.arange(n_blocks) * EXPERT_BLOCK, side='right'), N_EXPERTS - 1)
    h_pad = jnp.concatenate([hf, jnp.zeros((1, D), hf.dtype)], axis=0)

    def expert_block(acc, blk):
        toks, gw, e = blk
        yb = swiglu(h_pad[toks], w_gu[e], w_down[e])
        return acc.at[toks].add((yb * gw[:, None]).astype(acc.dtype)), None

    acc, _ = lax.scan(expert_block, jnp.zeros((N + 1, D), h.dtype),
                      (row_tok.reshape(n_blocks, EXPERT_BLOCK), row_gate.reshape(n_blocks, EXPERT_BLOCK), blk_expert))
    out = acc[:N] + swiglu(hf, sh_gu, sh_down)
    return out.reshape(B, T, D)


def setup_inputs(seed: int = 0) -> dict:
    key = jax.random.key(seed)
    ks = jax.random.split(key, 24)
    f32 = jnp.float32

    def nrm(k, shape, scale):
        return jax.random.normal(k, shape, f32) * scale

    L, D, E, F = DEPTH, D_MODEL, N_EXPERTS, EXPERT_FF
    return {
        "x": nrm(ks[0], (BATCH, SEQ, D), 1.0),
        "ln_in_g": 1.0 + nrm(ks[1], (D,), 0.05),
        "ln_in_b": nrm(ks[2], (D,), 0.02),
        "lb_logits": nrm(ks[3], (L, FORGET_DIM), 0.5),
        "w_in": nrm(ks[4], (L, D, IN_WIDTH), D ** -0.5),
        "b_in": nrm(ks[5], (L, IN_WIDTH), 0.02),
        "attn_sinks": nrm(ks[6], (L, ATTN_HEADS), 0.5),
        "rec_norm_g": 1.0 + nrm(ks[7], (L, REC_VAL_DIM), 0.05),
        "w_proj_attn": nrm(ks[8], (L, ATTN_WIDTH, D), ATTN_WIDTH ** -0.5 * DEEPNORM_BETA),
        "w_proj_rec": nrm(ks[9], (L, INPUT_DIM, D), INPUT_DIM ** -0.5 * DEEPNORM_BETA),
        "w_out": nrm(ks[10], (L, D, D), D ** -0.5 * DEEPNORM_BETA),
        "ln1_g": 1.0 + nrm(ks[11], (L, D), 0.05),
        "ln1_b": nrm(ks[12], (L, D), 0.02),
        "router_w": nrm(ks[13], (L, D, E), D ** -0.5),
        "router_bias": nrm(ks[14], (L, E), 0.01),
        "expert_w_gu": nrm(ks[15], (L, E, D, 2 * F), D ** -0.5),
        "expert_w_down": nrm(ks[16], (L, E, F, D), F ** -0.5 * DEEPNORM_BETA),
        "shared_w_gu": nrm(ks[17], (L, D, 2 * SHARED_FF), D ** -0.5),
        "shared_w_down": nrm(ks[18], (L, SHARED_FF, D), SHARED_FF ** -0.5 * DEEPNORM_BETA),
        "ln2_g": 1.0 + nrm(ks[19], (L, D), 0.05),
        "ln2_b": nrm(ks[20], (L, D), 0.02),
    }


def reference(x, ln_in_g, ln_in_b, lb_logits, w_in, b_in, attn_sinks, rec_norm_g, w_proj_attn, w_proj_rec,
              w_out, ln1_g, ln1_b, router_w, router_bias, expert_w_gu, expert_w_down, shared_w_gu,
              shared_w_down, ln2_g, ln2_b):
    p = jax.nn.softmax(lb_logits.astype(jnp.float32), axis=0)
    cum = jnp.cumsum(p, axis=0)
    lower_bounds = cum - cum[0:1]
    h = layer_norm(x, ln_in_g, ln_in_b)
    for l in range(DEPTH):
        y = hybrid_mixer(h, w_in[l], b_in[l], attn_sinks[l], rec_norm_g[l], lower_bounds[l],
                         w_proj_attn[l], w_proj_rec[l], w_out[l])
        h = layer_norm(DEEPNORM_ALPHA * h + y, ln1_g[l], ln1_b[l])
        y = moe_ffn(h, router_w[l], router_bias[l], expert_w_gu[l], expert_w_down[l],
                    shared_w_gu[l], shared_w_down[l])
        h = layer_norm(DEEPNORM_ALPHA * h + y, ln2_g[l], ln2_b[l])
    return h
```

```python
import functools

import jax
import jax.numpy as jnp
from jax import lax
from jax.experimental import pallas as pl
from jax.experimental.pallas import tpu as pltpu

F32 = jnp.float32
BF16 = jnp.bfloat16

D_MODEL = 1024
CHUNK = 64
ATTN_HEADS = 16
ATTN_KV_HEADS = 4
ATTN_HEAD_DIM = 64
ATTN_GROUP = ATTN_HEADS // ATTN_KV_HEADS
WIN_CHUNKS = 2
KV_WIDTH = ATTN_KV_HEADS * ATTN_HEAD_DIM
REC_HEADS = 8
REC_DIM = 128
N_EXPERTS = 256
TOP_K = 8
N_GROUPS = 8
GROUP_SIZE = N_EXPERTS // N_GROUPS
TOPK_GROUPS = 4
EXPERT_FF = 256
ROUTED_SCALE = 2.5
DEPTH = 2
DEEPNORM_ALPHA = (2 * DEPTH) ** 0.25
LN_EPS = 1e-5
RMS_EPS = 1e-5

VMEM_LIMIT_BYTES = 48 * 1024 * 1024

ROW_BLOCK = 128
ATTN_Q_BLOCK = 256
REC_TILE = 256
REC_SUB = 16
NEG_INF = float("-inf")


def _params(*sem):
    return pltpu.CompilerParams(dimension_semantics=sem, vmem_limit_bytes=VMEM_LIMIT_BYTES)


def _layer_norm_rows(x, g, b):
    mu = jnp.mean(x, axis=-1, keepdims=True)
    xc = x - mu
    var = jnp.mean(xc * xc, axis=-1, keepdims=True)
    return xc * lax.rsqrt(var + LN_EPS) * g + b


def _sigmoid(x):
    return 1.0 / (1.0 + jnp.exp(-x))


def _silu(x):
    return x * _sigmoid(x)


def _ln_in_kernel(x_ref, g_ref, b_ref, h_ref):
    h_ref[...] = _layer_norm_rows(x_ref[...], g_ref[...], b_ref[...])


def layer_norm_in(x, g, b, tm=512):
    n, d = x.shape
    return pl.pallas_call(
        _ln_in_kernel,
        out_shape=jax.ShapeDtypeStruct((n, d), F32),
        grid=(n // tm,),
        in_specs=[pl.BlockSpec((tm, d), lambda i: (i, 0)),
                  pl.BlockSpec((1, d), lambda i: (0, 0)),
                  pl.BlockSpec((1, d), lambda i: (0, 0))],
        out_specs=pl.BlockSpec((tm, d), lambda i: (i, 0)),
        compiler_params=_params("parallel"),
        name="ln_in",
    )(x, g.reshape(1, d), b.reshape(1, d))


def _matmul_bias_kernel(x_ref, w_ref, b_ref, o_ref):
    acc = jnp.dot(x_ref[...].astype(BF16), w_ref[...], preferred_element_type=F32)
    o_ref[...] = (acc + b_ref[...]).astype(o_ref.dtype)


def matmul_bias(x, w, b, out_dtype, tm=1024, tn=512, name="matmul_bias"):
    n, k = x.shape
    m = w.shape[1]
    return pl.pallas_call(
        _matmul_bias_kernel,
        out_shape=jax.ShapeDtypeStruct((n, m), out_dtype),
        grid=(m // tn, n // tm),
        in_specs=[pl.BlockSpec((tm, k), lambda j, i: (i, 0)),
                  pl.BlockSpec((k, tn), lambda j, i: (0, j)),
                  pl.BlockSpec((1, tn), lambda j, i: (0, j))],
        out_specs=pl.BlockSpec((tm, tn), lambda j, i: (i, j)),
        compiler_params=_params("parallel", "parallel"),
        name=name,
    )(x, w, b.reshape(1, m))


def _attn_kernel(sink_ref, q_ref, kp_ref, kc_ref, vp_ref, vc_ref, o_ref):
    i = pl.program_id(1)
    half = ATTN_Q_BLOCK // 2
    nk = ATTN_Q_BLOCK + half
    k = jnp.concatenate([kp_ref[half:, :], kc_ref[...]], axis=0)
    v = jnp.concatenate([vp_ref[half:, :], vc_ref[...]], axis=0)
    qc = lax.broadcasted_iota(jnp.int32, (ATTN_Q_BLOCK, nk), 0) // CHUNK
    kc = lax.broadcasted_iota(jnp.int32, (ATTN_Q_BLOCK, nk), 1) // CHUNK
    first = jnp.where(i == 0, WIN_CHUNKS, 0)
    valid = (kc >= qc) & (kc <= qc + WIN_CHUNKS) & (kc >= first)
    scale = ATTN_HEAD_DIM ** -0.5
    for h in range(ATTN_HEADS):
        kv = h // ATTN_GROUP
        qh = q_ref[:, h * ATTN_HEAD_DIM:(h + 1) * ATTN_HEAD_DIM]
        kh = k[:, kv * ATTN_HEAD_DIM:(kv + 1) * ATTN_HEAD_DIM]
        vh = v[:, kv * ATTN_HEAD_DIM:(kv + 1) * ATTN_HEAD_DIM]
        s = lax.dot_general(qh, kh, (((1,), (1,)), ((), ())), preferred_element_type=F32) * scale
        s = jnp.where(valid, s, NEG_INF)
        sink = sink_ref[h]
        m = jnp.maximum(jnp.max(s, axis=-1, keepdims=True), sink)
        e = jnp.exp(s - m)
        denom = jnp.sum(e, axis=-1, keepdims=True) + jnp.exp(sink - m)
        p = (e / denom).astype(BF16)
        oh = jnp.dot(p, vh, preferred_element_type=F32)
        o_ref[:, h * ATTN_HEAD_DIM:(h + 1) * ATTN_HEAD_DIM] = oh.astype(o_ref.dtype)


def swa_attention(proj, sinks, batch, seq, q_col, k_col, v_col):
    n = batch * seq
    nb = seq // ATTN_Q_BLOCK
    qb, kb, vb = q_col // D_MODEL, k_col // KV_WIDTH, v_col // KV_WIDTH

    def cur(col):
        return lambda b, i, s: (b * nb + i, col)

    def prev(col):
        return lambda b, i, s: (b * nb + jnp.maximum(i - 1, 0), col)

    return pl.pallas_call(
        _attn_kernel,
        out_shape=jax.ShapeDtypeStruct((n, D_MODEL), BF16),
        grid_spec=pltpu.PrefetchScalarGridSpec(
            num_scalar_prefetch=1,
            grid=(batch, nb),
            in_specs=[pl.BlockSpec((ATTN_Q_BLOCK, D_MODEL), cur(qb)),
                      pl.BlockSpec((ATTN_Q_BLOCK, KV_WIDTH), prev(kb)),
                      pl.BlockSpec((ATTN_Q_BLOCK, KV_WIDTH), cur(kb)),
                      pl.BlockSpec((ATTN_Q_BLOCK, KV_WIDTH), prev(vb)),
                      pl.BlockSpec((ATTN_Q_BLOCK, KV_WIDTH), cur(vb))],
            out_specs=pl.BlockSpec((ATTN_Q_BLOCK, D_MODEL), lambda b, i, s: (b * nb + i, 0))),
        compiler_params=_params("parallel", "parallel"),
        name="swa_attention",
    )(sinks.astype(F32), proj, proj, proj, proj, proj)


def _hgrn_kernel(q_ref, i_ref, g_ref, f_ref, loglb_ref, log1mlb_ref, ng_ref, o_ref,
                 st_ref, b_ref, k_ref, qs_ref):
    @pl.when(pl.program_id(1) == 0)
    def _():
        st_ref[...] = jnp.zeros_like(st_ref)

    fl = f_ref[...]
    log_sig = jnp.minimum(fl, 0.0) - jnp.log(1.0 + jnp.exp(-jnp.abs(fl)))
    a = loglb_ref[...]
    c = log1mlb_ref[...] + log_sig
    log_f = jnp.maximum(a, c) + jnp.log(1.0 + jnp.exp(-jnp.abs(a - c)))
    k_ref[...] = 1.0 - jnp.exp(log_f)
    rows = lax.broadcasted_iota(jnp.int32, log_f.shape, 0) % REC_SUB
    b = log_f
    shift = 1
    while shift < REC_SUB:
        b = b + jnp.where(rows >= shift, pltpu.roll(b, shift, axis=0), 0.0)
        shift *= 2
    b_ref[...] = b
    qs_ref[...] = _silu(q_ref[...].astype(F32))

    ones = jnp.ones((REC_DIM, REC_DIM), BF16)
    t_iota = lax.broadcasted_iota(jnp.int32, (REC_SUB, REC_DIM), 0)

    def step(j, carry):
        r0 = pl.multiple_of(j * REC_SUB, REC_SUB)
        for h in range(REC_HEADS):
            cols = slice(h * REC_DIM, (h + 1) * REC_DIM)
            bj = b_ref[pl.ds(r0, REC_SUB), cols]
            kj = k_ref[pl.ds(r0, REC_SUB), cols]
            qj = qs_ref[pl.ds(r0, REC_SUB), cols]
            vj = i_ref[pl.ds(r0, REC_SUB), cols].astype(F32)
            st = st_ref[h]
            qd = (qj * jnp.exp(bj)).astype(BF16)
            o = lax.dot_general(qd, st.astype(BF16), (((1,), (1,)), ((), ())), preferred_element_type=F32)
            parts = []
            for s in range(REC_SUB):
                bs = bj[s:s + 1, :]
                ks = kj[s:s + 1, :]
                dec = jnp.exp(jnp.where(t_iota >= s, bj - bs, NEG_INF))
                parts.append((qj * dec * ks).astype(BF16))
            pstack = jnp.concatenate(parts, axis=0)
            rsum = jnp.dot(pstack, ones, preferred_element_type=F32)
            for s in range(REC_SUB):
                o = o + rsum[s * REC_SUB:(s + 1) * REC_SUB, :] * vj[s:s + 1, :]
            b_end = bj[REC_SUB - 1:REC_SUB, :]
            kd = (kj * jnp.exp(b_end - bj)).astype(BF16)
            kv_t = lax.dot_general(vj.astype(BF16), kd, (((0,), (0,)), ((), ())), preferred_element_type=F32)
            st_ref[h] = st * jnp.exp(b_end) + kv_t
            ms = jnp.mean(o * o, axis=-1, keepdims=True)
            o = o * lax.rsqrt(ms + RMS_EPS) * ng_ref[...]
            gj = g_ref[pl.ds(r0, REC_SUB), cols].astype(F32)
            o_ref[pl.ds(r0, REC_SUB), cols] = (o * _silu(gj)).astype(o_ref.dtype)
        return carry

    lax.fori_loop(0, REC_TILE // REC_SUB, step, 0)


def hgrn2(proj, proj_f, log_lb, log1m_lb, norm_g, batch, seq, q_col, i_col, g_col):
    n = batch * seq
    nb = seq // REC_TILE
    d = D_MODEL

    def blk(col):
        return pl.BlockSpec((REC_TILE, d), lambda b, i: (b * nb + i, col // d))

    return pl.pallas_call(
        _hgrn_kernel,
        out_shape=jax.ShapeDtypeStruct((n, d), BF16),
        grid=(batch, nb),
        in_specs=[blk(q_col), blk(i_col), blk(g_col),
                  pl.BlockSpec((REC_TILE, d), lambda b, i: (b * nb + i, 0)),
                  pl.BlockSpec((1, d), lambda b, i: (0, 0)),
                  pl.BlockSpec((1, d), lambda b, i: (0, 0)),
                  pl.BlockSpec((1, REC_DIM), lambda b, i: (0, 0))],
        out_specs=pl.BlockSpec((REC_TILE, d), lambda b, i: (b * nb + i, 0)),
        scratch_shapes=[pltpu.VMEM((REC_HEADS, REC_DIM, REC_DIM), F32),
                        pltpu.VMEM((REC_TILE, d), F32),
                        pltpu.VMEM((REC_TILE, d), F32),
                        pltpu.VMEM((REC_TILE, d), F32)],
        compiler_params=_params("parallel", "arbitrary"),
        name="hgrn2",
    )(proj, proj, proj, proj_f, log_lb.reshape(1, d), log1m_lb.reshape(1, d), norm_g.reshape(1, REC_DIM))


def _merge_kernel(h_ref, attn_ref, rec_ref, ga_ref, gr_ref, wpa_ref, wpr_ref, wo_ref, g_ref, b_ref, o_ref):
    a = jnp.dot(attn_ref[...], wpa_ref[...], preferred_element_type=F32)
    r = jnp.dot(rec_ref[...], wpr_ref[...], preferred_element_type=F32)
    merged = _sigmoid(ga_ref[...].astype(F32)) * a + _sigmoid(gr_ref[...].astype(F32)) * r
    y = jnp.dot(merged.astype(BF16), wo_ref[...], preferred_element_type=F32)
    o_ref[...] = _layer_norm_rows(DEEPNORM_ALPHA * h_ref[...] + y, g_ref[...], b_ref[...])


def merge_outproj_ln(h, attn, rec, proj, ga_col, gr_col, wpa, wpr, wo, g, b, tm=256):
    n, d = h.shape
    row = lambda i: (i, 0)
    const = lambda i: (0, 0)
    return pl.pallas_call(
        _merge_kernel,
        out_shape=jax.ShapeDtypeStruct((n, d), F32),
        grid=(n // tm,),
        in_specs=[pl.BlockSpec((tm, d), row), pl.BlockSpec((tm, d), row), pl.BlockSpec((tm, d), row),
                  pl.BlockSpec((tm, d), lambda i: (i, ga_col // d)),
                  pl.BlockSpec((tm, d), lambda i: (i, gr_col // d)),
                  pl.BlockSpec((d, d), const), pl.BlockSpec((d, d), const), pl.BlockSpec((d, d), const),
                  pl.BlockSpec((1, d), const), pl.BlockSpec((1, d), const)],
        out_specs=pl.BlockSpec((tm, d), row),
        compiler_params=_params("parallel"),
        name="merge_outproj_ln",
    )(h, attn, rec, proj, proj, wpa, wpr, wo, g.reshape(1, d), b.reshape(1, d))


def _router_kernel(h_ref, whi_ref, wlo_ref, bias_ref, idx_ref, gate_ref, rank_ref, cnt_ref, carry_ref):
    @pl.when(pl.program_id(0) == 0)
    def _():
        carry_ref[...] = jnp.zeros_like(carry_ref)

    tm = h_ref.shape[0]
    h = h_ref[...]
    h_hi = h.astype(BF16)
    h_lo = (h - h_hi.astype(F32)).astype(BF16)
    nt = (((1,), (1,)), ((), ()))
    logits = (lax.dot_general(whi_ref[...], h_hi, nt, preferred_element_type=F32)
              + lax.dot_general(whi_ref[...], h_lo, nt, preferred_element_type=F32)
              + lax.dot_general(wlo_ref[...], h_hi, nt, preferred_element_type=F32))
    scores = _sigmoid(logits)
    sel = scores + bias_ref[...]
    e_iota = lax.broadcasted_iota(jnp.int32, (N_EXPERTS, tm), 0)

    g_iota = lax.broadcasted_iota(jnp.int32, (N_GROUPS, tm), 0)
    l_iota = lax.broadcasted_iota(jnp.int32, (GROUP_SIZE, tm), 0)
    grp = jnp.zeros((N_GROUPS, tm), F32)
    for g in range(N_GROUPS):
        sg = sel[g * GROUP_SIZE:(g + 1) * GROUP_SIZE, :]
        m1 = jnp.max(sg, axis=0, keepdims=True)
        i1 = jnp.min(jnp.where(sg == m1, l_iota, GROUP_SIZE), axis=0, keepdims=True)
        m2 = jnp.max(jnp.where(l_iota == i1, NEG_INF, sg), axis=0, keepdims=True)
        grp = jnp.where(g_iota == g, m1 + m2, grp)
    gsel = jnp.zeros((N_GROUPS, tm), jnp.int32)
    for _ in range(TOPK_GROUPS):
        m = jnp.max(grp, axis=0, keepdims=True)
        gi = jnp.min(jnp.where(grp == m, g_iota, N_GROUPS), axis=0, keepdims=True)
        hit = g_iota == gi
        gsel = jnp.where(hit, 1, gsel)
        grp = jnp.where(hit, NEG_INF, grp)
    masked = []
    for g in range(N_GROUPS):
        sg = sel[g * GROUP_SIZE:(g + 1) * GROUP_SIZE, :]
        masked.append(jnp.where(gsel[g:g + 1, :] > 0, sg, NEG_INF))
    selm = jnp.concatenate(masked, axis=0)

    k_iota = lax.broadcasted_iota(jnp.int32, (TOP_K, tm), 0)
    idx = jnp.zeros((TOP_K, tm), jnp.int32)
    gate = jnp.zeros((TOP_K, tm), F32)
    member = jnp.zeros((N_EXPERTS, tm), F32)
    for k in range(TOP_K):
        m = jnp.max(selm, axis=0, keepdims=True)
        ei = jnp.min(jnp.where(selm == m, e_iota, N_EXPERTS), axis=0, keepdims=True)
        hit = e_iota == ei
        gk = jnp.sum(jnp.where(hit, scores, 0.0), axis=0, keepdims=True)
        idx = jnp.where(k_iota == k, ei, idx)
        gate = jnp.where(k_iota == k, gk, gate)
        member = jnp.where(hit, 1.0, member)
        selm = jnp.where(hit, NEG_INF, selm)
    gate = gate / jnp.sum(gate, axis=0, keepdims=True) * ROUTED_SCALE

    upper = (lax.broadcasted_iota(jnp.int32, (tm, tm), 0) < lax.broadcasted_iota(jnp.int32, (tm, tm), 1))
    before = jnp.dot(member.astype(BF16), upper.astype(BF16), preferred_element_type=F32) + carry_ref[...]
    rank = jnp.zeros((TOP_K, tm), F32)
    for k in range(TOP_K):
        rk = jnp.sum(jnp.where(e_iota == idx[k:k + 1, :], before, 0.0), axis=0, keepdims=True)
        rank = jnp.where(k_iota == k, rk, rank)
    carry_ref[...] = carry_ref[...] + jnp.sum(member, axis=1, keepdims=True)

    idx_ref[...] = idx
    gate_ref[...] = gate
    rank_ref[...] = rank.astype(jnp.int32)
    cnt_ref[...] = jnp.broadcast_to(carry_ref[...], cnt_ref.shape).astype(jnp.int32)


def router(h, w_t_hi, w_t_lo, bias, tm=256):
    n, d = h.shape
    tok = lambda i: (0, i)
    const = lambda i: (0, 0)
    return pl.pallas_call(
        _router_kernel,
        out_shape=(jax.ShapeDtypeStruct((TOP_K, n), jnp.int32),
                   jax.ShapeDtypeStruct((TOP_K, n), F32),
                   jax.ShapeDtypeStruct((TOP_K, n), jnp.int32),
                   jax.ShapeDtypeStruct((N_EXPERTS, 128), jnp.int32)),
        grid=(n // tm,),
        in_specs=[pl.BlockSpec((tm, d), lambda i: (i, 0)),
                  pl.BlockSpec((N_EXPERTS, d), const),
                  pl.BlockSpec((N_EXPERTS, d), const),
                  pl.BlockSpec((N_EXPERTS, 1), const)],
        out_specs=(pl.BlockSpec((TOP_K, tm), tok), pl.BlockSpec((TOP_K, tm), tok),
                   pl.BlockSpec((TOP_K, tm), tok), pl.BlockSpec((N_EXPERTS, 128), const)),
        scratch_shapes=[pltpu.VMEM((N_EXPERTS, 1), F32)],
        compiler_params=_params("arbitrary"),
        name="router",
    )(h, w_t_hi, w_t_lo, bias.reshape(N_EXPERTS, 1))


def _dispatch_kernel(pos_ref, h_ref, xs_in_ref, xs_ref, sem):
    del xs_in_ref
    tm = h_ref.shape[0]

    def row_copy(t, k):
        return pltpu.make_async_copy(h_ref.at[pl.ds(t, 1), :], xs_ref.at[pl.ds(pos_ref[k, t], 1), :], sem)

    def body(t, carry):
        for k in range(TOP_K):
            row_copy(t, k).start()
        return carry

    lax.fori_loop(0, tm, body, 0)
    for _ in range(TOP_K):
        pltpu.make_async_copy(h_ref, xs_ref.at[pl.ds(0, tm), :], sem).wait()


def dispatch(h, pos_t, rows, tm=256):
    n, d = h.shape
    zeros = jnp.zeros((rows, d), h.dtype)
    return pl.pallas_call(
        _dispatch_kernel,
        out_shape=jax.ShapeDtypeStruct((rows, d), h.dtype),
        grid=(n // tm,),
        in_specs=[pl.BlockSpec((TOP_K, tm), lambda i: (0, i), memory_space=pltpu.SMEM),
                  pl.BlockSpec((tm, d), lambda i: (i, 0)),
                  pl.BlockSpec(memory_space=pl.ANY)],
        out_specs=pl.BlockSpec(memory_space=pl.ANY),
        scratch_shapes=[pltpu.SemaphoreType.DMA(())],
        input_output_aliases={2: 0},
        compiler_params=_params("arbitrary"),
        name="dispatch",
    )(pos_t, h, zeros)


def _expert_kernel(blk_expert_ref, n_used_ref, x_ref, wgu_ref, wd_ref, y_ref):
    del blk_expert_ref

    @pl.when(pl.program_id(0) < n_used_ref[0])
    def _():
        gu = jnp.dot(x_ref[...].astype(BF16), wgu_ref[0], preferred_element_type=F32)
        act = _silu(gu[:, :EXPERT_FF]) * gu[:, EXPERT_FF:]
        y_ref[...] = jnp.dot(act.astype(BF16), wd_ref[0], preferred_element_type=F32).astype(y_ref.dtype)

    @pl.when(pl.program_id(0) >= n_used_ref[0])
    def _():
        y_ref[...] = jnp.zeros_like(y_ref)


def expert_ffn(xs, blk_expert, n_used, w_gu, w_down):
    rows, d = xs.shape
    n_blocks = rows // ROW_BLOCK

    def row_map(j, be, nu):
        return (jnp.minimum(j, nu[0] - 1), 0)

    def w_map(j, be, nu):
        return (be[jnp.minimum(j, nu[0] - 1)], 0, 0)

    return pl.pallas_call(
        _expert_kernel,
        out_shape=jax.ShapeDtypeStruct((rows, d), F32),
        grid_spec=pltpu.PrefetchScalarGridSpec(
            num_scalar_prefetch=2,
            grid=(n_blocks,),
            in_specs=[pl.BlockSpec((ROW_BLOCK, d), row_map),
                      pl.BlockSpec((1, d, 2 * EXPERT_FF), w_map),
                      pl.BlockSpec((1, EXPERT_FF, d), w_map)],
            out_specs=pl.BlockSpec((ROW_BLOCK, d), lambda j, be, nu: (j, 0))),
        compiler_params=_params("arbitrary"),
        name="expert_ffn",
    )(blk_expert, n_used, xs, w_gu, w_down)


def _combine_kernel(pos_ref, h_ref, gate_ref, ys_ref, sgu_ref, sd_ref, g_ref, b_ref, o_ref, buf, sem):
    tm = h_ref.shape[0]

    def body(t, carry):
        for k in range(TOP_K):
            pltpu.make_async_copy(ys_ref.at[pl.ds(pos_ref[k, t], 1), :], buf.at[k, pl.ds(t, 1), :], sem).start()
        return carry

    lax.fori_loop(0, tm, body, 0)
    h = h_ref[...]
    gu = jnp.dot(h.astype(BF16), sgu_ref[...], preferred_element_type=F32)
    act = _silu(gu[:, :EXPERT_FF]) * gu[:, EXPERT_FF:]
    y = jnp.dot(act.astype(BF16), sd_ref[...], preferred_element_type=F32)
    for k in range(TOP_K):
        pltpu.make_async_copy(ys_ref.at[pl.ds(0, tm), :], buf.at[k], sem).wait()
    gate = gate_ref[...]
    for k in range(TOP_K):
        y = y + gate[:, k:k + 1] * buf[k]
    o_ref[...] = _layer_norm_rows(DEEPNORM_ALPHA * h + y, g_ref[...], b_ref[...])


def combine_shared_ln(h, pos_t, gate, ys, sh_gu, sh_down, g, b, tm=256):
    n, d = h.shape
    row = lambda i: (i, 0)
    const = lambda i: (0, 0)
    return pl.pallas_call(
        _combine_kernel,
        out_shape=jax.ShapeDtypeStruct((n, d), F32),
        grid=(n // tm,),
        in_specs=[pl.BlockSpec((TOP_K, tm), lambda i: (0, i), memory_space=pltpu.SMEM),
                  pl.BlockSpec((tm, d), row),
                  pl.BlockSpec((tm, TOP_K), row),
                  pl.BlockSpec(memory_space=pl.ANY),
                  pl.BlockSpec((d, 2 * EXPERT_FF), const),
                  pl.BlockSpec((EXPERT_FF, d), const),
                  pl.BlockSpec((1, d), const), pl.BlockSpec((1, d), const)],
        out_specs=pl.BlockSpec((tm, d), row),
        scratch_shapes=[pltpu.VMEM((TOP_K, tm, d), F32), pltpu.SemaphoreType.DMA(())],
        compiler_params=_params("arbitrary"),
        name="combine_shared_ln",
    )(pos_t, h, gate, ys, sh_gu, sh_down, g.reshape(1, d), b.reshape(1, d))


_Q_A, _Q_R, _I_R, _G_R, _GATE_A, _GATE_R = (i * D_MODEL for i in range(6))
_K_A = 6 * D_MODEL
_V_A = _K_A + KV_WIDTH


def _split_in_proj(w_in, b_in):
    d = D_MODEL
    o_k, o_v, o_qr = d, d + KV_WIDTH, d + 2 * KV_WIDTH
    o_f, o_i, o_g, o_ga, o_gr = o_qr + d, o_qr + 2 * d, o_qr + 3 * d, o_qr + 4 * d, o_qr + 5 * d

    def cols(a, start, width):
        return lax.slice_in_dim(a, start, start + width, axis=-1)

    order = [(0, d), (o_qr, d), (o_i, d), (o_g, d), (o_ga, d), (o_gr, d), (o_k, KV_WIDTH), (o_v, KV_WIDTH)]
    w_main = jnp.concatenate([cols(w_in, s, w) for s, w in order], axis=-1).astype(BF16)
    b_main = jnp.concatenate([cols(b_in, s, w) for s, w in order], axis=-1)
    return w_main, b_main, cols(w_in, o_f, d).astype(BF16), cols(b_in, o_f, d)


def kernel(x, ln_in_g, ln_in_b, lb_logits, w_in, b_in, attn_sinks, rec_norm_g, w_proj_attn, w_proj_rec, w_out,
           ln1_g, ln1_b, router_w, router_bias, expert_w_gu, expert_w_down, shared_w_gu, shared_w_down,
           ln2_g, ln2_b):
    batch, seq, d = x.shape
    n = batch * seq
    depth = w_in.shape[0]
    rows = (-(-n * TOP_K // ROW_BLOCK) + N_EXPERTS) * ROW_BLOCK
    n_blocks = rows // ROW_BLOCK

    p = jax.nn.softmax(lb_logits.astype(F32), axis=0)
    cum = jnp.cumsum(p, axis=0)
    lower = cum - cum[0:1]
    log_lb = jnp.log(lower)
    log1m_lb = jnp.log1p(-lower)

    h = layer_norm_in(x.reshape(n, d), ln_in_g, ln_in_b)
    for l in range(depth):
        w_main, b_main, w_f, b_f = _split_in_proj(w_in[l], b_in[l])
        proj = matmul_bias(h, w_main, b_main, BF16, name="in_proj_main")
        proj_f = matmul_bias(h, w_f, b_f, F32, name="in_proj_forget")
        attn = swa_attention(proj, attn_sinks[l], batch, seq, _Q_A, _K_A, _V_A)
        rec = hgrn2(proj, proj_f, log_lb[l], log1m_lb[l], rec_norm_g[l], batch, seq, _Q_R, _I_R, _G_R)
        h = merge_outproj_ln(h, attn, rec, proj, _GATE_A, _GATE_R,
                             w_proj_attn[l].astype(BF16), w_proj_rec[l].astype(BF16), w_out[l].astype(BF16),
                             ln1_g[l], ln1_b[l])

        rw_t = router_w[l].T
        rw_hi = rw_t.astype(BF16)
        rw_lo = (rw_t - rw_hi.astype(F32)).astype(BF16)
        idx_t, gate_t, rank_t, cnt = router(h, rw_hi, rw_lo, router_bias[l])
        counts = cnt[:, 0]
        padded = (counts + ROW_BLOCK - 1) // ROW_BLOCK * ROW_BLOCK
        pad_end = jnp.cumsum(padded)
        pad_start = pad_end - padded
        pos_t = pad_start[idx_t] + rank_t
        blk_expert = jnp.minimum(
            jnp.searchsorted(pad_end, jnp.arange(n_blocks, dtype=jnp.int32) * ROW_BLOCK, side="right"),
            N_EXPERTS - 1).astype(jnp.int32)
        n_used = (pad_end[-1:] // ROW_BLOCK).astype(jnp.int32)

        xs = dispatch(h, pos_t, rows)
        ys = expert_ffn(xs, blk_expert, n_used, expert_w_gu[l].astype(BF16), expert_w_down[l].astype(BF16))
        h = combine_shared_ln(h, pos_t, gate_t.T, ys, shared_w_gu[l].astype(BF16), shared_w_down[l].astype(BF16),
                              ln2_g[l], ln2_b[l])
    return h.reshape(batch, seq, d)
```

```python
import jax
import jax.numpy as jnp
from jax import lax
from jax.experimental import pallas as pl
from jax.experimental.pallas import tpu as pltpu

F32 = jnp.float32
BF16 = jnp.bfloat16
U32 = jnp.uint32
I32 = jnp.int32

D_MODEL = 1024
HALF = D_MODEL // 2
CHUNK = 64
ATTN_HEADS = 16
ATTN_KV_HEADS = 4
ATTN_HEAD_DIM = 64
ATTN_GROUP = ATTN_HEADS // ATTN_KV_HEADS
WIN_CHUNKS = 2
KV_WIDTH = ATTN_KV_HEADS * ATTN_HEAD_DIM
REC_HEADS = 8
REC_DIM = 128
N_EXPERTS = 256
TOP_K = 8
N_GROUPS = 8
GROUP_SIZE = N_EXPERTS // N_GROUPS
TOPK_GROUPS = 4
EXPERT_FF = 256
ROUTED_SCALE = 2.5
DEPTH = 2
DEEPNORM_ALPHA = (2 * DEPTH) ** 0.25
LN_EPS = 1e-5
RMS_EPS = 1e-5
NEG_INF = float("-inf")

VMEM_LIMIT_BYTES = 48 * 1024 * 1024

LN_ROWS = 512
PROJ_ROWS = 1024
PROJ_COLS = 512
ATTN_Q_BLOCK = 256
REC_TILE = 256
REC_SUB = 16
MERGE_ROWS = 256
ROUTER_ROWS = 256
MOE_ROWS = 256
ROW_BLOCK = 256

DECAY_LIMIT = 60.0


def _params(*sem):
    return pltpu.CompilerParams(dimension_semantics=sem, vmem_limit_bytes=VMEM_LIMIT_BYTES)


def _layer_norm_rows(x, g, b):
    mu = jnp.mean(x, axis=-1, keepdims=True)
    xc = x - mu
    var = jnp.mean(xc * xc, axis=-1, keepdims=True)
    return xc * lax.rsqrt(var + LN_EPS) * g + b


def _sigmoid(x):
    return 1.0 / (1.0 + jnp.exp(-x))


def _silu(x):
    return x * _sigmoid(x)


def _pack_halves(lo, hi):
    lo_bits = pltpu.bitcast(lo.astype(BF16).astype(F32), U32)
    hi_bits = pltpu.bitcast(hi.astype(BF16).astype(F32), U32)
    return lax.shift_right_logical(lo_bits, U32(16)) | (hi_bits & U32(0xFFFF0000))


def _unpack_halves(w):
    lo = pltpu.bitcast(lax.shift_left(w, U32(16)), F32)
    hi = pltpu.bitcast(w & U32(0xFFFF0000), F32)
    return lo, hi


def _ln_in_kernel(x_ref, g_ref, b_ref, h_ref, hb_ref):
    h = _layer_norm_rows(x_ref[...], g_ref[...], b_ref[...])
    h_ref[...] = h
    hb_ref[...] = h.astype(BF16)


def layer_norm_in(x, g, b):
    n, d = x.shape
    row = lambda i: (i, 0)
    const = lambda i: (0, 0)
    return pl.pallas_call(
        _ln_in_kernel,
        out_shape=(jax.ShapeDtypeStruct((n, d), F32), jax.ShapeDtypeStruct((n, d), BF16)),
        grid=(n // LN_ROWS,),
        in_specs=[pl.BlockSpec((LN_ROWS, d), row), pl.BlockSpec((1, d), const), pl.BlockSpec((1, d), const)],
        out_specs=(pl.BlockSpec((LN_ROWS, d), row), pl.BlockSpec((LN_ROWS, d), row)),
        compiler_params=_params("parallel"),
        name="ln_in",
    )(x, g.reshape(1, d), b.reshape(1, d))


def _in_proj_kernel(perm_ref, x_ref, w_ref, b_ref, o_ref):
    del perm_ref
    acc = jnp.dot(x_ref[...], w_ref[0].astype(BF16), preferred_element_type=F32)
    o_ref[...] = (acc + b_ref[0]).astype(o_ref.dtype)


def in_proj(xb, w, b, layer, col_blocks, out_dtype, name):
    n, k = xb.shape
    perm = jnp.asarray(col_blocks, I32)
    nblk = len(col_blocks)
    return pl.pallas_call(
        _in_proj_kernel,
        out_shape=jax.ShapeDtypeStruct((n, nblk * PROJ_COLS), out_dtype),
        grid_spec=pltpu.PrefetchScalarGridSpec(
            num_scalar_prefetch=1,
            grid=(n // PROJ_ROWS, nblk),
            in_specs=[pl.BlockSpec((PROJ_ROWS, k), lambda i, j, p: (i, 0)),
                      pl.BlockSpec((1, k, PROJ_COLS), lambda i, j, p: (layer, 0, p[j])),
                      pl.BlockSpec((1, 1, PROJ_COLS), lambda i, j, p: (layer, 0, p[j]))],
            out_specs=pl.BlockSpec((PROJ_ROWS, PROJ_COLS), lambda i, j, p: (i, j))),
        compiler_params=_params("parallel", "arbitrary"),
        name=name,
    )(perm, xb, w, b.reshape(b.shape[0], 1, -1))


def _attn_kernel(sink_ref, q_ref, kp_ref, kc_ref, vp_ref, vc_ref, o_ref):
    i = pl.program_id(1)
    half = ATTN_Q_BLOCK // 2
    nk = ATTN_Q_BLOCK + half
    k = jnp.concatenate([kp_ref[half:, :], kc_ref[...]], axis=0)
    v = jnp.concatenate([vp_ref[half:, :], vc_ref[...]], axis=0)
    qc = lax.broadcasted_iota(I32, (ATTN_Q_BLOCK, nk), 0) // CHUNK
    kc = lax.broadcasted_iota(I32, (ATTN_Q_BLOCK, nk), 1) // CHUNK
    first = jnp.where(i == 0, WIN_CHUNKS, 0)
    valid = (kc >= qc) & (kc <= qc + WIN_CHUNKS) & (kc >= first)
    mask_bias = jnp.where(valid, 0.0, NEG_INF)
    scale = ATTN_HEAD_DIM ** -0.5
    for h in range(ATTN_HEADS):
        kv = h // ATTN_GROUP
        qh = q_ref[:, h * ATTN_HEAD_DIM:(h + 1) * ATTN_HEAD_DIM] * scale
        kh = k[:, kv * ATTN_HEAD_DIM:(kv + 1) * ATTN_HEAD_DIM]
        vh = v[:, kv * ATTN_HEAD_DIM:(kv + 1) * ATTN_HEAD_DIM]
        s = lax.dot_general(qh, kh, (((1,), (1,)), ((), ())), preferred_element_type=F32) + mask_bias
        sink = sink_ref[h]
        m = jnp.maximum(jnp.max(s, axis=-1, keepdims=True), sink)
        e = jnp.exp(s - m)
        denom = jnp.sum(e, axis=-1, keepdims=True) + jnp.exp(sink - m)
        oh = jnp.dot(e.astype(BF16), vh, preferred_element_type=F32) * (1.0 / denom)
        o_ref[:, h * ATTN_HEAD_DIM:(h + 1) * ATTN_HEAD_DIM] = oh.astype(o_ref.dtype)


def swa_attention(proj, sinks, batch, seq, q_col, k_col, v_col):
    n = batch * seq
    nb = seq // ATTN_Q_BLOCK
    qb, kb, vb = q_col // D_MODEL, k_col // KV_WIDTH, v_col // KV_WIDTH

    def cur(col):
        return lambda b, i, s: (b * nb + i, col)

    def prev(col):
        return lambda b, i, s: (b * nb + jnp.maximum(i - 1, 0), col)

    return pl.pallas_call(
        _attn_kernel,
        out_shape=jax.ShapeDtypeStruct((n, D_MODEL), BF16),
        grid_spec=pltpu.PrefetchScalarGridSpec(
            num_scalar_prefetch=1,
            grid=(batch, nb),
            in_specs=[pl.BlockSpec((ATTN_Q_BLOCK, D_MODEL), cur(qb)),
                      pl.BlockSpec((ATTN_Q_BLOCK, KV_WIDTH), prev(kb)),
                      pl.BlockSpec((ATTN_Q_BLOCK, KV_WIDTH), cur(kb)),
                      pl.BlockSpec((ATTN_Q_BLOCK, KV_WIDTH), prev(vb)),
                      pl.BlockSpec((ATTN_Q_BLOCK, KV_WIDTH), cur(vb))],
            out_specs=pl.BlockSpec((ATTN_Q_BLOCK, D_MODEL), lambda b, i, s: (b * nb + i, 0))),
        compiler_params=_params("parallel", "parallel"),
        name="swa_attention",
    )(sinks.astype(F32), proj, proj, proj, proj, proj)


def _hgrn_kernel(q_ref, i_ref, g_ref, f_ref, loglb_ref, log1mlb_ref, ng_ref, o_ref,
                 st_ref, b_ref, k_ref, qs_ref, ep_ref, qd_ref, kt_ref):
    @pl.when(pl.program_id(1) == 0)
    def _():
        st_ref[...] = jnp.zeros_like(st_ref)

    fl = f_ref[...]
    log_sig = jnp.minimum(fl, 0.0) - jnp.log(1.0 + jnp.exp(-jnp.abs(fl)))
    a = loglb_ref[...]
    c = log1mlb_ref[...] + log_sig
    log_f = jnp.maximum(a, c) + jnp.log(1.0 + jnp.exp(-jnp.abs(a - c)))
    k_ref[...] = 1.0 - jnp.exp(log_f)
    rows = lax.broadcasted_iota(I32, log_f.shape, 0) % REC_SUB
    b = log_f
    shift = 1
    while shift < REC_SUB:
        b = b + jnp.where(rows >= shift, pltpu.roll(b, shift, axis=0), 0.0)
        shift *= 2
    b_ref[...] = b
    qs_ref[...] = _silu(q_ref[...].astype(F32))

    t_iota = lax.broadcasted_iota(I32, (REC_SUB, REC_DIM), 0)
    nt = (((1,), (1,)), ((), ()))
    tn = (((0,), (0,)), ((), ()))

    def finish(o, r0, cols):
        ms = jnp.mean(o * o, axis=-1, keepdims=True)
        o = o * lax.rsqrt(ms + RMS_EPS) * ng_ref[...]
        gj = g_ref[pl.ds(r0, REC_SUB), cols].astype(F32)
        o_ref[pl.ds(r0, REC_SUB), cols] = (o * _silu(gj)).astype(o_ref.dtype)

    decay_bounded = jnp.min(b) >= -DECAY_LIMIT

    @pl.when(decay_bounded)
    def _():
        ep = jnp.exp(b_ref[...])
        ep_ref[...] = ep
        qd_ref[...] = (qs_ref[...] * ep).astype(BF16)
        kt_ref[...] = (k_ref[...] * jnp.exp(-b_ref[...])).astype(BF16)
        causal = (lax.broadcasted_iota(I32, (REC_SUB, REC_SUB), 0)
                  >= lax.broadcasted_iota(I32, (REC_SUB, REC_SUB), 1))

        def step(j, carry):
            r0 = pl.multiple_of(j * REC_SUB, REC_SUB)
            for h in range(REC_HEADS):
                cols = slice(h * REC_DIM, (h + 1) * REC_DIM)
                qd = qd_ref[pl.ds(r0, REC_SUB), cols]
                kt = kt_ref[pl.ds(r0, REC_SUB), cols]
                vj = i_ref[pl.ds(r0, REC_SUB), cols]
                st = st_ref[h]
                o = lax.dot_general(qd, st.astype(BF16), nt, preferred_element_type=F32)
                att = lax.dot_general(qd, kt, nt, preferred_element_type=F32)
                att = jnp.where(causal, att, 0.0).astype(BF16)
                o = o + jnp.dot(att, vj, preferred_element_type=F32)
                e_end = ep_ref[pl.ds(r0, REC_SUB), cols][REC_SUB - 1:REC_SUB, :]
                kv_t = lax.dot_general(vj, kt, tn, preferred_element_type=F32)
                st_ref[h] = (st + kv_t) * e_end
                finish(o, r0, cols)
            return carry

        lax.fori_loop(0, REC_TILE // REC_SUB, step, 0)

    @pl.when(jnp.logical_not(decay_bounded))
    def _():
        ones = jnp.ones((REC_DIM, REC_DIM), BF16)

        def step(j, carry):
            r0 = pl.multiple_of(j * REC_SUB, REC_SUB)
            for h in range(REC_HEADS):
                cols = slice(h * REC_DIM, (h + 1) * REC_DIM)
                bj = b_ref[pl.ds(r0, REC_SUB), cols]
                kj = k_ref[pl.ds(r0, REC_SUB), cols]
                qj = qs_ref[pl.ds(r0, REC_SUB), cols]
                vj = i_ref[pl.ds(r0, REC_SUB), cols].astype(F32)
                st = st_ref[h]
                qd = (qj * jnp.exp(bj)).astype(BF16)
                o = lax.dot_general(qd, st.astype(BF16), nt, preferred_element_type=F32)
                parts = []
                for s in range(REC_SUB):
                    dec = jnp.exp(jnp.where(t_iota >= s, bj - bj[s:s + 1, :], NEG_INF))
                    parts.append((qj * dec * kj[s:s + 1, :]).astype(BF16))
                pstack = jnp.concatenate(parts, axis=0)
                rsum = jnp.dot(pstack, ones, preferred_element_type=F32)
                for s in range(REC_SUB):
                    o = o + rsum[s * REC_SUB:(s + 1) * REC_SUB, :] * vj[s:s + 1, :]
                b_end = bj[REC_SUB - 1:REC_SUB, :]
                kd = (kj * jnp.exp(b_end - bj)).astype(BF16)
                kv_t = lax.dot_general(vj.astype(BF16), kd, tn, preferred_element_type=F32)
                st_ref[h] = st * jnp.exp(b_end) + kv_t
                finish(o, r0, cols)
            return carry

        lax.fori_loop(0, REC_TILE // REC_SUB, step, 0)


def hgrn2(proj, proj_f, log_lb, log1m_lb, norm_g, batch, seq, q_col, i_col, g_col):
    n = batch * seq
    nb = seq // REC_TILE
    d = D_MODEL

    def blk(col):
        return pl.BlockSpec((REC_TILE, d), lambda b, i: (b * nb + i, col // d))

    return pl.pallas_call(
        _hgrn_kernel,
        out_shape=jax.ShapeDtypeStruct((n, d), BF16),
        grid=(batch, nb),
        in_specs=[blk(q_col), blk(i_col), blk(g_col),
                  pl.BlockSpec((REC_TILE, d), lambda b, i: (b * nb + i, 0)),
                  pl.BlockSpec((1, d), lambda b, i: (0, 0)),
                  pl.BlockSpec((1, d), lambda b, i: (0, 0)),
                  pl.BlockSpec((1, REC_DIM), lambda b, i: (0, 0))],
        out_specs=pl.BlockSpec((REC_TILE, d), lambda b, i: (b * nb + i, 0)),
        scratch_shapes=[pltpu.VMEM((REC_HEADS, REC_DIM, REC_DIM), F32),
                        pltpu.VMEM((REC_TILE, d), F32),
                        pltpu.VMEM((REC_TILE, d), F32),
                        pltpu.VMEM((REC_TILE, d), F32),
                        pltpu.VMEM((REC_TILE, d), F32),
                        pltpu.VMEM((REC_TILE, d), BF16),
                        pltpu.VMEM((REC_TILE, d), BF16)],
        compiler_params=_params("parallel", "arbitrary"),
        name="hgrn2",
    )(proj, proj, proj, proj_f, log_lb.reshape(1, d), log1m_lb.reshape(1, d), norm_g.reshape(1, REC_DIM))


def _merge_kernel(h_ref, attn_ref, rec_ref, ga_ref, gr_ref, wpa_ref, wpr_ref, wo_ref, g_ref, b_ref,
                  o_ref, op_ref):
    a = jnp.dot(attn_ref[...], wpa_ref[...], preferred_element_type=F32)
    r = jnp.dot(rec_ref[...], wpr_ref[...], preferred_element_type=F32)
    merged = _sigmoid(ga_ref[...].astype(F32)) * a + _sigmoid(gr_ref[...].astype(F32)) * r
    y = jnp.dot(merged.astype(BF16), wo_ref[...], preferred_element_type=F32)
    h1 = _layer_norm_rows(DEEPNORM_ALPHA * h_ref[...] + y, g_ref[...], b_ref[...])
    o_ref[...] = h1
    op_ref[...] = _pack_halves(h1[:, :HALF], h1[:, HALF:])


def merge_outproj_ln(h, attn, rec, proj, ga_col, gr_col, wpa, wpr, wo, g, b):
    n, d = h.shape
    tm = MERGE_ROWS
    row = lambda i: (i, 0)
    const = lambda i: (0, 0)
    return pl.pallas_call(
        _merge_kernel,
        out_shape=(jax.ShapeDtypeStruct((n, d), F32), jax.ShapeDtypeStruct((n, HALF), U32)),
        grid=(n // tm,),
        in_specs=[pl.BlockSpec((tm, d), row), pl.BlockSpec((tm, d), row), pl.BlockSpec((tm, d), row),
                  pl.BlockSpec((tm, d), lambda i: (i, ga_col // d)),
                  pl.BlockSpec((tm, d), lambda i: (i, gr_col // d)),
                  pl.BlockSpec((d, d), const), pl.BlockSpec((d, d), const), pl.BlockSpec((d, d), const),
                  pl.BlockSpec((1, d), const), pl.BlockSpec((1, d), const)],
        out_specs=(pl.BlockSpec((tm, d), row), pl.BlockSpec((tm, HALF), row)),
        compiler_params=_params("parallel"),
        name="merge_outproj_ln",
    )(h, attn, rec, proj, proj, wpa, wpr, wo, g.reshape(1, d), b.reshape(1, d))


def _router_kernel(h_ref, whi_ref, wlo_ref, bias_ref, idx_ref, gate_ref, rank_ref, cnt_ref, carry_ref):
    @pl.when(pl.program_id(0) == 0)
    def _():
        carry_ref[...] = jnp.zeros_like(carry_ref)

    tm = h_ref.shape[0]
    h = h_ref[...]
    h_hi = h.astype(BF16)
    h_lo = (h - h_hi.astype(F32)).astype(BF16)
    nt = (((1,), (1,)), ((), ()))
    logits = (lax.dot_general(whi_ref[...], h_hi, nt, preferred_element_type=F32)
              + lax.dot_general(whi_ref[...], h_lo, nt, preferred_element_type=F32)
              + lax.dot_general(wlo_ref[...], h_hi, nt, preferred_element_type=F32))
    scores = _sigmoid(logits)
    sel = scores + bias_ref[...]
    e_iota = lax.broadcasted_iota(I32, (N_EXPERTS, tm), 0)

    g_iota = lax.broadcasted_iota(I32, (N_GROUPS, tm), 0)
    l_iota = lax.broadcasted_iota(I32, (GROUP_SIZE, tm), 0)
    grp = jnp.zeros((N_GROUPS, tm), F32)
    for g in range(N_GROUPS):
        sg = sel[g * GROUP_SIZE:(g + 1) * GROUP_SIZE, :]
        m1 = jnp.max(sg, axis=0, keepdims=True)
        i1 = jnp.min(jnp.where(sg == m1, l_iota, GROUP_SIZE), axis=0, keepdims=True)
        m2 = jnp.max(jnp.where(l_iota == i1, NEG_INF, sg), axis=0, keepdims=True)
        grp = jnp.where(g_iota == g, m1 + m2, grp)
    gsel = jnp.zeros((N_GROUPS, tm), I32)
    for _ in range(TOPK_GROUPS):
        m = jnp.max(grp, axis=0, keepdims=True)
        gi = jnp.min(jnp.where(grp == m, g_iota, N_GROUPS), axis=0, keepdims=True)
        hit = g_iota == gi
        gsel = jnp.where(hit, 1, gsel)
        grp = jnp.where(hit, NEG_INF, grp)
    masked = []
    for g in range(N_GROUPS):
        sg = sel[g * GROUP_SIZE:(g + 1) * GROUP_SIZE, :]
        masked.append(jnp.where(gsel[g:g + 1, :] > 0, sg, NEG_INF))
    selm = jnp.concatenate(masked, axis=0)

    k_iota = lax.broadcasted_iota(I32, (TOP_K, tm), 0)
    idx = jnp.zeros((TOP_K, tm), I32)
    gate = jnp.zeros((TOP_K, tm), F32)
    member = jnp.zeros((N_EXPERTS, tm), F32)
    for k in range(TOP_K):
        m = jnp.max(selm, axis=0, keepdims=True)
        ei = jnp.min(jnp.where(selm == m, e_iota, N_EXPERTS), axis=0, keepdims=True)
        hit = e_iota == ei
        gk = jnp.sum(jnp.where(hit, scores, 0.0), axis=0, keepdims=True)
        idx = jnp.where(k_iota == k, ei, idx)
        gate = jnp.where(k_iota == k, gk, gate)
        member = jnp.where(hit, 1.0, member)
        selm = jnp.where(hit, NEG_INF, selm)
    gate = gate / jnp.sum(gate, axis=0, keepdims=True) * ROUTED_SCALE

    upper = lax.broadcasted_iota(I32, (tm, tm), 0) < lax.broadcasted_iota(I32, (tm, tm), 1)
    before = jnp.dot(member.astype(BF16), upper.astype(BF16), preferred_element_type=F32) + carry_ref[...]
    rank = jnp.zeros((TOP_K, tm), F32)
    for k in range(TOP_K):
        rk = jnp.sum(jnp.where(e_iota == idx[k:k + 1, :], before, 0.0), axis=0, keepdims=True)
        rank = jnp.where(k_iota == k, rk, rank)
    carry_ref[...] = carry_ref[...] + jnp.sum(member, axis=1, keepdims=True)

    idx_ref[...] = idx
    gate_ref[...] = gate
    rank_ref[...] = rank.astype(I32)
    cnt_ref[...] = jnp.broadcast_to(carry_ref[...], cnt_ref.shape).astype(I32)


def router(h, w_t_hi, w_t_lo, bias):
    n, d = h.shape
    tm = ROUTER_ROWS
    tok = lambda i: (0, i)
    const = lambda i: (0, 0)
    return pl.pallas_call(
        _router_kernel,
        out_shape=(jax.ShapeDtypeStruct((TOP_K, n), I32),
                   jax.ShapeDtypeStruct((TOP_K, n), F32),
                   jax.ShapeDtypeStruct((TOP_K, n), I32),
                   jax.ShapeDtypeStruct((N_EXPERTS, 128), I32)),
        grid=(n // tm,),
        in_specs=[pl.BlockSpec((tm, d), lambda i: (i, 0)),
                  pl.BlockSpec((N_EXPERTS, d), const),
                  pl.BlockSpec((N_EXPERTS, d), const),
                  pl.BlockSpec((N_EXPERTS, 1), const)],
        out_specs=(pl.BlockSpec((TOP_K, tm), tok), pl.BlockSpec((TOP_K, tm), tok),
                   pl.BlockSpec((TOP_K, tm), tok), pl.BlockSpec((N_EXPERTS, 128), const)),
        scratch_shapes=[pltpu.VMEM((N_EXPERTS, 1), F32)],
        compiler_params=_params("arbitrary"),
        name="router",
    )(h, w_t_hi, w_t_lo, bias.reshape(N_EXPERTS, 1))


def _slot_pos_kernel(idx_ref, rank_ref, start_ref, pos_ref):
    tm = idx_ref.shape[1]
    e_iota = lax.broadcasted_iota(I32, (N_EXPERTS, tm), 0)
    k_iota = lax.broadcasted_iota(I32, (TOP_K, tm), 0)
    idx = idx_ref[...]
    start = start_ref[...]
    base = jnp.zeros((TOP_K, tm), F32)
    for k in range(TOP_K):
        bk = jnp.sum(jnp.where(e_iota == idx[k:k + 1, :], start, 0.0), axis=0, keepdims=True)
        base = jnp.where(k_iota == k, bk, base)
    pos_ref[...] = base.astype(I32) + rank_ref[...]


def slot_positions(idx_t, rank_t, pad_start):
    n = idx_t.shape[1]
    tm = 1024
    tok = lambda i: (0, i)
    return pl.pallas_call(
        _slot_pos_kernel,
        out_shape=jax.ShapeDtypeStruct((TOP_K, n), I32),
        grid=(n // tm,),
        in_specs=[pl.BlockSpec((TOP_K, tm), tok), pl.BlockSpec((TOP_K, tm), tok),
                  pl.BlockSpec((N_EXPERTS, 1), lambda i: (0, 0))],
        out_specs=pl.BlockSpec((TOP_K, tm), tok),
        compiler_params=_params("parallel"),
        name="slot_positions",
    )(idx_t, rank_t, pad_start.astype(F32).reshape(N_EXPERTS, 1))


def _dispatch_kernel(pad_end_ref, padded_ref, pos_ref, h_ref, xs_ref, zero_ref, sem):
    tm = h_ref.shape[0]

    @pl.when(pl.program_id(0) == 0)
    def _():
        zero_ref[...] = jnp.zeros_like(zero_ref)

        def fill(e):
            start = pl.multiple_of(pad_end_ref[e] - ROW_BLOCK, ROW_BLOCK)
            return pltpu.make_async_copy(zero_ref, xs_ref.at[pl.ds(start, ROW_BLOCK), :], sem)

        def start_fill(e, carry):
            @pl.when(padded_ref[e] > 0)
            def _():
                fill(e).start()
            return carry

        def wait_fill(e, carry):
            @pl.when(padded_ref[e] > 0)
            def _():
                fill(e).wait()
            return carry

        lax.fori_loop(0, N_EXPERTS, start_fill, 0)
        lax.fori_loop(0, N_EXPERTS, wait_fill, 0)

        def tail(j):
            start = pl.multiple_of(j * ROW_BLOCK, ROW_BLOCK)
            return pltpu.make_async_copy(zero_ref, xs_ref.at[pl.ds(start, ROW_BLOCK), :], sem)

        first_unused = pad_end_ref[N_EXPERTS - 1] // ROW_BLOCK
        n_blocks = xs_ref.shape[0] // ROW_BLOCK
        lax.fori_loop(first_unused, n_blocks, lambda j, c: (tail(j).start(), c)[1], 0)
        lax.fori_loop(first_unused, n_blocks, lambda j, c: (tail(j).wait(), c)[1], 0)

    def body(t, carry):
        for k in range(TOP_K):
            pltpu.make_async_copy(h_ref.at[pl.ds(t, 1), :], xs_ref.at[pl.ds(pos_ref[k, t], 1), :], sem).start()
        return carry

    lax.fori_loop(0, tm, body, 0)
    for _ in range(TOP_K):
        pltpu.make_async_copy(h_ref, xs_ref.at[pl.ds(0, tm), :], sem).wait()


def dispatch(hp, pos_t, pad_end, padded, rows):
    n, w = hp.shape
    tm = MOE_ROWS
    return pl.pallas_call(
        _dispatch_kernel,
        out_shape=jax.ShapeDtypeStruct((rows, w), hp.dtype),
        grid_spec=pltpu.PrefetchScalarGridSpec(
            num_scalar_prefetch=2,
            grid=(n // tm,),
            in_specs=[pl.BlockSpec((TOP_K, tm), lambda i, pe, pd: (0, i), memory_space=pltpu.SMEM),
                      pl.BlockSpec((tm, w), lambda i, pe, pd: (i, 0))],
            out_specs=pl.BlockSpec(memory_space=pl.ANY),
            scratch_shapes=[pltpu.VMEM((ROW_BLOCK, w), hp.dtype), pltpu.SemaphoreType.DMA(())]),
        compiler_params=_params("arbitrary"),
        name="dispatch",
    )(pad_end, padded, pos_t, hp)


def _expert_kernel(blk_expert_ref, n_used_ref, x_ref, wgu_ref, wd_ref, y_ref, wgu_bf, wd_bf):
    j = pl.program_id(0)
    used = j < n_used_ref[0]
    prev = blk_expert_ref[jnp.maximum(j - 1, 0)]

    @pl.when(used & ((j == 0) | (blk_expert_ref[j] != prev)))
    def _():
        wgu_bf[...] = wgu_ref[0, 0].astype(BF16)
        wd_bf[...] = wd_ref[0, 0].astype(BF16)

    @pl.when(used)
    def _():
        lo, hi = _unpack_halves(x_ref[...])
        gu = (jnp.dot(lo.astype(BF16), wgu_bf[:HALF, :], preferred_element_type=F32)
              + jnp.dot(hi.astype(BF16), wgu_bf[HALF:, :], preferred_element_type=F32))
        act = (_silu(gu[:, :EXPERT_FF]) * gu[:, EXPERT_FF:]).astype(BF16)
        y = jnp.dot(act, wd_bf[...], preferred_element_type=F32)
        y_ref[...] = _pack_halves(y[:, :HALF], y[:, HALF:])

    @pl.when(jnp.logical_not(used))
    def _():
        y_ref[...] = jnp.zeros_like(y_ref)


def expert_ffn(xs, blk_expert, n_used, w_gu, w_down, layer):
    rows, w = xs.shape
    d = D_MODEL
    n_blocks = rows // ROW_BLOCK

    def row_map(j, be, nu):
        return (jnp.minimum(j, nu[0] - 1), 0)

    def w_map(j, be, nu):
        return (layer, be[jnp.minimum(j, nu[0] - 1)], 0, 0)

    return pl.pallas_call(
        _expert_kernel,
        out_shape=jax.ShapeDtypeStruct((rows, w), U32),
        grid_spec=pltpu.PrefetchScalarGridSpec(
            num_scalar_prefetch=2,
            grid=(n_blocks,),
            in_specs=[pl.BlockSpec((ROW_BLOCK, w), row_map),
                      pl.BlockSpec((1, 1, d, 2 * EXPERT_FF), w_map),
                      pl.BlockSpec((1, 1, EXPERT_FF, d), w_map)],
            out_specs=pl.BlockSpec((ROW_BLOCK, w), lambda j, be, nu: (j, 0)),
            scratch_shapes=[pltpu.VMEM((d, 2 * EXPERT_FF), BF16), pltpu.VMEM((EXPERT_FF, d), BF16)]),
        compiler_params=_params("arbitrary"),
        name="expert_ffn",
    )(blk_expert, n_used, xs, w_gu, w_down)


def _combine_kernel(pos_ref, h_ref, gate_ref, ys_ref, sgu_ref, sd_ref, g_ref, b_ref, o_ref, ob_ref, buf, sem):
    tm = h_ref.shape[0]

    def body(t, carry):
        for k in range(TOP_K):
            pltpu.make_async_copy(ys_ref.at[pl.ds(pos_ref[k, t], 1), :], buf.at[k, pl.ds(t, 1), :], sem).start()
        return carry

    lax.fori_loop(0, tm, body, 0)
    h = h_ref[...]
    gu = jnp.dot(h.astype(BF16), sgu_ref[...], preferred_element_type=F32)
    act = _silu(gu[:, :EXPERT_FF]) * gu[:, EXPERT_FF:]
    y = jnp.dot(act.astype(BF16), sd_ref[...], preferred_element_type=F32)
    for k in range(TOP_K):
        pltpu.make_async_copy(ys_ref.at[pl.ds(0, tm), :], buf.at[k], sem).wait()
    gate = gate_ref[...]
    y_lo = y[:, :HALF]
    y_hi = y[:, HALF:]
    for k in range(TOP_K):
        lo, hi = _unpack_halves(buf[k])
        gk = gate[:, k:k + 1]
        y_lo = y_lo + gk * lo
        y_hi = y_hi + gk * hi
    out = _layer_norm_rows(DEEPNORM_ALPHA * h + jnp.concatenate([y_lo, y_hi], axis=-1), g_ref[...], b_ref[...])
    o_ref[...] = out
    ob_ref[...] = out.astype(BF16)


def combine_shared_ln(h, pos_t, gate, ys, sh_gu, sh_down, g, b):
    n, d = h.shape
    tm = MOE_ROWS
    row = lambda i: (i, 0)
    const = lambda i: (0, 0)
    return pl.pallas_call(
        _combine_kernel,
        out_shape=(jax.ShapeDtypeStruct((n, d), F32), jax.ShapeDtypeStruct((n, d), BF16)),
        grid=(n // tm,),
        in_specs=[pl.BlockSpec((TOP_K, tm), lambda i: (0, i), memory_space=pltpu.SMEM),
                  pl.BlockSpec((tm, d), row),
                  pl.BlockSpec((tm, TOP_K), row),
                  pl.BlockSpec(memory_space=pl.ANY),
                  pl.BlockSpec((d, 2 * EXPERT_FF), const),
                  pl.BlockSpec((EXPERT_FF, d), const),
                  pl.BlockSpec((1, d), const), pl.BlockSpec((1, d), const)],
        out_specs=(pl.BlockSpec((tm, d), row), pl.BlockSpec((tm, d), row)),
        scratch_shapes=[pltpu.VMEM((TOP_K, tm, HALF), U32), pltpu.SemaphoreType.DMA(())],
        compiler_params=_params("arbitrary"),
        name="combine_shared_ln",
    )(pos_t, h, gate, ys, sh_gu, sh_down, g.reshape(1, d), b.reshape(1, d))


_MAIN_BLOCKS = (0, 1, 3, 4, 7, 8, 9, 10, 11, 12, 13, 14, 2)
_FORGET_BLOCKS = (5, 6)
_Q_A, _Q_R, _I_R, _G_R, _GATE_A, _GATE_R = (i * D_MODEL for i in range(6))
_K_A = 6 * D_MODEL
_V_A = _K_A + KV_WIDTH


def kernel(x, ln_in_g, ln_in_b, lb_logits, w_in, b_in, attn_sinks, rec_norm_g, w_proj_attn, w_proj_rec, w_out,
           ln1_g, ln1_b, router_w, router_bias, expert_w_gu, expert_w_down, shared_w_gu, shared_w_down,
           ln2_g, ln2_b):
    batch, seq, d = x.shape
    n = batch * seq
    depth = w_in.shape[0]
    n_blocks = n * TOP_K // ROW_BLOCK + N_EXPERTS
    rows = n_blocks * ROW_BLOCK

    p = jax.nn.softmax(lb_logits.astype(F32), axis=0)
    cum = jnp.cumsum(p, axis=0)
    lower = cum - cum[0:1]
    log_lb = jnp.log(lower)
    log1m_lb = jnp.log1p(-lower)

    h, hb = layer_norm_in(x.reshape(n, d), ln_in_g, ln_in_b)
    for l in range(depth):
        proj = in_proj(hb, w_in, b_in, l, _MAIN_BLOCKS, BF16, "in_proj_main")
        proj_f = in_proj(hb, w_in, b_in, l, _FORGET_BLOCKS, F32, "in_proj_forget")
        attn = swa_attention(proj, attn_sinks[l], batch, seq, _Q_A, _K_A, _V_A)
        rec = hgrn2(proj, proj_f, log_lb[l], log1m_lb[l], rec_norm_g[l], batch, seq, _Q_R, _I_R, _G_R)
        h, hp = merge_outproj_ln(h, attn, rec, proj, _GATE_A, _GATE_R,
                                 w_proj_attn[l].astype(BF16), w_proj_rec[l].astype(BF16), w_out[l].astype(BF16),
                                 ln1_g[l], ln1_b[l])

        rw_t = router_w[l].T
        rw_hi = rw_t.astype(BF16)
        rw_lo = (rw_t - rw_hi.astype(F32)).astype(BF16)
        idx_t, gate_t, rank_t, cnt = router(h, rw_hi, rw_lo, router_bias[l])
        counts = cnt[:, 0]
        padded = (counts + ROW_BLOCK - 1) // ROW_BLOCK * ROW_BLOCK
        pad_end = jnp.cumsum(padded)
        pos_t = slot_positions(idx_t, rank_t, pad_end - padded)
        blk_expert = jnp.minimum(
            jnp.searchsorted(pad_end, jnp.arange(n_blocks, dtype=I32) * ROW_BLOCK, side="right"),
            N_EXPERTS - 1).astype(I32)
        n_used = (pad_end[-1:] // ROW_BLOCK).astype(I32)

        xs = dispatch(hp, pos_t, pad_end.astype(I32), padded.astype(I32), rows)
        ys = expert_ffn(xs, blk_expert, n_used, expert_w_gu, expert_w_down, l)
        h, hb = combine_shared_ln(h, pos_t, gate_t.T, ys, shared_w_gu[l].astype(BF16),
                                  shared_w_down[l].astype(BF16), ln2_g[l], ln2_b[l])
    return h.reshape(batch, seq, d)
```

```python
import jax
import jax.numpy as jnp
from jax import lax
from jax.experimental import pallas as pl
from jax.experimental.pallas import tpu as pltpu
from jax.experimental.pallas import tpu_sc as plsc

F32 = jnp.float32
BF16 = jnp.bfloat16
U32 = jnp.uint32
I32 = jnp.int32

D_MODEL = 1024
HALF = D_MODEL // 2
CHUNK = 64
ATTN_HEADS = 16
ATTN_KV_HEADS = 4
ATTN_HEAD_DIM = 64
ATTN_GROUP = ATTN_HEADS // ATTN_KV_HEADS
WIN_CHUNKS = 2
KV_WIDTH = ATTN_KV_HEADS * ATTN_HEAD_DIM
REC_HEADS = 8
REC_DIM = 128
N_EXPERTS = 256
TOP_K = 8
N_GROUPS = 8
GROUP_SIZE = N_EXPERTS // N_GROUPS
TOPK_GROUPS = 4
EXPERT_FF = 256
ROUTED_SCALE = 2.5
DEPTH = 2
DEEPNORM_ALPHA = (2 * DEPTH) ** 0.25
LN_EPS = 1e-5
RMS_EPS = 1e-5
NEG_INF = float("-inf")

VMEM_LIMIT_BYTES = 48 * 1024 * 1024

LN_ROWS = 512
PROJ_ROWS = 2048
PROJ_COLS = 512
ATTN_Q_BLOCK = 256
REC_TILE = 256
REC_SUB = 16
MERGE_ROWS = 256
ROUTER_ROWS = 256
MOE_ROWS = 256
ROW_BLOCK = 256

DECAY_LIMIT = 60.0


def _params(*sem):
    return pltpu.CompilerParams(dimension_semantics=sem, vmem_limit_bytes=VMEM_LIMIT_BYTES)


def _layer_norm_rows(x, g, b):
    mu = jnp.mean(x, axis=-1, keepdims=True)
    xc = x - mu
    var = jnp.mean(xc * xc, axis=-1, keepdims=True)
    return xc * lax.rsqrt(var + LN_EPS) * g + b


def _sigmoid(x):
    return 1.0 / (1.0 + jnp.exp(-x))


def _silu(x):
    return x * _sigmoid(x)


def _pack_halves(lo, hi):
    lo_bits = pltpu.bitcast(lo.astype(BF16).astype(F32), U32)
    hi_bits = pltpu.bitcast(hi.astype(BF16).astype(F32), U32)
    return lax.shift_right_logical(lo_bits, U32(16)) | (hi_bits & U32(0xFFFF0000))


def _unpack_halves(w):
    lo = pltpu.bitcast(lax.shift_left(w, U32(16)), F32)
    hi = pltpu.bitcast(w & U32(0xFFFF0000), F32)
    return lo, hi


def _ln_in_kernel(x_ref, g_ref, b_ref, h_ref, hb_ref):
    h = _layer_norm_rows(x_ref[...], g_ref[...], b_ref[...])
    h_ref[...] = h
    hb_ref[...] = h.astype(BF16)


def layer_norm_in(x, g, b):
    n, d = x.shape
    row = lambda i: (i, 0)
    const = lambda i: (0, 0)
    return pl.pallas_call(
        _ln_in_kernel,
        out_shape=(jax.ShapeDtypeStruct((n, d), F32), jax.ShapeDtypeStruct((n, d), BF16)),
        grid=(n // LN_ROWS,),
        in_specs=[pl.BlockSpec((LN_ROWS, d), row), pl.BlockSpec((1, d), const), pl.BlockSpec((1, d), const)],
        out_specs=(pl.BlockSpec((LN_ROWS, d), row), pl.BlockSpec((LN_ROWS, d), row)),
        compiler_params=_params("parallel"),
        name="ln_in",
    )(x, g.reshape(1, d), b.reshape(1, d))


def _in_proj_kernel(perm_ref, x_ref, w_ref, b_ref, o_ref):
    del perm_ref
    acc = jnp.dot(x_ref[...], w_ref[0], preferred_element_type=F32)
    o_ref[...] = (acc + b_ref[0]).astype(o_ref.dtype)


def in_proj(xb, w, b, layer, col_blocks, out_dtype, name):
    n, k = xb.shape
    perm = jnp.asarray(col_blocks, I32)
    nblk = len(col_blocks)
    tm = min(PROJ_ROWS, n)
    return pl.pallas_call(
        _in_proj_kernel,
        out_shape=jax.ShapeDtypeStruct((n, nblk * PROJ_COLS), out_dtype),
        grid_spec=pltpu.PrefetchScalarGridSpec(
            num_scalar_prefetch=1,
            grid=(n // tm, nblk),
            in_specs=[pl.BlockSpec((tm, k), lambda i, j, p: (i, 0)),
                      pl.BlockSpec((1, k, PROJ_COLS), lambda i, j, p: (layer, 0, p[j])),
                      pl.BlockSpec((1, 1, PROJ_COLS), lambda i, j, p: (layer, 0, p[j]))],
            out_specs=pl.BlockSpec((tm, PROJ_COLS), lambda i, j, p: (i, j))),
        compiler_params=_params("parallel", "arbitrary"),
        name=name,
    )(perm, xb, w, b.reshape(b.shape[0], 1, -1))


def _attn_kernel(sink_ref, q_ref, kp_ref, kc_ref, vp_ref, vc_ref, o_ref):
    i = pl.program_id(1)
    half = ATTN_Q_BLOCK // 2
    nk = ATTN_Q_BLOCK + half
    k = jnp.concatenate([kp_ref[half:, :], kc_ref[...]], axis=0)
    v = jnp.concatenate([vp_ref[half:, :], vc_ref[...]], axis=0)
    qc = lax.broadcasted_iota(I32, (ATTN_Q_BLOCK, nk), 0) // CHUNK
    kc = lax.broadcasted_iota(I32, (ATTN_Q_BLOCK, nk), 1) // CHUNK
    first = jnp.where(i == 0, WIN_CHUNKS, 0)
    valid = (kc >= qc) & (kc <= qc + WIN_CHUNKS) & (kc >= first)
    mask_bias = jnp.where(valid, 0.0, NEG_INF)
    scale = ATTN_HEAD_DIM ** -0.5
    for h in range(ATTN_HEADS):
        kv = h // ATTN_GROUP
        qh = q_ref[:, h * ATTN_HEAD_DIM:(h + 1) * ATTN_HEAD_DIM] * scale
        kh = k[:, kv * ATTN_HEAD_DIM:(kv + 1) * ATTN_HEAD_DIM]
        vh = v[:, kv * ATTN_HEAD_DIM:(kv + 1) * ATTN_HEAD_DIM]
        s = lax.dot_general(qh, kh, (((1,), (1,)), ((), ())), preferred_element_type=F32) + mask_bias
        sink = sink_ref[h]
        m = jnp.maximum(jnp.max(s, axis=-1, keepdims=True), sink)
        e = jnp.exp(s - m)
        denom = jnp.sum(e, axis=-1, keepdims=True) + jnp.exp(sink - m)
        oh = jnp.dot(e.astype(BF16), vh, preferred_element_type=F32) * (1.0 / denom)
        o_ref[:, h * ATTN_HEAD_DIM:(h + 1) * ATTN_HEAD_DIM] = oh.astype(o_ref.dtype)


def swa_attention(proj, sinks, batch, seq, q_col, k_col, v_col):
    n = batch * seq
    nb = seq // ATTN_Q_BLOCK
    qb, kb, vb = q_col // D_MODEL, k_col // KV_WIDTH, v_col // KV_WIDTH

    def cur(col):
        return lambda b, i, s: (b * nb + i, col)

    def prev(col):
        return lambda b, i, s: (b * nb + jnp.maximum(i - 1, 0), col)

    return pl.pallas_call(
        _attn_kernel,
        out_shape=jax.ShapeDtypeStruct((n, D_MODEL), BF16),
        grid_spec=pltpu.PrefetchScalarGridSpec(
            num_scalar_prefetch=1,
            grid=(batch, nb),
            in_specs=[pl.BlockSpec((ATTN_Q_BLOCK, D_MODEL), cur(qb)),
                      pl.BlockSpec((ATTN_Q_BLOCK, KV_WIDTH), prev(kb)),
                      pl.BlockSpec((ATTN_Q_BLOCK, KV_WIDTH), cur(kb)),
                      pl.BlockSpec((ATTN_Q_BLOCK, KV_WIDTH), prev(vb)),
                      pl.BlockSpec((ATTN_Q_BLOCK, KV_WIDTH), cur(vb))],
            out_specs=pl.BlockSpec((ATTN_Q_BLOCK, D_MODEL), lambda b, i, s: (b * nb + i, 0))),
        compiler_params=_params("parallel", "parallel"),
        name="swa_attention",
    )(sinks.astype(F32), proj, proj, proj, proj, proj)


def _hgrn_kernel(q_ref, i_ref, g_ref, f_ref, loglb_ref, log1mlb_ref, ng_ref, o_ref,
                 b_ref, k_ref, qs_ref, ep_ref, intra_ref, inter_ref, qd_ref, kt_ref, *st_ref):
    @pl.when(pl.program_id(1) == 0)
    def _():
        for s_ref in st_ref:
            s_ref[...] = jnp.zeros_like(s_ref)

    fl = f_ref[...]
    log_sig = jnp.minimum(fl, 0.0) - jnp.log(1.0 + jnp.exp(-jnp.abs(fl)))
    a = loglb_ref[...]
    c = log1mlb_ref[...] + log_sig
    log_f = jnp.maximum(a, c) + jnp.log(1.0 + jnp.exp(-jnp.abs(a - c)))
    k_ref[...] = 1.0 - jnp.exp(log_f)
    rows = lax.broadcasted_iota(I32, log_f.shape, 0) % REC_SUB
    b = log_f
    shift = 1
    while shift < REC_SUB:
        b = b + jnp.where(rows >= shift, pltpu.roll(b, shift, axis=0), 0.0)
        shift *= 2
    b_ref[...] = b
    qs_ref[...] = _silu(q_ref[...].astype(F32))

    t_iota = lax.broadcasted_iota(I32, (REC_SUB, REC_DIM), 0)
    nt = (((1,), (1,)), ((), ()))
    tn = (((0,), (0,)), ((), ()))

    def finish(o, r0, cols):
        ms = jnp.mean(o * o, axis=-1, keepdims=True)
        o = o * lax.rsqrt(ms + RMS_EPS) * ng_ref[...]
        gj = g_ref[pl.ds(r0, REC_SUB), cols].astype(F32)
        o_ref[pl.ds(r0, REC_SUB), cols] = (o * _silu(gj)).astype(o_ref.dtype)

    decay_bounded = jnp.min(b) >= -DECAY_LIMIT

    @pl.when(decay_bounded)
    def _():
        ep = jnp.exp(b_ref[...])
        ep_ref[...] = ep
        qd_ref[...] = (qs_ref[...] * ep).astype(BF16)
        kt_ref[...] = (k_ref[...] * jnp.exp(-b_ref[...])).astype(BF16)
        ri = lax.broadcasted_iota(I32, (REC_TILE, REC_TILE), 0)
        ci = lax.broadcasted_iota(I32, (REC_TILE, REC_TILE), 1)
        keep = (ri >= ci) & (ri // REC_SUB == ci // REC_SUB)
        for h in range(REC_HEADS):
            cols = slice(h * REC_DIM, (h + 1) * REC_DIM)
            att = lax.dot_general(qd_ref[:, cols], kt_ref[:, cols], nt, preferred_element_type=F32)
            att = jnp.where(keep, att, 0.0).astype(BF16)
            intra_ref[:, cols] = jnp.dot(att, i_ref[:, cols], preferred_element_type=F32)

        def step(j, carry):
            r0 = pl.multiple_of(j * REC_SUB, REC_SUB)
            for h in range(REC_HEADS):
                cols = slice(h * REC_DIM, (h + 1) * REC_DIM)
                qd = qd_ref[pl.ds(r0, REC_SUB), cols]
                kt = kt_ref[pl.ds(r0, REC_SUB), cols]
                vj = i_ref[pl.ds(r0, REC_SUB), cols]
                st = st_ref[h][...]
                inter_ref[pl.ds(r0, REC_SUB), cols] = lax.dot_general(
                    qd, st.astype(BF16), nt, preferred_element_type=F32)
                e_end = ep_ref[pl.ds(r0, REC_SUB), cols][REC_SUB - 1:REC_SUB, :]
                kv_t = lax.dot_general(vj, kt, tn, preferred_element_type=F32)
                st_ref[h][...] = (st + kv_t) * e_end
            return carry

        lax.fori_loop(0, REC_TILE // REC_SUB, step, 0)

        for h in range(REC_HEADS):
            cols = slice(h * REC_DIM, (h + 1) * REC_DIM)
            o = intra_ref[:, cols] + inter_ref[:, cols]
            ms = jnp.mean(o * o, axis=-1, keepdims=True)
            o = o * lax.rsqrt(ms + RMS_EPS) * ng_ref[...]
            o_ref[:, cols] = (o * _silu(g_ref[:, cols].astype(F32))).astype(o_ref.dtype)

    @pl.when(jnp.logical_not(decay_bounded))
    def _():
        ones = jnp.ones((REC_DIM, REC_DIM), BF16)

        def step(j, carry):
            r0 = pl.multiple_of(j * REC_SUB, REC_SUB)
            for h in range(REC_HEADS):
                cols = slice(h * REC_DIM, (h + 1) * REC_DIM)
                bj = b_ref[pl.ds(r0, REC_SUB), cols]
                kj = k_ref[pl.ds(r0, REC_SUB), cols]
                qj = qs_ref[pl.ds(r0, REC_SUB), cols]
                vj = i_ref[pl.ds(r0, REC_SUB), cols].astype(F32)
                st = st_ref[h][...]
                qd = (qj * jnp.exp(bj)).astype(BF16)
                o = lax.dot_general(qd, st.astype(BF16), nt, preferred_element_type=F32)
                parts = []
                for s in range(REC_SUB):
                    dec = jnp.exp(jnp.where(t_iota >= s, bj - bj[s:s + 1, :], NEG_INF))
                    parts.append((qj * dec * kj[s:s + 1, :]).astype(BF16))
                pstack = jnp.concatenate(parts, axis=0)
                rsum = jnp.dot(pstack, ones, preferred_element_type=F32)
                for s in range(REC_SUB):
                    o = o + rsum[s * REC_SUB:(s + 1) * REC_SUB, :] * vj[s:s + 1, :]
                b_end = bj[REC_SUB - 1:REC_SUB, :]
                kd = (kj * jnp.exp(b_end - bj)).astype(BF16)
                kv_t = lax.dot_general(vj.astype(BF16), kd, tn, preferred_element_type=F32)
                st_ref[h][...] = st * jnp.exp(b_end) + kv_t
                finish(o, r0, cols)
            return carry

        lax.fori_loop(0, REC_TILE // REC_SUB, step, 0)


def hgrn2(proj, proj_f, log_lb, log1m_lb, norm_g, batch, seq, q_col, i_col, g_col):
    n = batch * seq
    nb = seq // REC_TILE
    d = D_MODEL

    def blk(col):
        return pl.BlockSpec((REC_TILE, d), lambda b, i: (b * nb + i, col // d))

    return pl.pallas_call(
        _hgrn_kernel,
        out_shape=jax.ShapeDtypeStruct((n, d), BF16),
        grid=(batch, nb),
        in_specs=[blk(q_col), blk(i_col), blk(g_col),
                  pl.BlockSpec((REC_TILE, d), lambda b, i: (b * nb + i, 0)),
                  pl.BlockSpec((1, d), lambda b, i: (0, 0)),
                  pl.BlockSpec((1, d), lambda b, i: (0, 0)),
                  pl.BlockSpec((1, REC_DIM), lambda b, i: (0, 0))],
        out_specs=pl.BlockSpec((REC_TILE, d), lambda b, i: (b * nb + i, 0)),
        scratch_shapes=[pltpu.VMEM((REC_TILE, d), F32) for _ in range(6)]
                       + [pltpu.VMEM((REC_TILE, d), BF16),
                        pltpu.VMEM((REC_TILE, d), BF16)]
                       + [pltpu.VMEM((REC_DIM, REC_DIM), F32) for _ in range(REC_HEADS)],
        compiler_params=_params("parallel", "arbitrary"),
        name="hgrn2",
    )(proj, proj, proj, proj_f, log_lb.reshape(1, d), log1m_lb.reshape(1, d), norm_g.reshape(1, REC_DIM))


def _merge_kernel(h_ref, attn_ref, rec_ref, ga_ref, gr_ref, wpa_ref, wpr_ref, wo_ref, g_ref, b_ref,
                  o_ref, op_ref):
    a = jnp.dot(attn_ref[...], wpa_ref[...], preferred_element_type=F32)
    r = jnp.dot(rec_ref[...], wpr_ref[...], preferred_element_type=F32)
    merged = _sigmoid(ga_ref[...].astype(F32)) * a + _sigmoid(gr_ref[...].astype(F32)) * r
    y = jnp.dot(merged.astype(BF16), wo_ref[...], preferred_element_type=F32)
    h1 = _layer_norm_rows(DEEPNORM_ALPHA * h_ref[...] + y, g_ref[...], b_ref[...])
    o_ref[...] = h1
    op_ref[...] = _pack_halves(h1[:, :HALF], h1[:, HALF:])


def merge_outproj_ln(h, attn, rec, proj, ga_col, gr_col, wpa, wpr, wo, g, b):
    n, d = h.shape
    tm = MERGE_ROWS
    row = lambda i: (i, 0)
    const = lambda i: (0, 0)
    return pl.pallas_call(
        _merge_kernel,
        out_shape=(jax.ShapeDtypeStruct((n, d), F32), jax.ShapeDtypeStruct((n, HALF), U32)),
        grid=(n // tm,),
        in_specs=[pl.BlockSpec((tm, d), row), pl.BlockSpec((tm, d), row), pl.BlockSpec((tm, d), row),
                  pl.BlockSpec((tm, d), lambda i: (i, ga_col // d)),
                  pl.BlockSpec((tm, d), lambda i: (i, gr_col // d)),
                  pl.BlockSpec((d, d), const), pl.BlockSpec((d, d), const), pl.BlockSpec((d, d), const),
                  pl.BlockSpec((1, d), const), pl.BlockSpec((1, d), const)],
        out_specs=(pl.BlockSpec((tm, d), row), pl.BlockSpec((tm, HALF), row)),
        compiler_params=_params("parallel"),
        name="merge_outproj_ln",
    )(h, attn, rec, proj, proj, wpa, wpr, wo, g.reshape(1, d), b.reshape(1, d))


def _router_kernel(h_ref, whi_ref, wlo_ref, bias_ref, idx_ref, gate_ref, rank_ref, cnt_ref, carry_ref):
    @pl.when(pl.program_id(0) == 0)
    def _():
        carry_ref[...] = jnp.zeros_like(carry_ref)

    tm = h_ref.shape[0]
    h = h_ref[...]
    h_hi = h.astype(BF16)
    h_lo = (h - h_hi.astype(F32)).astype(BF16)
    nt = (((1,), (1,)), ((), ()))
    logits = (lax.dot_general(whi_ref[...], h_hi, nt, preferred_element_type=F32)
              + lax.dot_general(whi_ref[...], h_lo, nt, preferred_element_type=F32)
              + lax.dot_general(wlo_ref[...], h_hi, nt, preferred_element_type=F32))
    scores = _sigmoid(logits)
    sel = scores + bias_ref[...]
    e_iota = lax.broadcasted_iota(I32, (N_EXPERTS, tm), 0)

    g_iota = lax.broadcasted_iota(I32, (N_GROUPS, tm), 0)
    l_iota = lax.broadcasted_iota(I32, (GROUP_SIZE, tm), 0)
    grp = jnp.zeros((N_GROUPS, tm), F32)
    for g in range(N_GROUPS):
        sg = sel[g * GROUP_SIZE:(g + 1) * GROUP_SIZE, :]
        m1 = jnp.max(sg, axis=0, keepdims=True)
        i1 = jnp.min(jnp.where(sg == m1, l_iota, GROUP_SIZE), axis=0, keepdims=True)
        m2 = jnp.max(jnp.where(l_iota == i1, NEG_INF, sg), axis=0, keepdims=True)
        grp = jnp.where(g_iota == g, m1 + m2, grp)
    gsel = jnp.zeros((N_GROUPS, tm), I32)
    for _ in range(TOPK_GROUPS):
        m = jnp.max(grp, axis=0, keepdims=True)
        gi = jnp.min(jnp.where(grp == m, g_iota, N_GROUPS), axis=0, keepdims=True)
        hit = g_iota == gi
        gsel = jnp.where(hit, 1, gsel)
        grp = jnp.where(hit, NEG_INF, grp)
    masked = []
    for g in range(N_GROUPS):
        sg = sel[g * GROUP_SIZE:(g + 1) * GROUP_SIZE, :]
        masked.append(jnp.where(gsel[g:g + 1, :] > 0, sg, NEG_INF))
    selm = jnp.concatenate(masked, axis=0)

    k_iota = lax.broadcasted_iota(I32, (TOP_K, tm), 0)
    idx = jnp.zeros((TOP_K, tm), I32)
    gate = jnp.zeros((TOP_K, tm), F32)
    member = jnp.zeros((N_EXPERTS, tm), F32)
    for k in range(TOP_K):
        m = jnp.max(selm, axis=0, keepdims=True)
        ei = jnp.min(jnp.where(selm == m, e_iota, N_EXPERTS), axis=0, keepdims=True)
        hit = e_iota == ei
        gk = jnp.sum(jnp.where(hit, scores, 0.0), axis=0, keepdims=True)
        idx = jnp.where(k_iota == k, ei, idx)
        gate = jnp.where(k_iota == k, gk, gate)
        member = jnp.where(hit, 1.0, member)
        selm = jnp.where(hit, NEG_INF, selm)
    gate = gate / jnp.sum(gate, axis=0, keepdims=True) * ROUTED_SCALE

    upper = lax.broadcasted_iota(I32, (tm, tm), 0) < lax.broadcasted_iota(I32, (tm, tm), 1)
    before = jnp.dot(member.astype(BF16), upper.astype(BF16), preferred_element_type=F32) + carry_ref[...]
    rank = jnp.zeros((TOP_K, tm), F32)
    for k in range(TOP_K):
        rk = jnp.sum(jnp.where(e_iota == idx[k:k + 1, :], before, 0.0), axis=0, keepdims=True)
        rank = jnp.where(k_iota == k, rk, rank)
    carry_ref[...] = carry_ref[...] + jnp.sum(member, axis=1, keepdims=True)

    idx_ref[...] = idx
    gate_ref[...] = gate
    rank_ref[...] = rank.astype(I32)
    cnt_ref[...] = jnp.broadcast_to(carry_ref[...], cnt_ref.shape).astype(I32)


def router(h, w_t_hi, w_t_lo, bias):
    n, d = h.shape
    tm = ROUTER_ROWS
    tok = lambda i: (0, i)
    const = lambda i: (0, 0)
    return pl.pallas_call(
        _router_kernel,
        out_shape=(jax.ShapeDtypeStruct((TOP_K, n), I32),
                   jax.ShapeDtypeStruct((TOP_K, n), F32),
                   jax.ShapeDtypeStruct((TOP_K, n), I32),
                   jax.ShapeDtypeStruct((N_EXPERTS, 128), I32)),
        grid=(n // tm,),
        in_specs=[pl.BlockSpec((tm, d), lambda i: (i, 0)),
                  pl.BlockSpec((N_EXPERTS, d), const),
                  pl.BlockSpec((N_EXPERTS, d), const),
                  pl.BlockSpec((N_EXPERTS, 1), const)],
        out_specs=(pl.BlockSpec((TOP_K, tm), tok), pl.BlockSpec((TOP_K, tm), tok),
                   pl.BlockSpec((TOP_K, tm), tok), pl.BlockSpec((N_EXPERTS, 128), const)),
        scratch_shapes=[pltpu.VMEM((N_EXPERTS, 1), F32)],
        compiler_params=_params("arbitrary"),
        name="router",
    )(h, w_t_hi, w_t_lo, bias.reshape(N_EXPERTS, 1))


def _slot_pos_kernel(idx_ref, rank_ref, start_ref, pos_ref):
    tm = idx_ref.shape[1]
    e_iota = lax.broadcasted_iota(I32, (N_EXPERTS, tm), 0)
    k_iota = lax.broadcasted_iota(I32, (TOP_K, tm), 0)
    idx = idx_ref[...]
    start = start_ref[...]
    base = jnp.zeros((TOP_K, tm), F32)
    for k in range(TOP_K):
        bk = jnp.sum(jnp.where(e_iota == idx[k:k + 1, :], start, 0.0), axis=0, keepdims=True)
        base = jnp.where(k_iota == k, bk, base)
    pos_ref[...] = base.astype(I32) + rank_ref[...]


def slot_positions(idx_t, rank_t, pad_start):
    n = idx_t.shape[1]
    tm = 1024
    tok = lambda i: (0, i)
    return pl.pallas_call(
        _slot_pos_kernel,
        out_shape=jax.ShapeDtypeStruct((TOP_K, n), I32),
        grid=(n // tm,),
        in_specs=[pl.BlockSpec((TOP_K, tm), tok), pl.BlockSpec((TOP_K, tm), tok),
                  pl.BlockSpec((N_EXPERTS, 1), lambda i: (0, 0))],
        out_specs=pl.BlockSpec((TOP_K, tm), tok),
        compiler_params=_params("parallel"),
        name="slot_positions",
    )(idx_t, rank_t, pad_start.astype(F32).reshape(N_EXPERTS, 1))


SC_WINDOW = 128
SC_WORDS = HALF // 2


def _sc_mesh():
    return plsc.VectorSubcoreMesh(core_axis_name="core", subcore_axis_name="subcore")


def _half_row_indices(pos):
    return (2 * pos[..., None] + jnp.arange(2, dtype=I32)).reshape(*pos.shape[:-1], 2 * pos.shape[-1])


def sc_scatter_rows(src, pos_t, rows):
    n = src.shape[0]
    src2 = src.reshape(2 * n, SC_WORDS)
    idx2 = _half_row_indices(pos_t)

    @pl.kernel(out_type=jax.ShapeDtypeStruct((2 * rows, SC_WORDS), src.dtype), mesh=_sc_mesh(), scratch_types=[])
    def scatter_kernel(x_hbm, i_hbm, o_hbm):
        def body(x_vmem, i_vmem):
            pltpu.sync_copy(x_vmem, o_hbm.at[i_vmem.at[0]])

        pltpu.emit_pipeline(
            body,
            grid=(2 * n // SC_WINDOW, TOP_K),
            in_specs=[pl.BlockSpec((SC_WINDOW, SC_WORDS), index_map=lambda i, k: (i, 0)),
                      pl.BlockSpec((1, SC_WINDOW), index_map=lambda i, k: (k, i))],
            out_specs=[],
            core_axis_name=("core", "subcore"),
            dimension_semantics=(pltpu.PARALLEL, pltpu.ARBITRARY),
        )(x_hbm, i_hbm)

    return scatter_kernel(src2, idx2).reshape(rows, 2 * SC_WORDS)


def sc_gather_rows(src, pos):
    r = src.shape[0]
    m = pos.shape[0]
    src2 = src.reshape(2 * r, SC_WORDS)
    idx2 = _half_row_indices(pos).reshape(1, 2 * m)

    @pl.kernel(out_type=jax.ShapeDtypeStruct((2 * m, SC_WORDS), src.dtype), mesh=_sc_mesh(), scratch_types=[])
    def gather_kernel(x_hbm, i_hbm, o_hbm):
        def body(i_vmem, o_vmem):
            pltpu.sync_copy(x_hbm.at[i_vmem.at[0]], o_vmem)

        pltpu.emit_pipeline(
            body,
            grid=(2 * m // SC_WINDOW,),
            in_specs=[pl.BlockSpec((1, SC_WINDOW), index_map=lambda i: (0, i))],
            out_specs=[pl.BlockSpec((SC_WINDOW, SC_WORDS), index_map=lambda i: (i, 0))],
            core_axis_name=("core", "subcore"),
            dimension_semantics=(pltpu.PARALLEL,),
        )(i_hbm, o_hbm)

    return gather_kernel(src2, idx2).reshape(m, 2 * SC_WORDS)


def _expert_kernel(blk_expert_ref, blk_valid_ref, n_used_ref, x_ref, wgu_ref, wd_ref, y_ref, wgu_bf, wd_bf):
    j = pl.program_id(0)
    used = j < n_used_ref[0]
    prev = blk_expert_ref[jnp.maximum(j - 1, 0)]

    @pl.when(used & ((j == 0) | (blk_expert_ref[j] != prev)))
    def _():
        wgu_bf[...] = wgu_ref[0, 0].astype(BF16)
        wd_bf[...] = wd_ref[0, 0].astype(BF16)

    @pl.when(used)
    def _():
        row = lax.broadcasted_iota(I32, x_ref.shape, 0)
        lo, hi = _unpack_halves(jnp.where(row < blk_valid_ref[j], x_ref[...], U32(0)))
        gu = (jnp.dot(lo.astype(BF16), wgu_bf[:HALF, :], preferred_element_type=F32)
              + jnp.dot(hi.astype(BF16), wgu_bf[HALF:, :], preferred_element_type=F32))
        act = (_silu(gu[:, :EXPERT_FF]) * gu[:, EXPERT_FF:]).astype(BF16)
        y = jnp.dot(act, wd_bf[...], preferred_element_type=F32)
        y_ref[...] = _pack_halves(y[:, :HALF], y[:, HALF:])

    @pl.when(jnp.logical_not(used))
    def _():
        y_ref[...] = jnp.zeros_like(y_ref)


def expert_ffn(xs, blk_expert, blk_valid, n_used, w_gu, w_down, layer):
    rows, w = xs.shape
    d = D_MODEL
    n_blocks = rows // ROW_BLOCK

    def row_map(j, be, bv, nu):
        return (jnp.minimum(j, nu[0] - 1), 0)

    def w_map(j, be, bv, nu):
        return (layer, be[jnp.minimum(j, nu[0] - 1)], 0, 0)

    return pl.pallas_call(
        _expert_kernel,
        out_shape=jax.ShapeDtypeStruct((rows, w), U32),
        grid_spec=pltpu.PrefetchScalarGridSpec(
            num_scalar_prefetch=3,
            grid=(n_blocks,),
            in_specs=[pl.BlockSpec((ROW_BLOCK, w), row_map),
                      pl.BlockSpec((1, 1, d, 2 * EXPERT_FF), w_map),
                      pl.BlockSpec((1, 1, EXPERT_FF, d), w_map)],
            out_specs=pl.BlockSpec((ROW_BLOCK, w), lambda j, be, bv, nu: (j, 0)),
            scratch_shapes=[pltpu.VMEM((d, 2 * EXPERT_FF), BF16), pltpu.VMEM((EXPERT_FF, d), BF16)]),
        compiler_params=_params("arbitrary"),
        name="expert_ffn",
    )(blk_expert, blk_valid, n_used, xs, w_gu, w_down)


def _combine_kernel(h_ref, gate_ref, ys_ref, sgu_ref, sd_ref, g_ref, b_ref, o_ref, ob_ref):
    h = h_ref[...]
    gu = jnp.dot(h.astype(BF16), sgu_ref[...], preferred_element_type=F32)
    act = _silu(gu[:, :EXPERT_FF]) * gu[:, EXPERT_FF:]
    y = jnp.dot(act.astype(BF16), sd_ref[...], preferred_element_type=F32)
    gate = gate_ref[...]
    y_lo = y[:, :HALF]
    y_hi = y[:, HALF:]
    for k in range(TOP_K):
        lo, hi = _unpack_halves(ys_ref[k])
        gk = gate[:, k:k + 1]
        y_lo = y_lo + gk * lo
        y_hi = y_hi + gk * hi
    out = _layer_norm_rows(DEEPNORM_ALPHA * h + jnp.concatenate([y_lo, y_hi], axis=-1), g_ref[...], b_ref[...])
    o_ref[...] = out
    ob_ref[...] = out.astype(BF16)


def combine_shared_ln(h, gate, y_slots, sh_gu, sh_down, g, b):
    n, d = h.shape
    tm = MOE_ROWS
    row = lambda i: (i, 0)
    const = lambda i: (0, 0)
    return pl.pallas_call(
        _combine_kernel,
        out_shape=(jax.ShapeDtypeStruct((n, d), F32), jax.ShapeDtypeStruct((n, d), BF16)),
        grid=(n // tm,),
        in_specs=[pl.BlockSpec((tm, d), row),
                  pl.BlockSpec((tm, TOP_K), row),
                  pl.BlockSpec((TOP_K, tm, HALF), lambda i: (0, i, 0)),
                  pl.BlockSpec((d, 2 * EXPERT_FF), const),
                  pl.BlockSpec((EXPERT_FF, d), const),
                  pl.BlockSpec((1, d), const), pl.BlockSpec((1, d), const)],
        out_specs=(pl.BlockSpec((tm, d), row), pl.BlockSpec((tm, d), row)),
        compiler_params=_params("parallel"),
        name="combine_shared_ln",
    )(h, gate, y_slots, sh_gu, sh_down, g.reshape(1, d), b.reshape(1, d))


_MAIN_BLOCKS = (0, 1, 3, 4, 7, 8, 9, 10, 11, 12, 13, 14, 2)
_FORGET_BLOCKS = (5, 6)
_Q_A, _Q_R, _I_R, _G_R, _GATE_A, _GATE_R = (i * D_MODEL for i in range(6))
_K_A = 6 * D_MODEL
_V_A = _K_A + KV_WIDTH


def kernel(x, ln_in_g, ln_in_b, lb_logits, w_in, b_in, attn_sinks, rec_norm_g, w_proj_attn, w_proj_rec, w_out,
           ln1_g, ln1_b, router_w, router_bias, expert_w_gu, expert_w_down, shared_w_gu, shared_w_down,
           ln2_g, ln2_b):
    batch, seq, d = x.shape
    n = batch * seq
    depth = w_in.shape[0]
    n_blocks = n * TOP_K // ROW_BLOCK + N_EXPERTS
    rows = n_blocks * ROW_BLOCK

    p = jax.nn.softmax(lb_logits.astype(F32), axis=0)
    cum = jnp.cumsum(p, axis=0)
    lower = cum - cum[0:1]
    log_lb = jnp.log(lower)
    log1m_lb = jnp.log1p(-lower)

    w_in_bf = w_in.astype(BF16)
    h, hb = layer_norm_in(x.reshape(n, d), ln_in_g, ln_in_b)
    for l in range(depth):
        proj = in_proj(hb, w_in_bf, b_in, l, _MAIN_BLOCKS, BF16, "in_proj_main")
        proj_f = in_proj(hb, w_in_bf, b_in, l, _FORGET_BLOCKS, F32, "in_proj_forget")
        attn = swa_attention(proj, attn_sinks[l], batch, seq, _Q_A, _K_A, _V_A)
        rec = hgrn2(proj, proj_f, log_lb[l], log1m_lb[l], rec_norm_g[l], batch, seq, _Q_R, _I_R, _G_R)
        h, hp = merge_outproj_ln(h, attn, rec, proj, _GATE_A, _GATE_R,
                                 w_proj_attn[l].astype(BF16), w_proj_rec[l].astype(BF16), w_out[l].astype(BF16),
                                 ln1_g[l], ln1_b[l])

        rw_t = router_w[l].T
        rw_hi = rw_t.astype(BF16)
        rw_lo = (rw_t - rw_hi.astype(F32)).astype(BF16)
        idx_t, gate_t, rank_t, cnt = router(h, rw_hi, rw_lo, router_bias[l])
        counts = cnt[:, 0]
        padded = (counts + ROW_BLOCK - 1) // ROW_BLOCK * ROW_BLOCK
        pad_end = jnp.cumsum(padded)
        pad_start = pad_end - padded
        pos_t = slot_positions(idx_t, rank_t, pad_start)
        blk_row = jnp.arange(n_blocks, dtype=I32) * ROW_BLOCK
        blk_expert = jnp.minimum(jnp.searchsorted(pad_end, blk_row, side="right"), N_EXPERTS - 1).astype(I32)
        blk_valid = jnp.clip(counts[blk_expert] - (blk_row - pad_start[blk_expert]), 0, ROW_BLOCK).astype(I32)
        n_used = (pad_end[-1:] // ROW_BLOCK).astype(I32)

        xs = sc_scatter_rows(hp, pos_t, rows)
        ys = expert_ffn(xs, blk_expert, blk_valid, n_used, expert_w_gu, expert_w_down, l)
        y_slots = sc_gather_rows(ys, pos_t.reshape(TOP_K * n)).reshape(TOP_K, n, HALF)
        h, hb = combine_shared_ln(h, gate_t.T, y_slots, shared_w_gu[l].astype(BF16),
                                  shared_w_down[l].astype(BF16), ln2_g[l], ln2_b[l])
    return h.reshape(batch, seq, d)
```

```python
import jax
import jax.numpy as jnp
from jax import lax
from jax.experimental import pallas as pl
from jax.experimental.pallas import tpu as pltpu
from jax.experimental.pallas import tpu_sc as plsc

F32 = jnp.float32
BF16 = jnp.bfloat16
U32 = jnp.uint32
I32 = jnp.int32

D_MODEL = 1024
QUARTER = D_MODEL // 4
CHUNK = 64
ATTN_HEADS = 16
ATTN_KV_HEADS = 4
ATTN_HEAD_DIM = 64
ATTN_GROUP = ATTN_HEADS // ATTN_KV_HEADS
WIN_CHUNKS = 2
KV_WIDTH = ATTN_KV_HEADS * ATTN_HEAD_DIM
REC_HEADS = 8
REC_DIM = 128
N_EXPERTS = 256
TOP_K = 8
N_GROUPS = 8
GROUP_SIZE = N_EXPERTS // N_GROUPS
TOPK_GROUPS = 4
EXPERT_FF = 256
ROUTED_SCALE = 2.5
DEPTH = 2
DEEPNORM_ALPHA = (2 * DEPTH) ** 0.25
LN_EPS = 1e-5
RMS_EPS = 1e-5
NEG_INF = float("-inf")

VMEM_LIMIT_BYTES = 48 * 1024 * 1024

LN_ROWS = 512
PROJ_ROWS = 2048
PROJ_COLS = 512
ATTN_Q_BLOCK = 256
REC_TILE = 256
REC_SUB = 16
MERGE_ROWS = 256
ROUTER_ROWS = 256
MOE_ROWS = 256
ROW_BLOCK = 256

DECAY_LIMIT = 60.0


def _params(*sem):
    return pltpu.CompilerParams(dimension_semantics=sem, vmem_limit_bytes=VMEM_LIMIT_BYTES)


def _layer_norm_rows(x, g, b):
    mu = jnp.mean(x, axis=-1, keepdims=True)
    xc = x - mu
    var = jnp.mean(xc * xc, axis=-1, keepdims=True)
    return xc * lax.rsqrt(var + LN_EPS) * g + b


def _sigmoid(x):
    return 1.0 / (1.0 + jnp.exp(-x))


def _silu(x):
    return x * _sigmoid(x)


def _pack_pair(lo, hi):
    lo_bits = pltpu.bitcast(lo.astype(BF16).astype(F32), U32)
    hi_bits = pltpu.bitcast(hi.astype(BF16).astype(F32), U32)
    return lax.shift_right_logical(lo_bits, U32(16)) | (hi_bits & U32(0xFFFF0000))


def _unpack_pair(w):
    lo = pltpu.bitcast(lax.shift_left(w, U32(16)), F32)
    hi = pltpu.bitcast(w & U32(0xFFFF0000), F32)
    return lo, hi


def _store_planes(ref, x):
    q = QUARTER
    ref[0] = _pack_pair(x[:, 0 * q:1 * q], x[:, 1 * q:2 * q])
    ref[1] = _pack_pair(x[:, 2 * q:3 * q], x[:, 3 * q:4 * q])


def _load_quarters(plane0, plane1):
    return _unpack_pair(plane0) + _unpack_pair(plane1)


def _ln_in_kernel(x_ref, g_ref, b_ref, h_ref, hb_ref):
    h = _layer_norm_rows(x_ref[...], g_ref[...], b_ref[...])
    h_ref[...] = h
    hb_ref[...] = h.astype(BF16)


def layer_norm_in(x, g, b):
    n, d = x.shape
    row = lambda i: (i, 0)
    const = lambda i: (0, 0)
    return pl.pallas_call(
        _ln_in_kernel,
        out_shape=(jax.ShapeDtypeStruct((n, d), F32), jax.ShapeDtypeStruct((n, d), BF16)),
        grid=(n // LN_ROWS,),
        in_specs=[pl.BlockSpec((LN_ROWS, d), row), pl.BlockSpec((1, d), const), pl.BlockSpec((1, d), const)],
        out_specs=(pl.BlockSpec((LN_ROWS, d), row), pl.BlockSpec((LN_ROWS, d), row)),
        compiler_params=_params("parallel"),
        name="ln_in",
    )(x, g.reshape(1, d), b.reshape(1, d))


def _in_proj_kernel(perm_ref, x_ref, w_ref, b_ref, o_ref):
    del perm_ref
    acc = jnp.dot(x_ref[...], w_ref[0], preferred_element_type=F32)
    o_ref[...] = (acc + b_ref[0]).astype(o_ref.dtype)


def in_proj(xb, w, b, layer, col_blocks, out_dtype, name):
    n, k = xb.shape
    perm = jnp.asarray(col_blocks, I32)
    nblk = len(col_blocks)
    tm = min(PROJ_ROWS, n)
    return pl.pallas_call(
        _in_proj_kernel,
        out_shape=jax.ShapeDtypeStruct((n, nblk * PROJ_COLS), out_dtype),
        grid_spec=pltpu.PrefetchScalarGridSpec(
            num_scalar_prefetch=1,
            grid=(n // tm, nblk),
            in_specs=[pl.BlockSpec((tm, k), lambda i, j, p: (i, 0)),
                      pl.BlockSpec((1, k, PROJ_COLS), lambda i, j, p: (layer, 0, p[j])),
                      pl.BlockSpec((1, 1, PROJ_COLS), lambda i, j, p: (layer, 0, p[j]))],
            out_specs=pl.BlockSpec((tm, PROJ_COLS), lambda i, j, p: (i, j))),
        compiler_params=_params("parallel", "arbitrary"),
        name=name,
    )(perm, xb, w, b.reshape(b.shape[0], 1, -1))


def _attn_kernel(sink_ref, q_ref, kp_ref, kc_ref, vp_ref, vc_ref, o_ref):
    i = pl.program_id(1)
    half = ATTN_Q_BLOCK // 2
    qc = lax.broadcasted_iota(I32, (half, ATTN_Q_BLOCK), 0) // CHUNK
    kc = lax.broadcasted_iota(I32, (half, ATTN_Q_BLOCK), 1) // CHUNK
    valid = (kc >= qc) & (kc <= qc + WIN_CHUNKS)
    first = jnp.where(i == 0, WIN_CHUNKS, 0)
    scale = ATTN_HEAD_DIM ** -0.5
    windows = ((jnp.concatenate([kp_ref[half:, :], kc_ref[:half, :]], axis=0),
                jnp.concatenate([vp_ref[half:, :], vc_ref[:half, :]], axis=0),
                jnp.where(valid & (kc >= first), 0.0, NEG_INF)),
               (kc_ref[...], vc_ref[...], jnp.where(valid, 0.0, NEG_INF)))
    for part, (k, v, mask_bias) in enumerate(windows):
        rows = slice(part * half, (part + 1) * half)
        for h in range(ATTN_HEADS):
            kv = h // ATTN_GROUP
            qh = q_ref[rows, h * ATTN_HEAD_DIM:(h + 1) * ATTN_HEAD_DIM] * scale
            kh = k[:, kv * ATTN_HEAD_DIM:(kv + 1) * ATTN_HEAD_DIM]
            vh = v[:, kv * ATTN_HEAD_DIM:(kv + 1) * ATTN_HEAD_DIM]
            s = lax.dot_general(qh, kh, (((1,), (1,)), ((), ())), preferred_element_type=F32) + mask_bias
            sink = sink_ref[h]
            m = jnp.maximum(jnp.max(s, axis=-1, keepdims=True), sink)
            e = jnp.exp(s - m)
            denom = jnp.sum(e, axis=-1, keepdims=True) + jnp.exp(sink - m)
            oh = jnp.dot(e.astype(BF16), vh, preferred_element_type=F32) * (1.0 / denom)
            o_ref[rows, h * ATTN_HEAD_DIM:(h + 1) * ATTN_HEAD_DIM] = oh.astype(o_ref.dtype)


def swa_attention(proj, sinks, batch, seq, q_col, k_col, v_col):
    n = batch * seq
    nb = seq // ATTN_Q_BLOCK
    qb, kb, vb = q_col // D_MODEL, k_col // KV_WIDTH, v_col // KV_WIDTH

    def cur(col):
        return lambda b, i, s: (b * nb + i, col)

    def prev(col):
        return lambda b, i, s: (b * nb + jnp.maximum(i - 1, 0), col)

    return pl.pallas_call(
        _attn_kernel,
        out_shape=jax.ShapeDtypeStruct((n, D_MODEL), BF16),
        grid_spec=pltpu.PrefetchScalarGridSpec(
            num_scalar_prefetch=1,
            grid=(batch, nb),
            in_specs=[pl.BlockSpec((ATTN_Q_BLOCK, D_MODEL), cur(qb)),
                      pl.BlockSpec((ATTN_Q_BLOCK, KV_WIDTH), prev(kb)),
                      pl.BlockSpec((ATTN_Q_BLOCK, KV_WIDTH), cur(kb)),
                      pl.BlockSpec((ATTN_Q_BLOCK, KV_WIDTH), prev(vb)),
                      pl.BlockSpec((ATTN_Q_BLOCK, KV_WIDTH), cur(vb))],
            out_specs=pl.BlockSpec((ATTN_Q_BLOCK, D_MODEL), lambda b, i, s: (b * nb + i, 0))),
        compiler_params=_params("parallel", "parallel"),
        name="swa_attention",
    )(sinks.astype(F32), proj, proj, proj, proj, proj)


def _hgrn_kernel(q_ref, i_ref, g_ref, f_ref, loglb_ref, log1mlb_ref, ng_ref, o_ref,
                 b_ref, k_ref, qs_ref, ep_ref, intra_ref, inter_ref, qd_ref, kt_ref, *st_ref):
    @pl.when(pl.program_id(1) == 0)
    def _():
        for s_ref in st_ref:
            s_ref[...] = jnp.zeros_like(s_ref)

    fl = f_ref[...]
    log_sig = jnp.minimum(fl, 0.0) - jnp.log(1.0 + jnp.exp(-jnp.abs(fl)))
    a = loglb_ref[...]
    c = log1mlb_ref[...] + log_sig
    log_f = jnp.maximum(a, c) + jnp.log(1.0 + jnp.exp(-jnp.abs(a - c)))
    k_ref[...] = 1.0 - jnp.exp(log_f)
    rows = lax.broadcasted_iota(I32, log_f.shape, 0) % REC_SUB
    b = log_f
    shift = 1
    while shift < REC_SUB:
        b = b + jnp.where(rows >= shift, pltpu.roll(b, shift, axis=0), 0.0)
        shift *= 2
    b_ref[...] = b
    qs_ref[...] = _silu(q_ref[...].astype(F32))

    t_iota = lax.broadcasted_iota(I32, (REC_SUB, REC_DIM), 0)
    nt = (((1,), (1,)), ((), ()))
    tn = (((0,), (0,)), ((), ()))

    def finish(o, r0, cols):
        ms = jnp.mean(o * o, axis=-1, keepdims=True)
        o = o * lax.rsqrt(ms + RMS_EPS) * ng_ref[...]
        gj = g_ref[pl.ds(r0, REC_SUB), cols].astype(F32)
        o_ref[pl.ds(r0, REC_SUB), cols] = (o * _silu(gj)).astype(o_ref.dtype)

    decay_bounded = jnp.min(b) >= -DECAY_LIMIT

    @pl.when(decay_bounded)
    def _():
        ep = jnp.exp(b_ref[...])
        ep_ref[...] = ep
        qd_ref[...] = (qs_ref[...] * ep).astype(BF16)
        kt_ref[...] = (k_ref[...] * jnp.exp(-b_ref[...])).astype(BF16)
        ri = lax.broadcasted_iota(I32, (REC_TILE, REC_TILE), 0)
        ci = lax.broadcasted_iota(I32, (REC_TILE, REC_TILE), 1)
        keep = (ri >= ci) & (ri // REC_SUB == ci // REC_SUB)
        for h in range(REC_HEADS):
            cols = slice(h * REC_DIM, (h + 1) * REC_DIM)
            att = lax.dot_general(qd_ref[:, cols], kt_ref[:, cols], nt, preferred_element_type=F32)
            att = jnp.where(keep, att, 0.0).astype(BF16)
            intra_ref[:, cols] = jnp.dot(att, i_ref[:, cols], preferred_element_type=F32)

        def step(j, carry):
            r0 = pl.multiple_of(j * REC_SUB, REC_SUB)
            for h in range(REC_HEADS):
                cols = slice(h * REC_DIM, (h + 1) * REC_DIM)
                qd = qd_ref[pl.ds(r0, REC_SUB), cols]
                kt = kt_ref[pl.ds(r0, REC_SUB), cols]
                vj = i_ref[pl.ds(r0, REC_SUB), cols]
                st = st_ref[h][...]
                inter_ref[pl.ds(r0, REC_SUB), cols] = lax.dot_general(
                    qd, st.astype(BF16), nt, preferred_element_type=F32)
                e_end = ep_ref[pl.ds(r0, REC_SUB), cols][REC_SUB - 1:REC_SUB, :]
                kv_t = lax.dot_general(vj, kt, tn, preferred_element_type=F32)
                st_ref[h][...] = (st + kv_t) * e_end
            return carry

        lax.fori_loop(0, REC_TILE // REC_SUB, step, 0)

        for h in range(REC_HEADS):
            cols = slice(h * REC_DIM, (h + 1) * REC_DIM)
            o = intra_ref[:, cols] + inter_ref[:, cols]
            ms = jnp.mean(o * o, axis=-1, keepdims=True)
            o = o * lax.rsqrt(ms + RMS_EPS) * ng_ref[...]
            o_ref[:, cols] = (o * _silu(g_ref[:, cols].astype(F32))).astype(o_ref.dtype)

    @pl.when(jnp.logical_not(decay_bounded))
    def _():
        ones = jnp.ones((REC_DIM, REC_DIM), BF16)

        def step(j, carry):
            r0 = pl.multiple_of(j * REC_SUB, REC_SUB)
            for h in range(REC_HEADS):
                cols = slice(h * REC_DIM, (h + 1) * REC_DIM)
                bj = b_ref[pl.ds(r0, REC_SUB), cols]
                kj = k_ref[pl.ds(r0, REC_SUB), cols]
                qj = qs_ref[pl.ds(r0, REC_SUB), cols]
                vj = i_ref[pl.ds(r0, REC_SUB), cols].astype(F32)
                st = st_ref[h][...]
                qd = (qj * jnp.exp(bj)).astype(BF16)
                o = lax.dot_general(qd, st.astype(BF16), nt, preferred_element_type=F32)
                parts = []
                for s in range(REC_SUB):
                    dec = jnp.exp(jnp.where(t_iota >= s, bj - bj[s:s + 1, :], NEG_INF))
                    parts.append((qj * dec * kj[s:s + 1, :]).astype(BF16))
                pstack = jnp.concatenate(parts, axis=0)
                rsum = jnp.dot(pstack, ones, preferred_element_type=F32)
                for s in range(REC_SUB):
                    o = o + rsum[s * REC_SUB:(s + 1) * REC_SUB, :] * vj[s:s + 1, :]
                b_end = bj[REC_SUB - 1:REC_SUB, :]
                kd = (kj * jnp.exp(b_end - bj)).astype(BF16)
                kv_t = lax.dot_general(vj.astype(BF16), kd, tn, preferred_element_type=F32)
                st_ref[h][...] = st * jnp.exp(b_end) + kv_t
                finish(o, r0, cols)
            return carry

        lax.fori_loop(0, REC_TILE // REC_SUB, step, 0)


def hgrn2(proj, proj_f, log_lb, log1m_lb, norm_g, batch, seq, q_col, i_col, g_col):
    n = batch * seq
    nb = seq // REC_TILE
    d = D_MODEL

    def blk(col):
        return pl.BlockSpec((REC_TILE, d), lambda b, i: (b * nb + i, col // d))

    return pl.pallas_call(
        _hgrn_kernel,
        out_shape=jax.ShapeDtypeStruct((n, d), BF16),
        grid=(batch, nb),
        in_specs=[blk(q_col), blk(i_col), blk(g_col),
                  pl.BlockSpec((REC_TILE, d), lambda b, i: (b * nb + i, 0)),
                  pl.BlockSpec((1, d), lambda b, i: (0, 0)),
                  pl.BlockSpec((1, d), lambda b, i: (0, 0)),
                  pl.BlockSpec((1, REC_DIM), lambda b, i: (0, 0))],
        out_specs=pl.BlockSpec((REC_TILE, d), lambda b, i: (b * nb + i, 0)),
        scratch_shapes=[pltpu.VMEM((REC_TILE, d), F32) for _ in range(6)]
                       + [pltpu.VMEM((REC_TILE, d), BF16),
                        pltpu.VMEM((REC_TILE, d), BF16)]
                       + [pltpu.VMEM((REC_DIM, REC_DIM), F32) for _ in range(REC_HEADS)],
        compiler_params=_params("parallel", "arbitrary"),
        name="hgrn2",
    )(proj, proj, proj, proj_f, log_lb.reshape(1, d), log1m_lb.reshape(1, d), norm_g.reshape(1, REC_DIM))


def _merge_kernel(h_ref, attn_ref, rec_ref, ga_ref, gr_ref, wpa_ref, wpr_ref, wo_ref, g_ref, b_ref,
                  o_ref, op_ref):
    a = jnp.dot(attn_ref[...], wpa_ref[...], preferred_element_type=F32)
    r = jnp.dot(rec_ref[...], wpr_ref[...], preferred_element_type=F32)
    merged = _sigmoid(ga_ref[...].astype(F32)) * a + _sigmoid(gr_ref[...].astype(F32)) * r
    y = jnp.dot(merged.astype(BF16), wo_ref[...], preferred_element_type=F32)
    h1 = _layer_norm_rows(DEEPNORM_ALPHA * h_ref[...] + y, g_ref[...], b_ref[...])
    o_ref[...] = h1
    _store_planes(op_ref, h1)


def merge_outproj_ln(h, attn, rec, proj, ga_col, gr_col, wpa, wpr, wo, g, b):
    n, d = h.shape
    tm = MERGE_ROWS
    row = lambda i: (i, 0)
    const = lambda i: (0, 0)
    return pl.pallas_call(
        _merge_kernel,
        out_shape=(jax.ShapeDtypeStruct((n, d), F32), jax.ShapeDtypeStruct((2, n, QUARTER), U32)),
        grid=(n // tm,),
        in_specs=[pl.BlockSpec((tm, d), row), pl.BlockSpec((tm, d), row), pl.BlockSpec((tm, d), row),
                  pl.BlockSpec((tm, d), lambda i: (i, ga_col // d)),
                  pl.BlockSpec((tm, d), lambda i: (i, gr_col // d)),
                  pl.BlockSpec((d, d), const), pl.BlockSpec((d, d), const), pl.BlockSpec((d, d), const),
                  pl.BlockSpec((1, d), const), pl.BlockSpec((1, d), const)],
        out_specs=(pl.BlockSpec((tm, d), row), pl.BlockSpec((2, tm, QUARTER), lambda i: (0, i, 0))),
        compiler_params=_params("parallel"),
        name="merge_outproj_ln",
    )(h, attn, rec, proj, proj, wpa, wpr, wo, g.reshape(1, d), b.reshape(1, d))


def _router_kernel(h_ref, whi_ref, wlo_ref, bias_ref, idx_ref, gate_ref, rank_ref, cnt_ref, carry_ref):
    @pl.when(pl.program_id(0) == 0)
    def _():
        carry_ref[...] = jnp.zeros_like(carry_ref)

    tm = h_ref.shape[0]
    h = h_ref[...]
    h_hi = h.astype(BF16)
    h_lo = (h - h_hi.astype(F32)).astype(BF16)
    nt = (((1,), (1,)), ((), ()))
    logits = (lax.dot_general(whi_ref[...], h_hi, nt, preferred_element_type=F32)
              + lax.dot_general(whi_ref[...], h_lo, nt, preferred_element_type=F32)
              + lax.dot_general(wlo_ref[...], h_hi, nt, preferred_element_type=F32))
    scores = _sigmoid(logits)
    sel = scores + bias_ref[...]
    e_iota = lax.broadcasted_iota(I32, (N_EXPERTS, tm), 0)

    g_iota = lax.broadcasted_iota(I32, (N_GROUPS, tm), 0)
    l_iota = lax.broadcasted_iota(I32, (GROUP_SIZE, tm), 0)
    grp = jnp.zeros((N_GROUPS, tm), F32)
    for g in range(N_GROUPS):
        sg = sel[g * GROUP_SIZE:(g + 1) * GROUP_SIZE, :]
        m1 = jnp.max(sg, axis=0, keepdims=True)
        i1 = jnp.min(jnp.where(sg == m1, l_iota, GROUP_SIZE), axis=0, keepdims=True)
        m2 = jnp.max(jnp.where(l_iota == i1, NEG_INF, sg), axis=0, keepdims=True)
        grp = jnp.where(g_iota == g, m1 + m2, grp)
    gsel = jnp.zeros((N_GROUPS, tm), I32)
    for _ in range(TOPK_GROUPS):
        m = jnp.max(grp, axis=0, keepdims=True)
        gi = jnp.min(jnp.where(grp == m, g_iota, N_GROUPS), axis=0, keepdims=True)
        hit = g_iota == gi
        gsel = jnp.where(hit, 1, gsel)
        grp = jnp.where(hit, NEG_INF, grp)
    masked = []
    for g in range(N_GROUPS):
        sg = sel[g * GROUP_SIZE:(g + 1) * GROUP_SIZE, :]
        masked.append(jnp.where(gsel[g:g + 1, :] > 0, sg, NEG_INF))
    selm = jnp.concatenate(masked, axis=0)

    k_iota = lax.broadcasted_iota(I32, (TOP_K, tm), 0)
    idx = jnp.zeros((TOP_K, tm), I32)
    gate = jnp.zeros((TOP_K, tm), F32)
    member = jnp.zeros((N_EXPERTS, tm), F32)
    for k in range(TOP_K):
        m = jnp.max(selm, axis=0, keepdims=True)
        ei = jnp.min(jnp.where(selm == m, e_iota, N_EXPERTS), axis=0, keepdims=True)
        hit = e_iota == ei
        gk = jnp.sum(jnp.where(hit, scores, 0.0), axis=0, keepdims=True)
        idx = jnp.where(k_iota == k, ei, idx)
        gate = jnp.where(k_iota == k, gk, gate)
        member = jnp.where(hit, 1.0, member)
        selm = jnp.where(hit, NEG_INF, selm)
    gate = gate / jnp.sum(gate, axis=0, keepdims=True) * ROUTED_SCALE

    upper = lax.broadcasted_iota(I32, (tm, tm), 0) < lax.broadcasted_iota(I32, (tm, tm), 1)
    before = jnp.dot(member.astype(BF16), upper.astype(BF16), preferred_element_type=F32) + carry_ref[...]
    rank = jnp.zeros((TOP_K, tm), F32)
    for k in range(TOP_K):
        rk = jnp.sum(jnp.where(e_iota == idx[k:k + 1, :], before, 0.0), axis=0, keepdims=True)
        rank = jnp.where(k_iota == k, rk, rank)
    carry_ref[...] = carry_ref[...] + jnp.sum(member, axis=1, keepdims=True)

    idx_ref[...] = idx
    gate_ref[...] = gate
    rank_ref[...] = rank.astype(I32)
    cnt_ref[...] = jnp.broadcast_to(carry_ref[...], cnt_ref.shape).astype(I32)


def router(h, w_t_hi, w_t_lo, bias):
    n, d = h.shape
    tm = ROUTER_ROWS
    tok = lambda i: (0, i)
    const = lambda i: (0, 0)
    return pl.pallas_call(
        _router_kernel,
        out_shape=(jax.ShapeDtypeStruct((TOP_K, n), I32),
                   jax.ShapeDtypeStruct((TOP_K, n), F32),
                   jax.ShapeDtypeStruct((TOP_K, n), I32),
                   jax.ShapeDtypeStruct((N_EXPERTS, 128), I32)),
        grid=(n // tm,),
        in_specs=[pl.BlockSpec((tm, d), lambda i: (i, 0)),
                  pl.BlockSpec((N_EXPERTS, d), const),
                  pl.BlockSpec((N_EXPERTS, d), const),
                  pl.BlockSpec((N_EXPERTS, 1), const)],
        out_specs=(pl.BlockSpec((TOP_K, tm), tok), pl.BlockSpec((TOP_K, tm), tok),
                   pl.BlockSpec((TOP_K, tm), tok), pl.BlockSpec((N_EXPERTS, 128), const)),
        scratch_shapes=[pltpu.VMEM((N_EXPERTS, 1), F32)],
        compiler_params=_params("arbitrary"),
        name="router",
    )(h, w_t_hi, w_t_lo, bias.reshape(N_EXPERTS, 1))


def _slot_pos_kernel(idx_ref, rank_ref, start_ref, pos_ref):
    tm = idx_ref.shape[1]
    e_iota = lax.broadcasted_iota(I32, (N_EXPERTS, tm), 0)
    k_iota = lax.broadcasted_iota(I32, (TOP_K, tm), 0)
    idx = idx_ref[...]
    start = start_ref[...]
    base = jnp.zeros((TOP_K, tm), F32)
    for k in range(TOP_K):
        bk = jnp.sum(jnp.where(e_iota == idx[k:k + 1, :], start, 0.0), axis=0, keepdims=True)
        base = jnp.where(k_iota == k, bk, base)
    pos_ref[...] = base.astype(I32) + rank_ref[...]


def slot_positions(idx_t, rank_t, pad_start):
    n = idx_t.shape[1]
    tm = 1024
    tok = lambda i: (0, i)
    return pl.pallas_call(
        _slot_pos_kernel,
        out_shape=jax.ShapeDtypeStruct((TOP_K, n), I32),
        grid=(n // tm,),
        in_specs=[pl.BlockSpec((TOP_K, tm), tok), pl.BlockSpec((TOP_K, tm), tok),
                  pl.BlockSpec((N_EXPERTS, 1), lambda i: (0, 0))],
        out_specs=pl.BlockSpec((TOP_K, tm), tok),
        compiler_params=_params("parallel"),
        name="slot_positions",
    )(idx_t, rank_t, pad_start.astype(F32).reshape(N_EXPERTS, 1))


SC_WINDOW = 128
SC_WORDS = QUARTER


def _sc_mesh():
    return plsc.VectorSubcoreMesh(core_axis_name="core", subcore_axis_name="subcore")


def sc_scatter_rows(src, pos_t, rows):
    n = src.shape[1]
    src2 = src.reshape(2 * n, SC_WORDS)
    idx2 = jnp.concatenate([pos_t, pos_t + rows], axis=1)

    @pl.kernel(out_type=jax.ShapeDtypeStruct((2 * rows, SC_WORDS), src.dtype), mesh=_sc_mesh(), scratch_types=[])
    def scatter_kernel(x_hbm, i_hbm, o_hbm):
        def body(x_vmem, i_vmem):
            pltpu.sync_copy(x_vmem, o_hbm.at[i_vmem.at[0]])

        pltpu.emit_pipeline(
            body,
            grid=(2 * n // SC_WINDOW, TOP_K),
            in_specs=[pl.BlockSpec((SC_WINDOW, SC_WORDS), index_map=lambda i, k: (i, 0)),
                      pl.BlockSpec((1, SC_WINDOW), index_map=lambda i, k: (k, i))],
            out_specs=[],
            core_axis_name=("core", "subcore"),
            dimension_semantics=(pltpu.PARALLEL, pltpu.ARBITRARY),
        )(x_hbm, i_hbm)

    return scatter_kernel(src2, idx2).reshape(2, rows, SC_WORDS)


def sc_gather_rows(src, pos):
    r = src.shape[1]
    m = pos.shape[0]
    src2 = src.reshape(2 * r, SC_WORDS)
    idx2 = jnp.concatenate([pos, pos + r]).reshape(1, 2 * m)

    @pl.kernel(out_type=jax.ShapeDtypeStruct((2 * m, SC_WORDS), src.dtype), mesh=_sc_mesh(), scratch_types=[])
    def gather_kernel(x_hbm, i_hbm, o_hbm):
        def body(i_vmem, o_vmem):
            pltpu.sync_copy(x_hbm.at[i_vmem.at[0]], o_vmem)

        pltpu.emit_pipeline(
            body,
            grid=(2 * m // SC_WINDOW,),
            in_specs=[pl.BlockSpec((1, SC_WINDOW), index_map=lambda i: (0, i))],
            out_specs=[pl.BlockSpec((SC_WINDOW, SC_WORDS), index_map=lambda i: (i, 0))],
            core_axis_name=("core", "subcore"),
            dimension_semantics=(pltpu.PARALLEL,),
        )(i_hbm, o_hbm)

    return gather_kernel(src2, idx2).reshape(2, m, SC_WORDS)


def _expert_kernel(blk_expert_ref, blk_valid_ref, n_used_ref, x_ref, wgu_ref, wd_ref, y_ref, wgu_bf, wd_bf):
    j = pl.program_id(0)
    used = j < n_used_ref[0]
    prev = blk_expert_ref[jnp.maximum(j - 1, 0)]

    @pl.when(used & ((j == 0) | (blk_expert_ref[j] != prev)))
    def _():
        wgu_bf[...] = wgu_ref[0, 0].astype(BF16)
        wd_bf[...] = wd_ref[0, 0].astype(BF16)

    @pl.when(used)
    def _():
        valid = lax.broadcasted_iota(I32, x_ref.shape[1:], 0) < blk_valid_ref[j]
        quarters = _load_quarters(jnp.where(valid, x_ref[0], U32(0)), jnp.where(valid, x_ref[1], U32(0)))
        gu = sum(jnp.dot(xq.astype(BF16), wgu_bf[c * QUARTER:(c + 1) * QUARTER, :], preferred_element_type=F32)
                 for c, xq in enumerate(quarters))
        act = (_silu(gu[:, :EXPERT_FF]) * gu[:, EXPERT_FF:]).astype(BF16)
        _store_planes(y_ref, jnp.dot(act, wd_bf[...], preferred_element_type=F32))

    @pl.when(jnp.logical_not(used))
    def _():
        y_ref[...] = jnp.zeros_like(y_ref)


def expert_ffn(xs, blk_expert, blk_valid, n_used, w_gu, w_down, layer):
    _, rows, w = xs.shape
    d = D_MODEL
    n_blocks = rows // ROW_BLOCK

    def row_map(j, be, bv, nu):
        return (0, jnp.minimum(j, nu[0] - 1), 0)

    def w_map(j, be, bv, nu):
        return (layer, be[jnp.minimum(j, nu[0] - 1)], 0, 0)

    return pl.pallas_call(
        _expert_kernel,
        out_shape=jax.ShapeDtypeStruct(xs.shape, U32),
        grid_spec=pltpu.PrefetchScalarGridSpec(
            num_scalar_prefetch=3,
            grid=(n_blocks,),
            in_specs=[pl.BlockSpec((2, ROW_BLOCK, w), row_map),
                      pl.BlockSpec((1, 1, d, 2 * EXPERT_FF), w_map),
                      pl.BlockSpec((1, 1, EXPERT_FF, d), w_map)],
            out_specs=pl.BlockSpec((2, ROW_BLOCK, w), lambda j, be, bv, nu: (0, j, 0)),
            scratch_shapes=[pltpu.VMEM((d, 2 * EXPERT_FF), BF16), pltpu.VMEM((EXPERT_FF, d), BF16)]),
        compiler_params=_params("arbitrary"),
        name="expert_ffn",
    )(blk_expert, blk_valid, n_used, xs, w_gu, w_down)


def _combine_kernel(h_ref, gate_ref, ys_ref, sgu_ref, sd_ref, g_ref, b_ref, o_ref, ob_ref):
    h = h_ref[...]
    gu = jnp.dot(h.astype(BF16), sgu_ref[...], preferred_element_type=F32)
    act = _silu(gu[:, :EXPERT_FF]) * gu[:, EXPERT_FF:]
    y = jnp.dot(act.astype(BF16), sd_ref[...], preferred_element_type=F32)
    gate = gate_ref[...]
    acc = [y[:, c * QUARTER:(c + 1) * QUARTER] for c in range(4)]
    for k in range(TOP_K):
        gk = gate[:, k:k + 1]
        acc = [a + gk * q for a, q in zip(acc, _load_quarters(ys_ref[0, k], ys_ref[1, k]))]
    out = _layer_norm_rows(DEEPNORM_ALPHA * h + jnp.concatenate(acc, axis=-1), g_ref[...], b_ref[...])
    o_ref[...] = out
    ob_ref[...] = out.astype(BF16)


def combine_shared_ln(h, gate, y_slots, sh_gu, sh_down, g, b):
    n, d = h.shape
    tm = MOE_ROWS
    row = lambda i: (i, 0)
    const = lambda i: (0, 0)
    return pl.pallas_call(
        _combine_kernel,
        out_shape=(jax.ShapeDtypeStruct((n, d), F32), jax.ShapeDtypeStruct((n, d), BF16)),
        grid=(n // tm,),
        in_specs=[pl.BlockSpec((tm, d), row),
                  pl.BlockSpec((tm, TOP_K), row),
                  pl.BlockSpec((2, TOP_K, tm, QUARTER), lambda i: (0, 0, i, 0)),
                  pl.BlockSpec((d, 2 * EXPERT_FF), const),
                  pl.BlockSpec((EXPERT_FF, d), const),
                  pl.BlockSpec((1, d), const), pl.BlockSpec((1, d), const)],
        out_specs=(pl.BlockSpec((tm, d), row), pl.BlockSpec((tm, d), row)),
        compiler_params=_params("parallel"),
        name="combine_shared_ln",
    )(h, gate, y_slots, sh_gu, sh_down, g.reshape(1, d), b.reshape(1, d))


_MAIN_BLOCKS = (0, 1, 3, 4, 7, 8, 9, 10, 11, 12, 13, 14, 2)
_FORGET_BLOCKS = (5, 6)
_Q_A, _Q_R, _I_R, _G_R, _GATE_A, _GATE_R = (i * D_MODEL for i in range(6))
_K_A = 6 * D_MODEL
_V_A = _K_A + KV_WIDTH


def kernel(x, ln_in_g, ln_in_b, lb_logits, w_in, b_in, attn_sinks, rec_norm_g, w_proj_attn, w_proj_rec, w_out,
           ln1_g, ln1_b, router_w, router_bias, expert_w_gu, expert_w_down, shared_w_gu, shared_w_down,
           ln2_g, ln2_b):
    batch, seq, d = x.shape
    n = batch * seq
    depth = w_in.shape[0]
    n_blocks = n * TOP_K // ROW_BLOCK + N_EXPERTS
    rows = n_blocks * ROW_BLOCK

    p = jax.nn.softmax(lb_logits.astype(F32), axis=0)
    cum = jnp.cumsum(p, axis=0)
    lower = cum - cum[0:1]
    log_lb = jnp.log(lower)
    log1m_lb = jnp.log1p(-lower)

    w_in_bf = w_in.astype(BF16)
    h, hb = layer_norm_in(x.reshape(n, d), ln_in_g, ln_in_b)
    for l in range(depth):
        proj = in_proj(hb, w_in_bf, b_in, l, _MAIN_BLOCKS, BF16, "in_proj_main")
        proj_f = in_proj(hb, w_in_bf, b_in, l, _FORGET_BLOCKS, F32, "in_proj_forget")
        attn = swa_attention(proj, attn_sinks[l], batch, seq, _Q_A, _K_A, _V_A)
        rec = hgrn2(proj, proj_f, log_lb[l], log1m_lb[l], rec_norm_g[l], batch, seq, _Q_R, _I_R, _G_R)
        h, hp = merge_outproj_ln(h, attn, rec, proj, _GATE_A, _GATE_R,
                                 w_proj_attn[l].astype(BF16), w_proj_rec[l].astype(BF16), w_out[l].astype(BF16),
                                 ln1_g[l], ln1_b[l])

        rw_t = router_w[l].T
        rw_hi = rw_t.astype(BF16)
        rw_lo = (rw_t - rw_hi.astype(F32)).astype(BF16)
        idx_t, gate_t, rank_t, cnt = router(h, rw_hi, rw_lo, router_bias[l])
        counts = cnt[:, 0]
        padded = (counts + ROW_BLOCK - 1) // ROW_BLOCK * ROW_BLOCK
        pad_end = jnp.cumsum(padded)
        pad_start = pad_end - padded
        pos_t = slot_positions(idx_t, rank_t, pad_start)
        blk_row = jnp.arange(n_blocks, dtype=I32) * ROW_BLOCK
        blk_expert = jnp.minimum(jnp.sum(pad_end[None, :] <= blk_row[:, None], axis=1), N_EXPERTS - 1).astype(I32)
        blk_valid = jnp.clip(counts[blk_expert] - (blk_row - pad_start[blk_expert]), 0, ROW_BLOCK).astype(I32)
        n_used = (pad_end[-1:] // ROW_BLOCK).astype(I32)

        xs = sc_scatter_rows(hp, pos_t, rows)
        ys = expert_ffn(xs, blk_expert, blk_valid, n_used, expert_w_gu, expert_w_down, l)
        y_slots = sc_gather_rows(ys, pos_t.reshape(TOP_K * n)).reshape(2, TOP_K, n, QUARTER)
        h, hb = combine_shared_ln(h, gate_t.T, y_slots, shared_w_gu[l].astype(BF16),
                                  shared_w_down[l].astype(BF16), ln2_g[l], ln2_b[l])
    return h.reshape(batch, seq, d)
```

```python
import jax
import jax.numpy as jnp
from jax import lax
from jax.experimental import pallas as pl
from jax.experimental.pallas import tpu as pltpu
from jax.experimental.pallas import tpu_sc as plsc

F32 = jnp.float32
BF16 = jnp.bfloat16
U32 = jnp.uint32
I32 = jnp.int32

D_MODEL = 1024
QUARTER = D_MODEL // 4
CHUNK = 64
ATTN_HEADS = 16
ATTN_KV_HEADS = 4
ATTN_HEAD_DIM = 64
ATTN_GROUP = ATTN_HEADS // ATTN_KV_HEADS
WIN_CHUNKS = 2
KV_WIDTH = ATTN_KV_HEADS * ATTN_HEAD_DIM
REC_HEADS = 8
REC_DIM = 128
N_EXPERTS = 256
TOP_K = 8
N_GROUPS = 8
GROUP_SIZE = N_EXPERTS // N_GROUPS
TOPK_GROUPS = 4
EXPERT_FF = 256
ROUTED_SCALE = 2.5
DEPTH = 2
DEEPNORM_ALPHA = (2 * DEPTH) ** 0.25
LN_EPS = 1e-5
RMS_EPS = 1e-5
NEG_INF = float("-inf")

VMEM_LIMIT_BYTES = 48 * 1024 * 1024

LN_ROWS = 512
PROJ_ROWS = 2048
PROJ_COLS = 512
ATTN_Q_BLOCK = 256
ATTN_STAGE_LAG = 3
REC_TILE = 256
REC_SUB = 16
MERGE_ROWS = 256
ROUTER_ROWS = 256
MOE_ROWS = 256
ROW_BLOCK = 512

DECAY_LIMIT = 60.0


def _params(*sem):
    return pltpu.CompilerParams(dimension_semantics=sem, vmem_limit_bytes=VMEM_LIMIT_BYTES)


def _layer_norm_rows(x, g, b):
    mu = jnp.mean(x, axis=-1, keepdims=True)
    xc = x - mu
    var = jnp.mean(xc * xc, axis=-1, keepdims=True)
    return xc * lax.rsqrt(var + LN_EPS) * g + b


def _sigmoid(x):
    return 1.0 / (1.0 + jnp.exp(-x))


def _silu(x):
    return x * _sigmoid(x)


def _pack_pair(lo, hi):
    lo_bits = pltpu.bitcast(lo.astype(BF16).astype(F32), U32)
    hi_bits = pltpu.bitcast(hi.astype(BF16).astype(F32), U32)
    return lax.shift_right_logical(lo_bits, U32(16)) | (hi_bits & U32(0xFFFF0000))


def _unpack_pair(w):
    lo = pltpu.bitcast(lax.shift_left(w, U32(16)), F32)
    hi = pltpu.bitcast(w & U32(0xFFFF0000), F32)
    return lo, hi


def _store_planes(ref, x):
    q = QUARTER
    ref[0] = _pack_pair(x[:, 0 * q:1 * q], x[:, 1 * q:2 * q])
    ref[1] = _pack_pair(x[:, 2 * q:3 * q], x[:, 3 * q:4 * q])


def _load_quarters(plane0, plane1):
    return _unpack_pair(plane0) + _unpack_pair(plane1)


def _ln_in_kernel(x_ref, g_ref, b_ref, h_ref, hb_ref):
    h = _layer_norm_rows(x_ref[...], g_ref[...], b_ref[...])
    h_ref[...] = h
    hb_ref[...] = h.astype(BF16)


def layer_norm_in(x, g, b):
    n, d = x.shape
    row = lambda i: (i, 0)
    const = lambda i: (0, 0)
    return pl.pallas_call(
        _ln_in_kernel,
        out_shape=(jax.ShapeDtypeStruct((n, d), F32), jax.ShapeDtypeStruct((n, d), BF16)),
        grid=(n // LN_ROWS,),
        in_specs=[pl.BlockSpec((LN_ROWS, d), row), pl.BlockSpec((1, d), const), pl.BlockSpec((1, d), const)],
        out_specs=(pl.BlockSpec((LN_ROWS, d), row), pl.BlockSpec((LN_ROWS, d), row)),
        compiler_params=_params("parallel"),
        name="ln_in",
    )(x, g.reshape(1, d), b.reshape(1, d))


def _in_proj_kernel(perm_ref, x_ref, w_ref, b_ref, o_ref):
    del perm_ref
    acc = jnp.dot(x_ref[...], w_ref[0], preferred_element_type=F32)
    o_ref[...] = (acc + b_ref[0]).astype(o_ref.dtype)


def in_proj(xb, w, b, layer, col_blocks, out_dtype, name):
    n, k = xb.shape
    perm = jnp.asarray(col_blocks, I32)
    nblk = len(col_blocks)
    tm = min(PROJ_ROWS, n)
    return pl.pallas_call(
        _in_proj_kernel,
        out_shape=jax.ShapeDtypeStruct((n, nblk * PROJ_COLS), out_dtype),
        grid_spec=pltpu.PrefetchScalarGridSpec(
            num_scalar_prefetch=1,
            grid=(n // tm, nblk),
            in_specs=[pl.BlockSpec((tm, k), lambda i, j, p: (i, 0)),
                      pl.BlockSpec((1, k, PROJ_COLS), lambda i, j, p: (layer, 0, p[j])),
                      pl.BlockSpec((1, 1, PROJ_COLS), lambda i, j, p: (layer, 0, p[j]))],
            out_specs=pl.BlockSpec((tm, PROJ_COLS), lambda i, j, p: (i, j))),
        compiler_params=_params("parallel", "arbitrary"),
        name=name,
    )(perm, xb, w, b.reshape(b.shape[0], 1, -1))


def _attn_kernel(sink_ref, q_ref, kp_ref, kc_ref, vp_ref, vc_ref, o_ref, s_ref, e_ref):
    i = pl.program_id(1)
    half = ATTN_Q_BLOCK // 2
    qc = lax.broadcasted_iota(I32, (half, ATTN_Q_BLOCK), 0) // CHUNK
    kc = lax.broadcasted_iota(I32, (half, ATTN_Q_BLOCK), 1) // CHUNK
    valid = (kc >= qc) & (kc <= qc + WIN_CHUNKS)
    first = jnp.where(i == 0, WIN_CHUNKS, 0)
    scale = ATTN_HEAD_DIM ** -0.5
    windows = ((jnp.concatenate([kp_ref[half:, :], kc_ref[:half, :]], axis=0),
                jnp.concatenate([vp_ref[half:, :], vc_ref[:half, :]], axis=0),
                jnp.where(valid & (kc >= first), 0.0, NEG_INF)),
               (kc_ref[...], vc_ref[...], jnp.where(valid, 0.0, NEG_INF)))
    for part, (k, _, mask_bias) in enumerate(windows):
        rows = slice(part * half, (part + 1) * half)
        for h in range(ATTN_HEADS):
            kv = h // ATTN_GROUP
            qh = q_ref[rows, h * ATTN_HEAD_DIM:(h + 1) * ATTN_HEAD_DIM] * scale
            kh = k[:, kv * ATTN_HEAD_DIM:(kv + 1) * ATTN_HEAD_DIM]
            s_ref[part * ATTN_HEADS + h] = (
                lax.dot_general(qh, kh, (((1,), (1,)), ((), ())), preferred_element_type=F32) + mask_bias)
    units = [(part, h) for part in range(2) for h in range(ATTN_HEADS)]

    def row_max(part, h):
        return jnp.maximum(jnp.max(s_ref[part * ATTN_HEADS + h], axis=-1, keepdims=True), sink_ref[h])

    def exponentials(part, h, m):
        e = jnp.exp(s_ref[part * ATTN_HEADS + h] - m)
        e_ref[part * ATTN_HEADS + h] = e.astype(BF16)
        return 1.0 / (jnp.sum(e, axis=-1, keepdims=True) + jnp.exp(sink_ref[h] - m))

    def values(part, h, rden):
        kv = h // ATTN_GROUP
        vh = windows[part][1][:, kv * ATTN_HEAD_DIM:(kv + 1) * ATTN_HEAD_DIM]
        oh = jnp.dot(e_ref[part * ATTN_HEADS + h], vh, preferred_element_type=F32) * rden
        rows = slice(part * half, (part + 1) * half)
        o_ref[rows, h * ATTN_HEAD_DIM:(h + 1) * ATTN_HEAD_DIM] = oh.astype(o_ref.dtype)

    maxes, rdens = {}, {}
    lag = ATTN_STAGE_LAG
    for t in range(len(units) + 2 * lag):
        if t < len(units):
            maxes[t] = row_max(*units[t])
        if 0 <= t - lag < len(units):
            rdens[t - lag] = exponentials(*units[t - lag], maxes.pop(t - lag))
        if 0 <= t - 2 * lag < len(units):
            values(*units[t - 2 * lag], rdens.pop(t - 2 * lag))


def swa_attention(proj, sinks, batch, seq, q_col, k_col, v_col):
    n = batch * seq
    nb = seq // ATTN_Q_BLOCK
    qb, kb, vb = q_col // D_MODEL, k_col // KV_WIDTH, v_col // KV_WIDTH

    def cur(col):
        return lambda b, i, s: (b * nb + i, col)

    def prev(col):
        return lambda b, i, s: (b * nb + jnp.maximum(i - 1, 0), col)

    return pl.pallas_call(
        _attn_kernel,
        out_shape=jax.ShapeDtypeStruct((n, D_MODEL), BF16),
        grid_spec=pltpu.PrefetchScalarGridSpec(
            num_scalar_prefetch=1,
            grid=(batch, nb),
            in_specs=[pl.BlockSpec((ATTN_Q_BLOCK, D_MODEL), cur(qb)),
                      pl.BlockSpec((ATTN_Q_BLOCK, KV_WIDTH), prev(kb)),
                      pl.BlockSpec((ATTN_Q_BLOCK, KV_WIDTH), cur(kb)),
                      pl.BlockSpec((ATTN_Q_BLOCK, KV_WIDTH), prev(vb)),
                      pl.BlockSpec((ATTN_Q_BLOCK, KV_WIDTH), cur(vb))],
            out_specs=pl.BlockSpec((ATTN_Q_BLOCK, D_MODEL), lambda b, i, s: (b * nb + i, 0)),
            scratch_shapes=[pltpu.VMEM((2 * ATTN_HEADS, ATTN_Q_BLOCK // 2, ATTN_Q_BLOCK), F32),
                            pltpu.VMEM((2 * ATTN_HEADS, ATTN_Q_BLOCK // 2, ATTN_Q_BLOCK), BF16)]),
        compiler_params=_params("parallel", "parallel"),
        name="swa_attention",
    )(sinks.astype(F32), proj, proj, proj, proj, proj)


def _hgrn_kernel(q_ref, i_ref, g_ref, f_ref, loglb_ref, log1mlb_ref, ng_ref, o_ref,
                 b_ref, k_ref, qs_ref, ep_ref, intra_ref, inter_ref, qd_ref, kt_ref, *st_ref):
    @pl.when(pl.program_id(1) == 0)
    def _():
        for s_ref in st_ref:
            s_ref[...] = jnp.zeros_like(s_ref)

    fl = f_ref[...]
    log_sig = jnp.minimum(fl, 0.0) - jnp.log(1.0 + jnp.exp(-jnp.abs(fl)))
    a = loglb_ref[...]
    c = log1mlb_ref[...] + log_sig
    log_f = jnp.maximum(a, c) + jnp.log(1.0 + jnp.exp(-jnp.abs(a - c)))
    k_ref[...] = 1.0 - jnp.exp(log_f)
    rows = lax.broadcasted_iota(I32, log_f.shape, 0) % REC_SUB
    b = log_f
    shift = 1
    while shift < REC_SUB:
        b = b + jnp.where(rows >= shift, pltpu.roll(b, shift, axis=0), 0.0)
        shift *= 2
    b_ref[...] = b
    qs_ref[...] = _silu(q_ref[...].astype(F32))

    t_iota = lax.broadcasted_iota(I32, (REC_SUB, REC_DIM), 0)
    nt = (((1,), (1,)), ((), ()))
    tn = (((0,), (0,)), ((), ()))

    def finish(o, r0, cols):
        ms = jnp.mean(o * o, axis=-1, keepdims=True)
        o = o * lax.rsqrt(ms + RMS_EPS) * ng_ref[...]
        gj = g_ref[pl.ds(r0, REC_SUB), cols].astype(F32)
        o_ref[pl.ds(r0, REC_SUB), cols] = (o * _silu(gj)).astype(o_ref.dtype)

    decay_bounded = jnp.min(b) >= -DECAY_LIMIT

    @pl.when(decay_bounded)
    def _():
        ep = jnp.exp(b_ref[...])
        ep_ref[...] = ep
        qd_ref[...] = (qs_ref[...] * ep).astype(BF16)
        kt_ref[...] = (k_ref[...] * jnp.exp(-b_ref[...])).astype(BF16)
        ri = lax.broadcasted_iota(I32, (REC_TILE, REC_TILE), 0)
        ci = lax.broadcasted_iota(I32, (REC_TILE, REC_TILE), 1)
        keep = (ri >= ci) & (ri // REC_SUB == ci // REC_SUB)
        for h in range(REC_HEADS):
            cols = slice(h * REC_DIM, (h + 1) * REC_DIM)
            att = lax.dot_general(qd_ref[:, cols], kt_ref[:, cols], nt, preferred_element_type=F32)
            att = jnp.where(keep, att, 0.0).astype(BF16)
            intra_ref[:, cols] = jnp.dot(att, i_ref[:, cols], preferred_element_type=F32)

        def step(j, carry):
            r0 = pl.multiple_of(j * REC_SUB, REC_SUB)
            for h in range(REC_HEADS):
                cols = slice(h * REC_DIM, (h + 1) * REC_DIM)
                qd = qd_ref[pl.ds(r0, REC_SUB), cols]
                kt = kt_ref[pl.ds(r0, REC_SUB), cols]
                vj = i_ref[pl.ds(r0, REC_SUB), cols]
                st = st_ref[h][...]
                inter_ref[pl.ds(r0, REC_SUB), cols] = lax.dot_general(
                    qd, st.astype(BF16), nt, preferred_element_type=F32)
                e_end = ep_ref[pl.ds(r0, REC_SUB), cols][REC_SUB - 1:REC_SUB, :]
                kv_t = lax.dot_general(vj, kt, tn, preferred_element_type=F32)
                st_ref[h][...] = (st + kv_t) * e_end
            return carry

        lax.fori_loop(0, REC_TILE // REC_SUB, step, 0)

        for h in range(REC_HEADS):
            cols = slice(h * REC_DIM, (h + 1) * REC_DIM)
            o = intra_ref[:, cols] + inter_ref[:, cols]
            ms = jnp.mean(o * o, axis=-1, keepdims=True)
            o = o * lax.rsqrt(ms + RMS_EPS) * ng_ref[...]
            o_ref[:, cols] = (o * _silu(g_ref[:, cols].astype(F32))).astype(o_ref.dtype)

    @pl.when(jnp.logical_not(decay_bounded))
    def _():
        ones = jnp.ones((REC_DIM, REC_DIM), BF16)

        def step(j, carry):
            r0 = pl.multiple_of(j * REC_SUB, REC_SUB)
            for h in range(REC_HEADS):
                cols = slice(h * REC_DIM, (h + 1) * REC_DIM)
                bj = b_ref[pl.ds(r0, REC_SUB), cols]
                kj = k_ref[pl.ds(r0, REC_SUB), cols]
                qj = qs_ref[pl.ds(r0, REC_SUB), cols]
                vj = i_ref[pl.ds(r0, REC_SUB), cols].astype(F32)
                st = st_ref[h][...]
                qd = (qj * jnp.exp(bj)).astype(BF16)
                o = lax.dot_general(qd, st.astype(BF16), nt, preferred_element_type=F32)
                parts = []
                for s in range(REC_SUB):
                    dec = jnp.exp(jnp.where(t_iota >= s, bj - bj[s:s + 1, :], NEG_INF))
                    parts.append((qj * dec * kj[s:s + 1, :]).astype(BF16))
                pstack = jnp.concatenate(parts, axis=0)
                rsum = jnp.dot(pstack, ones, preferred_element_type=F32)
                for s in range(REC_SUB):
                    o = o + rsum[s * REC_SUB:(s + 1) * REC_SUB, :] * vj[s:s + 1, :]
                b_end = bj[REC_SUB - 1:REC_SUB, :]
                kd = (kj * jnp.exp(b_end - bj)).astype(BF16)
                kv_t = lax.dot_general(vj.astype(BF16), kd, tn, preferred_element_type=F32)
                st_ref[h][...] = st * jnp.exp(b_end) + kv_t
                finish(o, r0, cols)
            return carry

        lax.fori_loop(0, REC_TILE // REC_SUB, step, 0)


def hgrn2(proj, proj_f, log_lb, log1m_lb, norm_g, batch, seq, q_col, i_col, g_col):
    n = batch * seq
    nb = seq // REC_TILE
    d = D_MODEL

    def blk(col):
        return pl.BlockSpec((REC_TILE, d), lambda b, i: (b * nb + i, col // d))

    return pl.pallas_call(
        _hgrn_kernel,
        out_shape=jax.ShapeDtypeStruct((n, d), BF16),
        grid=(batch, nb),
        in_specs=[blk(q_col), blk(i_col), blk(g_col),
                  pl.BlockSpec((REC_TILE, d), lambda b, i: (b * nb + i, 0)),
                  pl.BlockSpec((1, d), lambda b, i: (0, 0)),
                  pl.BlockSpec((1, d), lambda b, i: (0, 0)),
                  pl.BlockSpec((1, REC_DIM), lambda b, i: (0, 0))],
        out_specs=pl.BlockSpec((REC_TILE, d), lambda b, i: (b * nb + i, 0)),
        scratch_shapes=[pltpu.VMEM((REC_TILE, d), F32) for _ in range(6)]
                       + [pltpu.VMEM((REC_TILE, d), BF16),
                        pltpu.VMEM((REC_TILE, d), BF16)]
                       + [pltpu.VMEM((REC_DIM, REC_DIM), F32) for _ in range(REC_HEADS)],
        compiler_params=_params("parallel", "arbitrary"),
        name="hgrn2",
    )(proj, proj, proj, proj_f, log_lb.reshape(1, d), log1m_lb.reshape(1, d), norm_g.reshape(1, REC_DIM))


def _merge_kernel(h_ref, attn_ref, rec_ref, ga_ref, gr_ref, wpa_ref, wpr_ref, wo_ref, g_ref, b_ref,
                  o_ref, op_ref):
    a = jnp.dot(attn_ref[...], wpa_ref[...], preferred_element_type=F32)
    r = jnp.dot(rec_ref[...], wpr_ref[...], preferred_element_type=F32)
    merged = _sigmoid(ga_ref[...].astype(F32)) * a + _sigmoid(gr_ref[...].astype(F32)) * r
    y = jnp.dot(merged.astype(BF16), wo_ref[...], preferred_element_type=F32)
    h1 = _layer_norm_rows(DEEPNORM_ALPHA * h_ref[...] + y, g_ref[...], b_ref[...])
    o_ref[...] = h1
    _store_planes(op_ref, h1)


def merge_outproj_ln(h, attn, rec, proj, ga_col, gr_col, wpa, wpr, wo, g, b):
    n, d = h.shape
    tm = MERGE_ROWS
    row = lambda i: (i, 0)
    const = lambda i: (0, 0)
    return pl.pallas_call(
        _merge_kernel,
        out_shape=(jax.ShapeDtypeStruct((n, d), F32), jax.ShapeDtypeStruct((2, n, QUARTER), U32)),
        grid=(n // tm,),
        in_specs=[pl.BlockSpec((tm, d), row), pl.BlockSpec((tm, d), row), pl.BlockSpec((tm, d), row),
                  pl.BlockSpec((tm, d), lambda i: (i, ga_col // d)),
                  pl.BlockSpec((tm, d), lambda i: (i, gr_col // d)),
                  pl.BlockSpec((d, d), const), pl.BlockSpec((d, d), const), pl.BlockSpec((d, d), const),
                  pl.BlockSpec((1, d), const), pl.BlockSpec((1, d), const)],
        out_specs=(pl.BlockSpec((tm, d), row), pl.BlockSpec((2, tm, QUARTER), lambda i: (0, i, 0))),
        compiler_params=_params("parallel"),
        name="merge_outproj_ln",
    )(h, attn, rec, proj, proj, wpa, wpr, wo, g.reshape(1, d), b.reshape(1, d))


def _router_kernel(h_ref, whi_ref, wlo_ref, bias_ref, idx_ref, gate_ref, rank_ref, cnt_ref, carry_ref):
    @pl.when(pl.program_id(0) == 0)
    def _():
        carry_ref[...] = jnp.zeros_like(carry_ref)

    tm = h_ref.shape[0]
    h = h_ref[...]
    h_hi = h.astype(BF16)
    h_lo = (h - h_hi.astype(F32)).astype(BF16)
    nt = (((1,), (1,)), ((), ()))
    logits = (lax.dot_general(whi_ref[...], h_hi, nt, preferred_element_type=F32)
              + lax.dot_general(whi_ref[...], h_lo, nt, preferred_element_type=F32)
              + lax.dot_general(wlo_ref[...], h_hi, nt, preferred_element_type=F32))
    scores = _sigmoid(logits)
    sel = scores + bias_ref[...]
    e_iota = lax.broadcasted_iota(I32, (N_EXPERTS, tm), 0)

    g_iota = lax.broadcasted_iota(I32, (N_GROUPS, tm), 0)
    l_iota = lax.broadcasted_iota(I32, (GROUP_SIZE, tm), 0)
    grp = jnp.zeros((N_GROUPS, tm), F32)
    for g in range(N_GROUPS):
        sg = sel[g * GROUP_SIZE:(g + 1) * GROUP_SIZE, :]
        m1 = jnp.max(sg, axis=0, keepdims=True)
        i1 = jnp.min(jnp.where(sg == m1, l_iota, GROUP_SIZE), axis=0, keepdims=True)
        m2 = jnp.max(jnp.where(l_iota == i1, NEG_INF, sg), axis=0, keepdims=True)
        grp = jnp.where(g_iota == g, m1 + m2, grp)
    gsel = jnp.zeros((N_GROUPS, tm), I32)
    for _ in range(TOPK_GROUPS):
        m = jnp.max(grp, axis=0, keepdims=True)
        gi = jnp.min(jnp.where(grp == m, g_iota, N_GROUPS), axis=0, keepdims=True)
        hit = g_iota == gi
        gsel = jnp.where(hit, 1, gsel)
        grp = jnp.where(hit, NEG_INF, grp)
    masked = []
    for g in range(N_GROUPS):
        sg = sel[g * GROUP_SIZE:(g + 1) * GROUP_SIZE, :]
        masked.append(jnp.where(gsel[g:g + 1, :] > 0, sg, NEG_INF))
    selm = jnp.concatenate(masked, axis=0)

    k_iota = lax.broadcasted_iota(I32, (TOP_K, tm), 0)
    idx = jnp.zeros((TOP_K, tm), I32)
    gate = jnp.zeros((TOP_K, tm), F32)
    member = jnp.zeros((N_EXPERTS, tm), F32)
    for k in range(TOP_K):
        m = jnp.max(selm, axis=0, keepdims=True)
        ei = jnp.min(jnp.where(selm == m, e_iota, N_EXPERTS), axis=0, keepdims=True)
        hit = e_iota == ei
        gk = jnp.sum(jnp.where(hit, scores, 0.0), axis=0, keepdims=True)
        idx = jnp.where(k_iota == k, ei, idx)
        gate = jnp.where(k_iota == k, gk, gate)
        member = jnp.where(hit, 1.0, member)
        selm = jnp.where(hit, NEG_INF, selm)
    gate = gate / jnp.sum(gate, axis=0, keepdims=True) * ROUTED_SCALE

    upper = lax.broadcasted_iota(I32, (tm, tm), 0) < lax.broadcasted_iota(I32, (tm, tm), 1)
    before = jnp.dot(member.astype(BF16), upper.astype(BF16), preferred_element_type=F32) + carry_ref[...]
    rank = jnp.zeros((TOP_K, tm), F32)
    for k in range(TOP_K):
        rk = jnp.sum(jnp.where(e_iota == idx[k:k + 1, :], before, 0.0), axis=0, keepdims=True)
        rank = jnp.where(k_iota == k, rk, rank)
    carry_ref[...] = carry_ref[...] + jnp.sum(member, axis=1, keepdims=True)

    idx_ref[...] = idx
    gate_ref[...] = gate
    rank_ref[...] = rank.astype(I32)
    cnt_ref[...] = jnp.broadcast_to(carry_ref[...], cnt_ref.shape).astype(I32)


def router(h, w_t_hi, w_t_lo, bias):
    n, d = h.shape
    tm = ROUTER_ROWS
    tok = lambda i: (0, i)
    const = lambda i: (0, 0)
    return pl.pallas_call(
        _router_kernel,
        out_shape=(jax.ShapeDtypeStruct((TOP_K, n), I32),
                   jax.ShapeDtypeStruct((TOP_K, n), F32),
                   jax.ShapeDtypeStruct((TOP_K, n), I32),
                   jax.ShapeDtypeStruct((N_EXPERTS, 128), I32)),
        grid=(n // tm,),
        in_specs=[pl.BlockSpec((tm, d), lambda i: (i, 0)),
                  pl.BlockSpec((N_EXPERTS, d), const),
                  pl.BlockSpec((N_EXPERTS, d), const),
                  pl.BlockSpec((N_EXPERTS, 1), const)],
        out_specs=(pl.BlockSpec((TOP_K, tm), tok), pl.BlockSpec((TOP_K, tm), tok),
                   pl.BlockSpec((TOP_K, tm), tok), pl.BlockSpec((N_EXPERTS, 128), const)),
        scratch_shapes=[pltpu.VMEM((N_EXPERTS, 1), F32)],
        compiler_params=_params("arbitrary"),
        name="router",
    )(h, w_t_hi, w_t_lo, bias.reshape(N_EXPERTS, 1))


def _slot_pos_kernel(idx_ref, rank_ref, start_ref, pos_ref):
    tm = idx_ref.shape[1]
    e_iota = lax.broadcasted_iota(I32, (N_EXPERTS, tm), 0)
    k_iota = lax.broadcasted_iota(I32, (TOP_K, tm), 0)
    idx = idx_ref[...]
    start = start_ref[...]
    base = jnp.zeros((TOP_K, tm), F32)
    for k in range(TOP_K):
        bk = jnp.sum(jnp.where(e_iota == idx[k:k + 1, :], start, 0.0), axis=0, keepdims=True)
        base = jnp.where(k_iota == k, bk, base)
    pos_ref[...] = base.astype(I32) + rank_ref[...]


def slot_positions(idx_t, rank_t, pad_start):
    n = idx_t.shape[1]
    tm = 1024
    tok = lambda i: (0, i)
    return pl.pallas_call(
        _slot_pos_kernel,
        out_shape=jax.ShapeDtypeStruct((TOP_K, n), I32),
        grid=(n // tm,),
        in_specs=[pl.BlockSpec((TOP_K, tm), tok), pl.BlockSpec((TOP_K, tm), tok),
                  pl.BlockSpec((N_EXPERTS, 1), lambda i: (0, 0))],
        out_specs=pl.BlockSpec((TOP_K, tm), tok),
        compiler_params=_params("parallel"),
        name="slot_positions",
    )(idx_t, rank_t, pad_start.astype(F32).reshape(N_EXPERTS, 1))


SC_WINDOW = 128
SC_WORDS = QUARTER


def _sc_mesh():
    return plsc.VectorSubcoreMesh(core_axis_name="core", subcore_axis_name="subcore")


def sc_scatter_rows(src, pos_t, rows):
    n = src.shape[1]
    src2 = src.reshape(2 * n, SC_WORDS)
    idx2 = jnp.concatenate([pos_t, pos_t + rows], axis=1)

    @pl.kernel(out_type=jax.ShapeDtypeStruct((2 * rows, SC_WORDS), src.dtype), mesh=_sc_mesh(), scratch_types=[])
    def scatter_kernel(x_hbm, i_hbm, o_hbm):
        def body(x_vmem, i_vmem):
            pltpu.sync_copy(x_vmem, o_hbm.at[i_vmem.at[0]])

        pltpu.emit_pipeline(
            body,
            grid=(2 * n // SC_WINDOW, TOP_K),
            in_specs=[pl.BlockSpec((SC_WINDOW, SC_WORDS), index_map=lambda i, k: (i, 0)),
                      pl.BlockSpec((1, SC_WINDOW), index_map=lambda i, k: (k, i))],
            out_specs=[],
            core_axis_name=("core", "subcore"),
            dimension_semantics=(pltpu.PARALLEL, pltpu.ARBITRARY),
        )(x_hbm, i_hbm)

    return scatter_kernel(src2, idx2).reshape(2, rows, SC_WORDS)


def sc_gather_rows(src, pos):
    r = src.shape[1]
    m = pos.shape[0]
    src2 = src.reshape(2 * r, SC_WORDS)
    idx2 = jnp.concatenate([pos, pos + r]).reshape(1, 2 * m)

    @pl.kernel(out_type=jax.ShapeDtypeStruct((2 * m, SC_WORDS), src.dtype), mesh=_sc_mesh(), scratch_types=[])
    def gather_kernel(x_hbm, i_hbm, o_hbm):
        def body(i_vmem, o_vmem):
            pltpu.sync_copy(x_hbm.at[i_vmem.at[0]], o_vmem)

        pltpu.emit_pipeline(
            body,
            grid=(2 * m // SC_WINDOW,),
            in_specs=[pl.BlockSpec((1, SC_WINDOW), index_map=lambda i: (0, i))],
            out_specs=[pl.BlockSpec((SC_WINDOW, SC_WORDS), index_map=lambda i: (i, 0))],
            core_axis_name=("core", "subcore"),
            dimension_semantics=(pltpu.PARALLEL,),
        )(i_hbm, o_hbm)

    return gather_kernel(src2, idx2).reshape(2, m, SC_WORDS)


def _expert_kernel(blk_expert_ref, blk_valid_ref, n_used_ref, x_ref, wgu_ref, wd_ref, y_ref):
    del blk_expert_ref
    j = pl.program_id(0)
    used = j < n_used_ref[0]

    @pl.when(used)
    def _():
        valid = lax.broadcasted_iota(I32, x_ref.shape[1:], 0) < blk_valid_ref[j]
        quarters = _load_quarters(jnp.where(valid, x_ref[0], U32(0)), jnp.where(valid, x_ref[1], U32(0)))
        gu = sum(jnp.dot(xq.astype(BF16), wgu_ref[0, 0, c * QUARTER:(c + 1) * QUARTER, :].astype(BF16),
                         preferred_element_type=F32)
                 for c, xq in enumerate(quarters))
        act = (_silu(gu[:, :EXPERT_FF]) * gu[:, EXPERT_FF:]).astype(BF16)
        _store_planes(y_ref, jnp.dot(act, wd_ref[0, 0].astype(BF16), preferred_element_type=F32))

    @pl.when(jnp.logical_not(used))
    def _():
        y_ref[...] = jnp.zeros_like(y_ref)


def expert_ffn(xs, blk_expert, blk_valid, n_used, w_gu, w_down, layer):
    _, rows, w = xs.shape
    d = D_MODEL
    n_blocks = rows // ROW_BLOCK

    def row_map(j, be, bv, nu):
        return (0, jnp.minimum(j, nu[0] - 1), 0)

    def w_map(j, be, bv, nu):
        return (layer, be[jnp.minimum(j, nu[0] - 1)], 0, 0)

    return pl.pallas_call(
        _expert_kernel,
        out_shape=jax.ShapeDtypeStruct(xs.shape, U32),
        grid_spec=pltpu.PrefetchScalarGridSpec(
            num_scalar_prefetch=3,
            grid=(n_blocks,),
            in_specs=[pl.BlockSpec((2, ROW_BLOCK, w), row_map),
                      pl.BlockSpec((1, 1, d, 2 * EXPERT_FF), w_map),
                      pl.BlockSpec((1, 1, EXPERT_FF, d), w_map)],
            out_specs=pl.BlockSpec((2, ROW_BLOCK, w), lambda j, be, bv, nu: (0, j, 0))),
        compiler_params=_params("arbitrary"),
        name="expert_ffn",
    )(blk_expert, blk_valid, n_used, xs, w_gu, w_down)


def _combine_kernel(h_ref, gate_ref, ys_ref, sgu_ref, sd_ref, g_ref, b_ref, o_ref, ob_ref):
    h = h_ref[...]
    gu = jnp.dot(h.astype(BF16), sgu_ref[...], preferred_element_type=F32)
    act = _silu(gu[:, :EXPERT_FF]) * gu[:, EXPERT_FF:]
    y = jnp.dot(act.astype(BF16), sd_ref[...], preferred_element_type=F32)
    gate = gate_ref[...]
    acc = [y[:, c * QUARTER:(c + 1) * QUARTER] for c in range(4)]
    for k in range(TOP_K):
        gk = gate[:, k:k + 1]
        acc = [a + gk * q for a, q in zip(acc, _load_quarters(ys_ref[0, k], ys_ref[1, k]))]
    out = _layer_norm_rows(DEEPNORM_ALPHA * h + jnp.concatenate(acc, axis=-1), g_ref[...], b_ref[...])
    o_ref[...] = out
    ob_ref[...] = out.astype(BF16)


def combine_shared_ln(h, gate, y_slots, sh_gu, sh_down, g, b):
    n, d = h.shape
    tm = MOE_ROWS
    row = lambda i: (i, 0)
    const = lambda i: (0, 0)
    return pl.pallas_call(
        _combine_kernel,
        out_shape=(jax.ShapeDtypeStruct((n, d), F32), jax.ShapeDtypeStruct((n, d), BF16)),
        grid=(n // tm,),
        in_specs=[pl.BlockSpec((tm, d), row),
                  pl.BlockSpec((tm, TOP_K), row),
                  pl.BlockSpec((2, TOP_K, tm, QUARTER), lambda i: (0, 0, i, 0)),
                  pl.BlockSpec((d, 2 * EXPERT_FF), const),
                  pl.BlockSpec((EXPERT_FF, d), const),
                  pl.BlockSpec((1, d), const), pl.BlockSpec((1, d), const)],
        out_specs=(pl.BlockSpec((tm, d), row), pl.BlockSpec((tm, d), row)),
        compiler_params=_params("parallel"),
        name="combine_shared_ln",
    )(h, gate, y_slots, sh_gu, sh_down, g.reshape(1, d), b.reshape(1, d))


_MAIN_BLOCKS = (0, 1, 3, 4, 7, 8, 9, 10, 11, 12, 13, 14, 2)
_FORGET_BLOCKS = (5, 6)
_Q_A, _Q_R, _I_R, _G_R, _GATE_A, _GATE_R = (i * D_MODEL for i in range(6))
_K_A = 6 * D_MODEL
_V_A = _K_A + KV_WIDTH


def kernel(x, ln_in_g, ln_in_b, lb_logits, w_in, b_in, attn_sinks, rec_norm_g, w_proj_attn, w_proj_rec, w_out,
           ln1_g, ln1_b, router_w, router_bias, expert_w_gu, expert_w_down, shared_w_gu, shared_w_down,
           ln2_g, ln2_b):
    batch, seq, d = x.shape
    n = batch * seq
    depth = w_in.shape[0]
    n_blocks = n * TOP_K // ROW_BLOCK + N_EXPERTS
    rows = n_blocks * ROW_BLOCK

    p = jax.nn.softmax(lb_logits.astype(F32), axis=0)
    cum = jnp.cumsum(p, axis=0)
    lower = cum - cum[0:1]
    log_lb = jnp.log(lower)
    log1m_lb = jnp.log1p(-lower)

    w_in_bf = w_in.astype(BF16)
    h, hb = layer_norm_in(x.reshape(n, d), ln_in_g, ln_in_b)
    for l in range(depth):
        proj = in_proj(hb, w_in_bf, b_in, l, _MAIN_BLOCKS, BF16, "in_proj_main")
        proj_f = in_proj(hb, w_in_bf, b_in, l, _FORGET_BLOCKS, F32, "in_proj_forget")
        attn = swa_attention(proj, attn_sinks[l], batch, seq, _Q_A, _K_A, _V_A)
        rec = hgrn2(proj, proj_f, log_lb[l], log1m_lb[l], rec_norm_g[l], batch, seq, _Q_R, _I_R, _G_R)
        h, hp = merge_outproj_ln(h, attn, rec, proj, _GATE_A, _GATE_R,
                                 w_proj_attn[l].astype(BF16), w_proj_rec[l].astype(BF16), w_out[l].astype(BF16),
                                 ln1_g[l], ln1_b[l])

        rw_t = router_w[l].T
        rw_hi = rw_t.astype(BF16)
        rw_lo = (rw_t - rw_hi.astype(F32)).astype(BF16)
        idx_t, gate_t, rank_t, cnt = router(h, rw_hi, rw_lo, router_bias[l])
        counts = cnt[:, 0]
        padded = (counts + ROW_BLOCK - 1) // ROW_BLOCK * ROW_BLOCK
        pad_end = jnp.cumsum(padded)
        pad_start = pad_end - padded
        pos_t = slot_positions(idx_t, rank_t, pad_start)
        blk_row = jnp.arange(n_blocks, dtype=I32) * ROW_BLOCK
        blk_expert = jnp.minimum(jnp.sum(pad_end[None, :] <= blk_row[:, None], axis=1), N_EXPERTS - 1).astype(I32)
        blk_valid = jnp.clip(counts[blk_expert] - (blk_row - pad_start[blk_expert]), 0, ROW_BLOCK).astype(I32)
        n_used = (pad_end[-1:] // ROW_BLOCK).astype(I32)

        xs = sc_scatter_rows(hp, pos_t, rows)
        ys = expert_ffn(xs, blk_expert, blk_valid, n_used, expert_w_gu, expert_w_down, l)
        y_slots = sc_gather_rows(ys, pos_t.reshape(TOP_K * n)).reshape(2, TOP_K, n, QUARTER)
        h, hb = combine_shared_ln(h, gate_t.T, y_slots, shared_w_gu[l].astype(BF16),
                                  shared_w_down[l].astype(BF16), ln2_g[l], ln2_b[l])
    return h.reshape(batch, seq, d)
```

```python
import jax
import jax.numpy as jnp
from jax import lax
from jax.experimental import pallas as pl
from jax.experimental.pallas import tpu as pltpu
from jax.experimental.pallas import tpu_sc as plsc

F32 = jnp.float32
BF16 = jnp.bfloat16
U32 = jnp.uint32
I32 = jnp.int32

D_MODEL = 1024
QUARTER = D_MODEL // 4
CHUNK = 64
ATTN_HEADS = 16
ATTN_KV_HEADS = 4
ATTN_HEAD_DIM = 64
ATTN_GROUP = ATTN_HEADS // ATTN_KV_HEADS
WIN_CHUNKS = 2
KV_WIDTH = ATTN_KV_HEADS * ATTN_HEAD_DIM
REC_HEADS = 8
REC_DIM = 128
N_EXPERTS = 256
TOP_K = 8
N_GROUPS = 8
GROUP_SIZE = N_EXPERTS // N_GROUPS
TOPK_GROUPS = 4
EXPERT_FF = 256
ROUTED_SCALE = 2.5
DEPTH = 2
DEEPNORM_ALPHA = (2 * DEPTH) ** 0.25
LN_EPS = 1e-5
RMS_EPS = 1e-5
NEG_INF = float("-inf")

VMEM_LIMIT_BYTES = 48 * 1024 * 1024

LN_ROWS = 512
PROJ_ROWS = 2048
PROJ_COLS = 512
ATTN_Q_BLOCK = 256
ATTN_STAGE_LAG = 3
REC_TILE = 256
REC_BLOCK = 64
REC_SUB = 16
MERGE_ROWS = 256
ROUTER_ROWS = 256
MOE_ROWS = 256
ROW_BLOCK = 512

DECAY_LIMIT = 60.0


def _params(*sem):
    return pltpu.CompilerParams(dimension_semantics=sem, vmem_limit_bytes=VMEM_LIMIT_BYTES)


def _layer_norm_rows(x, g, b):
    mu = jnp.mean(x, axis=-1, keepdims=True)
    xc = x - mu
    var = jnp.mean(xc * xc, axis=-1, keepdims=True)
    return xc * lax.rsqrt(var + LN_EPS) * g + b


def _sigmoid(x):
    return 1.0 / (1.0 + jnp.exp(-x))


def _silu(x):
    return x * _sigmoid(x)


def _pack_pair(lo, hi):
    lo_bits = pltpu.bitcast(lo.astype(BF16).astype(F32), U32)
    hi_bits = pltpu.bitcast(hi.astype(BF16).astype(F32), U32)
    return lax.shift_right_logical(lo_bits, U32(16)) | (hi_bits & U32(0xFFFF0000))


def _unpack_pair(w):
    lo = pltpu.bitcast(lax.shift_left(w, U32(16)), F32)
    hi = pltpu.bitcast(w & U32(0xFFFF0000), F32)
    return lo, hi


def _store_planes(ref, x):
    q = QUARTER
    ref[0] = _pack_pair(x[:, 0 * q:1 * q], x[:, 1 * q:2 * q])
    ref[1] = _pack_pair(x[:, 2 * q:3 * q], x[:, 3 * q:4 * q])


def _load_quarters(plane0, plane1):
    return _unpack_pair(plane0) + _unpack_pair(plane1)


def _ln_in_kernel(x_ref, g_ref, b_ref, h_ref, hb_ref):
    h = _layer_norm_rows(x_ref[...], g_ref[...], b_ref[...])
    h_ref[...] = h
    hb_ref[...] = h.astype(BF16)


def layer_norm_in(x, g, b):
    n, d = x.shape
    row = lambda i: (i, 0)
    const = lambda i: (0, 0)
    return pl.pallas_call(
        _ln_in_kernel,
        out_shape=(jax.ShapeDtypeStruct((n, d), F32), jax.ShapeDtypeStruct((n, d), BF16)),
        grid=(n // LN_ROWS,),
        in_specs=[pl.BlockSpec((LN_ROWS, d), row), pl.BlockSpec((1, d), const), pl.BlockSpec((1, d), const)],
        out_specs=(pl.BlockSpec((LN_ROWS, d), row), pl.BlockSpec((LN_ROWS, d), row)),
        compiler_params=_params("parallel"),
        name="ln_in",
    )(x, g.reshape(1, d), b.reshape(1, d))


def _in_proj_kernel(perm_ref, x_ref, w_ref, b_ref, o_ref):
    del perm_ref
    acc = jnp.dot(x_ref[...], w_ref[0], preferred_element_type=F32)
    o_ref[...] = (acc + b_ref[0]).astype(o_ref.dtype)


def in_proj(xb, w, b, layer, col_blocks, out_dtype, name):
    n, k = xb.shape
    perm = jnp.asarray(col_blocks, I32)
    nblk = len(col_blocks)
    tm = min(PROJ_ROWS, n)
    return pl.pallas_call(
        _in_proj_kernel,
        out_shape=jax.ShapeDtypeStruct((n, nblk * PROJ_COLS), out_dtype),
        grid_spec=pltpu.PrefetchScalarGridSpec(
            num_scalar_prefetch=1,
            grid=(n // tm, nblk),
            in_specs=[pl.BlockSpec((tm, k), lambda i, j, p: (i, 0)),
                      pl.BlockSpec((1, k, PROJ_COLS), lambda i, j, p: (layer, 0, p[j])),
                      pl.BlockSpec((1, 1, PROJ_COLS), lambda i, j, p: (layer, 0, p[j]))],
            out_specs=pl.BlockSpec((tm, PROJ_COLS), lambda i, j, p: (i, j))),
        compiler_params=_params("parallel", "arbitrary"),
        name=name,
    )(perm, xb, w, b.reshape(b.shape[0], 1, -1))


def _attn_kernel(sink_ref, q_ref, kp_ref, kc_ref, vp_ref, vc_ref, o_ref, s_ref, e_ref):
    i = pl.program_id(1)
    half = ATTN_Q_BLOCK // 2
    qc = lax.broadcasted_iota(I32, (half, ATTN_Q_BLOCK), 0) // CHUNK
    kc = lax.broadcasted_iota(I32, (half, ATTN_Q_BLOCK), 1) // CHUNK
    valid = (kc >= qc) & (kc <= qc + WIN_CHUNKS)
    first = jnp.where(i == 0, WIN_CHUNKS, 0)
    scale = ATTN_HEAD_DIM ** -0.5
    windows = ((jnp.concatenate([kp_ref[half:, :], kc_ref[:half, :]], axis=0),
                jnp.concatenate([vp_ref[half:, :], vc_ref[:half, :]], axis=0),
                jnp.where(valid & (kc >= first), 0.0, NEG_INF)),
               (kc_ref[...], vc_ref[...], jnp.where(valid, 0.0, NEG_INF)))
    for part, (k, _, mask_bias) in enumerate(windows):
        rows = slice(part * half, (part + 1) * half)
        for h in range(ATTN_HEADS):
            kv = h // ATTN_GROUP
            qh = q_ref[rows, h * ATTN_HEAD_DIM:(h + 1) * ATTN_HEAD_DIM] * scale
            kh = k[:, kv * ATTN_HEAD_DIM:(kv + 1) * ATTN_HEAD_DIM]
            s_ref[part * ATTN_HEADS + h] = (
                lax.dot_general(qh, kh, (((1,), (1,)), ((), ())), preferred_element_type=F32) + mask_bias)
    units = [(part, h) for part in range(2) for h in range(ATTN_HEADS)]

    def row_max(part, h):
        return jnp.maximum(jnp.max(s_ref[part * ATTN_HEADS + h], axis=-1, keepdims=True), sink_ref[h])

    def exponentials(part, h, m):
        e = jnp.exp(s_ref[part * ATTN_HEADS + h] - m)
        e_ref[part * ATTN_HEADS + h] = e.astype(BF16)
        return 1.0 / (jnp.sum(e, axis=-1, keepdims=True) + jnp.exp(sink_ref[h] - m))

    def values(part, h, rden):
        kv = h // ATTN_GROUP
        vh = windows[part][1][:, kv * ATTN_HEAD_DIM:(kv + 1) * ATTN_HEAD_DIM]
        oh = jnp.dot(e_ref[part * ATTN_HEADS + h], vh, preferred_element_type=F32) * rden
        rows = slice(part * half, (part + 1) * half)
        o_ref[rows, h * ATTN_HEAD_DIM:(h + 1) * ATTN_HEAD_DIM] = oh.astype(o_ref.dtype)

    maxes, rdens = {}, {}
    lag = ATTN_STAGE_LAG
    for t in range(len(units) + 2 * lag):
        if t < len(units):
            maxes[t] = row_max(*units[t])
        if 0 <= t - lag < len(units):
            rdens[t - lag] = exponentials(*units[t - lag], maxes.pop(t - lag))
        if 0 <= t - 2 * lag < len(units):
            values(*units[t - 2 * lag], rdens.pop(t - 2 * lag))


def swa_attention(proj, sinks, batch, seq, q_col, k_col, v_col):
    n = batch * seq
    nb = seq // ATTN_Q_BLOCK
    qb, kb, vb = q_col // D_MODEL, k_col // KV_WIDTH, v_col // KV_WIDTH

    def cur(col):
        return lambda b, i, s: (b * nb + i, col)

    def prev(col):
        return lambda b, i, s: (b * nb + jnp.maximum(i - 1, 0), col)

    return pl.pallas_call(
        _attn_kernel,
        out_shape=jax.ShapeDtypeStruct((n, D_MODEL), BF16),
        grid_spec=pltpu.PrefetchScalarGridSpec(
            num_scalar_prefetch=1,
            grid=(batch, nb),
            in_specs=[pl.BlockSpec((ATTN_Q_BLOCK, D_MODEL), cur(qb)),
                      pl.BlockSpec((ATTN_Q_BLOCK, KV_WIDTH), prev(kb)),
                      pl.BlockSpec((ATTN_Q_BLOCK, KV_WIDTH), cur(kb)),
                      pl.BlockSpec((ATTN_Q_BLOCK, KV_WIDTH), prev(vb)),
                      pl.BlockSpec((ATTN_Q_BLOCK, KV_WIDTH), cur(vb))],
            out_specs=pl.BlockSpec((ATTN_Q_BLOCK, D_MODEL), lambda b, i, s: (b * nb + i, 0)),
            scratch_shapes=[pltpu.VMEM((2 * ATTN_HEADS, ATTN_Q_BLOCK // 2, ATTN_Q_BLOCK), F32),
                            pltpu.VMEM((2 * ATTN_HEADS, ATTN_Q_BLOCK // 2, ATTN_Q_BLOCK), BF16)]),
        compiler_params=_params("parallel", "parallel"),
        name="swa_attention",
    )(sinks.astype(F32), proj, proj, proj, proj, proj)


def _hgrn_kernel(q_ref, i_ref, g_ref, f_ref, loglb_ref, log1mlb_ref, ng_ref, o_ref,
                 b_ref, k_ref, qs_ref, lf_ref, intra_ref, inter_ref, qd_ref, kt_ref, *st_ref):
    @pl.when(pl.program_id(1) == 0)
    def _():
        for s_ref in st_ref:
            s_ref[...] = jnp.zeros_like(s_ref)

    fl = f_ref[...]
    log_sig = jnp.minimum(fl, 0.0) - jnp.log(1.0 + jnp.exp(-jnp.abs(fl)))
    a = loglb_ref[...]
    c = log1mlb_ref[...] + log_sig
    log_f = jnp.maximum(a, c) + jnp.log(1.0 + jnp.exp(-jnp.abs(a - c)))
    lf_ref[...] = log_f
    k_ref[...] = 1.0 - jnp.exp(log_f)
    qs_ref[...] = _silu(q_ref[...].astype(F32))

    def prefix_sum(x, width):
        pos = lax.broadcasted_iota(I32, x.shape, 0) % width
        shift = 1
        while shift < width:
            x = x + jnp.where(pos >= shift, pltpu.roll(x, shift, axis=0), 0.0)
            shift *= 2
        return x

    def suffix_sum(x, width):
        pos = lax.broadcasted_iota(I32, x.shape, 0) % width
        shift = 1
        while shift < width:
            x = x + jnp.where(pos < width - shift, pltpu.roll(x, x.shape[0] - shift, axis=0), 0.0)
            shift *= 2
        return x

    half = REC_BLOCK // 2
    pre = prefix_sum(log_f, half)
    b_ref[...] = pre
    second = (lax.broadcasted_iota(I32, log_f.shape, 0) % REC_BLOCK) >= half
    d = jnp.where(second, pre, log_f - suffix_sum(log_f, half))
    intra_ref[...] = d
    decay_bounded = jnp.max(jnp.abs(d)) <= DECAY_LIMIT

    t_iota = lax.broadcasted_iota(I32, (REC_SUB, REC_DIM), 0)
    nt = (((1,), (1,)), ((), ()))
    tn = (((0,), (0,)), ((), ()))

    def finish(o, r0, cols):
        ms = jnp.mean(o * o, axis=-1, keepdims=True)
        o = o * lax.rsqrt(ms + RMS_EPS) * ng_ref[...]
        gj = g_ref[pl.ds(r0, REC_SUB), cols].astype(F32)
        o_ref[pl.ds(r0, REC_SUB), cols] = (o * _silu(gj)).astype(o_ref.dtype)

    @pl.when(decay_bounded)
    def _():
        qd_ref[...] = (qs_ref[...] * jnp.exp(intra_ref[...])).astype(BF16)
        kt_ref[...] = (k_ref[...] * jnp.exp(-intra_ref[...])).astype(BF16)
        ri = lax.broadcasted_iota(I32, (REC_TILE, REC_TILE), 0)
        ci = lax.broadcasted_iota(I32, (REC_TILE, REC_TILE), 1)
        keep = (ri >= ci) & (ri // REC_BLOCK == ci // REC_BLOCK)
        for h in range(REC_HEADS):
            cols = slice(h * REC_DIM, (h + 1) * REC_DIM)
            att = lax.dot_general(qd_ref[:, cols], kt_ref[:, cols], nt, preferred_element_type=F32)
            att = jnp.where(keep, att, 0.0).astype(BF16)
            intra_ref[:, cols] = jnp.dot(att, i_ref[:, cols], preferred_element_type=F32)

        def step(j, carry):
            r0 = pl.multiple_of(j * REC_BLOCK, REC_BLOCK)
            for h in range(REC_HEADS):
                cols = slice(h * REC_DIM, (h + 1) * REC_DIM)
                qd = qd_ref[pl.ds(r0, REC_BLOCK), cols]
                kt = kt_ref[pl.ds(r0, REC_BLOCK), cols]
                vj = i_ref[pl.ds(r0, REC_BLOCK), cols]
                pre_j = b_ref[pl.ds(r0, REC_BLOCK), cols]
                e1 = jnp.exp(pre_j[half - 1:half, :])
                e2 = jnp.exp(pre_j[REC_BLOCK - 1:REC_BLOCK, :])
                st_mid = st_ref[h][...] * e1
                inter_ref[pl.ds(r0, REC_BLOCK), cols] = lax.dot_general(
                    qd, st_mid.astype(BF16), nt, preferred_element_type=F32)
                kv_t = lax.dot_general(vj, kt, tn, preferred_element_type=F32)
                st_ref[h][...] = (st_mid + kv_t) * e2
            return carry

        lax.fori_loop(0, REC_TILE // REC_BLOCK, step, 0)

        for h in range(REC_HEADS):
            cols = slice(h * REC_DIM, (h + 1) * REC_DIM)
            o = intra_ref[:, cols] + inter_ref[:, cols]
            ms = jnp.mean(o * o, axis=-1, keepdims=True)
            o = o * lax.rsqrt(ms + RMS_EPS) * ng_ref[...]
            o_ref[:, cols] = (o * _silu(g_ref[:, cols].astype(F32))).astype(o_ref.dtype)

    @pl.when(jnp.logical_not(decay_bounded))
    def _():
        b_ref[...] = prefix_sum(lf_ref[...], REC_SUB)
        ones = jnp.ones((REC_DIM, REC_DIM), BF16)

        def step(j, carry):
            r0 = pl.multiple_of(j * REC_SUB, REC_SUB)
            for h in range(REC_HEADS):
                cols = slice(h * REC_DIM, (h + 1) * REC_DIM)
                bj = b_ref[pl.ds(r0, REC_SUB), cols]
                kj = k_ref[pl.ds(r0, REC_SUB), cols]
                qj = qs_ref[pl.ds(r0, REC_SUB), cols]
                vj = i_ref[pl.ds(r0, REC_SUB), cols].astype(F32)
                st = st_ref[h][...]
                qd = (qj * jnp.exp(bj)).astype(BF16)
                o = lax.dot_general(qd, st.astype(BF16), nt, preferred_element_type=F32)
                parts = []
                for s in range(REC_SUB):
                    dec = jnp.exp(jnp.where(t_iota >= s, bj - bj[s:s + 1, :], NEG_INF))
                    parts.append((qj * dec * kj[s:s + 1, :]).astype(BF16))
                pstack = jnp.concatenate(parts, axis=0)
                rsum = jnp.dot(pstack, ones, preferred_element_type=F32)
                for s in range(REC_SUB):
                    o = o + rsum[s * REC_SUB:(s + 1) * REC_SUB, :] * vj[s:s + 1, :]
                b_end = bj[REC_SUB - 1:REC_SUB, :]
                kd = (kj * jnp.exp(b_end - bj)).astype(BF16)
                kv_t = lax.dot_general(vj.astype(BF16), kd, tn, preferred_element_type=F32)
                st_ref[h][...] = st * jnp.exp(b_end) + kv_t
                finish(o, r0, cols)
            return carry

        lax.fori_loop(0, REC_TILE // REC_SUB, step, 0)


def hgrn2(proj, proj_f, log_lb, log1m_lb, norm_g, batch, seq, q_col, i_col, g_col):
    n = batch * seq
    nb = seq // REC_TILE
    d = D_MODEL

    def blk(col):
        return pl.BlockSpec((REC_TILE, d), lambda b, i: (b * nb + i, col // d))

    return pl.pallas_call(
        _hgrn_kernel,
        out_shape=jax.ShapeDtypeStruct((n, d), BF16),
        grid=(batch, nb),
        in_specs=[blk(q_col), blk(i_col), blk(g_col),
                  pl.BlockSpec((REC_TILE, d), lambda b, i: (b * nb + i, 0)),
                  pl.BlockSpec((1, d), lambda b, i: (0, 0)),
                  pl.BlockSpec((1, d), lambda b, i: (0, 0)),
                  pl.BlockSpec((1, REC_DIM), lambda b, i: (0, 0))],
        out_specs=pl.BlockSpec((REC_TILE, d), lambda b, i: (b * nb + i, 0)),
        scratch_shapes=[pltpu.VMEM((REC_TILE, d), F32) for _ in range(6)]
                       + [pltpu.VMEM((REC_TILE, d), BF16),
                        pltpu.VMEM((REC_TILE, d), BF16)]
                       + [pltpu.VMEM((REC_DIM, REC_DIM), F32) for _ in range(REC_HEADS)],
        compiler_params=_params("parallel", "arbitrary"),
        name="hgrn2",
    )(proj, proj, proj, proj_f, log_lb.reshape(1, d), log1m_lb.reshape(1, d), norm_g.reshape(1, REC_DIM))


def _merge_kernel(h_ref, attn_ref, rec_ref, ga_ref, gr_ref, wpa_ref, wpr_ref, wo_ref, g_ref, b_ref,
                  o_ref, op_ref):
    a = jnp.dot(attn_ref[...], wpa_ref[...], preferred_element_type=F32)
    r = jnp.dot(rec_ref[...], wpr_ref[...], preferred_element_type=F32)
    merged = _sigmoid(ga_ref[...].astype(F32)) * a + _sigmoid(gr_ref[...].astype(F32)) * r
    y = jnp.dot(merged.astype(BF16), wo_ref[...], preferred_element_type=F32)
    h1 = _layer_norm_rows(DEEPNORM_ALPHA * h_ref[...] + y, g_ref[...], b_ref[...])
    o_ref[...] = h1
    _store_planes(op_ref, h1)


def merge_outproj_ln(h, attn, rec, proj, ga_col, gr_col, wpa, wpr, wo, g, b):
    n, d = h.shape
    tm = MERGE_ROWS
    row = lambda i: (i, 0)
    const = lambda i: (0, 0)
    return pl.pallas_call(
        _merge_kernel,
        out_shape=(jax.ShapeDtypeStruct((n, d), F32), jax.ShapeDtypeStruct((2, n, QUARTER), U32)),
        grid=(n // tm,),
        in_specs=[pl.BlockSpec((tm, d), row), pl.BlockSpec((tm, d), row), pl.BlockSpec((tm, d), row),
                  pl.BlockSpec((tm, d), lambda i: (i, ga_col // d)),
                  pl.BlockSpec((tm, d), lambda i: (i, gr_col // d)),
                  pl.BlockSpec((d, d), const), pl.BlockSpec((d, d), const), pl.BlockSpec((d, d), const),
                  pl.BlockSpec((1, d), const), pl.BlockSpec((1, d), const)],
        out_specs=(pl.BlockSpec((tm, d), row), pl.BlockSpec((2, tm, QUARTER), lambda i: (0, i, 0))),
        compiler_params=_params("parallel"),
        name="merge_outproj_ln",
    )(h, attn, rec, proj, proj, wpa, wpr, wo, g.reshape(1, d), b.reshape(1, d))


def _router_kernel(h_ref, whi_ref, wlo_ref, bias_ref, idx_ref, gate_ref, rank_ref, cnt_ref, carry_ref):
    @pl.when(pl.program_id(0) == 0)
    def _():
        carry_ref[...] = jnp.zeros_like(carry_ref)

    tm = h_ref.shape[0]
    h = h_ref[...]
    h_hi = h.astype(BF16)
    h_lo = (h - h_hi.astype(F32)).astype(BF16)
    nt = (((1,), (1,)), ((), ()))
    logits = (lax.dot_general(whi_ref[...], h_hi, nt, preferred_element_type=F32)
              + lax.dot_general(whi_ref[...], h_lo, nt, preferred_element_type=F32)
              + lax.dot_general(wlo_ref[...], h_hi, nt, preferred_element_type=F32))
    scores = _sigmoid(logits)
    sel = scores + bias_ref[...]
    e_iota = lax.broadcasted_iota(I32, (N_EXPERTS, tm), 0)

    g_iota = lax.broadcasted_iota(I32, (N_GROUPS, tm), 0)
    l_iota = lax.broadcasted_iota(I32, (GROUP_SIZE, tm), 0)
    grp = jnp.zeros((N_GROUPS, tm), F32)
    for g in range(N_GROUPS):
        sg = sel[g * GROUP_SIZE:(g + 1) * GROUP_SIZE, :]
        m1 = jnp.max(sg, axis=0, keepdims=True)
        i1 = jnp.min(jnp.where(sg == m1, l_iota, GROUP_SIZE), axis=0, keepdims=True)
        m2 = jnp.max(jnp.where(l_iota == i1, NEG_INF, sg), axis=0, keepdims=True)
        grp = jnp.where(g_iota == g, m1 + m2, grp)
    gsel = jnp.zeros((N_GROUPS, tm), I32)
    for _ in range(TOPK_GROUPS):
        m = jnp.max(grp, axis=0, keepdims=True)
        gi = jnp.min(jnp.where(grp == m, g_iota, N_GROUPS), axis=0, keepdims=True)
        hit = g_iota == gi
        gsel = jnp.where(hit, 1, gsel)
        grp = jnp.where(hit, NEG_INF, grp)
    masked = []
    for g in range(N_GROUPS):
        sg = sel[g * GROUP_SIZE:(g + 1) * GROUP_SIZE, :]
        masked.append(jnp.where(gsel[g:g + 1, :] > 0, sg, NEG_INF))
    selm = jnp.concatenate(masked, axis=0)

    k_iota = lax.broadcasted_iota(I32, (TOP_K, tm), 0)
    idx = jnp.zeros((TOP_K, tm), I32)
    gate = jnp.zeros((TOP_K, tm), F32)
    member = jnp.zeros((N_EXPERTS, tm), F32)
    for k in range(TOP_K):
        m = jnp.max(selm, axis=0, keepdims=True)
        ei = jnp.min(jnp.where(selm == m, e_iota, N_EXPERTS), axis=0, keepdims=True)
        hit = e_iota == ei
        gk = jnp.sum(jnp.where(hit, scores, 0.0), axis=0, keepdims=True)
        idx = jnp.where(k_iota == k, ei, idx)
        gate = jnp.where(k_iota == k, gk, gate)
        member = jnp.where(hit, 1.0, member)
        selm = jnp.where(hit, NEG_INF, selm)
    gate = gate / jnp.sum(gate, axis=0, keepdims=True) * ROUTED_SCALE

    upper = lax.broadcasted_iota(I32, (tm, tm), 0) < lax.broadcasted_iota(I32, (tm, tm), 1)
    before = jnp.dot(member.astype(BF16), upper.astype(BF16), preferred_element_type=F32) + carry_ref[...]
    rank = jnp.zeros((TOP_K, tm), F32)
    for k in range(TOP_K):
        rk = jnp.sum(jnp.where(e_iota == idx[k:k + 1, :], before, 0.0), axis=0, keepdims=True)
        rank = jnp.where(k_iota == k, rk, rank)
    carry_ref[...] = carry_ref[...] + jnp.sum(member, axis=1, keepdims=True)

    idx_ref[...] = idx
    gate_ref[...] = gate
    rank_ref[...] = rank.astype(I32)
    cnt_ref[...] = jnp.broadcast_to(carry_ref[...], cnt_ref.shape).astype(I32)


def router(h, w_t_hi, w_t_lo, bias):
    n, d = h.shape
    tm = ROUTER_ROWS
    tok = lambda i: (0, i)
    const = lambda i: (0, 0)
    return pl.pallas_call(
        _router_kernel,
        out_shape=(jax.ShapeDtypeStruct((TOP_K, n), I32),
                   jax.ShapeDtypeStruct((TOP_K, n), F32),
                   jax.ShapeDtypeStruct((TOP_K, n), I32),
                   jax.ShapeDtypeStruct((N_EXPERTS, 128), I32)),
        grid=(n // tm,),
        in_specs=[pl.BlockSpec((tm, d), lambda i: (i, 0)),
                  pl.BlockSpec((N_EXPERTS, d), const),
                  pl.BlockSpec((N_EXPERTS, d), const),
                  pl.BlockSpec((N_EXPERTS, 1), const)],
        out_specs=(pl.BlockSpec((TOP_K, tm), tok), pl.BlockSpec((TOP_K, tm), tok),
                   pl.BlockSpec((TOP_K, tm), tok), pl.BlockSpec((N_EXPERTS, 128), const)),
        scratch_shapes=[pltpu.VMEM((N_EXPERTS, 1), F32)],
        compiler_params=_params("arbitrary"),
        name="router",
    )(h, w_t_hi, w_t_lo, bias.reshape(N_EXPERTS, 1))


def _slot_pos_kernel(idx_ref, rank_ref, start_ref, pos_ref):
    tm = idx_ref.shape[1]
    e_iota = lax.broadcasted_iota(I32, (N_EXPERTS, tm), 0)
    k_iota = lax.broadcasted_iota(I32, (TOP_K, tm), 0)
    idx = idx_ref[...]
    start = start_ref[...]
    base = jnp.zeros((TOP_K, tm), F32)
    for k in range(TOP_K):
        bk = jnp.sum(jnp.where(e_iota == idx[k:k + 1, :], start, 0.0), axis=0, keepdims=True)
        base = jnp.where(k_iota == k, bk, base)
    pos_ref[...] = base.astype(I32) + rank_ref[...]


def slot_positions(idx_t, rank_t, pad_start):
    n = idx_t.shape[1]
    tm = 1024
    tok = lambda i: (0, i)
    return pl.pallas_call(
        _slot_pos_kernel,
        out_shape=jax.ShapeDtypeStruct((TOP_K, n), I32),
        grid=(n // tm,),
        in_specs=[pl.BlockSpec((TOP_K, tm), tok), pl.BlockSpec((TOP_K, tm), tok),
                  pl.BlockSpec((N_EXPERTS, 1), lambda i: (0, 0))],
        out_specs=pl.BlockSpec((TOP_K, tm), tok),
        compiler_params=_params("parallel"),
        name="slot_positions",
    )(idx_t, rank_t, pad_start.astype(F32).reshape(N_EXPERTS, 1))


SC_WINDOW = 128
SC_WORDS = QUARTER


def _sc_mesh():
    return plsc.VectorSubcoreMesh(core_axis_name="core", subcore_axis_name="subcore")


def sc_scatter_rows(src, pos_t, rows):
    n = src.shape[1]
    src2 = src.reshape(2 * n, SC_WORDS)
    idx2 = jnp.concatenate([pos_t, pos_t + rows], axis=1)

    @pl.kernel(out_type=jax.ShapeDtypeStruct((2 * rows, SC_WORDS), src.dtype), mesh=_sc_mesh(), scratch_types=[])
    def scatter_kernel(x_hbm, i_hbm, o_hbm):
        def body(x_vmem, i_vmem):
            pltpu.sync_copy(x_vmem, o_hbm.at[i_vmem.at[0]])

        pltpu.emit_pipeline(
            body,
            grid=(2 * n // SC_WINDOW, TOP_K),
            in_specs=[pl.BlockSpec((SC_WINDOW, SC_WORDS), index_map=lambda i, k: (i, 0)),
                      pl.BlockSpec((1, SC_WINDOW), index_map=lambda i, k: (k, i))],
            out_specs=[],
            core_axis_name=("core", "subcore"),
            dimension_semantics=(pltpu.PARALLEL, pltpu.ARBITRARY),
        )(x_hbm, i_hbm)

    return scatter_kernel(src2, idx2).reshape(2, rows, SC_WORDS)


def sc_gather_rows(src, pos):
    r = src.shape[1]
    m = pos.shape[0]
    src2 = src.reshape(2 * r, SC_WORDS)
    idx2 = jnp.concatenate([pos, pos + r]).reshape(1, 2 * m)

    @pl.kernel(out_type=jax.ShapeDtypeStruct((2 * m, SC_WORDS), src.dtype), mesh=_sc_mesh(), scratch_types=[])
    def gather_kernel(x_hbm, i_hbm, o_hbm):
        def body(i_vmem, o_vmem):
            pltpu.sync_copy(x_hbm.at[i_vmem.at[0]], o_vmem)

        pltpu.emit_pipeline(
            body,
            grid=(2 * m // SC_WINDOW,),
            in_specs=[pl.BlockSpec((1, SC_WINDOW), index_map=lambda i: (0, i))],
            out_specs=[pl.BlockSpec((SC_WINDOW, SC_WORDS), index_map=lambda i: (i, 0))],
            core_axis_name=("core", "subcore"),
            dimension_semantics=(pltpu.PARALLEL,),
        )(i_hbm, o_hbm)

    return gather_kernel(src2, idx2).reshape(2, m, SC_WORDS)


def _expert_kernel(blk_expert_ref, blk_valid_ref, n_used_ref, x_ref, wgu_ref, wd_ref, y_ref):
    del blk_expert_ref
    j = pl.program_id(0)
    used = j < n_used_ref[0]

    @pl.when(used)
    def _():
        valid = lax.broadcasted_iota(I32, x_ref.shape[1:], 0) < blk_valid_ref[j]
        quarters = _load_quarters(jnp.where(valid, x_ref[0], U32(0)), jnp.where(valid, x_ref[1], U32(0)))
        gu = sum(jnp.dot(xq.astype(BF16), wgu_ref[0, 0, c * QUARTER:(c + 1) * QUARTER, :].astype(BF16),
                         preferred_element_type=F32)
                 for c, xq in enumerate(quarters))
        act = (_silu(gu[:, :EXPERT_FF]) * gu[:, EXPERT_FF:]).astype(BF16)
        _store_planes(y_ref, jnp.dot(act, wd_ref[0, 0].astype(BF16), preferred_element_type=F32))

    @pl.when(jnp.logical_not(used))
    def _():
        y_ref[...] = jnp.zeros_like(y_ref)


def expert_ffn(xs, blk_expert, blk_valid, n_used, w_gu, w_down, layer):
    _, rows, w = xs.shape
    d = D_MODEL
    n_blocks = rows // ROW_BLOCK

    def row_map(j, be, bv, nu):
        return (0, jnp.minimum(j, nu[0] - 1), 0)

    def w_map(j, be, bv, nu):
        return (layer, be[jnp.minimum(j, nu[0] - 1)], 0, 0)

    return pl.pallas_call(
        _expert_kernel,
        out_shape=jax.ShapeDtypeStruct(xs.shape, U32),
        grid_spec=pltpu.PrefetchScalarGridSpec(
            num_scalar_prefetch=3,
            grid=(n_blocks,),
            in_specs=[pl.BlockSpec((2, ROW_BLOCK, w), row_map),
                      pl.BlockSpec((1, 1, d, 2 * EXPERT_FF), w_map),
                      pl.BlockSpec((1, 1, EXPERT_FF, d), w_map)],
            out_specs=pl.BlockSpec((2, ROW_BLOCK, w), lambda j, be, bv, nu: (0, j, 0))),
        compiler_params=_params("arbitrary"),
        name="expert_ffn",
    )(blk_expert, blk_valid, n_used, xs, w_gu, w_down)


def _combine_kernel(h_ref, gate_ref, ys_ref, sgu_ref, sd_ref, g_ref, b_ref, o_ref, ob_ref):
    h = h_ref[...]
    gu = jnp.dot(h.astype(BF16), sgu_ref[...], preferred_element_type=F32)
    act = _silu(gu[:, :EXPERT_FF]) * gu[:, EXPERT_FF:]
    y = jnp.dot(act.astype(BF16), sd_ref[...], preferred_element_type=F32)
    gate = gate_ref[...]
    acc = [y[:, c * QUARTER:(c + 1) * QUARTER] for c in range(4)]
    for k in range(TOP_K):
        gk = gate[:, k:k + 1]
        acc = [a + gk * q for a, q in zip(acc, _load_quarters(ys_ref[0, k], ys_ref[1, k]))]
    out = _layer_norm_rows(DEEPNORM_ALPHA * h + jnp.concatenate(acc, axis=-1), g_ref[...], b_ref[...])
    o_ref[...] = out
    ob_ref[...] = out.astype(BF16)


def combine_shared_ln(h, gate, y_slots, sh_gu, sh_down, g, b):
    n, d = h.shape
    tm = MOE_ROWS
    row = lambda i: (i, 0)
    const = lambda i: (0, 0)
    return pl.pallas_call(
        _combine_kernel,
        out_shape=(jax.ShapeDtypeStruct((n, d), F32), jax.ShapeDtypeStruct((n, d), BF16)),
        grid=(n // tm,),
        in_specs=[pl.BlockSpec((tm, d), row),
                  pl.BlockSpec((tm, TOP_K), row),
                  pl.BlockSpec((2, TOP_K, tm, QUARTER), lambda i: (0, 0, i, 0)),
                  pl.BlockSpec((d, 2 * EXPERT_FF), const),
                  pl.BlockSpec((EXPERT_FF, d), const),
                  pl.BlockSpec((1, d), const), pl.BlockSpec((1, d), const)],
        out_specs=(pl.BlockSpec((tm, d), row), pl.BlockSpec((tm, d), row)),
        compiler_params=_params("parallel"),
        name="combine_shared_ln",
    )(h, gate, y_slots, sh_gu, sh_down, g.reshape(1, d), b.reshape(1, d))


_MAIN_BLOCKS = (0, 1, 3, 4, 7, 8, 9, 10, 11, 12, 13, 14, 2)
_FORGET_BLOCKS = (5, 6)
_Q_A, _Q_R, _I_R, _G_R, _GATE_A, _GATE_R = (i * D_MODEL for i in range(6))
_K_A = 6 * D_MODEL
_V_A = _K_A + KV_WIDTH


def kernel(x, ln_in_g, ln_in_b, lb_logits, w_in, b_in, attn_sinks, rec_norm_g, w_proj_attn, w_proj_rec, w_out,
           ln1_g, ln1_b, router_w, router_bias, expert_w_gu, expert_w_down, shared_w_gu, shared_w_down,
           ln2_g, ln2_b):
    batch, seq, d = x.shape
    n = batch * seq
    depth = w_in.shape[0]
    n_blocks = n * TOP_K // ROW_BLOCK + N_EXPERTS
    rows = n_blocks * ROW_BLOCK

    p = jax.nn.softmax(lb_logits.astype(F32), axis=0)
    cum = jnp.cumsum(p, axis=0)
    lower = cum - cum[0:1]
    log_lb = jnp.log(lower)
    log1m_lb = jnp.log1p(-lower)

    w_in_bf = w_in.astype(BF16)
    h, hb = layer_norm_in(x.reshape(n, d), ln_in_g, ln_in_b)
    for l in range(depth):
        proj = in_proj(hb, w_in_bf, b_in, l, _MAIN_BLOCKS, BF16, "in_proj_main")
        proj_f = in_proj(hb, w_in_bf, b_in, l, _FORGET_BLOCKS, F32, "in_proj_forget")
        attn = swa_attention(proj, attn_sinks[l], batch, seq, _Q_A, _K_A, _V_A)
        rec = hgrn2(proj, proj_f, log_lb[l], log1m_lb[l], rec_norm_g[l], batch, seq, _Q_R, _I_R, _G_R)
        h, hp = merge_outproj_ln(h, attn, rec, proj, _GATE_A, _GATE_R,
                                 w_proj_attn[l].astype(BF16), w_proj_rec[l].astype(BF16), w_out[l].astype(BF16),
                                 ln1_g[l], ln1_b[l])

        rw_t = router_w[l].T
        rw_hi = rw_t.astype(BF16)
        rw_lo = (rw_t - rw_hi.astype(F32)).astype(BF16)
        idx_t, gate_t, rank_t, cnt = router(h, rw_hi, rw_lo, router_bias[l])
        counts = cnt[:, 0]
        padded = (counts + ROW_BLOCK - 1) // ROW_BLOCK * ROW_BLOCK
        pad_end = jnp.cumsum(padded)
        pad_start = pad_end - padded
        pos_t = slot_positions(idx_t, rank_t, pad_start)
        blk_row = jnp.arange(n_blocks, dtype=I32) * ROW_BLOCK
        blk_expert = jnp.minimum(jnp.sum(pad_end[None, :] <= blk_row[:, None], axis=1), N_EXPERTS - 1).astype(I32)
        blk_valid = jnp.clip(counts[blk_expert] - (blk_row - pad_start[blk_expert]), 0, ROW_BLOCK).astype(I32)
        n_used = (pad_end[-1:] // ROW_BLOCK).astype(I32)

        xs = sc_scatter_rows(hp, pos_t, rows)
        ys = expert_ffn(xs, blk_expert, blk_valid, n_used, expert_w_gu, expert_w_down, l)
        y_slots = sc_gather_rows(ys, pos_t.reshape(TOP_K * n)).reshape(2, TOP_K, n, QUARTER)
        h, hb = combine_shared_ln(h, gate_t.T, y_slots, shared_w_gu[l].astype(BF16),
                                  shared_w_down[l].astype(BF16), ln2_g[l], ln2_b[l])
    return h.reshape(batch, seq, d)
```

```python
import jax
import jax.numpy as jnp
from jax import lax
from jax.experimental import pallas as pl
from jax.experimental.pallas import tpu as pltpu
from jax.experimental.pallas import tpu_sc as plsc

F32 = jnp.float32
BF16 = jnp.bfloat16
U32 = jnp.uint32
I32 = jnp.int32

D_MODEL = 1024
QUARTER = D_MODEL // 4
CHUNK = 64
ATTN_HEADS = 16
ATTN_KV_HEADS = 4
ATTN_HEAD_DIM = 64
ATTN_GROUP = ATTN_HEADS // ATTN_KV_HEADS
WIN_CHUNKS = 2
KV_WIDTH = ATTN_KV_HEADS * ATTN_HEAD_DIM
REC_HEADS = 8
REC_DIM = 128
N_EXPERTS = 256
TOP_K = 8
N_GROUPS = 8
GROUP_SIZE = N_EXPERTS // N_GROUPS
TOPK_GROUPS = 4
EXPERT_FF = 256
ROUTED_SCALE = 2.5
DEPTH = 2
DEEPNORM_ALPHA = (2 * DEPTH) ** 0.25
LN_EPS = 1e-5
RMS_EPS = 1e-5
NEG_INF = float("-inf")

VMEM_LIMIT_BYTES = 48 * 1024 * 1024

LN_ROWS = 512
PROJ_ROWS = 2048
PROJ_COLS = 512
ATTN_Q_BLOCK = 256
ATTN_STAGE_LAG = 3
REC_TILE = 256
REC_BLOCK = 64
REC_SUB = 16
MERGE_ROWS = 256
ROUTER_ROWS = 256
MOE_ROWS = 256
MOE_CHUNKS = 4
ROW_BLOCK = 512

DECAY_LIMIT = 60.0


def _params(*sem):
    return pltpu.CompilerParams(dimension_semantics=sem, vmem_limit_bytes=VMEM_LIMIT_BYTES)


def _layer_norm_rows(x, g, b):
    mu = jnp.mean(x, axis=-1, keepdims=True)
    xc = x - mu
    var = jnp.mean(xc * xc, axis=-1, keepdims=True)
    return xc * lax.rsqrt(var + LN_EPS) * g + b


def _sigmoid(x):
    return 1.0 / (1.0 + jnp.exp(-x))


def _silu(x):
    return x * _sigmoid(x)


def _pack_pair(lo, hi):
    lo_bits = pltpu.bitcast(lo.astype(BF16).astype(F32), U32)
    hi_bits = pltpu.bitcast(hi.astype(BF16).astype(F32), U32)
    return lax.shift_right_logical(lo_bits, U32(16)) | (hi_bits & U32(0xFFFF0000))


def _unpack_pair(w):
    lo = pltpu.bitcast(lax.shift_left(w, U32(16)), F32)
    hi = pltpu.bitcast(w & U32(0xFFFF0000), F32)
    return lo, hi


def _store_planes(ref, x):
    q = QUARTER
    ref[0] = _pack_pair(x[:, 0 * q:1 * q], x[:, 1 * q:2 * q])
    ref[1] = _pack_pair(x[:, 2 * q:3 * q], x[:, 3 * q:4 * q])


def _load_quarters(plane0, plane1):
    return _unpack_pair(plane0) + _unpack_pair(plane1)


def _ln_in_kernel(x_ref, g_ref, b_ref, h_ref, hb_ref):
    h = _layer_norm_rows(x_ref[...], g_ref[...], b_ref[...])
    h_ref[...] = h
    hb_ref[...] = h.astype(BF16)


def layer_norm_in(x, g, b):
    n, d = x.shape
    row = lambda i: (i, 0)
    const = lambda i: (0, 0)
    return pl.pallas_call(
        _ln_in_kernel,
        out_shape=(jax.ShapeDtypeStruct((n, d), F32), jax.ShapeDtypeStruct((n, d), BF16)),
        grid=(n // LN_ROWS,),
        in_specs=[pl.BlockSpec((LN_ROWS, d), row), pl.BlockSpec((1, d), const), pl.BlockSpec((1, d), const)],
        out_specs=(pl.BlockSpec((LN_ROWS, d), row), pl.BlockSpec((LN_ROWS, d), row)),
        compiler_params=_params("parallel"),
        name="ln_in",
    )(x, g.reshape(1, d), b.reshape(1, d))


def _in_proj_kernel(perm_ref, x_ref, w_ref, b_ref, o_ref):
    del perm_ref
    acc = jnp.dot(x_ref[...], w_ref[0], preferred_element_type=F32)
    o_ref[...] = (acc + b_ref[0]).astype(o_ref.dtype)


def in_proj(xb, w, b, layer, col_blocks, out_dtype, name):
    n, k = xb.shape
    perm = jnp.asarray(col_blocks, I32)
    nblk = len(col_blocks)
    tm = min(PROJ_ROWS, n)
    return pl.pallas_call(
        _in_proj_kernel,
        out_shape=jax.ShapeDtypeStruct((n, nblk * PROJ_COLS), out_dtype),
        grid_spec=pltpu.PrefetchScalarGridSpec(
            num_scalar_prefetch=1,
            grid=(n // tm, nblk),
            in_specs=[pl.BlockSpec((tm, k), lambda i, j, p: (i, 0)),
                      pl.BlockSpec((1, k, PROJ_COLS), lambda i, j, p: (layer, 0, p[j])),
                      pl.BlockSpec((1, 1, PROJ_COLS), lambda i, j, p: (layer, 0, p[j]))],
            out_specs=pl.BlockSpec((tm, PROJ_COLS), lambda i, j, p: (i, j))),
        compiler_params=_params("parallel", "arbitrary"),
        name=name,
    )(perm, xb, w, b.reshape(b.shape[0], 1, -1))


def _attn_kernel(sink_ref, q_ref, kp_ref, kc_ref, vp_ref, vc_ref, o_ref, s_ref, e_ref):
    i = pl.program_id(1)
    half = ATTN_Q_BLOCK // 2
    qc = lax.broadcasted_iota(I32, (half, ATTN_Q_BLOCK), 0) // CHUNK
    kc = lax.broadcasted_iota(I32, (half, ATTN_Q_BLOCK), 1) // CHUNK
    valid = (kc >= qc) & (kc <= qc + WIN_CHUNKS)
    first = jnp.where(i == 0, WIN_CHUNKS, 0)
    scale = ATTN_HEAD_DIM ** -0.5
    windows = ((jnp.concatenate([kp_ref[half:, :], kc_ref[:half, :]], axis=0),
                jnp.concatenate([vp_ref[half:, :], vc_ref[:half, :]], axis=0),
                jnp.where(valid & (kc >= first), 0.0, NEG_INF)),
               (kc_ref[...], vc_ref[...], jnp.where(valid, 0.0, NEG_INF)))
    for part, (k, _, mask_bias) in enumerate(windows):
        rows = slice(part * half, (part + 1) * half)
        for h in range(ATTN_HEADS):
            kv = h // ATTN_GROUP
            qh = q_ref[rows, h * ATTN_HEAD_DIM:(h + 1) * ATTN_HEAD_DIM] * scale
            kh = k[:, kv * ATTN_HEAD_DIM:(kv + 1) * ATTN_HEAD_DIM]
            s_ref[part * ATTN_HEADS + h] = (
                lax.dot_general(qh, kh, (((1,), (1,)), ((), ())), preferred_element_type=F32) + mask_bias)
    units = [(part, h) for part in range(2) for h in range(ATTN_HEADS)]

    def row_max(part, h):
        return jnp.maximum(jnp.max(s_ref[part * ATTN_HEADS + h], axis=-1, keepdims=True), sink_ref[h])

    def exponentials(part, h, m):
        e = jnp.exp(s_ref[part * ATTN_HEADS + h] - m)
        e_ref[part * ATTN_HEADS + h] = e.astype(BF16)
        return 1.0 / (jnp.sum(e, axis=-1, keepdims=True) + jnp.exp(sink_ref[h] - m))

    def values(part, h, rden):
        kv = h // ATTN_GROUP
        vh = windows[part][1][:, kv * ATTN_HEAD_DIM:(kv + 1) * ATTN_HEAD_DIM]
        oh = jnp.dot(e_ref[part * ATTN_HEADS + h], vh, preferred_element_type=F32) * rden
        rows = slice(part * half, (part + 1) * half)
        o_ref[rows, h * ATTN_HEAD_DIM:(h + 1) * ATTN_HEAD_DIM] = oh.astype(o_ref.dtype)

    maxes, rdens = {}, {}
    lag = ATTN_STAGE_LAG
    for t in range(len(units) + 2 * lag):
        if t < len(units):
            maxes[t] = row_max(*units[t])
        if 0 <= t - lag < len(units):
            rdens[t - lag] = exponentials(*units[t - lag], maxes.pop(t - lag))
        if 0 <= t - 2 * lag < len(units):
            values(*units[t - 2 * lag], rdens.pop(t - 2 * lag))


def swa_attention(proj, sinks, batch, seq, q_col, k_col, v_col):
    n = batch * seq
    nb = seq // ATTN_Q_BLOCK
    qb, kb, vb = q_col // D_MODEL, k_col // KV_WIDTH, v_col // KV_WIDTH

    def cur(col):
        return lambda b, i, s: (b * nb + i, col)

    def prev(col):
        return lambda b, i, s: (b * nb + jnp.maximum(i - 1, 0), col)

    return pl.pallas_call(
        _attn_kernel,
        out_shape=jax.ShapeDtypeStruct((n, D_MODEL), BF16),
        grid_spec=pltpu.PrefetchScalarGridSpec(
            num_scalar_prefetch=1,
            grid=(batch, nb),
            in_specs=[pl.BlockSpec((ATTN_Q_BLOCK, D_MODEL), cur(qb)),
                      pl.BlockSpec((ATTN_Q_BLOCK, KV_WIDTH), prev(kb)),
                      pl.BlockSpec((ATTN_Q_BLOCK, KV_WIDTH), cur(kb)),
                      pl.BlockSpec((ATTN_Q_BLOCK, KV_WIDTH), prev(vb)),
                      pl.BlockSpec((ATTN_Q_BLOCK, KV_WIDTH), cur(vb))],
            out_specs=pl.BlockSpec((ATTN_Q_BLOCK, D_MODEL), lambda b, i, s: (b * nb + i, 0)),
            scratch_shapes=[pltpu.VMEM((2 * ATTN_HEADS, ATTN_Q_BLOCK // 2, ATTN_Q_BLOCK), F32),
                            pltpu.VMEM((2 * ATTN_HEADS, ATTN_Q_BLOCK // 2, ATTN_Q_BLOCK), BF16)]),
        compiler_params=_params("parallel", "parallel"),
        name="swa_attention",
    )(sinks.astype(F32), proj, proj, proj, proj, proj)


def _hgrn_kernel(q_ref, i_ref, g_ref, f_ref, loglb_ref, log1mlb_ref, ng_ref, o_ref,
                 b_ref, k_ref, qs_ref, lf_ref, intra_ref, inter_ref, qd_ref, kt_ref, *st_ref):
    @pl.when(pl.program_id(1) == 0)
    def _():
        for s_ref in st_ref:
            s_ref[...] = jnp.zeros_like(s_ref)

    fl = f_ref[...]
    log_sig = jnp.minimum(fl, 0.0) - jnp.log(1.0 + jnp.exp(-jnp.abs(fl)))
    a = loglb_ref[...]
    c = log1mlb_ref[...] + log_sig
    log_f = jnp.maximum(a, c) + jnp.log(1.0 + jnp.exp(-jnp.abs(a - c)))
    lf_ref[...] = log_f
    k_ref[...] = 1.0 - jnp.exp(log_f)
    qs_ref[...] = _silu(q_ref[...].astype(F32))

    def prefix_sum(x, width):
        pos = lax.broadcasted_iota(I32, x.shape, 0) % width
        shift = 1
        while shift < width:
            x = x + jnp.where(pos >= shift, pltpu.roll(x, shift, axis=0), 0.0)
            shift *= 2
        return x

    def suffix_sum(x, width):
        pos = lax.broadcasted_iota(I32, x.shape, 0) % width
        shift = 1
        while shift < width:
            x = x + jnp.where(pos < width - shift, pltpu.roll(x, x.shape[0] - shift, axis=0), 0.0)
            shift *= 2
        return x

    half = REC_BLOCK // 2
    pre = prefix_sum(log_f, half)
    b_ref[...] = pre
    second = (lax.broadcasted_iota(I32, log_f.shape, 0) % REC_BLOCK) >= half
    d = jnp.where(second, pre, log_f - suffix_sum(log_f, half))
    intra_ref[...] = d
    decay_bounded = jnp.max(jnp.abs(d)) <= DECAY_LIMIT

    t_iota = lax.broadcasted_iota(I32, (REC_SUB, REC_DIM), 0)
    nt = (((1,), (1,)), ((), ()))
    tn = (((0,), (0,)), ((), ()))

    def finish(o, r0, cols):
        ms = jnp.mean(o * o, axis=-1, keepdims=True)
        o = o * lax.rsqrt(ms + RMS_EPS) * ng_ref[...]
        gj = g_ref[pl.ds(r0, REC_SUB), cols].astype(F32)
        o_ref[pl.ds(r0, REC_SUB), cols] = (o * _silu(gj)).astype(o_ref.dtype)

    @pl.when(decay_bounded)
    def _():
        qd_ref[...] = (qs_ref[...] * jnp.exp(intra_ref[...])).astype(BF16)
        kt_ref[...] = (k_ref[...] * jnp.exp(-intra_ref[...])).astype(BF16)
        ri = lax.broadcasted_iota(I32, (REC_TILE, REC_TILE), 0)
        ci = lax.broadcasted_iota(I32, (REC_TILE, REC_TILE), 1)
        keep = (ri >= ci) & (ri // REC_BLOCK == ci // REC_BLOCK)
        for h in range(REC_HEADS):
            cols = slice(h * REC_DIM, (h + 1) * REC_DIM)
            att = lax.dot_general(qd_ref[:, cols], kt_ref[:, cols], nt, preferred_element_type=F32)
            att = jnp.where(keep, att, 0.0).astype(BF16)
            intra_ref[:, cols] = jnp.dot(att, i_ref[:, cols], preferred_element_type=F32)

        def step(j, carry):
            r0 = pl.multiple_of(j * REC_BLOCK, REC_BLOCK)
            for h in range(REC_HEADS):
                cols = slice(h * REC_DIM, (h + 1) * REC_DIM)
                qd = qd_ref[pl.ds(r0, REC_BLOCK), cols]
                kt = kt_ref[pl.ds(r0, REC_BLOCK), cols]
                vj = i_ref[pl.ds(r0, REC_BLOCK), cols]
                pre_j = b_ref[pl.ds(r0, REC_BLOCK), cols]
                e1 = jnp.exp(pre_j[half - 1:half, :])
                e2 = jnp.exp(pre_j[REC_BLOCK - 1:REC_BLOCK, :])
                st_mid = st_ref[h][...] * e1
                inter_ref[pl.ds(r0, REC_BLOCK), cols] = lax.dot_general(
                    qd, st_mid.astype(BF16), nt, preferred_element_type=F32)
                kv_t = lax.dot_general(vj, kt, tn, preferred_element_type=F32)
                st_ref[h][...] = (st_mid + kv_t) * e2
            return carry

        lax.fori_loop(0, REC_TILE // REC_BLOCK, step, 0)

        for h in range(REC_HEADS):
            cols = slice(h * REC_DIM, (h + 1) * REC_DIM)
            o = intra_ref[:, cols] + inter_ref[:, cols]
            ms = jnp.mean(o * o, axis=-1, keepdims=True)
            o = o * lax.rsqrt(ms + RMS_EPS) * ng_ref[...]
            o_ref[:, cols] = (o * _silu(g_ref[:, cols].astype(F32))).astype(o_ref.dtype)

    @pl.when(jnp.logical_not(decay_bounded))
    def _():
        b_ref[...] = prefix_sum(lf_ref[...], REC_SUB)
        ones = jnp.ones((REC_DIM, REC_DIM), BF16)

        def step(j, carry):
            r0 = pl.multiple_of(j * REC_SUB, REC_SUB)
            for h in range(REC_HEADS):
                cols = slice(h * REC_DIM, (h + 1) * REC_DIM)
                bj = b_ref[pl.ds(r0, REC_SUB), cols]
                kj = k_ref[pl.ds(r0, REC_SUB), cols]
                qj = qs_ref[pl.ds(r0, REC_SUB), cols]
                vj = i_ref[pl.ds(r0, REC_SUB), cols].astype(F32)
                st = st_ref[h][...]
                qd = (qj * jnp.exp(bj)).astype(BF16)
                o = lax.dot_general(qd, st.astype(BF16), nt, preferred_element_type=F32)
                parts = []
                for s in range(REC_SUB):
                    dec = jnp.exp(jnp.where(t_iota >= s, bj - bj[s:s + 1, :], NEG_INF))
                    parts.append((qj * dec * kj[s:s + 1, :]).astype(BF16))
                pstack = jnp.concatenate(parts, axis=0)
                rsum = jnp.dot(pstack, ones, preferred_element_type=F32)
                for s in range(REC_SUB):
                    o = o + rsum[s * REC_SUB:(s + 1) * REC_SUB, :] * vj[s:s + 1, :]
                b_end = bj[REC_SUB - 1:REC_SUB, :]
                kd = (kj * jnp.exp(b_end - bj)).astype(BF16)
                kv_t = lax.dot_general(vj.astype(BF16), kd, tn, preferred_element_type=F32)
                st_ref[h][...] = st * jnp.exp(b_end) + kv_t
                finish(o, r0, cols)
            return carry

        lax.fori_loop(0, REC_TILE // REC_SUB, step, 0)


def hgrn2(proj, proj_f, log_lb, log1m_lb, norm_g, batch, seq, q_col, i_col, g_col):
    n = batch * seq
    nb = seq // REC_TILE
    d = D_MODEL

    def blk(col):
        return pl.BlockSpec((REC_TILE, d), lambda b, i: (b * nb + i, col // d))

    return pl.pallas_call(
        _hgrn_kernel,
        out_shape=jax.ShapeDtypeStruct((n, d), BF16),
        grid=(batch, nb),
        in_specs=[blk(q_col), blk(i_col), blk(g_col),
                  pl.BlockSpec((REC_TILE, d), lambda b, i: (b * nb + i, 0)),
                  pl.BlockSpec((1, d), lambda b, i: (0, 0)),
                  pl.BlockSpec((1, d), lambda b, i: (0, 0)),
                  pl.BlockSpec((1, REC_DIM), lambda b, i: (0, 0))],
        out_specs=pl.BlockSpec((REC_TILE, d), lambda b, i: (b * nb + i, 0)),
        scratch_shapes=[pltpu.VMEM((REC_TILE, d), F32) for _ in range(6)]
                       + [pltpu.VMEM((REC_TILE, d), BF16),
                        pltpu.VMEM((REC_TILE, d), BF16)]
                       + [pltpu.VMEM((REC_DIM, REC_DIM), F32) for _ in range(REC_HEADS)],
        compiler_params=_params("parallel", "arbitrary"),
        name="hgrn2",
    )(proj, proj, proj, proj_f, log_lb.reshape(1, d), log1m_lb.reshape(1, d), norm_g.reshape(1, REC_DIM))


def _merge_kernel(h_ref, attn_ref, rec_ref, ga_ref, gr_ref, wpa_ref, wpr_ref, wo_ref, g_ref, b_ref,
                  o_ref, op_ref):
    a = jnp.dot(attn_ref[...], wpa_ref[...], preferred_element_type=F32)
    r = jnp.dot(rec_ref[...], wpr_ref[...], preferred_element_type=F32)
    merged = _sigmoid(ga_ref[...].astype(F32)) * a + _sigmoid(gr_ref[...].astype(F32)) * r
    y = jnp.dot(merged.astype(BF16), wo_ref[...], preferred_element_type=F32)
    h1 = _layer_norm_rows(DEEPNORM_ALPHA * h_ref[...] + y, g_ref[...], b_ref[...])
    o_ref[...] = h1
    _store_planes(op_ref, h1)


def merge_outproj_ln(h, attn, rec, proj, ga_col, gr_col, wpa, wpr, wo, g, b):
    n, d = h.shape
    tm = MERGE_ROWS
    row = lambda i: (i, 0)
    const = lambda i: (0, 0)
    return pl.pallas_call(
        _merge_kernel,
        out_shape=(jax.ShapeDtypeStruct((n, d), F32), jax.ShapeDtypeStruct((2, n, QUARTER), U32)),
        grid=(n // tm,),
        in_specs=[pl.BlockSpec((tm, d), row), pl.BlockSpec((tm, d), row), pl.BlockSpec((tm, d), row),
                  pl.BlockSpec((tm, d), lambda i: (i, ga_col // d)),
                  pl.BlockSpec((tm, d), lambda i: (i, gr_col // d)),
                  pl.BlockSpec((d, d), const), pl.BlockSpec((d, d), const), pl.BlockSpec((d, d), const),
                  pl.BlockSpec((1, d), const), pl.BlockSpec((1, d), const)],
        out_specs=(pl.BlockSpec((tm, d), row), pl.BlockSpec((2, tm, QUARTER), lambda i: (0, i, 0))),
        compiler_params=_params("parallel"),
        name="merge_outproj_ln",
    )(h, attn, rec, proj, proj, wpa, wpr, wo, g.reshape(1, d), b.reshape(1, d))


def _router_kernel(h_ref, whi_ref, wlo_ref, bias_ref, idx_ref, gate_ref, rank_ref, cnt_ref, carry_ref):
    @pl.when(pl.program_id(0) == 0)
    def _():
        carry_ref[...] = jnp.zeros_like(carry_ref)

    tm = h_ref.shape[0]
    h = h_ref[...]
    h_hi = h.astype(BF16)
    h_lo = (h - h_hi.astype(F32)).astype(BF16)
    nt = (((1,), (1,)), ((), ()))
    logits = (lax.dot_general(whi_ref[...], h_hi, nt, preferred_element_type=F32)
              + lax.dot_general(whi_ref[...], h_lo, nt, preferred_element_type=F32)
              + lax.dot_general(wlo_ref[...], h_hi, nt, preferred_element_type=F32))
    scores = _sigmoid(logits)
    sel = scores + bias_ref[...]
    e_iota = lax.broadcasted_iota(I32, (N_EXPERTS, tm), 0)

    g_iota = lax.broadcasted_iota(I32, (N_GROUPS, tm), 0)
    l_iota = lax.broadcasted_iota(I32, (GROUP_SIZE, tm), 0)
    grp = jnp.zeros((N_GROUPS, tm), F32)
    for g in range(N_GROUPS):
        sg = sel[g * GROUP_SIZE:(g + 1) * GROUP_SIZE, :]
        m1 = jnp.max(sg, axis=0, keepdims=True)
        i1 = jnp.min(jnp.where(sg == m1, l_iota, GROUP_SIZE), axis=0, keepdims=True)
        m2 = jnp.max(jnp.where(l_iota == i1, NEG_INF, sg), axis=0, keepdims=True)
        grp = jnp.where(g_iota == g, m1 + m2, grp)
    gsel = jnp.zeros((N_GROUPS, tm), I32)
    for _ in range(TOPK_GROUPS):
        m = jnp.max(grp, axis=0, keepdims=True)
        gi = jnp.min(jnp.where(grp == m, g_iota, N_GROUPS), axis=0, keepdims=True)
        hit = g_iota == gi
        gsel = jnp.where(hit, 1, gsel)
        grp = jnp.where(hit, NEG_INF, grp)
    masked = []
    for g in range(N_GROUPS):
        sg = sel[g * GROUP_SIZE:(g + 1) * GROUP_SIZE, :]
        masked.append(jnp.where(gsel[g:g + 1, :] > 0, sg, NEG_INF))
    selm = jnp.concatenate(masked, axis=0)

    k_iota = lax.broadcasted_iota(I32, (TOP_K, tm), 0)
    idx = jnp.zeros((TOP_K, tm), I32)
    gate = jnp.zeros((TOP_K, tm), F32)
    member = jnp.zeros((N_EXPERTS, tm), F32)
    for k in range(TOP_K):
        m = jnp.max(selm, axis=0, keepdims=True)
        ei = jnp.min(jnp.where(selm == m, e_iota, N_EXPERTS), axis=0, keepdims=True)
        hit = e_iota == ei
        gk = jnp.sum(jnp.where(hit, scores, 0.0), axis=0, keepdims=True)
        idx = jnp.where(k_iota == k, ei, idx)
        gate = jnp.where(k_iota == k, gk, gate)
        member = jnp.where(hit, 1.0, member)
        selm = jnp.where(hit, NEG_INF, selm)
    gate = gate / jnp.sum(gate, axis=0, keepdims=True) * ROUTED_SCALE

    upper = lax.broadcasted_iota(I32, (tm, tm), 0) < lax.broadcasted_iota(I32, (tm, tm), 1)
    before = jnp.dot(member.astype(BF16), upper.astype(BF16), preferred_element_type=F32) + carry_ref[...]
    rank = jnp.zeros((TOP_K, tm), F32)
    for k in range(TOP_K):
        rk = jnp.sum(jnp.where(e_iota == idx[k:k + 1, :], before, 0.0), axis=0, keepdims=True)
        rank = jnp.where(k_iota == k, rk, rank)
    carry_ref[...] = carry_ref[...] + jnp.sum(member, axis=1, keepdims=True)

    idx_ref[...] = idx
    gate_ref[...] = gate
    rank_ref[...] = rank.astype(I32)
    cnt_ref[...] = jnp.broadcast_to(carry_ref[...], cnt_ref.shape).astype(I32)


def router(h, w_t_hi, w_t_lo, bias):
    n, d = h.shape
    tm = ROUTER_ROWS
    tok = lambda i: (0, i)
    const = lambda i: (0, 0)
    return pl.pallas_call(
        _router_kernel,
        out_shape=(jax.ShapeDtypeStruct((TOP_K, n), I32),
                   jax.ShapeDtypeStruct((TOP_K, n), F32),
                   jax.ShapeDtypeStruct((TOP_K, n), I32),
                   jax.ShapeDtypeStruct((N_EXPERTS, 128), I32)),
        grid=(n // tm,),
        in_specs=[pl.BlockSpec((tm, d), lambda i: (i, 0)),
                  pl.BlockSpec((N_EXPERTS, d), const),
                  pl.BlockSpec((N_EXPERTS, d), const),
                  pl.BlockSpec((N_EXPERTS, 1), const)],
        out_specs=(pl.BlockSpec((TOP_K, tm), tok), pl.BlockSpec((TOP_K, tm), tok),
                   pl.BlockSpec((TOP_K, tm), tok), pl.BlockSpec((N_EXPERTS, 128), const)),
        scratch_shapes=[pltpu.VMEM((N_EXPERTS, 1), F32)],
        compiler_params=_params("arbitrary"),
        name="router",
    )(h, w_t_hi, w_t_lo, bias.reshape(N_EXPERTS, 1))


def _slot_pos_kernel(idx_ref, rank_ref, start_ref, pos_ref):
    tm = idx_ref.shape[1]
    e_iota = lax.broadcasted_iota(I32, (N_EXPERTS, tm), 0)
    k_iota = lax.broadcasted_iota(I32, (TOP_K, tm), 0)
    idx = idx_ref[...]
    start = start_ref[...]
    base = jnp.zeros((TOP_K, tm), F32)
    for k in range(TOP_K):
        bk = jnp.sum(jnp.where(e_iota == idx[k:k + 1, :], start, 0.0), axis=0, keepdims=True)
        base = jnp.where(k_iota == k, bk, base)
    pos_ref[...] = base.astype(I32) + rank_ref[...]


def slot_positions(idx_t, rank_t, pad_start):
    n = idx_t.shape[1]
    tm = 1024
    tok = lambda i: (0, i)
    return pl.pallas_call(
        _slot_pos_kernel,
        out_shape=jax.ShapeDtypeStruct((TOP_K, n), I32),
        grid=(n // tm,),
        in_specs=[pl.BlockSpec((TOP_K, tm), tok), pl.BlockSpec((TOP_K, tm), tok),
                  pl.BlockSpec((N_EXPERTS, 1), lambda i: (0, 0))],
        out_specs=pl.BlockSpec((TOP_K, tm), tok),
        compiler_params=_params("parallel"),
        name="slot_positions",
    )(idx_t, rank_t, pad_start.astype(F32).reshape(N_EXPERTS, 1))


SC_WINDOW = 128
SC_WORDS = QUARTER


def _sc_mesh():
    return plsc.VectorSubcoreMesh(core_axis_name="core", subcore_axis_name="subcore")


def sc_scatter_rows(src, pos_t, rows):
    n = src.shape[1]
    src2 = src.reshape(2 * n, SC_WORDS)
    idx2 = jnp.concatenate([pos_t, pos_t + rows], axis=1)

    @pl.kernel(out_type=jax.ShapeDtypeStruct((2 * rows, SC_WORDS), src.dtype), mesh=_sc_mesh(), scratch_types=[])
    def scatter_kernel(x_hbm, i_hbm, o_hbm):
        def body(x_vmem, i_vmem):
            pltpu.sync_copy(x_vmem, o_hbm.at[i_vmem.at[0]])

        pltpu.emit_pipeline(
            body,
            grid=(2 * n // SC_WINDOW, TOP_K),
            in_specs=[pl.BlockSpec((SC_WINDOW, SC_WORDS), index_map=lambda i, k: (i, 0)),
                      pl.BlockSpec((1, SC_WINDOW), index_map=lambda i, k: (k, i))],
            out_specs=[],
            core_axis_name=("core", "subcore"),
            dimension_semantics=(pltpu.PARALLEL, pltpu.ARBITRARY),
        )(x_hbm, i_hbm)

    return scatter_kernel(src2, idx2).reshape(2, rows, SC_WORDS)


def sc_gather_rows(src, pos):
    r = src.shape[1]
    m = pos.shape[0]
    src2 = src.reshape(2 * r, SC_WORDS)
    idx2 = jnp.concatenate([pos, pos + r]).reshape(1, 2 * m)

    @pl.kernel(out_type=jax.ShapeDtypeStruct((2 * m, SC_WORDS), src.dtype), mesh=_sc_mesh(), scratch_types=[])
    def gather_kernel(x_hbm, i_hbm, o_hbm):
        def body(i_vmem, o_vmem):
            pltpu.sync_copy(x_hbm.at[i_vmem.at[0]], o_vmem)

        pltpu.emit_pipeline(
            body,
            grid=(2 * m // SC_WINDOW,),
            in_specs=[pl.BlockSpec((1, SC_WINDOW), index_map=lambda i: (0, i))],
            out_specs=[pl.BlockSpec((SC_WINDOW, SC_WORDS), index_map=lambda i: (i, 0))],
            core_axis_name=("core", "subcore"),
            dimension_semantics=(pltpu.PARALLEL,),
        )(i_hbm, o_hbm)

    return gather_kernel(src2, idx2).reshape(2, m, SC_WORDS)


def _expert_kernel(blk_expert_ref, blk_valid_ref, n_used_ref, x_ref, wgu_ref, wd_ref, y_ref):
    del blk_expert_ref
    j = pl.program_id(0)
    used = j < n_used_ref[0]

    @pl.when(used)
    def _():
        valid = lax.broadcasted_iota(I32, x_ref.shape[1:], 0) < blk_valid_ref[j]
        quarters = _load_quarters(jnp.where(valid, x_ref[0], U32(0)), jnp.where(valid, x_ref[1], U32(0)))
        gu = sum(jnp.dot(xq.astype(BF16), wgu_ref[0, 0, c * QUARTER:(c + 1) * QUARTER, :].astype(BF16),
                         preferred_element_type=F32)
                 for c, xq in enumerate(quarters))
        act = (_silu(gu[:, :EXPERT_FF]) * gu[:, EXPERT_FF:]).astype(BF16)
        _store_planes(y_ref, jnp.dot(act, wd_ref[0, 0].astype(BF16), preferred_element_type=F32))

    @pl.when(jnp.logical_not(used))
    def _():
        y_ref[...] = jnp.zeros_like(y_ref)


def expert_ffn(xs, blk_expert, blk_valid, n_used, w_gu, w_down, layer):
    _, rows, w = xs.shape
    d = D_MODEL
    n_blocks = rows // ROW_BLOCK

    def row_map(j, be, bv, nu):
        return (0, jnp.minimum(j, nu[0] - 1), 0)

    def w_map(j, be, bv, nu):
        return (layer, be[jnp.minimum(j, nu[0] - 1)], 0, 0)

    return pl.pallas_call(
        _expert_kernel,
        out_shape=jax.ShapeDtypeStruct(xs.shape, U32),
        grid_spec=pltpu.PrefetchScalarGridSpec(
            num_scalar_prefetch=3,
            grid=(n_blocks,),
            in_specs=[pl.BlockSpec((2, ROW_BLOCK, w), row_map),
                      pl.BlockSpec((1, 1, d, 2 * EXPERT_FF), w_map),
                      pl.BlockSpec((1, 1, EXPERT_FF, d), w_map)],
            out_specs=pl.BlockSpec((2, ROW_BLOCK, w), lambda j, be, bv, nu: (0, j, 0))),
        compiler_params=_params("arbitrary"),
        name="expert_ffn",
    )(blk_expert, blk_valid, n_used, xs, w_gu, w_down)


def _combine_kernel(h_ref, gate_ref, ys_ref, sgu_ref, sd_ref, g_ref, b_ref, *refs):
    o_ref, ob_ref = refs[-2:]
    h = h_ref[...]
    gu = jnp.dot(h.astype(BF16), sgu_ref[...], preferred_element_type=F32)
    act = _silu(gu[:, :EXPERT_FF]) * gu[:, EXPERT_FF:]
    y = jnp.dot(act.astype(BF16), sd_ref[...], preferred_element_type=F32)
    gate = gate_ref[...]
    acc = [y[:, c * QUARTER:(c + 1) * QUARTER] for c in range(4)]
    for k in range(TOP_K):
        gk = gate[:, k:k + 1]
        acc = [a + gk * q for a, q in zip(acc, _load_quarters(ys_ref[0, k], ys_ref[1, k]))]
    out = _layer_norm_rows(DEEPNORM_ALPHA * h + jnp.concatenate(acc, axis=-1), g_ref[...], b_ref[...])
    o_ref[...] = out
    ob_ref[...] = out.astype(BF16)


def combine_shared_ln(h, gate, y_slots, chunk, prev_out, sh_gu, sh_down, g, b):
    n, d = h.shape
    tm = MOE_ROWS
    steps = y_slots.shape[2] // tm
    row = lambda i: (i + chunk * steps, 0)
    const = lambda i: (0, 0)
    passthrough = () if prev_out is None else tuple(prev_out)
    n_in = 7
    return pl.pallas_call(
        _combine_kernel,
        out_shape=(jax.ShapeDtypeStruct((n, d), F32), jax.ShapeDtypeStruct((n, d), BF16)),
        grid=(steps,),
        in_specs=[pl.BlockSpec((tm, d), row),
                  pl.BlockSpec((tm, TOP_K), row),
                  pl.BlockSpec((2, TOP_K, tm, QUARTER), lambda i: (0, 0, i, 0)),
                  pl.BlockSpec((d, 2 * EXPERT_FF), const),
                  pl.BlockSpec((EXPERT_FF, d), const),
                  pl.BlockSpec((1, d), const), pl.BlockSpec((1, d), const)]
                 + [pl.BlockSpec(memory_space=pl.ANY) for _ in passthrough],
        out_specs=(pl.BlockSpec((tm, d), row), pl.BlockSpec((tm, d), row)),
        input_output_aliases={n_in + i: i for i in range(len(passthrough))},
        compiler_params=_params("parallel"),
        name="combine_shared_ln",
    )(h, gate, y_slots, sh_gu, sh_down, g.reshape(1, d), b.reshape(1, d), *passthrough)


_MAIN_BLOCKS = (0, 1, 3, 4, 7, 8, 9, 10, 11, 12, 13, 14, 2)
_FORGET_BLOCKS = (5, 6)
_Q_A, _Q_R, _I_R, _G_R, _GATE_A, _GATE_R = (i * D_MODEL for i in range(6))
_K_A = 6 * D_MODEL
_V_A = _K_A + KV_WIDTH


def kernel(x, ln_in_g, ln_in_b, lb_logits, w_in, b_in, attn_sinks, rec_norm_g, w_proj_attn, w_proj_rec, w_out,
           ln1_g, ln1_b, router_w, router_bias, expert_w_gu, expert_w_down, shared_w_gu, shared_w_down,
           ln2_g, ln2_b):
    batch, seq, d = x.shape
    n = batch * seq
    depth = w_in.shape[0]
    n_blocks = n * TOP_K // ROW_BLOCK + N_EXPERTS
    rows = n_blocks * ROW_BLOCK

    p = jax.nn.softmax(lb_logits.astype(F32), axis=0)
    cum = jnp.cumsum(p, axis=0)
    lower = cum - cum[0:1]
    log_lb = jnp.log(lower)
    log1m_lb = jnp.log1p(-lower)

    w_in_bf = w_in.astype(BF16)
    h, hb = layer_norm_in(x.reshape(n, d), ln_in_g, ln_in_b)
    for l in range(depth):
        proj = in_proj(hb, w_in_bf, b_in, l, _MAIN_BLOCKS, BF16, "in_proj_main")
        proj_f = in_proj(hb, w_in_bf, b_in, l, _FORGET_BLOCKS, F32, "in_proj_forget")
        attn = swa_attention(proj, attn_sinks[l], batch, seq, _Q_A, _K_A, _V_A)
        rec = hgrn2(proj, proj_f, log_lb[l], log1m_lb[l], rec_norm_g[l], batch, seq, _Q_R, _I_R, _G_R)
        h, hp = merge_outproj_ln(h, attn, rec, proj, _GATE_A, _GATE_R,
                                 w_proj_attn[l].astype(BF16), w_proj_rec[l].astype(BF16), w_out[l].astype(BF16),
                                 ln1_g[l], ln1_b[l])

        rw_t = router_w[l].T
        rw_hi = rw_t.astype(BF16)
        rw_lo = (rw_t - rw_hi.astype(F32)).astype(BF16)
        idx_t, gate_t, rank_t, cnt = router(h, rw_hi, rw_lo, router_bias[l])
        counts = cnt[:, 0]
        padded = (counts + ROW_BLOCK - 1) // ROW_BLOCK * ROW_BLOCK
        pad_end = jnp.cumsum(padded)
        pad_start = pad_end - padded
        pos_t = slot_positions(idx_t, rank_t, pad_start)
        blk_row = jnp.arange(n_blocks, dtype=I32) * ROW_BLOCK
        blk_expert = jnp.minimum(jnp.sum(pad_end[None, :] <= blk_row[:, None], axis=1), N_EXPERTS - 1).astype(I32)
        blk_valid = jnp.clip(counts[blk_expert] - (blk_row - pad_start[blk_expert]), 0, ROW_BLOCK).astype(I32)
        n_used = (pad_end[-1:] // ROW_BLOCK).astype(I32)

        xs = sc_scatter_rows(hp, pos_t, rows)
        ys = expert_ffn(xs, blk_expert, blk_valid, n_used, expert_w_gu, expert_w_down, l)
        nc = n // MOE_CHUNKS
        gate, out = gate_t.T, None
        for c in range(MOE_CHUNKS):
            pos_c = lax.slice_in_dim(pos_t, c * nc, (c + 1) * nc, axis=1).reshape(TOP_K * nc)
            y_slots = sc_gather_rows(ys, pos_c).reshape(2, TOP_K, nc, QUARTER)
            out = combine_shared_ln(h, gate, y_slots, c, out, shared_w_gu[l].astype(BF16),
                                    shared_w_down[l].astype(BF16), ln2_g[l], ln2_b[l])
        h, hb = out
    return h.reshape(batch, seq, d)
```

```python
import jax
import jax.numpy as jnp
from jax import lax
from jax.experimental import pallas as pl
from jax.experimental.pallas import tpu as pltpu
from jax.experimental.pallas import tpu_sc as plsc

F32 = jnp.float32
BF16 = jnp.bfloat16
U32 = jnp.uint32
I32 = jnp.int32

D_MODEL = 1024
QUARTER = D_MODEL // 4
CHUNK = 64
ATTN_HEADS = 16
ATTN_KV_HEADS = 4
ATTN_HEAD_DIM = 64
ATTN_GROUP = ATTN_HEADS // ATTN_KV_HEADS
WIN_CHUNKS = 2
KV_WIDTH = ATTN_KV_HEADS * ATTN_HEAD_DIM
REC_HEADS = 8
REC_DIM = 128
N_EXPERTS = 256
TOP_K = 8
N_GROUPS = 8
GROUP_SIZE = N_EXPERTS // N_GROUPS
TOPK_GROUPS = 4
EXPERT_FF = 256
ROUTED_SCALE = 2.5
DEPTH = 2
DEEPNORM_ALPHA = (2 * DEPTH) ** 0.25
LN_EPS = 1e-5
RMS_EPS = 1e-5
NEG_INF = float("-inf")

VMEM_LIMIT_BYTES = 48 * 1024 * 1024

LN_ROWS = 512
PROJ_ROWS = 4096
PROJ_COLS = 512
ATTN_Q_BLOCK = 256
ATTN_STAGE_LAG = 3
REC_TILE = 256
REC_BLOCK = 64
REC_SUB = 16
MERGE_ROWS = 512
MERGE_PIECE = 256
ROUTER_ROWS = 256
MOE_ROWS = 256
MOE_CHUNKS = 4
ROW_BLOCK = 512

DECAY_LIMIT = 60.0


def _params(*sem):
    return pltpu.CompilerParams(dimension_semantics=sem, vmem_limit_bytes=VMEM_LIMIT_BYTES)


def _layer_norm_rows(x, g, b):
    mu = jnp.mean(x, axis=-1, keepdims=True)
    xc = x - mu
    var = jnp.mean(xc * xc, axis=-1, keepdims=True)
    return xc * lax.rsqrt(var + LN_EPS) * g + b


def _sigmoid(x):
    return 1.0 / (1.0 + jnp.exp(-x))


def _silu(x):
    return x * _sigmoid(x)


def _pack_pair(lo, hi):
    lo_bits = pltpu.bitcast(lo.astype(BF16).astype(F32), U32)
    hi_bits = pltpu.bitcast(hi.astype(BF16).astype(F32), U32)
    return lax.shift_right_logical(lo_bits, U32(16)) | (hi_bits & U32(0xFFFF0000))


def _unpack_pair(w):
    lo = pltpu.bitcast(lax.shift_left(w, U32(16)), F32)
    hi = pltpu.bitcast(w & U32(0xFFFF0000), F32)
    return lo, hi


def _store_planes(ref, x):
    q = QUARTER
    ref[0] = _pack_pair(x[:, 0 * q:1 * q], x[:, 1 * q:2 * q])
    ref[1] = _pack_pair(x[:, 2 * q:3 * q], x[:, 3 * q:4 * q])


def _load_quarters(plane0, plane1):
    return _unpack_pair(plane0) + _unpack_pair(plane1)


def _ln_in_kernel(x_ref, g_ref, b_ref, h_ref, hb_ref):
    h = _layer_norm_rows(x_ref[...], g_ref[...], b_ref[...])
    h_ref[...] = h
    hb_ref[...] = h.astype(BF16)


def layer_norm_in(x, g, b):
    n, d = x.shape
    row = lambda i: (i, 0)
    const = lambda i: (0, 0)
    return pl.pallas_call(
        _ln_in_kernel,
        out_shape=(jax.ShapeDtypeStruct((n, d), F32), jax.ShapeDtypeStruct((n, d), BF16)),
        grid=(n // LN_ROWS,),
        in_specs=[pl.BlockSpec((LN_ROWS, d), row), pl.BlockSpec((1, d), const), pl.BlockSpec((1, d), const)],
        out_specs=(pl.BlockSpec((LN_ROWS, d), row), pl.BlockSpec((LN_ROWS, d), row)),
        compiler_params=_params("parallel"),
        name="ln_in",
    )(x, g.reshape(1, d), b.reshape(1, d))


def _in_proj_kernel(perm_ref, x_ref, w_ref, b_ref, o_ref):
    del perm_ref
    acc = jnp.dot(x_ref[...], w_ref[0], preferred_element_type=F32)
    o_ref[...] = (acc + b_ref[0]).astype(o_ref.dtype)


def in_proj(xb, w, b, layer, col_blocks, out_dtype, name):
    n, k = xb.shape
    perm = jnp.asarray(col_blocks, I32)
    nblk = len(col_blocks)
    tm = min(PROJ_ROWS, n)
    return pl.pallas_call(
        _in_proj_kernel,
        out_shape=jax.ShapeDtypeStruct((n, nblk * PROJ_COLS), out_dtype),
        grid_spec=pltpu.PrefetchScalarGridSpec(
            num_scalar_prefetch=1,
            grid=(n // tm, nblk),
            in_specs=[pl.BlockSpec((tm, k), lambda i, j, p: (i, 0)),
                      pl.BlockSpec((1, k, PROJ_COLS), lambda i, j, p: (layer, 0, p[j])),
                      pl.BlockSpec((1, 1, PROJ_COLS), lambda i, j, p: (layer, 0, p[j]))],
            out_specs=pl.BlockSpec((tm, PROJ_COLS), lambda i, j, p: (i, j))),
        compiler_params=_params("parallel", "arbitrary"),
        name=name,
    )(perm, xb, w, b.reshape(b.shape[0], 1, -1))


def _attn_kernel(sink_ref, q_ref, kp_ref, kc_ref, vp_ref, vc_ref, o_ref, s_ref, e_ref):
    i = pl.program_id(1)
    half = ATTN_Q_BLOCK // 2
    qc = lax.broadcasted_iota(I32, (half, ATTN_Q_BLOCK), 0) // CHUNK
    kc = lax.broadcasted_iota(I32, (half, ATTN_Q_BLOCK), 1) // CHUNK
    valid = (kc >= qc) & (kc <= qc + WIN_CHUNKS)
    first = jnp.where(i == 0, WIN_CHUNKS, 0)
    scale = ATTN_HEAD_DIM ** -0.5
    windows = ((jnp.concatenate([kp_ref[half:, :], kc_ref[:half, :]], axis=0),
                jnp.concatenate([vp_ref[half:, :], vc_ref[:half, :]], axis=0),
                jnp.where(valid & (kc >= first), 0.0, NEG_INF)),
               (kc_ref[...], vc_ref[...], jnp.where(valid, 0.0, NEG_INF)))
    for part, (k, _, mask_bias) in enumerate(windows):
        rows = slice(part * half, (part + 1) * half)
        for h in range(ATTN_HEADS):
            kv = h // ATTN_GROUP
            qh = q_ref[rows, h * ATTN_HEAD_DIM:(h + 1) * ATTN_HEAD_DIM] * scale
            kh = k[:, kv * ATTN_HEAD_DIM:(kv + 1) * ATTN_HEAD_DIM]
            s_ref[part * ATTN_HEADS + h] = (
                lax.dot_general(qh, kh, (((1,), (1,)), ((), ())), preferred_element_type=F32) + mask_bias)
    units = [(part, h) for part in range(2) for h in range(ATTN_HEADS)]

    def row_max(part, h):
        return jnp.maximum(jnp.max(s_ref[part * ATTN_HEADS + h], axis=-1, keepdims=True), sink_ref[h])

    def exponentials(part, h, m):
        e = jnp.exp(s_ref[part * ATTN_HEADS + h] - m)
        e_ref[part * ATTN_HEADS + h] = e.astype(BF16)
        return 1.0 / (jnp.sum(e, axis=-1, keepdims=True) + jnp.exp(sink_ref[h] - m))

    def values(part, h, rden):
        kv = h // ATTN_GROUP
        vh = windows[part][1][:, kv * ATTN_HEAD_DIM:(kv + 1) * ATTN_HEAD_DIM]
        oh = jnp.dot(e_ref[part * ATTN_HEADS + h], vh, preferred_element_type=F32) * rden
        rows = slice(part * half, (part + 1) * half)
        o_ref[rows, h * ATTN_HEAD_DIM:(h + 1) * ATTN_HEAD_DIM] = oh.astype(o_ref.dtype)

    maxes, rdens = {}, {}
    lag = ATTN_STAGE_LAG
    for t in range(len(units) + 2 * lag):
        if t < len(units):
            maxes[t] = row_max(*units[t])
        if 0 <= t - lag < len(units):
            rdens[t - lag] = exponentials(*units[t - lag], maxes.pop(t - lag))
        if 0 <= t - 2 * lag < len(units):
            values(*units[t - 2 * lag], rdens.pop(t - 2 * lag))


def swa_attention(proj, sinks, batch, seq, q_col, k_col, v_col):
    n = batch * seq
    nb = seq // ATTN_Q_BLOCK
    qb, kb, vb = q_col // D_MODEL, k_col // KV_WIDTH, v_col // KV_WIDTH

    def cur(col):
        return lambda b, i, s: (b * nb + i, col)

    def prev(col):
        return lambda b, i, s: (b * nb + jnp.maximum(i - 1, 0), col)

    return pl.pallas_call(
        _attn_kernel,
        out_shape=jax.ShapeDtypeStruct((n, D_MODEL), BF16),
        grid_spec=pltpu.PrefetchScalarGridSpec(
            num_scalar_prefetch=1,
            grid=(batch, nb),
            in_specs=[pl.BlockSpec((ATTN_Q_BLOCK, D_MODEL), cur(qb)),
                      pl.BlockSpec((ATTN_Q_BLOCK, KV_WIDTH), prev(kb)),
                      pl.BlockSpec((ATTN_Q_BLOCK, KV_WIDTH), cur(kb)),
                      pl.BlockSpec((ATTN_Q_BLOCK, KV_WIDTH), prev(vb)),
                      pl.BlockSpec((ATTN_Q_BLOCK, KV_WIDTH), cur(vb))],
            out_specs=pl.BlockSpec((ATTN_Q_BLOCK, D_MODEL), lambda b, i, s: (b * nb + i, 0)),
            scratch_shapes=[pltpu.VMEM((2 * ATTN_HEADS, ATTN_Q_BLOCK // 2, ATTN_Q_BLOCK), F32),
                            pltpu.VMEM((2 * ATTN_HEADS, ATTN_Q_BLOCK // 2, ATTN_Q_BLOCK), BF16)]),
        compiler_params=_params("parallel", "parallel"),
        name="swa_attention",
    )(sinks.astype(F32), proj, proj, proj, proj, proj)


def _hgrn_kernel(q_ref, i_ref, g_ref, f_ref, loglb_ref, log1mlb_ref, ng_ref, o_ref,
                 b_ref, k_ref, qs_ref, lf_ref, intra_ref, inter_ref, qd_ref, kt_ref, *st_ref):
    @pl.when(pl.program_id(1) == 0)
    def _():
        for s_ref in st_ref:
            s_ref[...] = jnp.zeros_like(s_ref)

    fl = f_ref[...]
    log_sig = jnp.minimum(fl, 0.0) - jnp.log(1.0 + jnp.exp(-jnp.abs(fl)))
    a = loglb_ref[...]
    c = log1mlb_ref[...] + log_sig
    log_f = jnp.maximum(a, c) + jnp.log(1.0 + jnp.exp(-jnp.abs(a - c)))
    lf_ref[...] = log_f
    k_ref[...] = 1.0 - jnp.exp(log_f)
    qs_ref[...] = _silu(q_ref[...].astype(F32))

    def prefix_sum(x, width):
        pos = lax.broadcasted_iota(I32, x.shape, 0) % width
        shift = 1
        while shift < width:
            x = x + jnp.where(pos >= shift, pltpu.roll(x, shift, axis=0), 0.0)
            shift *= 2
        return x

    def suffix_sum(x, width):
        pos = lax.broadcasted_iota(I32, x.shape, 0) % width
        shift = 1
        while shift < width:
            x = x + jnp.where(pos < width - shift, pltpu.roll(x, x.shape[0] - shift, axis=0), 0.0)
            shift *= 2
        return x

    half = REC_BLOCK // 2
    pre = prefix_sum(log_f, half)
    b_ref[...] = pre
    second = (lax.broadcasted_iota(I32, log_f.shape, 0) % REC_BLOCK) >= half
    d = jnp.where(second, pre, log_f - suffix_sum(log_f, half))
    intra_ref[...] = d
    decay_bounded = jnp.max(jnp.abs(d)) <= DECAY_LIMIT

    t_iota = lax.broadcasted_iota(I32, (REC_SUB, REC_DIM), 0)
    nt = (((1,), (1,)), ((), ()))
    tn = (((0,), (0,)), ((), ()))

    def finish(o, r0, cols):
        ms = jnp.mean(o * o, axis=-1, keepdims=True)
        o = o * lax.rsqrt(ms + RMS_EPS) * ng_ref[...]
        gj = g_ref[pl.ds(r0, REC_SUB), cols].astype(F32)
        o_ref[pl.ds(r0, REC_SUB), cols] = (o * _silu(gj)).astype(o_ref.dtype)

    @pl.when(decay_bounded)
    def _():
        qd_ref[...] = (qs_ref[...] * jnp.exp(intra_ref[...])).astype(BF16)
        kt_ref[...] = (k_ref[...] * jnp.exp(-intra_ref[...])).astype(BF16)
        ri = lax.broadcasted_iota(I32, (REC_TILE, REC_TILE), 0)
        ci = lax.broadcasted_iota(I32, (REC_TILE, REC_TILE), 1)
        keep = (ri >= ci) & (ri // REC_BLOCK == ci // REC_BLOCK)
        for h in range(REC_HEADS):
            cols = slice(h * REC_DIM, (h + 1) * REC_DIM)
            att = lax.dot_general(qd_ref[:, cols], kt_ref[:, cols], nt, preferred_element_type=F32)
            att = jnp.where(keep, att, 0.0).astype(BF16)
            intra_ref[:, cols] = jnp.dot(att, i_ref[:, cols], preferred_element_type=F32)

        def step(j, carry):
            r0 = pl.multiple_of(j * REC_BLOCK, REC_BLOCK)
            for h in range(REC_HEADS):
                cols = slice(h * REC_DIM, (h + 1) * REC_DIM)
                qd = qd_ref[pl.ds(r0, REC_BLOCK), cols]
                kt = kt_ref[pl.ds(r0, REC_BLOCK), cols]
                vj = i_ref[pl.ds(r0, REC_BLOCK), cols]
                pre_j = b_ref[pl.ds(r0, REC_BLOCK), cols]
                e1 = jnp.exp(pre_j[half - 1:half, :])
                e2 = jnp.exp(pre_j[REC_BLOCK - 1:REC_BLOCK, :])
                st_mid = st_ref[h][...] * e1
                inter_ref[pl.ds(r0, REC_BLOCK), cols] = lax.dot_general(
                    qd, st_mid.astype(BF16), nt, preferred_element_type=F32)
                kv_t = lax.dot_general(vj, kt, tn, preferred_element_type=F32)
                st_ref[h][...] = (st_mid + kv_t) * e2
            return carry

        lax.fori_loop(0, REC_TILE // REC_BLOCK, step, 0)

        for h in range(REC_HEADS):
            cols = slice(h * REC_DIM, (h + 1) * REC_DIM)
            o = intra_ref[:, cols] + inter_ref[:, cols]
            ms = jnp.mean(o * o, axis=-1, keepdims=True)
            o = o * lax.rsqrt(ms + RMS_EPS) * ng_ref[...]
            o_ref[:, cols] = (o * _silu(g_ref[:, cols].astype(F32))).astype(o_ref.dtype)

    @pl.when(jnp.logical_not(decay_bounded))
    def _():
        b_ref[...] = prefix_sum(lf_ref[...], REC_SUB)
        ones = jnp.ones((REC_DIM, REC_DIM), BF16)

        def step(j, carry):
            r0 = pl.multiple_of(j * REC_SUB, REC_SUB)
            for h in range(REC_HEADS):
                cols = slice(h * REC_DIM, (h + 1) * REC_DIM)
                bj = b_ref[pl.ds(r0, REC_SUB), cols]
                kj = k_ref[pl.ds(r0, REC_SUB), cols]
                qj = qs_ref[pl.ds(r0, REC_SUB), cols]
                vj = i_ref[pl.ds(r0, REC_SUB), cols].astype(F32)
                st = st_ref[h][...]
                qd = (qj * jnp.exp(bj)).astype(BF16)
                o = lax.dot_general(qd, st.astype(BF16), nt, preferred_element_type=F32)
                parts = []
                for s in range(REC_SUB):
                    dec = jnp.exp(jnp.where(t_iota >= s, bj - bj[s:s + 1, :], NEG_INF))
                    parts.append((qj * dec * kj[s:s + 1, :]).astype(BF16))
                pstack = jnp.concatenate(parts, axis=0)
                rsum = jnp.dot(pstack, ones, preferred_element_type=F32)
                for s in range(REC_SUB):
                    o = o + rsum[s * REC_SUB:(s + 1) * REC_SUB, :] * vj[s:s + 1, :]
                b_end = bj[REC_SUB - 1:REC_SUB, :]
                kd = (kj * jnp.exp(b_end - bj)).astype(BF16)
                kv_t = lax.dot_general(vj.astype(BF16), kd, tn, preferred_element_type=F32)
                st_ref[h][...] = st * jnp.exp(b_end) + kv_t
                finish(o, r0, cols)
            return carry

        lax.fori_loop(0, REC_TILE // REC_SUB, step, 0)


def hgrn2(proj, proj_f, log_lb, log1m_lb, norm_g, batch, seq, q_col, i_col, g_col):
    n = batch * seq
    nb = seq // REC_TILE
    d = D_MODEL

    def blk(col):
        return pl.BlockSpec((REC_TILE, d), lambda b, i: (b * nb + i, col // d))

    return pl.pallas_call(
        _hgrn_kernel,
        out_shape=jax.ShapeDtypeStruct((n, d), BF16),
        grid=(batch, nb),
        in_specs=[blk(q_col), blk(i_col), blk(g_col),
                  pl.BlockSpec((REC_TILE, d), lambda b, i: (b * nb + i, 0)),
                  pl.BlockSpec((1, d), lambda b, i: (0, 0)),
                  pl.BlockSpec((1, d), lambda b, i: (0, 0)),
                  pl.BlockSpec((1, REC_DIM), lambda b, i: (0, 0))],
        out_specs=pl.BlockSpec((REC_TILE, d), lambda b, i: (b * nb + i, 0)),
        scratch_shapes=[pltpu.VMEM((REC_TILE, d), F32) for _ in range(6)]
                       + [pltpu.VMEM((REC_TILE, d), BF16),
                        pltpu.VMEM((REC_TILE, d), BF16)]
                       + [pltpu.VMEM((REC_DIM, REC_DIM), F32) for _ in range(REC_HEADS)],
        compiler_params=_params("parallel", "arbitrary"),
        name="hgrn2",
    )(proj, proj, proj, proj_f, log_lb.reshape(1, d), log1m_lb.reshape(1, d), norm_g.reshape(1, REC_DIM))


def _merge_kernel(h_ref, attn_ref, rec_ref, ga_ref, gr_ref, wpa_ref, wpr_ref, wo_ref, g_ref, b_ref,
                  o_ref, op_ref):
    tm = h_ref.shape[0]
    pieces = [slice(p * MERGE_PIECE, (p + 1) * MERGE_PIECE) for p in range(tm // MERGE_PIECE)]
    branches = [(jnp.dot(attn_ref[rows, :], wpa_ref[...], preferred_element_type=F32),
                 jnp.dot(rec_ref[rows, :], wpr_ref[...], preferred_element_type=F32)) for rows in pieces]
    merged = [(_sigmoid(ga_ref[rows, :].astype(F32)) * a + _sigmoid(gr_ref[rows, :].astype(F32)) * r).astype(BF16)
              for rows, (a, r) in zip(pieces, branches)]
    ys = [jnp.dot(m, wo_ref[...], preferred_element_type=F32) for m in merged]
    for rows, y in zip(pieces, ys):
        h1 = _layer_norm_rows(DEEPNORM_ALPHA * h_ref[rows, :] + y, g_ref[...], b_ref[...])
        o_ref[rows, :] = h1
        op_ref[0, rows, :] = _pack_pair(h1[:, 0 * QUARTER:1 * QUARTER], h1[:, 1 * QUARTER:2 * QUARTER])
        op_ref[1, rows, :] = _pack_pair(h1[:, 2 * QUARTER:3 * QUARTER], h1[:, 3 * QUARTER:4 * QUARTER])


def merge_outproj_ln(h, attn, rec, proj, ga_col, gr_col, wpa, wpr, wo, g, b):
    n, d = h.shape
    tm = MERGE_ROWS
    row = lambda i: (i, 0)
    const = lambda i: (0, 0)
    return pl.pallas_call(
        _merge_kernel,
        out_shape=(jax.ShapeDtypeStruct((n, d), F32), jax.ShapeDtypeStruct((2, n, QUARTER), U32)),
        grid=(n // tm,),
        in_specs=[pl.BlockSpec((tm, d), row), pl.BlockSpec((tm, d), row), pl.BlockSpec((tm, d), row),
                  pl.BlockSpec((tm, d), lambda i: (i, ga_col // d)),
                  pl.BlockSpec((tm, d), lambda i: (i, gr_col // d)),
                  pl.BlockSpec((d, d), const), pl.BlockSpec((d, d), const), pl.BlockSpec((d, d), const),
                  pl.BlockSpec((1, d), const), pl.BlockSpec((1, d), const)],
        out_specs=(pl.BlockSpec((tm, d), row), pl.BlockSpec((2, tm, QUARTER), lambda i: (0, i, 0))),
        compiler_params=_params("parallel"),
        name="merge_outproj_ln",
    )(h, attn, rec, proj, proj, wpa, wpr, wo, g.reshape(1, d), b.reshape(1, d))


def _router_kernel(h_ref, whi_ref, wlo_ref, bias_ref, idx_ref, gate_ref, rank_ref, cnt_ref, carry_ref):
    @pl.when(pl.program_id(0) == 0)
    def _():
        carry_ref[...] = jnp.zeros_like(carry_ref)

    tm = h_ref.shape[0]
    h = h_ref[...]
    h_hi = h.astype(BF16)
    h_lo = (h - h_hi.astype(F32)).astype(BF16)
    nt = (((1,), (1,)), ((), ()))
    logits = (lax.dot_general(whi_ref[...], h_hi, nt, preferred_element_type=F32)
              + lax.dot_general(whi_ref[...], h_lo, nt, preferred_element_type=F32)
              + lax.dot_general(wlo_ref[...], h_hi, nt, preferred_element_type=F32))
    scores = _sigmoid(logits)
    sel = scores + bias_ref[...]
    e_iota = lax.broadcasted_iota(I32, (N_EXPERTS, tm), 0)

    g_iota = lax.broadcasted_iota(I32, (N_GROUPS, tm), 0)
    l_iota = lax.broadcasted_iota(I32, (GROUP_SIZE, tm), 0)
    grp = jnp.zeros((N_GROUPS, tm), F32)
    for g in range(N_GROUPS):
        sg = sel[g * GROUP_SIZE:(g + 1) * GROUP_SIZE, :]
        m1 = jnp.max(sg, axis=0, keepdims=True)
        i1 = jnp.min(jnp.where(sg == m1, l_iota, GROUP_SIZE), axis=0, keepdims=True)
        m2 = jnp.max(jnp.where(l_iota == i1, NEG_INF, sg), axis=0, keepdims=True)
        grp = jnp.where(g_iota == g, m1 + m2, grp)
    gsel = jnp.zeros((N_GROUPS, tm), I32)
    for _ in range(TOPK_GROUPS):
        m = jnp.max(grp, axis=0, keepdims=True)
        gi = jnp.min(jnp.where(grp == m, g_iota, N_GROUPS), axis=0, keepdims=True)
        hit = g_iota == gi
        gsel = jnp.where(hit, 1, gsel)
        grp = jnp.where(hit, NEG_INF, grp)
    masked = []
    for g in range(N_GROUPS):
        sg = sel[g * GROUP_SIZE:(g + 1) * GROUP_SIZE, :]
        masked.append(jnp.where(gsel[g:g + 1, :] > 0, sg, NEG_INF))
    selm = jnp.concatenate(masked, axis=0)

    k_iota = lax.broadcasted_iota(I32, (TOP_K, tm), 0)
    idx = jnp.zeros((TOP_K, tm), I32)
    gate = jnp.zeros((TOP_K, tm), F32)
    member = jnp.zeros((N_EXPERTS, tm), F32)
    for k in range(TOP_K):
        m = jnp.max(selm, axis=0, keepdims=True)
        ei = jnp.min(jnp.where(selm == m, e_iota, N_EXPERTS), axis=0, keepdims=True)
        hit = e_iota == ei
        gk = jnp.sum(jnp.where(hit, scores, 0.0), axis=0, keepdims=True)
        idx = jnp.where(k_iota == k, ei, idx)
        gate = jnp.where(k_iota == k, gk, gate)
        member = jnp.where(hit, 1.0, member)
        selm = jnp.where(hit, NEG_INF, selm)
    gate = gate / jnp.sum(gate, axis=0, keepdims=True) * ROUTED_SCALE

    upper = lax.broadcasted_iota(I32, (tm, tm), 0) < lax.broadcasted_iota(I32, (tm, tm), 1)
    before = jnp.dot(member.astype(BF16), upper.astype(BF16), preferred_element_type=F32) + carry_ref[...]
    rank = jnp.zeros((TOP_K, tm), F32)
    for k in range(TOP_K):
        rk = jnp.sum(jnp.where(e_iota == idx[k:k + 1, :], before, 0.0), axis=0, keepdims=True)
        rank = jnp.where(k_iota == k, rk, rank)
    carry_ref[...] = carry_ref[...] + jnp.sum(member, axis=1, keepdims=True)

    idx_ref[...] = idx
    gate_ref[...] = gate
    rank_ref[...] = rank.astype(I32)
    cnt_ref[...] = jnp.broadcast_to(carry_ref[...], cnt_ref.shape).astype(I32)


def router(h, w_t_hi, w_t_lo, bias):
    n, d = h.shape
    tm = ROUTER_ROWS
    tok = lambda i: (0, i)
    const = lambda i: (0, 0)
    return pl.pallas_call(
        _router_kernel,
        out_shape=(jax.ShapeDtypeStruct((TOP_K, n), I32),
                   jax.ShapeDtypeStruct((TOP_K, n), F32),
                   jax.ShapeDtypeStruct((TOP_K, n), I32),
                   jax.ShapeDtypeStruct((N_EXPERTS, 128), I32)),
        grid=(n // tm,),
        in_specs=[pl.BlockSpec((tm, d), lambda i: (i, 0)),
                  pl.BlockSpec((N_EXPERTS, d), const),
                  pl.BlockSpec((N_EXPERTS, d), const),
                  pl.BlockSpec((N_EXPERTS, 1), const)],
        out_specs=(pl.BlockSpec((TOP_K, tm), tok), pl.BlockSpec((TOP_K, tm), tok),
                   pl.BlockSpec((TOP_K, tm), tok), pl.BlockSpec((N_EXPERTS, 128), const)),
        scratch_shapes=[pltpu.VMEM((N_EXPERTS, 1), F32)],
        compiler_params=_params("arbitrary"),
        name="router",
    )(h, w_t_hi, w_t_lo, bias.reshape(N_EXPERTS, 1))


def _slot_pos_kernel(idx_ref, rank_ref, start_ref, pos_ref):
    tm = idx_ref.shape[1]
    e_iota = lax.broadcasted_iota(I32, (N_EXPERTS, tm), 0)
    k_iota = lax.broadcasted_iota(I32, (TOP_K, tm), 0)
    idx = idx_ref[...]
    start = start_ref[...]
    base = jnp.zeros((TOP_K, tm), F32)
    for k in range(TOP_K):
        bk = jnp.sum(jnp.where(e_iota == idx[k:k + 1, :], start, 0.0), axis=0, keepdims=True)
        base = jnp.where(k_iota == k, bk, base)
    pos_ref[...] = base.astype(I32) + rank_ref[...]


def slot_positions(idx_t, rank_t, pad_start):
    n = idx_t.shape[1]
    tm = 1024
    tok = lambda i: (0, i)
    return pl.pallas_call(
        _slot_pos_kernel,
        out_shape=jax.ShapeDtypeStruct((TOP_K, n), I32),
        grid=(n // tm,),
        in_specs=[pl.BlockSpec((TOP_K, tm), tok), pl.BlockSpec((TOP_K, tm), tok),
                  pl.BlockSpec((N_EXPERTS, 1), lambda i: (0, 0))],
        out_specs=pl.BlockSpec((TOP_K, tm), tok),
        compiler_params=_params("parallel"),
        name="slot_positions",
    )(idx_t, rank_t, pad_start.astype(F32).reshape(N_EXPERTS, 1))


SC_WINDOW = 128
SC_WORDS = QUARTER


def _sc_mesh():
    return plsc.VectorSubcoreMesh(core_axis_name="core", subcore_axis_name="subcore")


def sc_scatter_rows(src, pos_t, rows):
    n = src.shape[1]
    src2 = src.reshape(2 * n, SC_WORDS)
    idx2 = jnp.concatenate([pos_t, pos_t + rows], axis=1)

    @pl.kernel(out_type=jax.ShapeDtypeStruct((2 * rows, SC_WORDS), src.dtype), mesh=_sc_mesh(), scratch_types=[])
    def scatter_kernel(x_hbm, i_hbm, o_hbm):
        def body(x_vmem, i_vmem):
            pltpu.sync_copy(x_vmem, o_hbm.at[i_vmem.at[0]])

        pltpu.emit_pipeline(
            body,
            grid=(2 * n // SC_WINDOW, TOP_K),
            in_specs=[pl.BlockSpec((SC_WINDOW, SC_WORDS), index_map=lambda i, k: (i, 0)),
                      pl.BlockSpec((1, SC_WINDOW), index_map=lambda i, k: (k, i))],
            out_specs=[],
            core_axis_name=("core", "subcore"),
            dimension_semantics=(pltpu.PARALLEL, pltpu.ARBITRARY),
        )(x_hbm, i_hbm)

    return scatter_kernel(src2, idx2).reshape(2, rows, SC_WORDS)


def sc_gather_rows(src, pos):
    r = src.shape[1]
    m = pos.shape[0]
    src2 = src.reshape(2 * r, SC_WORDS)
    idx2 = jnp.concatenate([pos, pos + r]).reshape(1, 2 * m)

    @pl.kernel(out_type=jax.ShapeDtypeStruct((2 * m, SC_WORDS), src.dtype), mesh=_sc_mesh(), scratch_types=[])
    def gather_kernel(x_hbm, i_hbm, o_hbm):
        def body(i_vmem, o_vmem):
            pltpu.sync_copy(x_hbm.at[i_vmem.at[0]], o_vmem)

        pltpu.emit_pipeline(
            body,
            grid=(2 * m // SC_WINDOW,),
            in_specs=[pl.BlockSpec((1, SC_WINDOW), index_map=lambda i: (0, i))],
            out_specs=[pl.BlockSpec((SC_WINDOW, SC_WORDS), index_map=lambda i: (i, 0))],
            core_axis_name=("core", "subcore"),
            dimension_semantics=(pltpu.PARALLEL,),
        )(i_hbm, o_hbm)

    return gather_kernel(src2, idx2).reshape(2, m, SC_WORDS)


def _expert_kernel(blk_expert_ref, blk_valid_ref, n_used_ref, x_ref, wgu_ref, wd_ref, y_ref):
    del blk_expert_ref
    j = pl.program_id(0)
    used = j < n_used_ref[0]

    @pl.when(used)
    def _():
        valid = lax.broadcasted_iota(I32, x_ref.shape[1:], 0) < blk_valid_ref[j]
        quarters = _load_quarters(jnp.where(valid, x_ref[0], U32(0)), jnp.where(valid, x_ref[1], U32(0)))
        gu = sum(jnp.dot(xq.astype(BF16), wgu_ref[0, 0, c * QUARTER:(c + 1) * QUARTER, :].astype(BF16),
                         preferred_element_type=F32)
                 for c, xq in enumerate(quarters))
        act = (_silu(gu[:, :EXPERT_FF]) * gu[:, EXPERT_FF:]).astype(BF16)
        _store_planes(y_ref, jnp.dot(act, wd_ref[0, 0].astype(BF16), preferred_element_type=F32))

    @pl.when(jnp.logical_not(used))
    def _():
        y_ref[...] = jnp.zeros_like(y_ref)


def expert_ffn(xs, blk_expert, blk_valid, n_used, w_gu, w_down, layer):
    _, rows, w = xs.shape
    d = D_MODEL
    n_blocks = rows // ROW_BLOCK

    def row_map(j, be, bv, nu):
        return (0, jnp.minimum(j, nu[0] - 1), 0)

    def w_map(j, be, bv, nu):
        return (layer, be[jnp.minimum(j, nu[0] - 1)], 0, 0)

    return pl.pallas_call(
        _expert_kernel,
        out_shape=jax.ShapeDtypeStruct(xs.shape, U32),
        grid_spec=pltpu.PrefetchScalarGridSpec(
            num_scalar_prefetch=3,
            grid=(n_blocks,),
            in_specs=[pl.BlockSpec((2, ROW_BLOCK, w), row_map),
                      pl.BlockSpec((1, 1, d, 2 * EXPERT_FF), w_map),
                      pl.BlockSpec((1, 1, EXPERT_FF, d), w_map)],
            out_specs=pl.BlockSpec((2, ROW_BLOCK, w), lambda j, be, bv, nu: (0, j, 0))),
        compiler_params=_params("arbitrary"),
        name="expert_ffn",
    )(blk_expert, blk_valid, n_used, xs, w_gu, w_down)


def _combine_kernel(h_ref, gate_ref, ys_ref, sgu_ref, sd_ref, g_ref, b_ref, *refs):
    o_ref, ob_ref = refs[-2:]
    h = h_ref[...]
    gu = jnp.dot(h.astype(BF16), sgu_ref[...], preferred_element_type=F32)
    act = _silu(gu[:, :EXPERT_FF]) * gu[:, EXPERT_FF:]
    y = jnp.dot(act.astype(BF16), sd_ref[...], preferred_element_type=F32)
    gate = gate_ref[...]
    acc = [y[:, c * QUARTER:(c + 1) * QUARTER] for c in range(4)]
    for k in range(TOP_K):
        gk = gate[:, k:k + 1]
        acc = [a + gk * q for a, q in zip(acc, _load_quarters(ys_ref[0, k], ys_ref[1, k]))]
    out = _layer_norm_rows(DEEPNORM_ALPHA * h + jnp.concatenate(acc, axis=-1), g_ref[...], b_ref[...])
    o_ref[...] = out
    ob_ref[...] = out.astype(BF16)


def combine_shared_ln(h, gate, y_slots, chunk, prev_out, sh_gu, sh_down, g, b):
    n, d = h.shape
    tm = MOE_ROWS
    steps = y_slots.shape[2] // tm
    row = lambda i: (i + chunk * steps, 0)
    const = lambda i: (0, 0)
    passthrough = () if prev_out is None else tuple(prev_out)
    n_in = 7
    return pl.pallas_call(
        _combine_kernel,
        out_shape=(jax.ShapeDtypeStruct((n, d), F32), jax.ShapeDtypeStruct((n, d), BF16)),
        grid=(steps,),
        in_specs=[pl.BlockSpec((tm, d), row),
                  pl.BlockSpec((tm, TOP_K), row),
                  pl.BlockSpec((2, TOP_K, tm, QUARTER), lambda i: (0, 0, i, 0)),
                  pl.BlockSpec((d, 2 * EXPERT_FF), const),
                  pl.BlockSpec((EXPERT_FF, d), const),
                  pl.BlockSpec((1, d), const), pl.BlockSpec((1, d), const)]
                 + [pl.BlockSpec(memory_space=pl.ANY) for _ in passthrough],
        out_specs=(pl.BlockSpec((tm, d), row), pl.BlockSpec((tm, d), row)),
        input_output_aliases={n_in + i: i for i in range(len(passthrough))},
        compiler_params=_params("parallel"),
        name="combine_shared_ln",
    )(h, gate, y_slots, sh_gu, sh_down, g.reshape(1, d), b.reshape(1, d), *passthrough)


_MAIN_BLOCKS = (0, 1, 3, 4, 7, 8, 9, 10, 11, 12, 13, 14, 2)
_FORGET_BLOCKS = (5, 6)
_Q_A, _Q_R, _I_R, _G_R, _GATE_A, _GATE_R = (i * D_MODEL for i in range(6))
_K_A = 6 * D_MODEL
_V_A = _K_A + KV_WIDTH


def kernel(x, ln_in_g, ln_in_b, lb_logits, w_in, b_in, attn_sinks, rec_norm_g, w_proj_attn, w_proj_rec, w_out,
           ln1_g, ln1_b, router_w, router_bias, expert_w_gu, expert_w_down, shared_w_gu, shared_w_down,
           ln2_g, ln2_b):
    batch, seq, d = x.shape
    n = batch * seq
    depth = w_in.shape[0]
    n_blocks = n * TOP_K // ROW_BLOCK + N_EXPERTS
    rows = n_blocks * ROW_BLOCK

    p = jax.nn.softmax(lb_logits.astype(F32), axis=0)
    cum = jnp.cumsum(p, axis=0)
    lower = cum - cum[0:1]
    log_lb = jnp.log(lower)
    log1m_lb = jnp.log1p(-lower)

    w_in_bf = w_in.astype(BF16)
    h, hb = layer_norm_in(x.reshape(n, d), ln_in_g, ln_in_b)
    for l in range(depth):
        proj = in_proj(hb, w_in_bf, b_in, l, _MAIN_BLOCKS, BF16, "in_proj_main")
        proj_f = in_proj(hb, w_in_bf, b_in, l, _FORGET_BLOCKS, F32, "in_proj_forget")
        attn = swa_attention(proj, attn_sinks[l], batch, seq, _Q_A, _K_A, _V_A)
        rec = hgrn2(proj, proj_f, log_lb[l], log1m_lb[l], rec_norm_g[l], batch, seq, _Q_R, _I_R, _G_R)
        h, hp = merge_outproj_ln(h, attn, rec, proj, _GATE_A, _GATE_R,
                                 w_proj_attn[l].astype(BF16), w_proj_rec[l].astype(BF16), w_out[l].astype(BF16),
                                 ln1_g[l], ln1_b[l])

        rw_t = router_w[l].T
        rw_hi = rw_t.astype(BF16)
        rw_lo = (rw_t - rw_hi.astype(F32)).astype(BF16)
        idx_t, gate_t, rank_t, cnt = router(h, rw_hi, rw_lo, router_bias[l])
        counts = cnt[:, 0]
        padded = (counts + ROW_BLOCK - 1) // ROW_BLOCK * ROW_BLOCK
        pad_end = jnp.cumsum(padded)
        pad_start = pad_end - padded
        pos_t = slot_positions(idx_t, rank_t, pad_start)
        blk_row = jnp.arange(n_blocks, dtype=I32) * ROW_BLOCK
        blk_expert = jnp.minimum(jnp.sum(pad_end[None, :] <= blk_row[:, None], axis=1), N_EXPERTS - 1).astype(I32)
        blk_valid = jnp.clip(counts[blk_expert] - (blk_row - pad_start[blk_expert]), 0, ROW_BLOCK).astype(I32)
        n_used = (pad_end[-1:] // ROW_BLOCK).astype(I32)

        xs = sc_scatter_rows(hp, pos_t, rows)
        ys = expert_ffn(xs, blk_expert, blk_valid, n_used, expert_w_gu, expert_w_down, l)
        nc = n // MOE_CHUNKS
        gate, out = gate_t.T, None
        for c in range(MOE_CHUNKS):
            pos_c = lax.slice_in_dim(pos_t, c * nc, (c + 1) * nc, axis=1).reshape(TOP_K * nc)
            y_slots = sc_gather_rows(ys, pos_c).reshape(2, TOP_K, nc, QUARTER)
            out = combine_shared_ln(h, gate, y_slots, c, out, shared_w_gu[l].astype(BF16),
                                    shared_w_down[l].astype(BF16), ln2_g[l], ln2_b[l])
        h, hb = out
    return h.reshape(batch, seq, d)
```

```python
import jax
import jax.numpy as jnp
from jax import lax
from jax.experimental import pallas as pl
from jax.experimental.pallas import tpu as pltpu
from jax.experimental.pallas import tpu_sc as plsc

F32 = jnp.float32
BF16 = jnp.bfloat16
U32 = jnp.uint32
I32 = jnp.int32

D_MODEL = 1024
QUARTER = D_MODEL // 4
CHUNK = 64
ATTN_HEADS = 16
ATTN_KV_HEADS = 4
ATTN_HEAD_DIM = 64
ATTN_GROUP = ATTN_HEADS // ATTN_KV_HEADS
WIN_CHUNKS = 2
KV_WIDTH = ATTN_KV_HEADS * ATTN_HEAD_DIM
REC_HEADS = 8
REC_DIM = 128
N_EXPERTS = 256
TOP_K = 8
N_GROUPS = 8
GROUP_SIZE = N_EXPERTS // N_GROUPS
TOPK_GROUPS = 4
EXPERT_FF = 256
ROUTED_SCALE = 2.5
DEPTH = 2
DEEPNORM_ALPHA = (2 * DEPTH) ** 0.25
LN_EPS = 1e-5
RMS_EPS = 1e-5
NEG_INF = float("-inf")

SUBLANES = 8
VMEM_LIMIT_BYTES = 48 * 1024 * 1024

LN_ROWS = 512
PROJ_ROWS = 4096
PROJ_COLS = 512
ATTN_Q_BLOCK = 256
ATTN_STAGE_LAG = 3
REC_TILE = 256
REC_BLOCK = 64
REC_SUB = 16
MERGE_ROWS = 512
MERGE_PIECE = 256
ROUTER_ROWS = 256
MOE_ROWS = 256
MOE_CHUNKS = 4
ROW_BLOCK = 512

DECAY_LIMIT = 60.0


def _params(*sem):
    return pltpu.CompilerParams(dimension_semantics=sem, vmem_limit_bytes=VMEM_LIMIT_BYTES)


def _layer_norm_rows(x, g, b):
    mu = jnp.mean(x, axis=-1, keepdims=True)
    xc = x - mu
    var = jnp.mean(xc * xc, axis=-1, keepdims=True)
    return xc * lax.rsqrt(var + LN_EPS) * g + b


def _sigmoid(x):
    return 1.0 / (1.0 + jnp.exp(-x))


def _silu(x):
    return x * _sigmoid(x)


def _pack_pair(lo, hi):
    lo_bits = pltpu.bitcast(lo.astype(BF16).astype(F32), U32)
    hi_bits = pltpu.bitcast(hi.astype(BF16).astype(F32), U32)
    return lax.shift_right_logical(lo_bits, U32(16)) | (hi_bits & U32(0xFFFF0000))


def _unpack_pair(w):
    lo = pltpu.bitcast(lax.shift_left(w, U32(16)), F32)
    hi = pltpu.bitcast(w & U32(0xFFFF0000), F32)
    return lo, hi


def _store_planes(ref, x):
    q = QUARTER
    ref[0] = _pack_pair(x[:, 0 * q:1 * q], x[:, 1 * q:2 * q])
    ref[1] = _pack_pair(x[:, 2 * q:3 * q], x[:, 3 * q:4 * q])


def _load_quarters(plane0, plane1):
    return _unpack_pair(plane0) + _unpack_pair(plane1)


def _ln_in_kernel(x_ref, g_ref, b_ref, h_ref, hb_ref):
    h = _layer_norm_rows(x_ref[...], g_ref[...], b_ref[...])
    h_ref[...] = h
    hb_ref[...] = h.astype(BF16)


def layer_norm_in(x, g, b):
    n, d = x.shape
    row = lambda i: (i, 0)
    const = lambda i: (0, 0)
    return pl.pallas_call(
        _ln_in_kernel,
        out_shape=(jax.ShapeDtypeStruct((n, d), F32), jax.ShapeDtypeStruct((n, d), BF16)),
        grid=(n // LN_ROWS,),
        in_specs=[pl.BlockSpec((LN_ROWS, d), row), pl.BlockSpec((1, d), const), pl.BlockSpec((1, d), const)],
        out_specs=(pl.BlockSpec((LN_ROWS, d), row), pl.BlockSpec((LN_ROWS, d), row)),
        compiler_params=_params("parallel"),
        name="ln_in",
    )(x, g.reshape(1, d), b.reshape(1, d))


def _in_proj_kernel(perm_ref, x_ref, w_ref, b_ref, o_ref):
    del perm_ref
    acc = jnp.dot(x_ref[...], w_ref[0], preferred_element_type=F32)
    o_ref[...] = (acc + b_ref[0]).astype(o_ref.dtype)


def in_proj(xb, w, b, layer, col_blocks, out_dtype, name):
    n, k = xb.shape
    perm = jnp.asarray(col_blocks, I32)
    nblk = len(col_blocks)
    tm = min(PROJ_ROWS, n)
    return pl.pallas_call(
        _in_proj_kernel,
        out_shape=jax.ShapeDtypeStruct((n, nblk * PROJ_COLS), out_dtype),
        grid_spec=pltpu.PrefetchScalarGridSpec(
            num_scalar_prefetch=1,
            grid=(n // tm, nblk),
            in_specs=[pl.BlockSpec((tm, k), lambda i, j, p: (i, 0)),
                      pl.BlockSpec((1, k, PROJ_COLS), lambda i, j, p: (layer, 0, p[j])),
                      pl.BlockSpec((1, 1, PROJ_COLS), lambda i, j, p: (layer, 0, p[j]))],
            out_specs=pl.BlockSpec((tm, PROJ_COLS), lambda i, j, p: (i, j))),
        compiler_params=_params("parallel", "arbitrary"),
        name=name,
    )(perm, xb, w, b.reshape(b.shape[0], 1, -1))


def _attn_kernel(sink_ref, q_ref, kp_ref, kc_ref, vp_ref, vc_ref, o_ref, s_ref, e_ref):
    i = pl.program_id(1)
    half = ATTN_Q_BLOCK // 2
    qc = lax.broadcasted_iota(I32, (half, ATTN_Q_BLOCK), 0) // CHUNK
    kc = lax.broadcasted_iota(I32, (half, ATTN_Q_BLOCK), 1) // CHUNK
    valid = (kc >= qc) & (kc <= qc + WIN_CHUNKS)
    first = jnp.where(i == 0, WIN_CHUNKS, 0)
    scale = ATTN_HEAD_DIM ** -0.5
    windows = ((jnp.concatenate([kp_ref[half:, :], kc_ref[:half, :]], axis=0),
                jnp.concatenate([vp_ref[half:, :], vc_ref[:half, :]], axis=0),
                jnp.where(valid & (kc >= first), 0.0, NEG_INF)),
               (kc_ref[...], vc_ref[...], jnp.where(valid, 0.0, NEG_INF)))
    for part, (k, _, mask_bias) in enumerate(windows):
        rows = slice(part * half, (part + 1) * half)
        for h in range(ATTN_HEADS):
            kv = h // ATTN_GROUP
            qh = q_ref[rows, h * ATTN_HEAD_DIM:(h + 1) * ATTN_HEAD_DIM] * scale
            kh = k[:, kv * ATTN_HEAD_DIM:(kv + 1) * ATTN_HEAD_DIM]
            s_ref[part * ATTN_HEADS + h] = (
                lax.dot_general(qh, kh, (((1,), (1,)), ((), ())), preferred_element_type=F32) + mask_bias)
    units = [(part, h) for part in range(2) for h in range(ATTN_HEADS)]

    def row_max(part, h):
        return jnp.maximum(jnp.max(s_ref[part * ATTN_HEADS + h], axis=-1, keepdims=True), sink_ref[h])

    def exponentials(part, h, m):
        e = jnp.exp(s_ref[part * ATTN_HEADS + h] - m)
        e_ref[part * ATTN_HEADS + h] = e.astype(BF16)
        return 1.0 / (jnp.sum(e, axis=-1, keepdims=True) + jnp.exp(sink_ref[h] - m))

    def values(part, h, rden):
        kv = h // ATTN_GROUP
        vh = windows[part][1][:, kv * ATTN_HEAD_DIM:(kv + 1) * ATTN_HEAD_DIM]
        oh = jnp.dot(e_ref[part * ATTN_HEADS + h], vh, preferred_element_type=F32) * rden
        rows = slice(part * half, (part + 1) * half)
        o_ref[rows, h * ATTN_HEAD_DIM:(h + 1) * ATTN_HEAD_DIM] = oh.astype(o_ref.dtype)

    maxes, rdens = {}, {}
    lag = ATTN_STAGE_LAG
    for t in range(len(units) + 2 * lag):
        if t < len(units):
            maxes[t] = row_max(*units[t])
        if 0 <= t - lag < len(units):
            rdens[t - lag] = exponentials(*units[t - lag], maxes.pop(t - lag))
        if 0 <= t - 2 * lag < len(units):
            values(*units[t - 2 * lag], rdens.pop(t - 2 * lag))


def swa_attention(proj, sinks, batch, seq, q_col, k_col, v_col):
    n = batch * seq
    nb = seq // ATTN_Q_BLOCK
    qb, kb, vb = q_col // D_MODEL, k_col // KV_WIDTH, v_col // KV_WIDTH

    def cur(col):
        return lambda b, i, s: (b * nb + i, col)

    def prev(col):
        return lambda b, i, s: (b * nb + jnp.maximum(i - 1, 0), col)

    return pl.pallas_call(
        _attn_kernel,
        out_shape=jax.ShapeDtypeStruct((n, D_MODEL), BF16),
        grid_spec=pltpu.PrefetchScalarGridSpec(
            num_scalar_prefetch=1,
            grid=(batch, nb),
            in_specs=[pl.BlockSpec((ATTN_Q_BLOCK, D_MODEL), cur(qb)),
                      pl.BlockSpec((ATTN_Q_BLOCK, KV_WIDTH), prev(kb)),
                      pl.BlockSpec((ATTN_Q_BLOCK, KV_WIDTH), cur(kb)),
                      pl.BlockSpec((ATTN_Q_BLOCK, KV_WIDTH), prev(vb)),
                      pl.BlockSpec((ATTN_Q_BLOCK, KV_WIDTH), cur(vb))],
            out_specs=pl.BlockSpec((ATTN_Q_BLOCK, D_MODEL), lambda b, i, s: (b * nb + i, 0)),
            scratch_shapes=[pltpu.VMEM((2 * ATTN_HEADS, ATTN_Q_BLOCK // 2, ATTN_Q_BLOCK), F32),
                            pltpu.VMEM((2 * ATTN_HEADS, ATTN_Q_BLOCK // 2, ATTN_Q_BLOCK), BF16)]),
        compiler_params=_params("parallel", "parallel"),
        name="swa_attention",
    )(sinks.astype(F32), proj, proj, proj, proj, proj)


def _hgrn_kernel(q_ref, i_ref, g_ref, f_ref, loglb_ref, log1mlb_ref, ng_ref, o_ref,
                 b_ref, k_ref, qs_ref, lf_ref, intra_ref, inter_ref, qd_ref, kt_ref, *st_ref):
    @pl.when(pl.program_id(1) == 0)
    def _():
        for s_ref in st_ref:
            s_ref[...] = jnp.zeros_like(s_ref)

    fl = f_ref[...]
    log_sig = jnp.minimum(fl, 0.0) - jnp.log(1.0 + jnp.exp(-jnp.abs(fl)))
    a = loglb_ref[...]
    c = log1mlb_ref[...] + log_sig
    log_f = jnp.maximum(a, c) + jnp.log(1.0 + jnp.exp(-jnp.abs(a - c)))
    lf_ref[...] = log_f
    k_ref[...] = 1.0 - jnp.exp(log_f)
    qs_ref[...] = _silu(q_ref[...].astype(F32))

    def prefix_sum(x, width):
        pos = lax.broadcasted_iota(I32, x.shape, 0) % width
        shift = 1
        while shift < width:
            x = x + jnp.where(pos >= shift, pltpu.roll(x, shift, axis=0), 0.0)
            shift *= 2
        return x

    half = REC_BLOCK // 2
    ti = lax.broadcasted_iota(I32, (REC_TILE, REC_TILE), 0)
    si = lax.broadcasted_iota(I32, (REC_TILE, REC_TILE), 1)
    tm, sm = ti % REC_BLOCK, si % REC_BLOCK
    same = ti // REC_BLOCK == si // REC_BLOCK
    plus = same & (tm >= half) & (sm >= half) & (sm <= tm)
    minus = same & (tm < half) & (sm > tm) & (sm < half)
    signed = jnp.where(plus, 1.0, jnp.where(minus, -1.0, 0.0)).astype(BF16)
    hi = log_f.astype(BF16)
    rest = log_f - hi.astype(F32)
    mid = rest.astype(BF16)
    lo = (rest - mid.astype(F32)).astype(BF16)
    d = (jnp.dot(signed, hi, preferred_element_type=F32) + jnp.dot(signed, mid, preferred_element_type=F32)
         + jnp.dot(signed, lo, preferred_element_type=F32))
    b_ref[...] = d
    decay_bounded = jnp.max(jnp.abs(d)) <= DECAY_LIMIT

    t_iota = lax.broadcasted_iota(I32, (REC_SUB, REC_DIM), 0)
    nt = (((1,), (1,)), ((), ()))
    tn = (((0,), (0,)), ((), ()))

    def finish(o, r0, cols):
        ms = jnp.mean(o * o, axis=-1, keepdims=True)
        o = o * lax.rsqrt(ms + RMS_EPS) * ng_ref[...]
        gj = g_ref[pl.ds(r0, REC_SUB), cols].astype(F32)
        o_ref[pl.ds(r0, REC_SUB), cols] = (o * _silu(gj)).astype(o_ref.dtype)

    @pl.when(decay_bounded)
    def _():
        qd_ref[...] = (qs_ref[...] * jnp.exp(b_ref[...])).astype(BF16)
        kt_ref[...] = (k_ref[...] * jnp.exp(-b_ref[...])).astype(BF16)
        ri = lax.broadcasted_iota(I32, (REC_TILE, REC_TILE), 0)
        ci = lax.broadcasted_iota(I32, (REC_TILE, REC_TILE), 1)
        keep = (ri >= ci) & (ri // REC_BLOCK == ci // REC_BLOCK)
        for h in range(REC_HEADS):
            cols = slice(h * REC_DIM, (h + 1) * REC_DIM)
            att = lax.dot_general(qd_ref[:, cols], kt_ref[:, cols], nt, preferred_element_type=F32)
            att = jnp.where(keep, att, 0.0).astype(BF16)
            intra_ref[:, cols] = jnp.dot(att, i_ref[:, cols], preferred_element_type=F32)

        def step(j, carry):
            r0 = pl.multiple_of(j * REC_BLOCK, REC_BLOCK)
            for h in range(REC_HEADS):
                cols = slice(h * REC_DIM, (h + 1) * REC_DIM)
                qd = qd_ref[pl.ds(r0, REC_BLOCK), cols]
                kt = kt_ref[pl.ds(r0, REC_BLOCK), cols]
                vj = i_ref[pl.ds(r0, REC_BLOCK), cols]
                last = pl.multiple_of(r0 + REC_BLOCK - SUBLANES, SUBLANES)
                e1 = jnp.exp(lf_ref[pl.ds(r0, SUBLANES), cols][0:1, :] - b_ref[pl.ds(r0, SUBLANES), cols][0:1, :])
                e2 = jnp.exp(b_ref[pl.ds(last, SUBLANES), cols][SUBLANES - 1:SUBLANES, :])
                st_mid = st_ref[h][...] * e1
                inter_ref[pl.ds(r0, REC_BLOCK), cols] = lax.dot_general(
                    qd, st_mid.astype(BF16), nt, preferred_element_type=F32)
                kv_t = lax.dot_general(vj, kt, tn, preferred_element_type=F32)
                st_ref[h][...] = (st_mid + kv_t) * e2
            return carry

        lax.fori_loop(0, REC_TILE // REC_BLOCK, step, 0)

        for h in range(REC_HEADS):
            cols = slice(h * REC_DIM, (h + 1) * REC_DIM)
            o = intra_ref[:, cols] + inter_ref[:, cols]
            ms = jnp.mean(o * o, axis=-1, keepdims=True)
            o = o * lax.rsqrt(ms + RMS_EPS) * ng_ref[...]
            o_ref[:, cols] = (o * _silu(g_ref[:, cols].astype(F32))).astype(o_ref.dtype)

    @pl.when(jnp.logical_not(decay_bounded))
    def _():
        b_ref[...] = prefix_sum(lf_ref[...], REC_SUB)
        ones = jnp.ones((REC_DIM, REC_DIM), BF16)

        def step(j, carry):
            r0 = pl.multiple_of(j * REC_SUB, REC_SUB)
            for h in range(REC_HEADS):
                cols = slice(h * REC_DIM, (h + 1) * REC_DIM)
                bj = b_ref[pl.ds(r0, REC_SUB), cols]
                kj = k_ref[pl.ds(r0, REC_SUB), cols]
                qj = qs_ref[pl.ds(r0, REC_SUB), cols]
                vj = i_ref[pl.ds(r0, REC_SUB), cols].astype(F32)
                st = st_ref[h][...]
                qd = (qj * jnp.exp(bj)).astype(BF16)
                o = lax.dot_general(qd, st.astype(BF16), nt, preferred_element_type=F32)
                parts = []
                for s in range(REC_SUB):
                    dec = jnp.exp(jnp.where(t_iota >= s, bj - bj[s:s + 1, :], NEG_INF))
                    parts.append((qj * dec * kj[s:s + 1, :]).astype(BF16))
                pstack = jnp.concatenate(parts, axis=0)
                rsum = jnp.dot(pstack, ones, preferred_element_type=F32)
                for s in range(REC_SUB):
                    o = o + rsum[s * REC_SUB:(s + 1) * REC_SUB, :] * vj[s:s + 1, :]
                b_end = bj[REC_SUB - 1:REC_SUB, :]
                kd = (kj * jnp.exp(b_end - bj)).astype(BF16)
                kv_t = lax.dot_general(vj.astype(BF16), kd, tn, preferred_element_type=F32)
                st_ref[h][...] = st * jnp.exp(b_end) + kv_t
                finish(o, r0, cols)
            return carry

        lax.fori_loop(0, REC_TILE // REC_SUB, step, 0)


def hgrn2(proj, proj_f, log_lb, log1m_lb, norm_g, batch, seq, q_col, i_col, g_col):
    n = batch * seq
    nb = seq // REC_TILE
    d = D_MODEL

    def blk(col):
        return pl.BlockSpec((REC_TILE, d), lambda b, i: (b * nb + i, col // d))

    return pl.pallas_call(
        _hgrn_kernel,
        out_shape=jax.ShapeDtypeStruct((n, d), BF16),
        grid=(batch, nb),
        in_specs=[blk(q_col), blk(i_col), blk(g_col),
                  pl.BlockSpec((REC_TILE, d), lambda b, i: (b * nb + i, 0)),
                  pl.BlockSpec((1, d), lambda b, i: (0, 0)),
                  pl.BlockSpec((1, d), lambda b, i: (0, 0)),
                  pl.BlockSpec((1, REC_DIM), lambda b, i: (0, 0))],
        out_specs=pl.BlockSpec((REC_TILE, d), lambda b, i: (b * nb + i, 0)),
        scratch_shapes=[pltpu.VMEM((REC_TILE, d), F32) for _ in range(6)]
                       + [pltpu.VMEM((REC_TILE, d), BF16),
                        pltpu.VMEM((REC_TILE, d), BF16)]
                       + [pltpu.VMEM((REC_DIM, REC_DIM), F32) for _ in range(REC_HEADS)],
        compiler_params=_params("parallel", "arbitrary"),
        name="hgrn2",
    )(proj, proj, proj, proj_f, log_lb.reshape(1, d), log1m_lb.reshape(1, d), norm_g.reshape(1, REC_DIM))


def _merge_kernel(h_ref, attn_ref, rec_ref, ga_ref, gr_ref, wpa_ref, wpr_ref, wo_ref, g_ref, b_ref,
                  o_ref, op_ref):
    tm = h_ref.shape[0]
    pieces = [slice(p * MERGE_PIECE, (p + 1) * MERGE_PIECE) for p in range(tm // MERGE_PIECE)]
    branches = [(jnp.dot(attn_ref[rows, :], wpa_ref[...], preferred_element_type=F32),
                 jnp.dot(rec_ref[rows, :], wpr_ref[...], preferred_element_type=F32)) for rows in pieces]
    merged = [(_sigmoid(ga_ref[rows, :].astype(F32)) * a + _sigmoid(gr_ref[rows, :].astype(F32)) * r).astype(BF16)
              for rows, (a, r) in zip(pieces, branches)]
    ys = [jnp.dot(m, wo_ref[...], preferred_element_type=F32) for m in merged]
    for rows, y in zip(pieces, ys):
        h1 = _layer_norm_rows(DEEPNORM_ALPHA * h_ref[rows, :] + y, g_ref[...], b_ref[...])
        o_ref[rows, :] = h1
        op_ref[0, rows, :] = _pack_pair(h1[:, 0 * QUARTER:1 * QUARTER], h1[:, 1 * QUARTER:2 * QUARTER])
        op_ref[1, rows, :] = _pack_pair(h1[:, 2 * QUARTER:3 * QUARTER], h1[:, 3 * QUARTER:4 * QUARTER])


def merge_outproj_ln(h, attn, rec, proj, ga_col, gr_col, wpa, wpr, wo, g, b):
    n, d = h.shape
    tm = MERGE_ROWS
    row = lambda i: (i, 0)
    const = lambda i: (0, 0)
    return pl.pallas_call(
        _merge_kernel,
        out_shape=(jax.ShapeDtypeStruct((n, d), F32), jax.ShapeDtypeStruct((2, n, QUARTER), U32)),
        grid=(n // tm,),
        in_specs=[pl.BlockSpec((tm, d), row), pl.BlockSpec((tm, d), row), pl.BlockSpec((tm, d), row),
                  pl.BlockSpec((tm, d), lambda i: (i, ga_col // d)),
                  pl.BlockSpec((tm, d), lambda i: (i, gr_col // d)),
                  pl.BlockSpec((d, d), const), pl.BlockSpec((d, d), const), pl.BlockSpec((d, d), const),
                  pl.BlockSpec((1, d), const), pl.BlockSpec((1, d), const)],
        out_specs=(pl.BlockSpec((tm, d), row), pl.BlockSpec((2, tm, QUARTER), lambda i: (0, i, 0))),
        compiler_params=_params("parallel"),
        name="merge_outproj_ln",
    )(h, attn, rec, proj, proj, wpa, wpr, wo, g.reshape(1, d), b.reshape(1, d))


def _router_kernel(h_ref, whi_ref, wlo_ref, bias_ref, idx_ref, gate_ref, rank_ref, cnt_ref, carry_ref):
    @pl.when(pl.program_id(0) == 0)
    def _():
        carry_ref[...] = jnp.zeros_like(carry_ref)

    tm = h_ref.shape[0]
    h = h_ref[...]
    h_hi = h.astype(BF16)
    h_lo = (h - h_hi.astype(F32)).astype(BF16)
    nt = (((1,), (1,)), ((), ()))
    logits = (lax.dot_general(whi_ref[...], h_hi, nt, preferred_element_type=F32)
              + lax.dot_general(whi_ref[...], h_lo, nt, preferred_element_type=F32)
              + lax.dot_general(wlo_ref[...], h_hi, nt, preferred_element_type=F32))
    scores = _sigmoid(logits)
    sel = scores + bias_ref[...]
    e_iota = lax.broadcasted_iota(I32, (N_EXPERTS, tm), 0)

    g_iota = lax.broadcasted_iota(I32, (N_GROUPS, tm), 0)
    l_iota = lax.broadcasted_iota(I32, (GROUP_SIZE, tm), 0)
    grp = jnp.zeros((N_GROUPS, tm), F32)
    for g in range(N_GROUPS):
        sg = sel[g * GROUP_SIZE:(g + 1) * GROUP_SIZE, :]
        m1 = jnp.max(sg, axis=0, keepdims=True)
        i1 = jnp.min(jnp.where(sg == m1, l_iota, GROUP_SIZE), axis=0, keepdims=True)
        m2 = jnp.max(jnp.where(l_iota == i1, NEG_INF, sg), axis=0, keepdims=True)
        grp = jnp.where(g_iota == g, m1 + m2, grp)
    gsel = jnp.zeros((N_GROUPS, tm), I32)
    for _ in range(TOPK_GROUPS):
        m = jnp.max(grp, axis=0, keepdims=True)
        gi = jnp.min(jnp.where(grp == m, g_iota, N_GROUPS), axis=0, keepdims=True)
        hit = g_iota == gi
        gsel = jnp.where(hit, 1, gsel)
        grp = jnp.where(hit, NEG_INF, grp)
    masked = []
    for g in range(N_GROUPS):
        sg = sel[g * GROUP_SIZE:(g + 1) * GROUP_SIZE, :]
        masked.append(jnp.where(gsel[g:g + 1, :] > 0, sg, NEG_INF))
    selm = jnp.concatenate(masked, axis=0)

    k_iota = lax.broadcasted_iota(I32, (TOP_K, tm), 0)
    idx = jnp.zeros((TOP_K, tm), I32)
    gate = jnp.zeros((TOP_K, tm), F32)
    member = jnp.zeros((N_EXPERTS, tm), F32)
    for k in range(TOP_K):
        m = jnp.max(selm, axis=0, keepdims=True)
        ei = jnp.min(jnp.where(selm == m, e_iota, N_EXPERTS), axis=0, keepdims=True)
        hit = e_iota == ei
        gk = jnp.sum(jnp.where(hit, scores, 0.0), axis=0, keepdims=True)
        idx = jnp.where(k_iota == k, ei, idx)
        gate = jnp.where(k_iota == k, gk, gate)
        member = jnp.where(hit, 1.0, member)
        selm = jnp.where(hit, NEG_INF, selm)
    gate = gate / jnp.sum(gate, axis=0, keepdims=True) * ROUTED_SCALE

    upper = lax.broadcasted_iota(I32, (tm, tm), 0) < lax.broadcasted_iota(I32, (tm, tm), 1)
    before = jnp.dot(member.astype(BF16), upper.astype(BF16), preferred_element_type=F32) + carry_ref[...]
    rank = jnp.zeros((TOP_K, tm), F32)
    for k in range(TOP_K):
        rk = jnp.sum(jnp.where(e_iota == idx[k:k + 1, :], before, 0.0), axis=0, keepdims=True)
        rank = jnp.where(k_iota == k, rk, rank)
    carry_ref[...] = carry_ref[...] + jnp.sum(member, axis=1, keepdims=True)

    idx_ref[...] = idx
    gate_ref[...] = gate
    rank_ref[...] = rank.astype(I32)
    cnt_ref[...] = jnp.broadcast_to(carry_ref[...], cnt_ref.shape).astype(I32)


def router(h, w_t_hi, w_t_lo, bias):
    n, d = h.shape
    tm = ROUTER_ROWS
    tok = lambda i: (0, i)
    const = lambda i: (0, 0)
    return pl.pallas_call(
        _router_kernel,
        out_shape=(jax.ShapeDtypeStruct((TOP_K, n), I32),
                   jax.ShapeDtypeStruct((TOP_K, n), F32),
                   jax.ShapeDtypeStruct((TOP_K, n), I32),
                   jax.ShapeDtypeStruct((N_EXPERTS, 128), I32)),
        grid=(n // tm,),
        in_specs=[pl.BlockSpec((tm, d), lambda i: (i, 0)),
                  pl.BlockSpec((N_EXPERTS, d), const),
                  pl.BlockSpec((N_EXPERTS, d), const),
                  pl.BlockSpec((N_EXPERTS, 1), const)],
        out_specs=(pl.BlockSpec((TOP_K, tm), tok), pl.BlockSpec((TOP_K, tm), tok),
                   pl.BlockSpec((TOP_K, tm), tok), pl.BlockSpec((N_EXPERTS, 128), const)),
        scratch_shapes=[pltpu.VMEM((N_EXPERTS, 1), F32)],
        compiler_params=_params("arbitrary"),
        name="router",
    )(h, w_t_hi, w_t_lo, bias.reshape(N_EXPERTS, 1))


def _slot_pos_kernel(idx_ref, rank_ref, start_ref, pos_ref):
    tm = idx_ref.shape[1]
    e_iota = lax.broadcasted_iota(I32, (N_EXPERTS, tm), 0)
    k_iota = lax.broadcasted_iota(I32, (TOP_K, tm), 0)
    idx = idx_ref[...]
    start = start_ref[...]
    base = jnp.zeros((TOP_K, tm), F32)
    for k in range(TOP_K):
        bk = jnp.sum(jnp.where(e_iota == idx[k:k + 1, :], start, 0.0), axis=0, keepdims=True)
        base = jnp.where(k_iota == k, bk, base)
    pos_ref[...] = base.astype(I32) + rank_ref[...]


def slot_positions(idx_t, rank_t, pad_start):
    n = idx_t.shape[1]
    tm = 1024
    tok = lambda i: (0, i)
    return pl.pallas_call(
        _slot_pos_kernel,
        out_shape=jax.ShapeDtypeStruct((TOP_K, n), I32),
        grid=(n // tm,),
        in_specs=[pl.BlockSpec((TOP_K, tm), tok), pl.BlockSpec((TOP_K, tm), tok),
                  pl.BlockSpec((N_EXPERTS, 1), lambda i: (0, 0))],
        out_specs=pl.BlockSpec((TOP_K, tm), tok),
        compiler_params=_params("parallel"),
        name="slot_positions",
    )(idx_t, rank_t, pad_start.astype(F32).reshape(N_EXPERTS, 1))


SC_WINDOW = 128
SC_WORDS = QUARTER


def _sc_mesh():
    return plsc.VectorSubcoreMesh(core_axis_name="core", subcore_axis_name="subcore")


def sc_scatter_rows(src, pos_t, rows):
    n = src.shape[1]
    src2 = src.reshape(2 * n, SC_WORDS)
    idx2 = jnp.concatenate([pos_t, pos_t + rows], axis=1)

    @pl.kernel(out_type=jax.ShapeDtypeStruct((2 * rows, SC_WORDS), src.dtype), mesh=_sc_mesh(), scratch_types=[])
    def scatter_kernel(x_hbm, i_hbm, o_hbm):
        def body(x_vmem, i_vmem):
            pltpu.sync_copy(x_vmem, o_hbm.at[i_vmem.at[0]])

        pltpu.emit_pipeline(
            body,
            grid=(2 * n // SC_WINDOW, TOP_K),
            in_specs=[pl.BlockSpec((SC_WINDOW, SC_WORDS), index_map=lambda i, k: (i, 0)),
                      pl.BlockSpec((1, SC_WINDOW), index_map=lambda i, k: (k, i))],
            out_specs=[],
            core_axis_name=("core", "subcore"),
            dimension_semantics=(pltpu.PARALLEL, pltpu.ARBITRARY),
        )(x_hbm, i_hbm)

    return scatter_kernel(src2, idx2).reshape(2, rows, SC_WORDS)


def sc_gather_rows(src, pos):
    r = src.shape[1]
    m = pos.shape[0]
    src2 = src.reshape(2 * r, SC_WORDS)
    idx2 = jnp.concatenate([pos, pos + r]).reshape(1, 2 * m)

    @pl.kernel(out_type=jax.ShapeDtypeStruct((2 * m, SC_WORDS), src.dtype), mesh=_sc_mesh(), scratch_types=[])
    def gather_kernel(x_hbm, i_hbm, o_hbm):
        def body(i_vmem, o_vmem):
            pltpu.sync_copy(x_hbm.at[i_vmem.at[0]], o_vmem)

        pltpu.emit_pipeline(
            body,
            grid=(2 * m // SC_WINDOW,),
            in_specs=[pl.BlockSpec((1, SC_WINDOW), index_map=lambda i: (0, i))],
            out_specs=[pl.BlockSpec((SC_WINDOW, SC_WORDS), index_map=lambda i: (i, 0))],
            core_axis_name=("core", "subcore"),
            dimension_semantics=(pltpu.PARALLEL,),
        )(i_hbm, o_hbm)

    return gather_kernel(src2, idx2).reshape(2, m, SC_WORDS)


def _expert_kernel(blk_expert_ref, blk_valid_ref, n_used_ref, x_ref, wgu_ref, wd_ref, y_ref):
    del blk_expert_ref
    j = pl.program_id(0)
    used = j < n_used_ref[0]

    @pl.when(used)
    def _():
        valid = lax.broadcasted_iota(I32, x_ref.shape[1:], 0) < blk_valid_ref[j]
        quarters = _load_quarters(jnp.where(valid, x_ref[0], U32(0)), jnp.where(valid, x_ref[1], U32(0)))
        gu = sum(jnp.dot(xq.astype(BF16), wgu_ref[0, 0, c * QUARTER:(c + 1) * QUARTER, :].astype(BF16),
                         preferred_element_type=F32)
                 for c, xq in enumerate(quarters))
        act = (_silu(gu[:, :EXPERT_FF]) * gu[:, EXPERT_FF:]).astype(BF16)
        _store_planes(y_ref, jnp.dot(act, wd_ref[0, 0].astype(BF16), preferred_element_type=F32))

    @pl.when(jnp.logical_not(used))
    def _():
        y_ref[...] = jnp.zeros_like(y_ref)


def expert_ffn(xs, blk_expert, blk_valid, n_used, w_gu, w_down, layer):
    _, rows, w = xs.shape
    d = D_MODEL
    n_blocks = rows // ROW_BLOCK

    def row_map(j, be, bv, nu):
        return (0, jnp.minimum(j, nu[0] - 1), 0)

    def w_map(j, be, bv, nu):
        return (layer, be[jnp.minimum(j, nu[0] - 1)], 0, 0)

    return pl.pallas_call(
        _expert_kernel,
        out_shape=jax.ShapeDtypeStruct(xs.shape, U32),
        grid_spec=pltpu.PrefetchScalarGridSpec(
            num_scalar_prefetch=3,
            grid=(n_blocks,),
            in_specs=[pl.BlockSpec((2, ROW_BLOCK, w), row_map),
                      pl.BlockSpec((1, 1, d, 2 * EXPERT_FF), w_map),
                      pl.BlockSpec((1, 1, EXPERT_FF, d), w_map)],
            out_specs=pl.BlockSpec((2, ROW_BLOCK, w), lambda j, be, bv, nu: (0, j, 0))),
        compiler_params=_params("arbitrary"),
        name="expert_ffn",
    )(blk_expert, blk_valid, n_used, xs, w_gu, w_down)


def _combine_kernel(h_ref, gate_ref, ys_ref, sgu_ref, sd_ref, g_ref, b_ref, *refs):
    o_ref, ob_ref = refs[-2:]
    h = h_ref[...]
    gu = jnp.dot(h.astype(BF16), sgu_ref[...], preferred_element_type=F32)
    act = _silu(gu[:, :EXPERT_FF]) * gu[:, EXPERT_FF:]
    y = jnp.dot(act.astype(BF16), sd_ref[...], preferred_element_type=F32)
    gate = gate_ref[...]
    acc = [y[:, c * QUARTER:(c + 1) * QUARTER] for c in range(4)]
    for k in range(TOP_K):
        gk = gate[:, k:k + 1]
        acc = [a + gk * q for a, q in zip(acc, _load_quarters(ys_ref[0, k], ys_ref[1, k]))]
    out = _layer_norm_rows(DEEPNORM_ALPHA * h + jnp.concatenate(acc, axis=-1), g_ref[...], b_ref[...])
    o_ref[...] = out
    ob_ref[...] = out.astype(BF16)


def combine_shared_ln(h, gate, y_slots, chunk, prev_out, sh_gu, sh_down, g, b):
    n, d = h.shape
    tm = MOE_ROWS
    steps = y_slots.shape[2] // tm
    row = lambda i: (i + chunk * steps, 0)
    const = lambda i: (0, 0)
    passthrough = () if prev_out is None else tuple(prev_out)
    n_in = 7
    return pl.pallas_call(
        _combine_kernel,
        out_shape=(jax.ShapeDtypeStruct((n, d), F32), jax.ShapeDtypeStruct((n, d), BF16)),
        grid=(steps,),
        in_specs=[pl.BlockSpec((tm, d), row),
                  pl.BlockSpec((tm, TOP_K), row),
                  pl.BlockSpec((2, TOP_K, tm, QUARTER), lambda i: (0, 0, i, 0)),
                  pl.BlockSpec((d, 2 * EXPERT_FF), const),
                  pl.BlockSpec((EXPERT_FF, d), const),
                  pl.BlockSpec((1, d), const), pl.BlockSpec((1, d), const)]
                 + [pl.BlockSpec(memory_space=pl.ANY) for _ in passthrough],
        out_specs=(pl.BlockSpec((tm, d), row), pl.BlockSpec((tm, d), row)),
        input_output_aliases={n_in + i: i for i in range(len(passthrough))},
        compiler_params=_params("parallel"),
        name="combine_shared_ln",
    )(h, gate, y_slots, sh_gu, sh_down, g.reshape(1, d), b.reshape(1, d), *passthrough)


_MAIN_BLOCKS = (0, 1, 3, 4, 7, 8, 9, 10, 11, 12, 13, 14, 2)
_FORGET_BLOCKS = (5, 6)
_Q_A, _Q_R, _I_R, _G_R, _GATE_A, _GATE_R = (i * D_MODEL for i in range(6))
_K_A = 6 * D_MODEL
_V_A = _K_A + KV_WIDTH


def kernel(x, ln_in_g, ln_in_b, lb_logits, w_in, b_in, attn_sinks, rec_norm_g, w_proj_attn, w_proj_rec, w_out,
           ln1_g, ln1_b, router_w, router_bias, expert_w_gu, expert_w_down, shared_w_gu, shared_w_down,
           ln2_g, ln2_b):
    batch, seq, d = x.shape
    n = batch * seq
    depth = w_in.shape[0]
    n_blocks = n * TOP_K // ROW_BLOCK + N_EXPERTS
    rows = n_blocks * ROW_BLOCK

    p = jax.nn.softmax(lb_logits.astype(F32), axis=0)
    cum = jnp.cumsum(p, axis=0)
    lower = cum - cum[0:1]
    log_lb = jnp.log(lower)
    log1m_lb = jnp.log1p(-lower)

    w_in_bf = w_in.astype(BF16)
    h, hb = layer_norm_in(x.reshape(n, d), ln_in_g, ln_in_b)
    for l in range(depth):
        proj = in_proj(hb, w_in_bf, b_in, l, _MAIN_BLOCKS, BF16, "in_proj_main")
        proj_f = in_proj(hb, w_in_bf, b_in, l, _FORGET_BLOCKS, F32, "in_proj_forget")
        attn = swa_attention(proj, attn_sinks[l], batch, seq, _Q_A, _K_A, _V_A)
        rec = hgrn2(proj, proj_f, log_lb[l], log1m_lb[l], rec_norm_g[l], batch, seq, _Q_R, _I_R, _G_R)
        h, hp = merge_outproj_ln(h, attn, rec, proj, _GATE_A, _GATE_R,
                                 w_proj_attn[l].astype(BF16), w_proj_rec[l].astype(BF16), w_out[l].astype(BF16),
                                 ln1_g[l], ln1_b[l])

        rw_t = router_w[l].T
        rw_hi = rw_t.astype(BF16)
        rw_lo = (rw_t - rw_hi.astype(F32)).astype(BF16)
        idx_t, gate_t, rank_t, cnt = router(h, rw_hi, rw_lo, router_bias[l])
        counts = cnt[:, 0]
        padded = (counts + ROW_BLOCK - 1) // ROW_BLOCK * ROW_BLOCK
        pad_end = jnp.cumsum(padded)
        pad_start = pad_end - padded
        pos_t = slot_positions(idx_t, rank_t, pad_start)
        blk_row = jnp.arange(n_blocks, dtype=I32) * ROW_BLOCK
        blk_expert = jnp.minimum(jnp.sum(pad_end[None, :] <= blk_row[:, None], axis=1), N_EXPERTS - 1).astype(I32)
        blk_valid = jnp.clip(counts[blk_expert] - (blk_row - pad_start[blk_expert]), 0, ROW_BLOCK).astype(I32)
        n_used = (pad_end[-1:] // ROW_BLOCK).astype(I32)

        xs = sc_scatter_rows(hp, pos_t, rows)
        ys = expert_ffn(xs, blk_expert, blk_valid, n_used, expert_w_gu, expert_w_down, l)
        nc = n // MOE_CHUNKS
        gate, out = gate_t.T, None
        for c in range(MOE_CHUNKS):
            pos_c = lax.slice_in_dim(pos_t, c * nc, (c + 1) * nc, axis=1).reshape(TOP_K * nc)
            y_slots = sc_gather_rows(ys, pos_c).reshape(2, TOP_K, nc, QUARTER)
            out = combine_shared_ln(h, gate, y_slots, c, out, shared_w_gu[l].astype(BF16),
                                    shared_w_down[l].astype(BF16), ln2_g[l], ln2_b[l])
        h, hb = out
    return h.reshape(batch, seq, d)
```

```python
import functools

import jax
import jax.numpy as jnp
from jax import lax
from jax.experimental import pallas as pl
from jax.experimental.pallas import tpu as pltpu
from jax.experimental.pallas import tpu_sc as plsc

F32 = jnp.float32
BF16 = jnp.bfloat16
U32 = jnp.uint32
I32 = jnp.int32

D_MODEL = 1024
QUARTER = D_MODEL // 4
CHUNK = 64
ATTN_HEADS = 16
ATTN_KV_HEADS = 4
ATTN_HEAD_DIM = 64
ATTN_GROUP = ATTN_HEADS // ATTN_KV_HEADS
WIN_CHUNKS = 2
KV_WIDTH = ATTN_KV_HEADS * ATTN_HEAD_DIM
REC_HEADS = 8
REC_DIM = 128
N_EXPERTS = 256
TOP_K = 8
N_GROUPS = 8
GROUP_SIZE = N_EXPERTS // N_GROUPS
TOPK_GROUPS = 4
EXPERT_FF = 256
ROUTED_SCALE = 2.5
DEPTH = 2
DEEPNORM_ALPHA = (2 * DEPTH) ** 0.25
LN_EPS = 1e-5
RMS_EPS = 1e-5
NEG_INF = float("-inf")

SUBLANES = 8
VMEM_LIMIT_BYTES = 48 * 1024 * 1024

LN_ROWS = 512
PROJ_ROWS = 4096
PROJ_COLS = 512
ATTN_Q_BLOCK = 256
ATTN_STAGE_LAG = 3
REC_TILE = 256
REC_BLOCK = 64
REC_SUB = 16
MERGE_ROWS = 512
MERGE_PIECE = 256
ROUTER_ROWS = 256
MOE_ROWS = 256
MOE_CHUNKS = 4
ROW_BLOCK = 512

DECAY_LIMIT = 60.0


def _params(*sem):
    return pltpu.CompilerParams(dimension_semantics=sem, vmem_limit_bytes=VMEM_LIMIT_BYTES)


def _layer_norm_rows(x, g, b):
    mu = jnp.mean(x, axis=-1, keepdims=True)
    xc = x - mu
    var = jnp.mean(xc * xc, axis=-1, keepdims=True)
    return xc * lax.rsqrt(var + LN_EPS) * g + b


def _sigmoid(x):
    return 1.0 / (1.0 + jnp.exp(-x))


def _silu(x):
    return x * _sigmoid(x)


def _pack_pair(lo, hi):
    lo_bits = pltpu.bitcast(lo.astype(BF16).astype(F32), U32)
    hi_bits = pltpu.bitcast(hi.astype(BF16).astype(F32), U32)
    return lax.shift_right_logical(lo_bits, U32(16)) | (hi_bits & U32(0xFFFF0000))


def _unpack_pair(w):
    lo = pltpu.bitcast(lax.shift_left(w, U32(16)), F32)
    hi = pltpu.bitcast(w & U32(0xFFFF0000), F32)
    return lo, hi


def _store_planes(ref, x):
    q = QUARTER
    ref[0] = _pack_pair(x[:, 0 * q:1 * q], x[:, 1 * q:2 * q])
    ref[1] = _pack_pair(x[:, 2 * q:3 * q], x[:, 3 * q:4 * q])


def _load_quarters(plane0, plane1):
    return _unpack_pair(plane0) + _unpack_pair(plane1)


def _ln_in_kernel(x_ref, g_ref, b_ref, h_ref, hb_ref):
    h = _layer_norm_rows(x_ref[...], g_ref[...], b_ref[...])
    h_ref[...] = h
    hb_ref[...] = h.astype(BF16)


def layer_norm_in(x, g, b):
    n, d = x.shape
    row = lambda i: (i, 0)
    const = lambda i: (0, 0)
    return pl.pallas_call(
        _ln_in_kernel,
        out_shape=(jax.ShapeDtypeStruct((n, d), F32), jax.ShapeDtypeStruct((n, d), BF16)),
        grid=(n // LN_ROWS,),
        in_specs=[pl.BlockSpec((LN_ROWS, d), row), pl.BlockSpec((1, d), const), pl.BlockSpec((1, d), const)],
        out_specs=(pl.BlockSpec((LN_ROWS, d), row), pl.BlockSpec((LN_ROWS, d), row)),
        compiler_params=_params("parallel"),
        name="ln_in",
    )(x, g.reshape(1, d), b.reshape(1, d))


def _in_proj_kernel(perm_ref, x_ref, w_ref, b_ref, o_ref):
    del perm_ref
    acc = jnp.dot(x_ref[...], w_ref[0], preferred_element_type=F32)
    o_ref[...] = (acc + b_ref[0]).astype(o_ref.dtype)


def in_proj(xb, w, b, layer, col_blocks, out_dtype, name):
    n, k = xb.shape
    perm = jnp.asarray(col_blocks, I32)
    nblk = len(col_blocks)
    tm = min(PROJ_ROWS, n)
    return pl.pallas_call(
        _in_proj_kernel,
        out_shape=jax.ShapeDtypeStruct((n, nblk * PROJ_COLS), out_dtype),
        grid_spec=pltpu.PrefetchScalarGridSpec(
            num_scalar_prefetch=1,
            grid=(n // tm, nblk),
            in_specs=[pl.BlockSpec((tm, k), lambda i, j, p: (i, 0)),
                      pl.BlockSpec((1, k, PROJ_COLS), lambda i, j, p: (layer, 0, p[j])),
                      pl.BlockSpec((1, 1, PROJ_COLS), lambda i, j, p: (layer, 0, p[j]))],
            out_specs=pl.BlockSpec((tm, PROJ_COLS), lambda i, j, p: (i, j))),
        compiler_params=_params("parallel", "arbitrary"),
        name=name,
    )(perm, xb, w, b.reshape(b.shape[0], 1, -1))


def _attn_kernel(sink_ref, q_ref, kp_ref, kc_ref, vp_ref, vc_ref, o_ref, s_ref, e_ref):
    i = pl.program_id(1)
    half = ATTN_Q_BLOCK // 2
    qc = lax.broadcasted_iota(I32, (half, ATTN_Q_BLOCK), 0) // CHUNK
    kc = lax.broadcasted_iota(I32, (half, ATTN_Q_BLOCK), 1) // CHUNK
    valid = (kc >= qc) & (kc <= qc + WIN_CHUNKS)
    first = jnp.where(i == 0, WIN_CHUNKS, 0)
    scale = ATTN_HEAD_DIM ** -0.5
    windows = ((jnp.concatenate([kp_ref[half:, :], kc_ref[:half, :]], axis=0),
                jnp.concatenate([vp_ref[half:, :], vc_ref[:half, :]], axis=0),
                jnp.where(valid & (kc >= first), 0.0, NEG_INF)),
               (kc_ref[...], vc_ref[...], jnp.where(valid, 0.0, NEG_INF)))
    for part, (k, _, mask_bias) in enumerate(windows):
        rows = slice(part * half, (part + 1) * half)
        for h in range(ATTN_HEADS):
            kv = h // ATTN_GROUP
            qh = q_ref[rows, h * ATTN_HEAD_DIM:(h + 1) * ATTN_HEAD_DIM] * scale
            kh = k[:, kv * ATTN_HEAD_DIM:(kv + 1) * ATTN_HEAD_DIM]
            s_ref[part * ATTN_HEADS + h] = (
                lax.dot_general(qh, kh, (((1,), (1,)), ((), ())), preferred_element_type=F32) + mask_bias)
    units = [(part, h) for part in range(2) for h in range(ATTN_HEADS)]

    def row_max(part, h):
        return jnp.maximum(jnp.max(s_ref[part * ATTN_HEADS + h], axis=-1, keepdims=True), sink_ref[h])

    def exponentials(part, h, m):
        e = jnp.exp(s_ref[part * ATTN_HEADS + h] - m)
        e_ref[part * ATTN_HEADS + h] = e.astype(BF16)
        return 1.0 / (jnp.sum(e, axis=-1, keepdims=True) + jnp.exp(sink_ref[h] - m))

    def values(part, h, rden):
        kv = h // ATTN_GROUP
        vh = windows[part][1][:, kv * ATTN_HEAD_DIM:(kv + 1) * ATTN_HEAD_DIM]
        oh = jnp.dot(e_ref[part * ATTN_HEADS + h], vh, preferred_element_type=F32) * rden
        rows = slice(part * half, (part + 1) * half)
        o_ref[rows, h * ATTN_HEAD_DIM:(h + 1) * ATTN_HEAD_DIM] = oh.astype(o_ref.dtype)

    maxes, rdens = {}, {}
    lag = ATTN_STAGE_LAG
    for t in range(len(units) + 2 * lag):
        if t < len(units):
            maxes[t] = row_max(*units[t])
        if 0 <= t - lag < len(units):
            rdens[t - lag] = exponentials(*units[t - lag], maxes.pop(t - lag))
        if 0 <= t - 2 * lag < len(units):
            values(*units[t - 2 * lag], rdens.pop(t - 2 * lag))


def swa_attention(proj, sinks, batch, seq, q_col, k_col, v_col):
    n = batch * seq
    nb = seq // ATTN_Q_BLOCK
    qb, kb, vb = q_col // D_MODEL, k_col // KV_WIDTH, v_col // KV_WIDTH

    def cur(col):
        return lambda b, i, s: (b * nb + i, col)

    def prev(col):
        return lambda b, i, s: (b * nb + jnp.maximum(i - 1, 0), col)

    return pl.pallas_call(
        _attn_kernel,
        out_shape=jax.ShapeDtypeStruct((n, D_MODEL), BF16),
        grid_spec=pltpu.PrefetchScalarGridSpec(
            num_scalar_prefetch=1,
            grid=(batch, nb),
            in_specs=[pl.BlockSpec((ATTN_Q_BLOCK, D_MODEL), cur(qb)),
                      pl.BlockSpec((ATTN_Q_BLOCK, KV_WIDTH), prev(kb)),
                      pl.BlockSpec((ATTN_Q_BLOCK, KV_WIDTH), cur(kb)),
                      pl.BlockSpec((ATTN_Q_BLOCK, KV_WIDTH), prev(vb)),
                      pl.BlockSpec((ATTN_Q_BLOCK, KV_WIDTH), cur(vb))],
            out_specs=pl.BlockSpec((ATTN_Q_BLOCK, D_MODEL), lambda b, i, s: (b * nb + i, 0)),
            scratch_shapes=[pltpu.VMEM((2 * ATTN_HEADS, ATTN_Q_BLOCK // 2, ATTN_Q_BLOCK), F32),
                            pltpu.VMEM((2 * ATTN_HEADS, ATTN_Q_BLOCK // 2, ATTN_Q_BLOCK), BF16)]),
        compiler_params=_params("parallel", "parallel"),
        name="swa_attention",
    )(sinks.astype(F32), proj, proj, proj, proj, proj)


def _hgrn_kernel(q_ref, i_ref, g_ref, f_ref, loglb_ref, log1mlb_ref, ng_ref, o_ref,
                 b_ref, k_ref, qs_ref, lf_ref, intra_ref, inter_ref, qd_ref, kt_ref, *st_ref):
    @pl.when(pl.program_id(1) == 0)
    def _():
        for s_ref in st_ref:
            s_ref[...] = jnp.zeros_like(s_ref)

    fl = f_ref[...]
    log_sig = jnp.minimum(fl, 0.0) - jnp.log(1.0 + jnp.exp(-jnp.abs(fl)))
    a = loglb_ref[...]
    c = log1mlb_ref[...] + log_sig
    log_f = jnp.maximum(a, c) + jnp.log(1.0 + jnp.exp(-jnp.abs(a - c)))
    lf_ref[...] = log_f
    k_ref[...] = 1.0 - jnp.exp(log_f)
    qs_ref[...] = _silu(q_ref[...].astype(F32))

    def prefix_sum(x, width):
        pos = lax.broadcasted_iota(I32, x.shape, 0) % width
        shift = 1
        while shift < width:
            x = x + jnp.where(pos >= shift, pltpu.roll(x, shift, axis=0), 0.0)
            shift *= 2
        return x

    half = REC_BLOCK // 2
    ti = lax.broadcasted_iota(I32, (REC_TILE, REC_TILE), 0)
    si = lax.broadcasted_iota(I32, (REC_TILE, REC_TILE), 1)
    tm, sm = ti % REC_BLOCK, si % REC_BLOCK
    same = ti // REC_BLOCK == si // REC_BLOCK
    plus = same & (tm >= half) & (sm >= half) & (sm <= tm)
    minus = same & (tm < half) & (sm > tm) & (sm < half)
    signed = jnp.where(plus, 1.0, jnp.where(minus, -1.0, 0.0)).astype(BF16)
    hi = log_f.astype(BF16)
    rest = log_f - hi.astype(F32)
    mid = rest.astype(BF16)
    lo = (rest - mid.astype(F32)).astype(BF16)
    d = (jnp.dot(signed, hi, preferred_element_type=F32) + jnp.dot(signed, mid, preferred_element_type=F32)
         + jnp.dot(signed, lo, preferred_element_type=F32))
    b_ref[...] = d
    decay_bounded = jnp.max(jnp.abs(d)) <= DECAY_LIMIT

    t_iota = lax.broadcasted_iota(I32, (REC_SUB, REC_DIM), 0)
    nt = (((1,), (1,)), ((), ()))
    tn = (((0,), (0,)), ((), ()))

    def finish(o, r0, cols):
        ms = jnp.mean(o * o, axis=-1, keepdims=True)
        o = o * lax.rsqrt(ms + RMS_EPS) * ng_ref[...]
        gj = g_ref[pl.ds(r0, REC_SUB), cols].astype(F32)
        o_ref[pl.ds(r0, REC_SUB), cols] = (o * _silu(gj)).astype(o_ref.dtype)

    @pl.when(decay_bounded)
    def _():
        qd_ref[...] = (qs_ref[...] * jnp.exp(b_ref[...])).astype(BF16)
        kt_ref[...] = (k_ref[...] * jnp.exp(-b_ref[...])).astype(BF16)
        ri = lax.broadcasted_iota(I32, (REC_TILE, REC_TILE), 0)
        ci = lax.broadcasted_iota(I32, (REC_TILE, REC_TILE), 1)
        keep = (ri >= ci) & (ri // REC_BLOCK == ci // REC_BLOCK)
        for h in range(REC_HEADS):
            cols = slice(h * REC_DIM, (h + 1) * REC_DIM)
            att = lax.dot_general(qd_ref[:, cols], kt_ref[:, cols], nt, preferred_element_type=F32)
            att = jnp.where(keep, att, 0.0).astype(BF16)
            intra_ref[:, cols] = jnp.dot(att, i_ref[:, cols], preferred_element_type=F32)

        def step(j, carry):
            r0 = pl.multiple_of(j * REC_BLOCK, REC_BLOCK)
            for h in range(REC_HEADS):
                cols = slice(h * REC_DIM, (h + 1) * REC_DIM)
                qd = qd_ref[pl.ds(r0, REC_BLOCK), cols]
                kt = kt_ref[pl.ds(r0, REC_BLOCK), cols]
                vj = i_ref[pl.ds(r0, REC_BLOCK), cols]
                last = pl.multiple_of(r0 + REC_BLOCK - SUBLANES, SUBLANES)
                e1 = jnp.exp(lf_ref[pl.ds(r0, SUBLANES), cols][0:1, :] - b_ref[pl.ds(r0, SUBLANES), cols][0:1, :])
                e2 = jnp.exp(b_ref[pl.ds(last, SUBLANES), cols][SUBLANES - 1:SUBLANES, :])
                st_mid = st_ref[h][...] * e1
                inter_ref[pl.ds(r0, REC_BLOCK), cols] = lax.dot_general(
                    qd, st_mid.astype(BF16), nt, preferred_element_type=F32)
                kv_t = lax.dot_general(vj, kt, tn, preferred_element_type=F32)
                st_ref[h][...] = (st_mid + kv_t) * e2
            return carry

        lax.fori_loop(0, REC_TILE // REC_BLOCK, step, 0)

        for h in range(REC_HEADS):
            cols = slice(h * REC_DIM, (h + 1) * REC_DIM)
            o = intra_ref[:, cols] + inter_ref[:, cols]
            ms = jnp.mean(o * o, axis=-1, keepdims=True)
            o = o * lax.rsqrt(ms + RMS_EPS) * ng_ref[...]
            o_ref[:, cols] = (o * _silu(g_ref[:, cols].astype(F32))).astype(o_ref.dtype)

    @pl.when(jnp.logical_not(decay_bounded))
    def _():
        b_ref[...] = prefix_sum(lf_ref[...], REC_SUB)
        ones = jnp.ones((REC_DIM, REC_DIM), BF16)

        def step(j, carry):
            r0 = pl.multiple_of(j * REC_SUB, REC_SUB)
            for h in range(REC_HEADS):
                cols = slice(h * REC_DIM, (h + 1) * REC_DIM)
                bj = b_ref[pl.ds(r0, REC_SUB), cols]
                kj = k_ref[pl.ds(r0, REC_SUB), cols]
                qj = qs_ref[pl.ds(r0, REC_SUB), cols]
                vj = i_ref[pl.ds(r0, REC_SUB), cols].astype(F32)
                st = st_ref[h][...]
                qd = (qj * jnp.exp(bj)).astype(BF16)
                o = lax.dot_general(qd, st.astype(BF16), nt, preferred_element_type=F32)
                parts = []
                for s in range(REC_SUB):
                    dec = jnp.exp(jnp.where(t_iota >= s, bj - bj[s:s + 1, :], NEG_INF))
                    parts.append((qj * dec * kj[s:s + 1, :]).astype(BF16))
                pstack = jnp.concatenate(parts, axis=0)
                rsum = jnp.dot(pstack, ones, preferred_element_type=F32)
                for s in range(REC_SUB):
                    o = o + rsum[s * REC_SUB:(s + 1) * REC_SUB, :] * vj[s:s + 1, :]
                b_end = bj[REC_SUB - 1:REC_SUB, :]
                kd = (kj * jnp.exp(b_end - bj)).astype(BF16)
                kv_t = lax.dot_general(vj.astype(BF16), kd, tn, preferred_element_type=F32)
                st_ref[h][...] = st * jnp.exp(b_end) + kv_t
                finish(o, r0, cols)
            return carry

        lax.fori_loop(0, REC_TILE // REC_SUB, step, 0)


def hgrn2(proj, proj_f, log_lb, log1m_lb, norm_g, batch, seq, q_col, i_col, g_col):
    n = batch * seq
    nb = seq // REC_TILE
    d = D_MODEL

    def blk(col):
        return pl.BlockSpec((REC_TILE, d), lambda b, i: (b * nb + i, col // d))

    return pl.pallas_call(
        _hgrn_kernel,
        out_shape=jax.ShapeDtypeStruct((n, d), BF16),
        grid=(batch, nb),
        in_specs=[blk(q_col), blk(i_col), blk(g_col),
                  pl.BlockSpec((REC_TILE, d), lambda b, i: (b * nb + i, 0)),
                  pl.BlockSpec((1, d), lambda b, i: (0, 0)),
                  pl.BlockSpec((1, d), lambda b, i: (0, 0)),
                  pl.BlockSpec((1, REC_DIM), lambda b, i: (0, 0))],
        out_specs=pl.BlockSpec((REC_TILE, d), lambda b, i: (b * nb + i, 0)),
        scratch_shapes=[pltpu.VMEM((REC_TILE, d), F32) for _ in range(6)]
                       + [pltpu.VMEM((REC_TILE, d), BF16),
                        pltpu.VMEM((REC_TILE, d), BF16)]
                       + [pltpu.VMEM((REC_DIM, REC_DIM), F32) for _ in range(REC_HEADS)],
        compiler_params=_params("parallel", "arbitrary"),
        name="hgrn2",
    )(proj, proj, proj, proj_f, log_lb.reshape(1, d), log1m_lb.reshape(1, d), norm_g.reshape(1, REC_DIM))


def _merge_kernel(h_ref, attn_ref, rec_ref, ga_ref, gr_ref, wpa_ref, wpr_ref, wo_ref, g_ref, b_ref,
                  o_ref, op_ref):
    tm = h_ref.shape[0]
    pieces = [slice(p * MERGE_PIECE, (p + 1) * MERGE_PIECE) for p in range(tm // MERGE_PIECE)]
    branches = [(jnp.dot(attn_ref[rows, :], wpa_ref[...], preferred_element_type=F32),
                 jnp.dot(rec_ref[rows, :], wpr_ref[...], preferred_element_type=F32)) for rows in pieces]
    merged = [(_sigmoid(ga_ref[rows, :].astype(F32)) * a + _sigmoid(gr_ref[rows, :].astype(F32)) * r).astype(BF16)
              for rows, (a, r) in zip(pieces, branches)]
    ys = [jnp.dot(m, wo_ref[...], preferred_element_type=F32) for m in merged]
    for rows, y in zip(pieces, ys):
        h1 = _layer_norm_rows(DEEPNORM_ALPHA * h_ref[rows, :] + y, g_ref[...], b_ref[...])
        o_ref[rows, :] = h1
        op_ref[0, rows, :] = _pack_pair(h1[:, 0 * QUARTER:1 * QUARTER], h1[:, 1 * QUARTER:2 * QUARTER])
        op_ref[1, rows, :] = _pack_pair(h1[:, 2 * QUARTER:3 * QUARTER], h1[:, 3 * QUARTER:4 * QUARTER])


def merge_outproj_ln(h, attn, rec, proj, ga_col, gr_col, wpa, wpr, wo, g, b):
    n, d = h.shape
    tm = MERGE_ROWS
    row = lambda i: (i, 0)
    const = lambda i: (0, 0)
    return pl.pallas_call(
        _merge_kernel,
        out_shape=(jax.ShapeDtypeStruct((n, d), F32), jax.ShapeDtypeStruct((2, n, QUARTER), U32)),
        grid=(n // tm,),
        in_specs=[pl.BlockSpec((tm, d), row), pl.BlockSpec((tm, d), row), pl.BlockSpec((tm, d), row),
                  pl.BlockSpec((tm, d), lambda i: (i, ga_col // d)),
                  pl.BlockSpec((tm, d), lambda i: (i, gr_col // d)),
                  pl.BlockSpec((d, d), const), pl.BlockSpec((d, d), const), pl.BlockSpec((d, d), const),
                  pl.BlockSpec((1, d), const), pl.BlockSpec((1, d), const)],
        out_specs=(pl.BlockSpec((tm, d), row), pl.BlockSpec((2, tm, QUARTER), lambda i: (0, i, 0))),
        compiler_params=_params("parallel"),
        name="merge_outproj_ln",
    )(h, attn, rec, proj, proj, wpa, wpr, wo, g.reshape(1, d), b.reshape(1, d))


def _router_kernel(h_ref, whi_ref, wlo_ref, bias_ref, idx_ref, gate_ref, rank_ref, cnt_ref, carry_ref):
    @pl.when(pl.program_id(0) == 0)
    def _():
        carry_ref[...] = jnp.zeros_like(carry_ref)

    tm = h_ref.shape[0]
    h = h_ref[...]
    h_hi = h.astype(BF16)
    h_lo = (h - h_hi.astype(F32)).astype(BF16)
    nt = (((1,), (1,)), ((), ()))
    logits = (lax.dot_general(whi_ref[...], h_hi, nt, preferred_element_type=F32)
              + lax.dot_general(whi_ref[...], h_lo, nt, preferred_element_type=F32)
              + lax.dot_general(wlo_ref[...], h_hi, nt, preferred_element_type=F32))
    scores = _sigmoid(logits)
    sel = scores + bias_ref[...]
    e_iota = lax.broadcasted_iota(I32, (N_EXPERTS, tm), 0)

    g_iota = lax.broadcasted_iota(I32, (N_GROUPS, tm), 0)
    l_iota = lax.broadcasted_iota(I32, (GROUP_SIZE, tm), 0)
    grp = jnp.zeros((N_GROUPS, tm), F32)
    for g in range(N_GROUPS):
        sg = sel[g * GROUP_SIZE:(g + 1) * GROUP_SIZE, :]
        m1 = jnp.max(sg, axis=0, keepdims=True)
        i1 = jnp.min(jnp.where(sg == m1, l_iota, GROUP_SIZE), axis=0, keepdims=True)
        m2 = jnp.max(jnp.where(l_iota == i1, NEG_INF, sg), axis=0, keepdims=True)
        grp = jnp.where(g_iota == g, m1 + m2, grp)
    gsel = jnp.zeros((N_GROUPS, tm), I32)
    for _ in range(TOPK_GROUPS):
        m = jnp.max(grp, axis=0, keepdims=True)
        gi = jnp.min(jnp.where(grp == m, g_iota, N_GROUPS), axis=0, keepdims=True)
        hit = g_iota == gi
        gsel = jnp.where(hit, 1, gsel)
        grp = jnp.where(hit, NEG_INF, grp)
    masked = []
    for g in range(N_GROUPS):
        sg = sel[g * GROUP_SIZE:(g + 1) * GROUP_SIZE, :]
        masked.append(jnp.where(gsel[g:g + 1, :] > 0, sg, NEG_INF))
    selm = jnp.concatenate(masked, axis=0)

    k_iota = lax.broadcasted_iota(I32, (TOP_K, tm), 0)
    idx = jnp.zeros((TOP_K, tm), I32)
    gate = jnp.zeros((TOP_K, tm), F32)
    member = jnp.zeros((N_EXPERTS, tm), F32)
    for k in range(TOP_K):
        m = jnp.max(selm, axis=0, keepdims=True)
        ei = jnp.min(jnp.where(selm == m, e_iota, N_EXPERTS), axis=0, keepdims=True)
        hit = e_iota == ei
        gk = jnp.sum(jnp.where(hit, scores, 0.0), axis=0, keepdims=True)
        idx = jnp.where(k_iota == k, ei, idx)
        gate = jnp.where(k_iota == k, gk, gate)
        member = jnp.where(hit, 1.0, member)
        selm = jnp.where(hit, NEG_INF, selm)
    gate = gate / jnp.sum(gate, axis=0, keepdims=True) * ROUTED_SCALE

    upper = lax.broadcasted_iota(I32, (tm, tm), 0) < lax.broadcasted_iota(I32, (tm, tm), 1)
    before = jnp.dot(member.astype(BF16), upper.astype(BF16), preferred_element_type=F32) + carry_ref[...]
    rank = jnp.zeros((TOP_K, tm), F32)
    for k in range(TOP_K):
        rk = jnp.sum(jnp.where(e_iota == idx[k:k + 1, :], before, 0.0), axis=0, keepdims=True)
        rank = jnp.where(k_iota == k, rk, rank)
    carry_ref[...] = carry_ref[...] + jnp.sum(member, axis=1, keepdims=True)

    idx_ref[...] = idx
    gate_ref[...] = gate
    rank_ref[...] = rank.astype(I32)
    cnt_ref[...] = jnp.broadcast_to(carry_ref[...], cnt_ref.shape).astype(I32)


def router(h, w_t_hi, w_t_lo, bias):
    n, d = h.shape
    tm = ROUTER_ROWS
    tok = lambda i: (0, i)
    const = lambda i: (0, 0)
    return pl.pallas_call(
        _router_kernel,
        out_shape=(jax.ShapeDtypeStruct((TOP_K, n), I32),
                   jax.ShapeDtypeStruct((TOP_K, n), F32),
                   jax.ShapeDtypeStruct((TOP_K, n), I32),
                   jax.ShapeDtypeStruct((N_EXPERTS, 128), I32)),
        grid=(n // tm,),
        in_specs=[pl.BlockSpec((tm, d), lambda i: (i, 0)),
                  pl.BlockSpec((N_EXPERTS, d), const),
                  pl.BlockSpec((N_EXPERTS, d), const),
                  pl.BlockSpec((N_EXPERTS, 1), const)],
        out_specs=(pl.BlockSpec((TOP_K, tm), tok), pl.BlockSpec((TOP_K, tm), tok),
                   pl.BlockSpec((TOP_K, tm), tok), pl.BlockSpec((N_EXPERTS, 128), const)),
        scratch_shapes=[pltpu.VMEM((N_EXPERTS, 1), F32)],
        compiler_params=_params("arbitrary"),
        name="router",
    )(h, w_t_hi, w_t_lo, bias.reshape(N_EXPERTS, 1))


def _slot_pos_kernel(idx_ref, rank_ref, start_ref, pos_ref):
    tm = idx_ref.shape[1]
    e_iota = lax.broadcasted_iota(I32, (N_EXPERTS, tm), 0)
    k_iota = lax.broadcasted_iota(I32, (TOP_K, tm), 0)
    idx = idx_ref[...]
    start = start_ref[...]
    base = jnp.zeros((TOP_K, tm), F32)
    for k in range(TOP_K):
        bk = jnp.sum(jnp.where(e_iota == idx[k:k + 1, :], start, 0.0), axis=0, keepdims=True)
        base = jnp.where(k_iota == k, bk, base)
    pos_ref[...] = base.astype(I32) + rank_ref[...]


def slot_positions(idx_t, rank_t, pad_start):
    n = idx_t.shape[1]
    tm = 1024
    tok = lambda i: (0, i)
    return pl.pallas_call(
        _slot_pos_kernel,
        out_shape=jax.ShapeDtypeStruct((TOP_K, n), I32),
        grid=(n // tm,),
        in_specs=[pl.BlockSpec((TOP_K, tm), tok), pl.BlockSpec((TOP_K, tm), tok),
                  pl.BlockSpec((N_EXPERTS, 1), lambda i: (0, 0))],
        out_specs=pl.BlockSpec((TOP_K, tm), tok),
        compiler_params=_params("parallel"),
        name="slot_positions",
    )(idx_t, rank_t, pad_start.astype(F32).reshape(N_EXPERTS, 1))


SC_WINDOW = 128
SC_WORDS = QUARTER


def _sc_mesh():
    return plsc.VectorSubcoreMesh(core_axis_name="core", subcore_axis_name="subcore")


def sc_scatter_rows(src, pos_t, rows):
    n = src.shape[1]
    src2 = src.reshape(2 * n, SC_WORDS)
    idx2 = jnp.concatenate([pos_t, pos_t + rows], axis=1)

    @pl.kernel(out_type=jax.ShapeDtypeStruct((2 * rows, SC_WORDS), src.dtype), mesh=_sc_mesh(), scratch_types=[])
    def scatter_kernel(x_hbm, i_hbm, o_hbm):
        def body(x_vmem, i_vmem):
            pltpu.sync_copy(x_vmem, o_hbm.at[i_vmem.at[0]])

        pltpu.emit_pipeline(
            body,
            grid=(2 * n // SC_WINDOW, TOP_K),
            in_specs=[pl.BlockSpec((SC_WINDOW, SC_WORDS), index_map=lambda i, k: (i, 0)),
                      pl.BlockSpec((1, SC_WINDOW), index_map=lambda i, k: (k, i))],
            out_specs=[],
            core_axis_name=("core", "subcore"),
            dimension_semantics=(pltpu.PARALLEL, pltpu.ARBITRARY),
        )(x_hbm, i_hbm)

    return scatter_kernel(src2, idx2).reshape(2, rows, SC_WORDS)


def sc_gather_rows(src, pos):
    r = src.shape[1]
    m = pos.shape[0]
    src2 = src.reshape(2 * r, SC_WORDS)
    idx2 = jnp.concatenate([pos, pos + r]).reshape(1, 2 * m)

    @pl.kernel(out_type=jax.ShapeDtypeStruct((2 * m, SC_WORDS), src.dtype), mesh=_sc_mesh(), scratch_types=[])
    def gather_kernel(x_hbm, i_hbm, o_hbm):
        def body(i_vmem, o_vmem):
            pltpu.sync_copy(x_hbm.at[i_vmem.at[0]], o_vmem)

        pltpu.emit_pipeline(
            body,
            grid=(2 * m // SC_WINDOW,),
            in_specs=[pl.BlockSpec((1, SC_WINDOW), index_map=lambda i: (0, i))],
            out_specs=[pl.BlockSpec((SC_WINDOW, SC_WORDS), index_map=lambda i: (i, 0))],
            core_axis_name=("core", "subcore"),
            dimension_semantics=(pltpu.PARALLEL,),
        )(i_hbm, o_hbm)

    return gather_kernel(src2, idx2).reshape(2, m, SC_WORDS)


def _expert_kernel(blk_expert_ref, blk_valid_ref, blk_first_ref, blk_slot_ref, blk_next_ref, n_used_ref,
                   x_ref, wgu_hbm, wd_hbm, y_ref, wgu_buf, wd_buf, sem, *, layer):
    j = pl.program_id(0)
    used = j < n_used_ref[0]

    def weight_copies(e, slot):
        return (pltpu.make_async_copy(wgu_hbm.at[layer, e], wgu_buf.at[slot], sem.at[0, slot]),
                pltpu.make_async_copy(wd_hbm.at[layer, e], wd_buf.at[slot], sem.at[1, slot]))

    @pl.when(used)
    def _():
        slot = blk_slot_ref[j]

        @pl.when(blk_first_ref[j] == 1)
        def _():
            @pl.when(j == 0)
            def _():
                for copy in weight_copies(blk_expert_ref[0], slot):
                    copy.start()

            for copy in weight_copies(blk_expert_ref[j], slot):
                copy.wait()

            @pl.when(blk_next_ref[j] >= 0)
            def _():
                for copy in weight_copies(blk_next_ref[j], 1 - slot):
                    copy.start()

        valid = lax.broadcasted_iota(I32, x_ref.shape[1:], 0) < blk_valid_ref[j]
        quarters = _load_quarters(jnp.where(valid, x_ref[0], U32(0)), jnp.where(valid, x_ref[1], U32(0)))
        gu = sum(jnp.dot(xq.astype(BF16), wgu_buf[slot, c * QUARTER:(c + 1) * QUARTER, :].astype(BF16),
                         preferred_element_type=F32)
                 for c, xq in enumerate(quarters))
        act = (_silu(gu[:, :EXPERT_FF]) * gu[:, EXPERT_FF:]).astype(BF16)
        _store_planes(y_ref, jnp.dot(act, wd_buf[slot].astype(BF16), preferred_element_type=F32))

    @pl.when(jnp.logical_not(used))
    def _():
        y_ref[...] = jnp.zeros_like(y_ref)


def expert_plan(counts, pad_start, pad_end, n_blocks):
    blk_row = jnp.arange(n_blocks, dtype=I32) * ROW_BLOCK
    blk_expert = jnp.minimum(jnp.sum(pad_end[None, :] <= blk_row[:, None], axis=1), N_EXPERTS - 1).astype(I32)
    blk_valid = jnp.clip(counts[blk_expert] - (blk_row - pad_start[blk_expert]), 0, ROW_BLOCK).astype(I32)
    used = blk_row < pad_end[-1]
    prev_expert = jnp.concatenate([jnp.full((1,), -1, I32), blk_expert[:-1]])
    blk_first = (used & (blk_expert != prev_expert)).astype(I32)
    blk_slot = ((jnp.cumsum(blk_first) - 1) % 2).astype(I32)
    expert_ids = jnp.arange(N_EXPERTS, dtype=I32)
    nonempty_at_or_after = lax.cummin(jnp.where(counts > 0, expert_ids, N_EXPERTS), reverse=True)
    nonempty_after = jnp.concatenate([nonempty_at_or_after[1:], jnp.full((1,), N_EXPERTS, I32)])
    nxt = nonempty_after[blk_expert]
    blk_next = jnp.where(nxt < N_EXPERTS, nxt, -1).astype(I32)
    n_used = (pad_end[-1:] // ROW_BLOCK).astype(I32)
    return blk_expert, blk_valid, blk_first, blk_slot, blk_next, n_used


def expert_ffn(xs, plan, w_gu, w_down, layer):
    _, rows, w = xs.shape
    d = D_MODEL
    n_blocks = rows // ROW_BLOCK

    def row_map(j, be, bv, bf, bs, bn, nu):
        return (0, jnp.minimum(j, nu[0] - 1), 0)

    return pl.pallas_call(
        functools.partial(_expert_kernel, layer=layer),
        out_shape=jax.ShapeDtypeStruct(xs.shape, U32),
        grid_spec=pltpu.PrefetchScalarGridSpec(
            num_scalar_prefetch=6,
            grid=(n_blocks,),
            in_specs=[pl.BlockSpec((2, ROW_BLOCK, w), row_map),
                      pl.BlockSpec(memory_space=pl.ANY),
                      pl.BlockSpec(memory_space=pl.ANY)],
            out_specs=pl.BlockSpec((2, ROW_BLOCK, w), lambda j, be, bv, bf, bs, bn, nu: (0, j, 0)),
            scratch_shapes=[pltpu.VMEM((2, d, 2 * EXPERT_FF), F32),
                            pltpu.VMEM((2, EXPERT_FF, d), F32),
                            pltpu.SemaphoreType.DMA((2, 2))]),
        compiler_params=_params("arbitrary"),
        name="expert_ffn",
    )(*plan, xs, w_gu, w_down)


def _combine_kernel(h_ref, gate_ref, ys_ref, sgu_ref, sd_ref, g_ref, b_ref, *refs):
    o_ref, ob_ref = refs[-2:]
    h = h_ref[...]
    gu = jnp.dot(h.astype(BF16), sgu_ref[...], preferred_element_type=F32)
    act = _silu(gu[:, :EXPERT_FF]) * gu[:, EXPERT_FF:]
    y = jnp.dot(act.astype(BF16), sd_ref[...], preferred_element_type=F32)
    gate = gate_ref[...]
    acc = [y[:, c * QUARTER:(c + 1) * QUARTER] for c in range(4)]
    for k in range(TOP_K):
        gk = gate[:, k:k + 1]
        acc = [a + gk * q for a, q in zip(acc, _load_quarters(ys_ref[0, k], ys_ref[1, k]))]
    out = _layer_norm_rows(DEEPNORM_ALPHA * h + jnp.concatenate(acc, axis=-1), g_ref[...], b_ref[...])
    o_ref[...] = out
    ob_ref[...] = out.astype(BF16)


def combine_shared_ln(h, gate, y_slots, chunk, prev_out, sh_gu, sh_down, g, b):
    n, d = h.shape
    tm = MOE_ROWS
    steps = y_slots.shape[2] // tm
    row = lambda i: (i + chunk * steps, 0)
    const = lambda i: (0, 0)
    passthrough = () if prev_out is None else tuple(prev_out)
    n_in = 7
    return pl.pallas_call(
        _combine_kernel,
        out_shape=(jax.ShapeDtypeStruct((n, d), F32), jax.ShapeDtypeStruct((n, d), BF16)),
        grid=(steps,),
        in_specs=[pl.BlockSpec((tm, d), row),
                  pl.BlockSpec((tm, TOP_K), row),
                  pl.BlockSpec((2, TOP_K, tm, QUARTER), lambda i: (0, 0, i, 0)),
                  pl.BlockSpec((d, 2 * EXPERT_FF), const),
                  pl.BlockSpec((EXPERT_FF, d), const),
                  pl.BlockSpec((1, d), const), pl.BlockSpec((1, d), const)]
                 + [pl.BlockSpec(memory_space=pl.ANY) for _ in passthrough],
        out_specs=(pl.BlockSpec((tm, d), row), pl.BlockSpec((tm, d), row)),
        input_output_aliases={n_in + i: i for i in range(len(passthrough))},
        compiler_params=_params("parallel"),
        name="combine_shared_ln",
    )(h, gate, y_slots, sh_gu, sh_down, g.reshape(1, d), b.reshape(1, d), *passthrough)


_MAIN_BLOCKS = (0, 1, 3, 4, 7, 8, 9, 10, 11, 12, 13, 14, 2)
_FORGET_BLOCKS = (5, 6)
_Q_A, _Q_R, _I_R, _G_R, _GATE_A, _GATE_R = (i * D_MODEL for i in range(6))
_K_A = 6 * D_MODEL
_V_A = _K_A + KV_WIDTH


def kernel(x, ln_in_g, ln_in_b, lb_logits, w_in, b_in, attn_sinks, rec_norm_g, w_proj_attn, w_proj_rec, w_out,
           ln1_g, ln1_b, router_w, router_bias, expert_w_gu, expert_w_down, shared_w_gu, shared_w_down,
           ln2_g, ln2_b):
    batch, seq, d = x.shape
    n = batch * seq
    depth = w_in.shape[0]
    n_blocks = n * TOP_K // ROW_BLOCK + N_EXPERTS
    rows = n_blocks * ROW_BLOCK

    p = jax.nn.softmax(lb_logits.astype(F32), axis=0)
    cum = jnp.cumsum(p, axis=0)
    lower = cum - cum[0:1]
    log_lb = jnp.log(lower)
    log1m_lb = jnp.log1p(-lower)

    w_in_bf = w_in.astype(BF16)
    h, hb = layer_norm_in(x.reshape(n, d), ln_in_g, ln_in_b)
    for l in range(depth):
        proj = in_proj(hb, w_in_bf, b_in, l, _MAIN_BLOCKS, BF16, "in_proj_main")
        proj_f = in_proj(hb, w_in_bf, b_in, l, _FORGET_BLOCKS, F32, "in_proj_forget")
        attn = swa_attention(proj, attn_sinks[l], batch, seq, _Q_A, _K_A, _V_A)
        rec = hgrn2(proj, proj_f, log_lb[l], log1m_lb[l], rec_norm_g[l], batch, seq, _Q_R, _I_R, _G_R)
        h, hp = merge_outproj_ln(h, attn, rec, proj, _GATE_A, _GATE_R,
                                 w_proj_attn[l].astype(BF16), w_proj_rec[l].astype(BF16), w_out[l].astype(BF16),
                                 ln1_g[l], ln1_b[l])

        rw_t = router_w[l].T
        rw_hi = rw_t.astype(BF16)
        rw_lo = (rw_t - rw_hi.astype(F32)).astype(BF16)
        idx_t, gate_t, rank_t, cnt = router(h, rw_hi, rw_lo, router_bias[l])
        counts = cnt[:, 0]
        padded = (counts + ROW_BLOCK - 1) // ROW_BLOCK * ROW_BLOCK
        pad_end = jnp.cumsum(padded)
        pad_start = pad_end - padded
        pos_t = slot_positions(idx_t, rank_t, pad_start)
        plan = expert_plan(counts, pad_start, pad_end, n_blocks)

        xs = sc_scatter_rows(hp, pos_t, rows)
        ys = expert_ffn(xs, plan, expert_w_gu, expert_w_down, l)
        nc = n // MOE_CHUNKS
        gate, out = gate_t.T, None
        for c in range(MOE_CHUNKS):
            pos_c = lax.slice_in_dim(pos_t, c * nc, (c + 1) * nc, axis=1).reshape(TOP_K * nc)
            y_slots = sc_gather_rows(ys, pos_c).reshape(2, TOP_K, nc, QUARTER)
            out = combine_shared_ln(h, gate, y_slots, c, out, shared_w_gu[l].astype(BF16),
                                    shared_w_down[l].astype(BF16), ln2_g[l], ln2_b[l])
        h, hb = out
    return h.reshape(batch, seq, d)
```

```python
import functools

import jax
import jax.numpy as jnp
from jax import lax
from jax.experimental import pallas as pl
from jax.experimental.pallas import tpu as pltpu
from jax.experimental.pallas import tpu_sc as plsc

F32 = jnp.float32
BF16 = jnp.bfloat16
U32 = jnp.uint32
I32 = jnp.int32

D_MODEL = 1024
QUARTER = D_MODEL // 4
CHUNK = 64
ATTN_HEADS = 16
ATTN_KV_HEADS = 4
ATTN_HEAD_DIM = 64
ATTN_GROUP = ATTN_HEADS // ATTN_KV_HEADS
WIN_CHUNKS = 2
KV_WIDTH = ATTN_KV_HEADS * ATTN_HEAD_DIM
REC_HEADS = 8
REC_DIM = 128
N_EXPERTS = 256
TOP_K = 8
N_GROUPS = 8
GROUP_SIZE = N_EXPERTS // N_GROUPS
TOPK_GROUPS = 4
EXPERT_FF = 256
ROUTED_SCALE = 2.5
DEPTH = 2
DEEPNORM_ALPHA = (2 * DEPTH) ** 0.25
LN_EPS = 1e-5
RMS_EPS = 1e-5
NEG_INF = float("-inf")

SUBLANES = 8
VMEM_LIMIT_BYTES = 48 * 1024 * 1024

LN_ROWS = 512
PROJ_ROWS = 4096
PROJ_COLS = 512
ATTN_Q_BLOCK = 256
ATTN_STAGE_LAG = 3
REC_TILE = 256
REC_BLOCK = 64
REC_SUB = 16
MERGE_ROWS = 512
MERGE_PIECE = 256
ROUTER_ROWS = 256
MOE_ROWS = 256
MOE_CHUNKS = 4
ROW_BLOCK = 512

DECAY_LIMIT = 60.0


def _params(*sem):
    return pltpu.CompilerParams(dimension_semantics=sem, vmem_limit_bytes=VMEM_LIMIT_BYTES)


def _layer_norm_rows(x, g, b):
    mu = jnp.mean(x, axis=-1, keepdims=True)
    xc = x - mu
    var = jnp.mean(xc * xc, axis=-1, keepdims=True)
    return xc * lax.rsqrt(var + LN_EPS) * g + b


def _sigmoid(x):
    return 1.0 / (1.0 + jnp.exp(-x))


def _silu(x):
    return x * _sigmoid(x)


def _pack_pair(lo, hi):
    lo_bits = pltpu.bitcast(lo.astype(BF16).astype(F32), U32)
    hi_bits = pltpu.bitcast(hi.astype(BF16).astype(F32), U32)
    return lax.shift_right_logical(lo_bits, U32(16)) | (hi_bits & U32(0xFFFF0000))


def _unpack_pair(w):
    lo = pltpu.bitcast(lax.shift_left(w, U32(16)), F32)
    hi = pltpu.bitcast(w & U32(0xFFFF0000), F32)
    return lo, hi


def _store_planes(ref, x):
    q = QUARTER
    ref[0] = _pack_pair(x[:, 0 * q:1 * q], x[:, 1 * q:2 * q])
    ref[1] = _pack_pair(x[:, 2 * q:3 * q], x[:, 3 * q:4 * q])


def _load_quarters(plane0, plane1):
    return _unpack_pair(plane0) + _unpack_pair(plane1)


def _ln_in_kernel(x_ref, g_ref, b_ref, h_ref, hb_ref):
    h = _layer_norm_rows(x_ref[...], g_ref[...], b_ref[...])
    h_ref[...] = h
    hb_ref[...] = h.astype(BF16)


def layer_norm_in(x, g, b):
    n, d = x.shape
    row = lambda i: (i, 0)
    const = lambda i: (0, 0)
    return pl.pallas_call(
        _ln_in_kernel,
        out_shape=(jax.ShapeDtypeStruct((n, d), F32), jax.ShapeDtypeStruct((n, d), BF16)),
        grid=(n // LN_ROWS,),
        in_specs=[pl.BlockSpec((LN_ROWS, d), row), pl.BlockSpec((1, d), const), pl.BlockSpec((1, d), const)],
        out_specs=(pl.BlockSpec((LN_ROWS, d), row), pl.BlockSpec((LN_ROWS, d), row)),
        compiler_params=_params("parallel"),
        name="ln_in",
    )(x, g.reshape(1, d), b.reshape(1, d))


def _in_proj_kernel(perm_ref, x_ref, w_ref, b_ref, o_ref):
    del perm_ref
    acc = jnp.dot(x_ref[...], w_ref[0], preferred_element_type=F32)
    o_ref[...] = (acc + b_ref[0]).astype(o_ref.dtype)


def in_proj(xb, w, b, layer, col_blocks, out_dtype, name):
    n, k = xb.shape
    perm = jnp.asarray(col_blocks, I32)
    nblk = len(col_blocks)
    tm = min(PROJ_ROWS, n)
    return pl.pallas_call(
        _in_proj_kernel,
        out_shape=jax.ShapeDtypeStruct((n, nblk * PROJ_COLS), out_dtype),
        grid_spec=pltpu.PrefetchScalarGridSpec(
            num_scalar_prefetch=1,
            grid=(n // tm, nblk),
            in_specs=[pl.BlockSpec((tm, k), lambda i, j, p: (i, 0)),
                      pl.BlockSpec((1, k, PROJ_COLS), lambda i, j, p: (layer, 0, p[j])),
                      pl.BlockSpec((1, 1, PROJ_COLS), lambda i, j, p: (layer, 0, p[j]))],
            out_specs=pl.BlockSpec((tm, PROJ_COLS), lambda i, j, p: (i, j))),
        compiler_params=_params("parallel", "arbitrary"),
        name=name,
    )(perm, xb, w, b.reshape(b.shape[0], 1, -1))


def _attn_kernel(sink_ref, q_ref, kp_ref, kc_ref, vp_ref, vc_ref, o_ref, s_ref, e_ref):
    i = pl.program_id(1)
    half = ATTN_Q_BLOCK // 2
    qc = lax.broadcasted_iota(I32, (half, ATTN_Q_BLOCK), 0) // CHUNK
    kc = lax.broadcasted_iota(I32, (half, ATTN_Q_BLOCK), 1) // CHUNK
    valid = (kc >= qc) & (kc <= qc + WIN_CHUNKS)
    first = jnp.where(i == 0, WIN_CHUNKS, 0)
    scale = ATTN_HEAD_DIM ** -0.5
    windows = ((jnp.concatenate([kp_ref[half:, :], kc_ref[:half, :]], axis=0),
                jnp.concatenate([vp_ref[half:, :], vc_ref[:half, :]], axis=0),
                jnp.where(valid & (kc >= first), 0.0, NEG_INF)),
               (kc_ref[...], vc_ref[...], jnp.where(valid, 0.0, NEG_INF)))
    for part, (k, _, mask_bias) in enumerate(windows):
        rows = slice(part * half, (part + 1) * half)
        for h in range(ATTN_HEADS):
            kv = h // ATTN_GROUP
            qh = q_ref[rows, h * ATTN_HEAD_DIM:(h + 1) * ATTN_HEAD_DIM] * scale
            kh = k[:, kv * ATTN_HEAD_DIM:(kv + 1) * ATTN_HEAD_DIM]
            s_ref[part * ATTN_HEADS + h] = (
                lax.dot_general(qh, kh, (((1,), (1,)), ((), ())), preferred_element_type=F32) + mask_bias)
    units = [(part, h) for part in range(2) for h in range(ATTN_HEADS)]

    def row_max(part, h):
        return jnp.maximum(jnp.max(s_ref[part * ATTN_HEADS + h], axis=-1, keepdims=True), sink_ref[h])

    def exponentials(part, h, m):
        e = jnp.exp(s_ref[part * ATTN_HEADS + h] - m)
        e_ref[part * ATTN_HEADS + h] = e.astype(BF16)
        return 1.0 / (jnp.sum(e, axis=-1, keepdims=True) + jnp.exp(sink_ref[h] - m))

    def values(part, h, rden):
        kv = h // ATTN_GROUP
        vh = windows[part][1][:, kv * ATTN_HEAD_DIM:(kv + 1) * ATTN_HEAD_DIM]
        oh = jnp.dot(e_ref[part * ATTN_HEADS + h], vh, preferred_element_type=F32) * rden
        rows = slice(part * half, (part + 1) * half)
        o_ref[rows, h * ATTN_HEAD_DIM:(h + 1) * ATTN_HEAD_DIM] = oh.astype(o_ref.dtype)

    maxes, rdens = {}, {}
    lag = ATTN_STAGE_LAG
    for t in range(len(units) + 2 * lag):
        if t < len(units):
            maxes[t] = row_max(*units[t])
        if 0 <= t - lag < len(units):
            rdens[t - lag] = exponentials(*units[t - lag], maxes.pop(t - lag))
        if 0 <= t - 2 * lag < len(units):
            values(*units[t - 2 * lag], rdens.pop(t - 2 * lag))


def swa_attention(proj, sinks, batch, seq, q_col, k_col, v_col):
    n = batch * seq
    nb = seq // ATTN_Q_BLOCK
    qb, kb, vb = q_col // D_MODEL, k_col // KV_WIDTH, v_col // KV_WIDTH

    def cur(col):
        return lambda b, i, s: (b * nb + i, col)

    def prev(col):
        return lambda b, i, s: (b * nb + jnp.maximum(i - 1, 0), col)

    return pl.pallas_call(
        _attn_kernel,
        out_shape=jax.ShapeDtypeStruct((n, D_MODEL), BF16),
        grid_spec=pltpu.PrefetchScalarGridSpec(
            num_scalar_prefetch=1,
            grid=(batch, nb),
            in_specs=[pl.BlockSpec((ATTN_Q_BLOCK, D_MODEL), cur(qb)),
                      pl.BlockSpec((ATTN_Q_BLOCK, KV_WIDTH), prev(kb)),
                      pl.BlockSpec((ATTN_Q_BLOCK, KV_WIDTH), cur(kb)),
                      pl.BlockSpec((ATTN_Q_BLOCK, KV_WIDTH), prev(vb)),
                      pl.BlockSpec((ATTN_Q_BLOCK, KV_WIDTH), cur(vb))],
            out_specs=pl.BlockSpec((ATTN_Q_BLOCK, D_MODEL), lambda b, i, s: (b * nb + i, 0)),
            scratch_shapes=[pltpu.VMEM((2 * ATTN_HEADS, ATTN_Q_BLOCK // 2, ATTN_Q_BLOCK), F32),
                            pltpu.VMEM((2 * ATTN_HEADS, ATTN_Q_BLOCK // 2, ATTN_Q_BLOCK), BF16)]),
        compiler_params=_params("parallel", "parallel"),
        name="swa_attention",
    )(sinks.astype(F32), proj, proj, proj, proj, proj)


def _hgrn_kernel(q_ref, i_ref, g_ref, f_ref, loglb_ref, log1mlb_ref, ng_ref, o_ref,
                 b_ref, k_ref, qs_ref, lf_ref, intra_ref, inter_ref, qd_ref, kt_ref, *st_ref):
    @pl.when(pl.program_id(1) == 0)
    def _():
        for s_ref in st_ref:
            s_ref[...] = jnp.zeros_like(s_ref)

    fl = f_ref[...]
    log_sig = jnp.minimum(fl, 0.0) - jnp.log(1.0 + jnp.exp(-jnp.abs(fl)))
    a = loglb_ref[...]
    c = log1mlb_ref[...] + log_sig
    log_f = jnp.maximum(a, c) + jnp.log(1.0 + jnp.exp(-jnp.abs(a - c)))
    lf_ref[...] = log_f
    k_ref[...] = 1.0 - jnp.exp(log_f)
    qs_ref[...] = _silu(q_ref[...].astype(F32))

    def prefix_sum(x, width):
        pos = lax.broadcasted_iota(I32, x.shape, 0) % width
        shift = 1
        while shift < width:
            x = x + jnp.where(pos >= shift, pltpu.roll(x, shift, axis=0), 0.0)
            shift *= 2
        return x

    half = REC_BLOCK // 2
    ti = lax.broadcasted_iota(I32, (REC_TILE, REC_TILE), 0)
    si = lax.broadcasted_iota(I32, (REC_TILE, REC_TILE), 1)
    tm, sm = ti % REC_BLOCK, si % REC_BLOCK
    same = ti // REC_BLOCK == si // REC_BLOCK
    plus = same & (tm >= half) & (sm >= half) & (sm <= tm)
    minus = same & (tm < half) & (sm > tm) & (sm < half)
    signed = jnp.where(plus, 1.0, jnp.where(minus, -1.0, 0.0)).astype(BF16)
    hi = log_f.astype(BF16)
    rest = log_f - hi.astype(F32)
    mid = rest.astype(BF16)
    lo = (rest - mid.astype(F32)).astype(BF16)
    d = (jnp.dot(signed, hi, preferred_element_type=F32) + jnp.dot(signed, mid, preferred_element_type=F32)
         + jnp.dot(signed, lo, preferred_element_type=F32))
    b_ref[...] = d
    decay_bounded = jnp.max(jnp.abs(d)) <= DECAY_LIMIT

    t_iota = lax.broadcasted_iota(I32, (REC_SUB, REC_DIM), 0)
    nt = (((1,), (1,)), ((), ()))
    tn = (((0,), (0,)), ((), ()))

    def finish(o, r0, cols):
        ms = jnp.mean(o * o, axis=-1, keepdims=True)
        o = o * lax.rsqrt(ms + RMS_EPS) * ng_ref[...]
        gj = g_ref[pl.ds(r0, REC_SUB), cols].astype(F32)
        o_ref[pl.ds(r0, REC_SUB), cols] = (o * _silu(gj)).astype(o_ref.dtype)

    @pl.when(decay_bounded)
    def _():
        qd_ref[...] = (qs_ref[...] * jnp.exp(b_ref[...])).astype(BF16)
        kt_ref[...] = (k_ref[...] * jnp.exp(-b_ref[...])).astype(BF16)
        ri = lax.broadcasted_iota(I32, (REC_TILE, REC_TILE), 0)
        ci = lax.broadcasted_iota(I32, (REC_TILE, REC_TILE), 1)
        keep = (ri >= ci) & (ri // REC_BLOCK == ci // REC_BLOCK)
        for h in range(REC_HEADS):
            cols = slice(h * REC_DIM, (h + 1) * REC_DIM)
            att = lax.dot_general(qd_ref[:, cols], kt_ref[:, cols], nt, preferred_element_type=F32)
            att = jnp.where(keep, att, 0.0).astype(BF16)
            intra_ref[:, cols] = jnp.dot(att, i_ref[:, cols], preferred_element_type=F32)

        for j in range(REC_TILE // REC_BLOCK):
            blk = slice(j * REC_BLOCK, (j + 1) * REC_BLOCK)
            for h in range(REC_HEADS):
                cols = slice(h * REC_DIM, (h + 1) * REC_DIM)
                first, last = j * REC_BLOCK, (j + 1) * REC_BLOCK - 1
                e1 = jnp.exp(lf_ref[first:first + 1, cols] - b_ref[first:first + 1, cols])
                e2 = jnp.exp(b_ref[last:last + 1, cols])
                st_mid = st_ref[h][...] * e1
                inter_ref[blk, cols] = lax.dot_general(
                    qd_ref[blk, cols], st_mid.astype(BF16), nt, preferred_element_type=F32)
                kv_t = lax.dot_general(i_ref[blk, cols], kt_ref[blk, cols], tn, preferred_element_type=F32)
                st_ref[h][...] = (st_mid + kv_t) * e2

        for h in range(REC_HEADS):
            cols = slice(h * REC_DIM, (h + 1) * REC_DIM)
            o = intra_ref[:, cols] + inter_ref[:, cols]
            ms = jnp.mean(o * o, axis=-1, keepdims=True)
            o = o * lax.rsqrt(ms + RMS_EPS) * ng_ref[...]
            o_ref[:, cols] = (o * _silu(g_ref[:, cols].astype(F32))).astype(o_ref.dtype)

    @pl.when(jnp.logical_not(decay_bounded))
    def _():
        b_ref[...] = prefix_sum(lf_ref[...], REC_SUB)
        ones = jnp.ones((REC_DIM, REC_DIM), BF16)

        def step(j, carry):
            r0 = pl.multiple_of(j * REC_SUB, REC_SUB)
            for h in range(REC_HEADS):
                cols = slice(h * REC_DIM, (h + 1) * REC_DIM)
                bj = b_ref[pl.ds(r0, REC_SUB), cols]
                kj = k_ref[pl.ds(r0, REC_SUB), cols]
                qj = qs_ref[pl.ds(r0, REC_SUB), cols]
                vj = i_ref[pl.ds(r0, REC_SUB), cols].astype(F32)
                st = st_ref[h][...]
                qd = (qj * jnp.exp(bj)).astype(BF16)
                o = lax.dot_general(qd, st.astype(BF16), nt, preferred_element_type=F32)
                parts = []
                for s in range(REC_SUB):
                    dec = jnp.exp(jnp.where(t_iota >= s, bj - bj[s:s + 1, :], NEG_INF))
                    parts.append((qj * dec * kj[s:s + 1, :]).astype(BF16))
                pstack = jnp.concatenate(parts, axis=0)
                rsum = jnp.dot(pstack, ones, preferred_element_type=F32)
                for s in range(REC_SUB):
                    o = o + rsum[s * REC_SUB:(s + 1) * REC_SUB, :] * vj[s:s + 1, :]
                b_end = bj[REC_SUB - 1:REC_SUB, :]
                kd = (kj * jnp.exp(b_end - bj)).astype(BF16)
                kv_t = lax.dot_general(vj.astype(BF16), kd, tn, preferred_element_type=F32)
                st_ref[h][...] = st * jnp.exp(b_end) + kv_t
                finish(o, r0, cols)
            return carry

        lax.fori_loop(0, REC_TILE // REC_SUB, step, 0)


def hgrn2(proj, proj_f, log_lb, log1m_lb, norm_g, batch, seq, q_col, i_col, g_col):
    n = batch * seq
    nb = seq // REC_TILE
    d = D_MODEL

    def blk(col):
        return pl.BlockSpec((REC_TILE, d), lambda b, i: (b * nb + i, col // d))

    return pl.pallas_call(
        _hgrn_kernel,
        out_shape=jax.ShapeDtypeStruct((n, d), BF16),
        grid=(batch, nb),
        in_specs=[blk(q_col), blk(i_col), blk(g_col),
                  pl.BlockSpec((REC_TILE, d), lambda b, i: (b * nb + i, 0)),
                  pl.BlockSpec((1, d), lambda b, i: (0, 0)),
                  pl.BlockSpec((1, d), lambda b, i: (0, 0)),
                  pl.BlockSpec((1, REC_DIM), lambda b, i: (0, 0))],
        out_specs=pl.BlockSpec((REC_TILE, d), lambda b, i: (b * nb + i, 0)),
        scratch_shapes=[pltpu.VMEM((REC_TILE, d), F32) for _ in range(6)]
                       + [pltpu.VMEM((REC_TILE, d), BF16),
                        pltpu.VMEM((REC_TILE, d), BF16)]
                       + [pltpu.VMEM((REC_DIM, REC_DIM), F32) for _ in range(REC_HEADS)],
        compiler_params=_params("parallel", "arbitrary"),
        name="hgrn2",
    )(proj, proj, proj, proj_f, log_lb.reshape(1, d), log1m_lb.reshape(1, d), norm_g.reshape(1, REC_DIM))


def _merge_kernel(h_ref, attn_ref, rec_ref, ga_ref, gr_ref, wpa_ref, wpr_ref, wo_ref, g_ref, b_ref,
                  o_ref, op_ref):
    tm = h_ref.shape[0]
    pieces = [slice(p * MERGE_PIECE, (p + 1) * MERGE_PIECE) for p in range(tm // MERGE_PIECE)]
    branches = [(jnp.dot(attn_ref[rows, :], wpa_ref[...], preferred_element_type=F32),
                 jnp.dot(rec_ref[rows, :], wpr_ref[...], preferred_element_type=F32)) for rows in pieces]
    merged = [(_sigmoid(ga_ref[rows, :].astype(F32)) * a + _sigmoid(gr_ref[rows, :].astype(F32)) * r).astype(BF16)
              for rows, (a, r) in zip(pieces, branches)]
    ys = [jnp.dot(m, wo_ref[...], preferred_element_type=F32) for m in merged]
    for rows, y in zip(pieces, ys):
        h1 = _layer_norm_rows(DEEPNORM_ALPHA * h_ref[rows, :] + y, g_ref[...], b_ref[...])
        o_ref[rows, :] = h1
        op_ref[0, rows, :] = _pack_pair(h1[:, 0 * QUARTER:1 * QUARTER], h1[:, 1 * QUARTER:2 * QUARTER])
        op_ref[1, rows, :] = _pack_pair(h1[:, 2 * QUARTER:3 * QUARTER], h1[:, 3 * QUARTER:4 * QUARTER])


def merge_outproj_ln(h, attn, rec, proj, ga_col, gr_col, wpa, wpr, wo, g, b):
    n, d = h.shape
    tm = MERGE_ROWS
    row = lambda i: (i, 0)
    const = lambda i: (0, 0)
    return pl.pallas_call(
        _merge_kernel,
        out_shape=(jax.ShapeDtypeStruct((n, d), F32), jax.ShapeDtypeStruct((2, n, QUARTER), U32)),
        grid=(n // tm,),
        in_specs=[pl.BlockSpec((tm, d), row), pl.BlockSpec((tm, d), row), pl.BlockSpec((tm, d), row),
                  pl.BlockSpec((tm, d), lambda i: (i, ga_col // d)),
                  pl.BlockSpec((tm, d), lambda i: (i, gr_col // d)),
                  pl.BlockSpec((d, d), const), pl.BlockSpec((d, d), const), pl.BlockSpec((d, d), const),
                  pl.BlockSpec((1, d), const), pl.BlockSpec((1, d), const)],
        out_specs=(pl.BlockSpec((tm, d), row), pl.BlockSpec((2, tm, QUARTER), lambda i: (0, i, 0))),
        compiler_params=_params("parallel"),
        name="merge_outproj_ln",
    )(h, attn, rec, proj, proj, wpa, wpr, wo, g.reshape(1, d), b.reshape(1, d))


def _router_kernel(h_ref, whi_ref, wlo_ref, bias_ref, idx_ref, gate_ref, rank_ref, cnt_ref, carry_ref):
    @pl.when(pl.program_id(0) == 0)
    def _():
        carry_ref[...] = jnp.zeros_like(carry_ref)

    tm = h_ref.shape[0]
    h = h_ref[...]
    h_hi = h.astype(BF16)
    h_lo = (h - h_hi.astype(F32)).astype(BF16)
    nt = (((1,), (1,)), ((), ()))
    logits = (lax.dot_general(whi_ref[...], h_hi, nt, preferred_element_type=F32)
              + lax.dot_general(whi_ref[...], h_lo, nt, preferred_element_type=F32)
              + lax.dot_general(wlo_ref[...], h_hi, nt, preferred_element_type=F32))
    scores = _sigmoid(logits)
    sel = scores + bias_ref[...]
    e_iota = lax.broadcasted_iota(I32, (N_EXPERTS, tm), 0)

    g_iota = lax.broadcasted_iota(I32, (N_GROUPS, tm), 0)
    l_iota = lax.broadcasted_iota(I32, (GROUP_SIZE, tm), 0)
    grp = jnp.zeros((N_GROUPS, tm), F32)
    for g in range(N_GROUPS):
        sg = sel[g * GROUP_SIZE:(g + 1) * GROUP_SIZE, :]
        m1 = jnp.max(sg, axis=0, keepdims=True)
        i1 = jnp.min(jnp.where(sg == m1, l_iota, GROUP_SIZE), axis=0, keepdims=True)
        m2 = jnp.max(jnp.where(l_iota == i1, NEG_INF, sg), axis=0, keepdims=True)
        grp = jnp.where(g_iota == g, m1 + m2, grp)
    gsel = jnp.zeros((N_GROUPS, tm), I32)
    for _ in range(TOPK_GROUPS):
        m = jnp.max(grp, axis=0, keepdims=True)
        gi = jnp.min(jnp.where(grp == m, g_iota, N_GROUPS), axis=0, keepdims=True)
        hit = g_iota == gi
        gsel = jnp.where(hit, 1, gsel)
        grp = jnp.where(hit, NEG_INF, grp)
    masked = []
    for g in range(N_GROUPS):
        sg = sel[g * GROUP_SIZE:(g + 1) * GROUP_SIZE, :]
        masked.append(jnp.where(gsel[g:g + 1, :] > 0, sg, NEG_INF))
    selm = jnp.concatenate(masked, axis=0)

    k_iota = lax.broadcasted_iota(I32, (TOP_K, tm), 0)
    idx = jnp.zeros((TOP_K, tm), I32)
    gate = jnp.zeros((TOP_K, tm), F32)
    member = jnp.zeros((N_EXPERTS, tm), F32)
    for k in range(TOP_K):
        m = jnp.max(selm, axis=0, keepdims=True)
        ei = jnp.min(jnp.where(selm == m, e_iota, N_EXPERTS), axis=0, keepdims=True)
        hit = e_iota == ei
        gk = jnp.sum(jnp.where(hit, scores, 0.0), axis=0, keepdims=True)
        idx = jnp.where(k_iota == k, ei, idx)
        gate = jnp.where(k_iota == k, gk, gate)
        member = jnp.where(hit, 1.0, member)
        selm = jnp.where(hit, NEG_INF, selm)
    gate = gate / jnp.sum(gate, axis=0, keepdims=True) * ROUTED_SCALE

    upper = lax.broadcasted_iota(I32, (tm, tm), 0) < lax.broadcasted_iota(I32, (tm, tm), 1)
    before = jnp.dot(member.astype(BF16), upper.astype(BF16), preferred_element_type=F32) + carry_ref[...]
    rank = jnp.zeros((TOP_K, tm), F32)
    for k in range(TOP_K):
        rk = jnp.sum(jnp.where(e_iota == idx[k:k + 1, :], before, 0.0), axis=0, keepdims=True)
        rank = jnp.where(k_iota == k, rk, rank)
    carry_ref[...] = carry_ref[...] + jnp.sum(member, axis=1, keepdims=True)

    idx_ref[...] = idx
    gate_ref[...] = gate
    rank_ref[...] = rank.astype(I32)
    cnt_ref[...] = jnp.broadcast_to(carry_ref[...], cnt_ref.shape).astype(I32)


def router(h, w_t_hi, w_t_lo, bias):
    n, d = h.shape
    tm = ROUTER_ROWS
    tok = lambda i: (0, i)
    const = lambda i: (0, 0)
    return pl.pallas_call(
        _router_kernel,
        out_shape=(jax.ShapeDtypeStruct((TOP_K, n), I32),
                   jax.ShapeDtypeStruct((TOP_K, n), F32),
                   jax.ShapeDtypeStruct((TOP_K, n), I32),
                   jax.ShapeDtypeStruct((N_EXPERTS, 128), I32)),
        grid=(n // tm,),
        in_specs=[pl.BlockSpec((tm, d), lambda i: (i, 0)),
                  pl.BlockSpec((N_EXPERTS, d), const),
                  pl.BlockSpec((N_EXPERTS, d), const),
                  pl.BlockSpec((N_EXPERTS, 1), const)],
        out_specs=(pl.BlockSpec((TOP_K, tm), tok), pl.BlockSpec((TOP_K, tm), tok),
                   pl.BlockSpec((TOP_K, tm), tok), pl.BlockSpec((N_EXPERTS, 128), const)),
        scratch_shapes=[pltpu.VMEM((N_EXPERTS, 1), F32)],
        compiler_params=_params("arbitrary"),
        name="router",
    )(h, w_t_hi, w_t_lo, bias.reshape(N_EXPERTS, 1))


def _slot_pos_kernel(idx_ref, rank_ref, start_ref, pos_ref):
    tm = idx_ref.shape[1]
    e_iota = lax.broadcasted_iota(I32, (N_EXPERTS, tm), 0)
    k_iota = lax.broadcasted_iota(I32, (TOP_K, tm), 0)
    idx = idx_ref[...]
    start = start_ref[...]
    base = jnp.zeros((TOP_K, tm), F32)
    for k in range(TOP_K):
        bk = jnp.sum(jnp.where(e_iota == idx[k:k + 1, :], start, 0.0), axis=0, keepdims=True)
        base = jnp.where(k_iota == k, bk, base)
    pos_ref[...] = base.astype(I32) + rank_ref[...]


def slot_positions(idx_t, rank_t, pad_start):
    n = idx_t.shape[1]
    tm = 1024
    tok = lambda i: (0, i)
    return pl.pallas_call(
        _slot_pos_kernel,
        out_shape=jax.ShapeDtypeStruct((TOP_K, n), I32),
        grid=(n // tm,),
        in_specs=[pl.BlockSpec((TOP_K, tm), tok), pl.BlockSpec((TOP_K, tm), tok),
                  pl.BlockSpec((N_EXPERTS, 1), lambda i: (0, 0))],
        out_specs=pl.BlockSpec((TOP_K, tm), tok),
        compiler_params=_params("parallel"),
        name="slot_positions",
    )(idx_t, rank_t, pad_start.astype(F32).reshape(N_EXPERTS, 1))


SC_WINDOW = 128
SC_WORDS = QUARTER


def _sc_mesh():
    return plsc.VectorSubcoreMesh(core_axis_name="core", subcore_axis_name="subcore")


def sc_scatter_rows(src, pos_t, rows):
    n = src.shape[1]
    src2 = src.reshape(2 * n, SC_WORDS)
    idx2 = jnp.concatenate([pos_t, pos_t + rows], axis=1)

    @pl.kernel(out_type=jax.ShapeDtypeStruct((2 * rows, SC_WORDS), src.dtype), mesh=_sc_mesh(), scratch_types=[])
    def scatter_kernel(x_hbm, i_hbm, o_hbm):
        def body(x_vmem, i_vmem):
            pltpu.sync_copy(x_vmem, o_hbm.at[i_vmem.at[0]])

        pltpu.emit_pipeline(
            body,
            grid=(2 * n // SC_WINDOW, TOP_K),
            in_specs=[pl.BlockSpec((SC_WINDOW, SC_WORDS), index_map=lambda i, k: (i, 0)),
                      pl.BlockSpec((1, SC_WINDOW), index_map=lambda i, k: (k, i))],
            out_specs=[],
            core_axis_name=("core", "subcore"),
            dimension_semantics=(pltpu.PARALLEL, pltpu.ARBITRARY),
        )(x_hbm, i_hbm)

    return scatter_kernel(src2, idx2).reshape(2, rows, SC_WORDS)


def sc_gather_rows(src, pos):
    r = src.shape[1]
    m = pos.shape[0]
    src2 = src.reshape(2 * r, SC_WORDS)
    idx2 = jnp.concatenate([pos, pos + r]).reshape(1, 2 * m)

    @pl.kernel(out_type=jax.ShapeDtypeStruct((2 * m, SC_WORDS), src.dtype), mesh=_sc_mesh(), scratch_types=[])
    def gather_kernel(x_hbm, i_hbm, o_hbm):
        def body(i_vmem, o_vmem):
            pltpu.sync_copy(x_hbm.at[i_vmem.at[0]], o_vmem)

        pltpu.emit_pipeline(
            body,
            grid=(2 * m // SC_WINDOW,),
            in_specs=[pl.BlockSpec((1, SC_WINDOW), index_map=lambda i: (0, i))],
            out_specs=[pl.BlockSpec((SC_WINDOW, SC_WORDS), index_map=lambda i: (i, 0))],
            core_axis_name=("core", "subcore"),
            dimension_semantics=(pltpu.PARALLEL,),
        )(i_hbm, o_hbm)

    return gather_kernel(src2, idx2).reshape(2, m, SC_WORDS)


def _expert_kernel(blk_expert_ref, blk_valid_ref, blk_first_ref, blk_slot_ref, blk_next_ref, n_used_ref,
                   x_ref, wgu_hbm, wd_hbm, y_ref, wgu_buf, wd_buf, sem, *, layer):
    j = pl.program_id(0)
    used = j < n_used_ref[0]

    def weight_copies(e, slot):
        return (pltpu.make_async_copy(wgu_hbm.at[layer, e], wgu_buf.at[slot], sem.at[0, slot]),
                pltpu.make_async_copy(wd_hbm.at[layer, e], wd_buf.at[slot], sem.at[1, slot]))

    @pl.when(used)
    def _():
        slot = blk_slot_ref[j]

        @pl.when(blk_first_ref[j] == 1)
        def _():
            @pl.when(j == 0)
            def _():
                for copy in weight_copies(blk_expert_ref[0], slot):
                    copy.start()

            for copy in weight_copies(blk_expert_ref[j], slot):
                copy.wait()

            @pl.when(blk_next_ref[j] >= 0)
            def _():
                for copy in weight_copies(blk_next_ref[j], 1 - slot):
                    copy.start()

        valid = lax.broadcasted_iota(I32, x_ref.shape[1:], 0) < blk_valid_ref[j]
        quarters = _load_quarters(jnp.where(valid, x_ref[0], U32(0)), jnp.where(valid, x_ref[1], U32(0)))
        gu = sum(jnp.dot(xq.astype(BF16), wgu_buf[slot, c * QUARTER:(c + 1) * QUARTER, :].astype(BF16),
                         preferred_element_type=F32)
                 for c, xq in enumerate(quarters))
        act = (_silu(gu[:, :EXPERT_FF]) * gu[:, EXPERT_FF:]).astype(BF16)
        _store_planes(y_ref, jnp.dot(act, wd_buf[slot].astype(BF16), preferred_element_type=F32))

    @pl.when(jnp.logical_not(used))
    def _():
        y_ref[...] = jnp.zeros_like(y_ref)


def expert_plan(counts, pad_start, pad_end, n_blocks):
    blk_row = jnp.arange(n_blocks, dtype=I32) * ROW_BLOCK
    blk_expert = jnp.minimum(jnp.sum(pad_end[None, :] <= blk_row[:, None], axis=1), N_EXPERTS - 1).astype(I32)
    blk_valid = jnp.clip(counts[blk_expert] - (blk_row - pad_start[blk_expert]), 0, ROW_BLOCK).astype(I32)
    used = blk_row < pad_end[-1]
    prev_expert = jnp.concatenate([jnp.full((1,), -1, I32), blk_expert[:-1]])
    blk_first = (used & (blk_expert != prev_expert)).astype(I32)
    blk_slot = ((jnp.cumsum(blk_first) - 1) % 2).astype(I32)
    expert_ids = jnp.arange(N_EXPERTS, dtype=I32)
    nonempty_at_or_after = lax.cummin(jnp.where(counts > 0, expert_ids, N_EXPERTS), reverse=True)
    nonempty_after = jnp.concatenate([nonempty_at_or_after[1:], jnp.full((1,), N_EXPERTS, I32)])
    nxt = nonempty_after[blk_expert]
    blk_next = jnp.where(nxt < N_EXPERTS, nxt, -1).astype(I32)
    n_used = (pad_end[-1:] // ROW_BLOCK).astype(I32)
    return blk_expert, blk_valid, blk_first, blk_slot, blk_next, n_used


def expert_ffn(xs, plan, w_gu, w_down, layer):
    _, rows, w = xs.shape
    d = D_MODEL
    n_blocks = rows // ROW_BLOCK

    def row_map(j, be, bv, bf, bs, bn, nu):
        return (0, jnp.minimum(j, nu[0] - 1), 0)

    return pl.pallas_call(
        functools.partial(_expert_kernel, layer=layer),
        out_shape=jax.ShapeDtypeStruct(xs.shape, U32),
        grid_spec=pltpu.PrefetchScalarGridSpec(
            num_scalar_prefetch=6,
            grid=(n_blocks,),
            in_specs=[pl.BlockSpec((2, ROW_BLOCK, w), row_map),
                      pl.BlockSpec(memory_space=pl.ANY),
                      pl.BlockSpec(memory_space=pl.ANY)],
            out_specs=pl.BlockSpec((2, ROW_BLOCK, w), lambda j, be, bv, bf, bs, bn, nu: (0, j, 0)),
            scratch_shapes=[pltpu.VMEM((2, d, 2 * EXPERT_FF), F32),
                            pltpu.VMEM((2, EXPERT_FF, d), F32),
                            pltpu.SemaphoreType.DMA((2, 2))]),
        compiler_params=_params("arbitrary"),
        name="expert_ffn",
    )(*plan, xs, w_gu, w_down)


def _combine_kernel(h_ref, gate_ref, ys_ref, sgu_ref, sd_ref, g_ref, b_ref, *refs):
    o_ref, ob_ref = refs[-2:]
    h = h_ref[...]
    gu = jnp.dot(h.astype(BF16), sgu_ref[...], preferred_element_type=F32)
    act = _silu(gu[:, :EXPERT_FF]) * gu[:, EXPERT_FF:]
    y = jnp.dot(act.astype(BF16), sd_ref[...], preferred_element_type=F32)
    gate = gate_ref[...]
    acc = [y[:, c * QUARTER:(c + 1) * QUARTER] for c in range(4)]
    for k in range(TOP_K):
        gk = gate[:, k:k + 1]
        acc = [a + gk * q for a, q in zip(acc, _load_quarters(ys_ref[0, k], ys_ref[1, k]))]
    out = _layer_norm_rows(DEEPNORM_ALPHA * h + jnp.concatenate(acc, axis=-1), g_ref[...], b_ref[...])
    o_ref[...] = out
    ob_ref[...] = out.astype(BF16)


def combine_shared_ln(h, gate, y_slots, chunk, prev_out, sh_gu, sh_down, g, b):
    n, d = h.shape
    tm = MOE_ROWS
    steps = y_slots.shape[2] // tm
    row = lambda i: (i + chunk * steps, 0)
    const = lambda i: (0, 0)
    passthrough = () if prev_out is None else tuple(prev_out)
    n_in = 7
    return pl.pallas_call(
        _combine_kernel,
        out_shape=(jax.ShapeDtypeStruct((n, d), F32), jax.ShapeDtypeStruct((n, d), BF16)),
        grid=(steps,),
        in_specs=[pl.BlockSpec((tm, d), row),
                  pl.BlockSpec((tm, TOP_K), row),
                  pl.BlockSpec((2, TOP_K, tm, QUARTER), lambda i: (0, 0, i, 0)),
                  pl.BlockSpec((d, 2 * EXPERT_FF), const),
                  pl.BlockSpec((EXPERT_FF, d), const),
                  pl.BlockSpec((1, d), const), pl.BlockSpec((1, d), const)]
                 + [pl.BlockSpec(memory_space=pl.ANY) for _ in passthrough],
        out_specs=(pl.BlockSpec((tm, d), row), pl.BlockSpec((tm, d), row)),
        input_output_aliases={n_in + i: i for i in range(len(passthrough))},
        compiler_params=_params("parallel"),
        name="combine_shared_ln",
    )(h, gate, y_slots, sh_gu, sh_down, g.reshape(1, d), b.reshape(1, d), *passthrough)


_MAIN_BLOCKS = (0, 1, 3, 4, 7, 8, 9, 10, 11, 12, 13, 14, 2)
_FORGET_BLOCKS = (5, 6)
_Q_A, _Q_R, _I_R, _G_R, _GATE_A, _GATE_R = (i * D_MODEL for i in range(6))
_K_A = 6 * D_MODEL
_V_A = _K_A + KV_WIDTH


def kernel(x, ln_in_g, ln_in_b, lb_logits, w_in, b_in, attn_sinks, rec_norm_g, w_proj_attn, w_proj_rec, w_out,
           ln1_g, ln1_b, router_w, router_bias, expert_w_gu, expert_w_down, shared_w_gu, shared_w_down,
           ln2_g, ln2_b):
    batch, seq, d = x.shape
    n = batch * seq
    depth = w_in.shape[0]
    n_blocks = n * TOP_K // ROW_BLOCK + N_EXPERTS
    rows = n_blocks * ROW_BLOCK

    p = jax.nn.softmax(lb_logits.astype(F32), axis=0)
    cum = jnp.cumsum(p, axis=0)
    lower = cum - cum[0:1]
    log_lb = jnp.log(lower)
    log1m_lb = jnp.log1p(-lower)

    w_in_bf = w_in.astype(BF16)
    h, hb = layer_norm_in(x.reshape(n, d), ln_in_g, ln_in_b)
    for l in range(depth):
        proj = in_proj(hb, w_in_bf, b_in, l, _MAIN_BLOCKS, BF16, "in_proj_main")
        proj_f = in_proj(hb, w_in_bf, b_in, l, _FORGET_BLOCKS, F32, "in_proj_forget")
        attn = swa_attention(proj, attn_sinks[l], batch, seq, _Q_A, _K_A, _V_A)
        rec = hgrn2(proj, proj_f, log_lb[l], log1m_lb[l], rec_norm_g[l], batch, seq, _Q_R, _I_R, _G_R)
        h, hp = merge_outproj_ln(h, attn, rec, proj, _GATE_A, _GATE_R,
                                 w_proj_attn[l].astype(BF16), w_proj_rec[l].astype(BF16), w_out[l].astype(BF16),
                                 ln1_g[l], ln1_b[l])

        rw_t = router_w[l].T
        rw_hi = rw_t.astype(BF16)
        rw_lo = (rw_t - rw_hi.astype(F32)).astype(BF16)
        idx_t, gate_t, rank_t, cnt = router(h, rw_hi, rw_lo, router_bias[l])
        counts = cnt[:, 0]
        padded = (counts + ROW_BLOCK - 1) // ROW_BLOCK * ROW_BLOCK
        pad_end = jnp.cumsum(padded)
        pad_start = pad_end - padded
        pos_t = slot_positions(idx_t, rank_t, pad_start)
        plan = expert_plan(counts, pad_start, pad_end, n_blocks)

        xs = sc_scatter_rows(hp, pos_t, rows)
        ys = expert_ffn(xs, plan, expert_w_gu, expert_w_down, l)
        nc = n // MOE_CHUNKS
        gate, out = gate_t.T, None
        for c in range(MOE_CHUNKS):
            pos_c = lax.slice_in_dim(pos_t, c * nc, (c + 1) * nc, axis=1).reshape(TOP_K * nc)
            y_slots = sc_gather_rows(ys, pos_c).reshape(2, TOP_K, nc, QUARTER)
            out = combine_shared_ln(h, gate, y_slots, c, out, shared_w_gu[l].astype(BF16),
                                    shared_w_down[l].astype(BF16), ln2_g[l], ln2_b[l])
        h, hb = out
    return h.reshape(batch, seq, d)
```

```python
import functools

import jax
import jax.numpy as jnp
from jax import lax
from jax.experimental import pallas as pl
from jax.experimental.pallas import tpu as pltpu
from jax.experimental.pallas import tpu_sc as plsc

F32 = jnp.float32
BF16 = jnp.bfloat16
U32 = jnp.uint32
I32 = jnp.int32

D_MODEL = 1024
QUARTER = D_MODEL // 4
CHUNK = 64
ATTN_HEADS = 16
ATTN_KV_HEADS = 4
ATTN_HEAD_DIM = 64
ATTN_GROUP = ATTN_HEADS // ATTN_KV_HEADS
WIN_CHUNKS = 2
KV_WIDTH = ATTN_KV_HEADS * ATTN_HEAD_DIM
REC_HEADS = 8
REC_DIM = 128
N_EXPERTS = 256
TOP_K = 8
N_GROUPS = 8
GROUP_SIZE = N_EXPERTS // N_GROUPS
TOPK_GROUPS = 4
EXPERT_FF = 256
ROUTED_SCALE = 2.5
DEPTH = 2
DEEPNORM_ALPHA = (2 * DEPTH) ** 0.25
LN_EPS = 1e-5
RMS_EPS = 1e-5
NEG_INF = float("-inf")

SUBLANES = 8
VMEM_LIMIT_BYTES = 48 * 1024 * 1024

LN_ROWS = 512
PROJ_ROWS = 4096
PROJ_COLS = 512
ATTN_Q_BLOCK = 256
ATTN_STAGE_LAG = 3
REC_TILE = 256
REC_BLOCK = 64
REC_SUB = 16
MERGE_ROWS = 512
MERGE_PIECE = 256
ROUTER_ROWS = 256
MOE_ROWS = 256
MOE_CHUNKS = 4
ROW_BLOCK = 512

DECAY_LIMIT = 60.0


def _params(*sem):
    return pltpu.CompilerParams(dimension_semantics=sem, vmem_limit_bytes=VMEM_LIMIT_BYTES)


def _layer_norm_rows(x, g, b):
    mu = jnp.mean(x, axis=-1, keepdims=True)
    xc = x - mu
    var = jnp.mean(xc * xc, axis=-1, keepdims=True)
    return xc * lax.rsqrt(var + LN_EPS) * g + b


def _sigmoid(x):
    return 1.0 / (1.0 + jnp.exp(-x))


def _silu(x):
    return x * _sigmoid(x)


def _pack_pair(lo, hi):
    lo_bits = pltpu.bitcast(lo.astype(BF16).astype(F32), U32)
    hi_bits = pltpu.bitcast(hi.astype(BF16).astype(F32), U32)
    return lax.shift_right_logical(lo_bits, U32(16)) | (hi_bits & U32(0xFFFF0000))


def _unpack_pair(w):
    lo = pltpu.bitcast(lax.shift_left(w, U32(16)), F32)
    hi = pltpu.bitcast(w & U32(0xFFFF0000), F32)
    return lo, hi


def _store_planes(ref, x):
    q = QUARTER
    ref[0] = _pack_pair(x[:, 0 * q:1 * q], x[:, 1 * q:2 * q])
    ref[1] = _pack_pair(x[:, 2 * q:3 * q], x[:, 3 * q:4 * q])


def _load_quarters(plane0, plane1):
    return _unpack_pair(plane0) + _unpack_pair(plane1)


def _ln_in_kernel(x_ref, g_ref, b_ref, h_ref, hb_ref):
    h = _layer_norm_rows(x_ref[...], g_ref[...], b_ref[...])
    h_ref[...] = h
    hb_ref[...] = h.astype(BF16)


def layer_norm_in(x, g, b):
    n, d = x.shape
    row = lambda i: (i, 0)
    const = lambda i: (0, 0)
    return pl.pallas_call(
        _ln_in_kernel,
        out_shape=(jax.ShapeDtypeStruct((n, d), F32), jax.ShapeDtypeStruct((n, d), BF16)),
        grid=(n // LN_ROWS,),
        in_specs=[pl.BlockSpec((LN_ROWS, d), row), pl.BlockSpec((1, d), const), pl.BlockSpec((1, d), const)],
        out_specs=(pl.BlockSpec((LN_ROWS, d), row), pl.BlockSpec((LN_ROWS, d), row)),
        compiler_params=_params("parallel"),
        name="ln_in",
    )(x, g.reshape(1, d), b.reshape(1, d))


def _in_proj_kernel(perm_ref, x_ref, w_ref, b_ref, o_ref):
    del perm_ref
    acc = jnp.dot(x_ref[...], w_ref[0], preferred_element_type=F32)
    o_ref[...] = (acc + b_ref[0]).astype(o_ref.dtype)


def in_proj(xb, w, b, layer, col_blocks, out_dtype, name):
    n, k = xb.shape
    perm = jnp.asarray(col_blocks, I32)
    nblk = len(col_blocks)
    tm = min(PROJ_ROWS, n)
    return pl.pallas_call(
        _in_proj_kernel,
        out_shape=jax.ShapeDtypeStruct((n, nblk * PROJ_COLS), out_dtype),
        grid_spec=pltpu.PrefetchScalarGridSpec(
            num_scalar_prefetch=1,
            grid=(n // tm, nblk),
            in_specs=[pl.BlockSpec((tm, k), lambda i, j, p: (i, 0)),
                      pl.BlockSpec((1, k, PROJ_COLS), lambda i, j, p: (layer, 0, p[j])),
                      pl.BlockSpec((1, 1, PROJ_COLS), lambda i, j, p: (layer, 0, p[j]))],
            out_specs=pl.BlockSpec((tm, PROJ_COLS), lambda i, j, p: (i, j))),
        compiler_params=_params("parallel", "arbitrary"),
        name=name,
    )(perm, xb, w, b.reshape(b.shape[0], 1, -1))


def _attn_kernel(sink_ref, q_ref, kp_ref, kc_ref, vp_ref, vc_ref, o_ref, s_ref, e_ref):
    i = pl.program_id(1)
    half = ATTN_Q_BLOCK // 2
    qc = lax.broadcasted_iota(I32, (half, ATTN_Q_BLOCK), 0) // CHUNK
    kc = lax.broadcasted_iota(I32, (half, ATTN_Q_BLOCK), 1) // CHUNK
    valid = (kc >= qc) & (kc <= qc + WIN_CHUNKS)
    first = jnp.where(i == 0, WIN_CHUNKS, 0)
    scale = ATTN_HEAD_DIM ** -0.5
    windows = ((jnp.concatenate([kp_ref[half:, :], kc_ref[:half, :]], axis=0),
                jnp.concatenate([vp_ref[half:, :], vc_ref[:half, :]], axis=0),
                jnp.where(valid & (kc >= first), 0.0, NEG_INF)),
               (kc_ref[...], vc_ref[...], jnp.where(valid, 0.0, NEG_INF)))
    for part, (k, _, mask_bias) in enumerate(windows):
        rows = slice(part * half, (part + 1) * half)
        for h in range(ATTN_HEADS):
            kv = h // ATTN_GROUP
            qh = q_ref[rows, h * ATTN_HEAD_DIM:(h + 1) * ATTN_HEAD_DIM] * scale
            kh = k[:, kv * ATTN_HEAD_DIM:(kv + 1) * ATTN_HEAD_DIM]
            s_ref[part * ATTN_HEADS + h] = (
                lax.dot_general(qh, kh, (((1,), (1,)), ((), ())), preferred_element_type=F32) + mask_bias)
    units = [(part, h) for part in range(2) for h in range(ATTN_HEADS)]

    def row_max(part, h):
        return jnp.maximum(jnp.max(s_ref[part * ATTN_HEADS + h], axis=-1, keepdims=True), sink_ref[h])

    def exponentials(part, h, m):
        e_ref[part * ATTN_HEADS + h] = jnp.exp(s_ref[part * ATTN_HEADS + h] - m).astype(BF16)
        return jnp.exp(sink_ref[h] - m)

    ones = jnp.ones((ATTN_Q_BLOCK, ATTN_HEAD_DIM), BF16)

    def values(part, h, sink_term):
        kv = h // ATTN_GROUP
        vh = windows[part][1][:, kv * ATTN_HEAD_DIM:(kv + 1) * ATTN_HEAD_DIM]
        e = e_ref[part * ATTN_HEADS + h]
        denom = jnp.dot(e, ones, preferred_element_type=F32) + sink_term
        oh = jnp.dot(e, vh, preferred_element_type=F32) / denom
        rows = slice(part * half, (part + 1) * half)
        o_ref[rows, h * ATTN_HEAD_DIM:(h + 1) * ATTN_HEAD_DIM] = oh.astype(o_ref.dtype)

    maxes, rdens = {}, {}
    lag = ATTN_STAGE_LAG
    for t in range(len(units) + 2 * lag):
        if t < len(units):
            maxes[t] = row_max(*units[t])
        if 0 <= t - lag < len(units):
            rdens[t - lag] = exponentials(*units[t - lag], maxes.pop(t - lag))
        if 0 <= t - 2 * lag < len(units):
            values(*units[t - 2 * lag], rdens.pop(t - 2 * lag))


def swa_attention(proj, sinks, batch, seq, q_col, k_col, v_col):
    n = batch * seq
    nb = seq // ATTN_Q_BLOCK
    qb, kb, vb = q_col // D_MODEL, k_col // KV_WIDTH, v_col // KV_WIDTH

    def cur(col):
        return lambda b, i, s: (b * nb + i, col)

    def prev(col):
        return lambda b, i, s: (b * nb + jnp.maximum(i - 1, 0), col)

    return pl.pallas_call(
        _attn_kernel,
        out_shape=jax.ShapeDtypeStruct((n, D_MODEL), BF16),
        grid_spec=pltpu.PrefetchScalarGridSpec(
            num_scalar_prefetch=1,
            grid=(batch, nb),
            in_specs=[pl.BlockSpec((ATTN_Q_BLOCK, D_MODEL), cur(qb)),
                      pl.BlockSpec((ATTN_Q_BLOCK, KV_WIDTH), prev(kb)),
                      pl.BlockSpec((ATTN_Q_BLOCK, KV_WIDTH), cur(kb)),
                      pl.BlockSpec((ATTN_Q_BLOCK, KV_WIDTH), prev(vb)),
                      pl.BlockSpec((ATTN_Q_BLOCK, KV_WIDTH), cur(vb))],
            out_specs=pl.BlockSpec((ATTN_Q_BLOCK, D_MODEL), lambda b, i, s: (b * nb + i, 0)),
            scratch_shapes=[pltpu.VMEM((2 * ATTN_HEADS, ATTN_Q_BLOCK // 2, ATTN_Q_BLOCK), F32),
                            pltpu.VMEM((2 * ATTN_HEADS, ATTN_Q_BLOCK // 2, ATTN_Q_BLOCK), BF16)]),
        compiler_params=_params("parallel", "parallel"),
        name="swa_attention",
    )(sinks.astype(F32), proj, proj, proj, proj, proj)


def _hgrn_kernel(q_ref, i_ref, g_ref, f_ref, loglb_ref, log1mlb_ref, ng_ref, o_ref,
                 b_ref, k_ref, qs_ref, lf_ref, intra_ref, inter_ref, qd_ref, kt_ref, *st_ref):
    @pl.when(pl.program_id(1) == 0)
    def _():
        for s_ref in st_ref:
            s_ref[...] = jnp.zeros_like(s_ref)

    fl = f_ref[...]
    log_sig = jnp.minimum(fl, 0.0) - jnp.log(1.0 + jnp.exp(-jnp.abs(fl)))
    a = loglb_ref[...]
    c = log1mlb_ref[...] + log_sig
    log_f = jnp.maximum(a, c) + jnp.log(1.0 + jnp.exp(-jnp.abs(a - c)))
    lf_ref[...] = log_f
    k_ref[...] = 1.0 - jnp.exp(log_f)
    qs_ref[...] = _silu(q_ref[...].astype(F32))

    def prefix_sum(x, width):
        pos = lax.broadcasted_iota(I32, x.shape, 0) % width
        shift = 1
        while shift < width:
            x = x + jnp.where(pos >= shift, pltpu.roll(x, shift, axis=0), 0.0)
            shift *= 2
        return x

    half = REC_BLOCK // 2
    ti = lax.broadcasted_iota(I32, (REC_TILE, REC_TILE), 0)
    si = lax.broadcasted_iota(I32, (REC_TILE, REC_TILE), 1)
    tm, sm = ti % REC_BLOCK, si % REC_BLOCK
    same = ti // REC_BLOCK == si // REC_BLOCK
    plus = same & (tm >= half) & (sm >= half) & (sm <= tm)
    minus = same & (tm < half) & (sm > tm) & (sm < half)
    signed = jnp.where(plus, 1.0, jnp.where(minus, -1.0, 0.0)).astype(BF16)
    hi = log_f.astype(BF16)
    rest = log_f - hi.astype(F32)
    mid = rest.astype(BF16)
    lo = (rest - mid.astype(F32)).astype(BF16)
    d = (jnp.dot(signed, hi, preferred_element_type=F32) + jnp.dot(signed, mid, preferred_element_type=F32)
         + jnp.dot(signed, lo, preferred_element_type=F32))
    b_ref[...] = d
    decay_bounded = jnp.max(jnp.abs(d)) <= DECAY_LIMIT

    t_iota = lax.broadcasted_iota(I32, (REC_SUB, REC_DIM), 0)
    nt = (((1,), (1,)), ((), ()))
    tn = (((0,), (0,)), ((), ()))

    def finish(o, r0, cols):
        ms = jnp.mean(o * o, axis=-1, keepdims=True)
        o = o * lax.rsqrt(ms + RMS_EPS) * ng_ref[...]
        gj = g_ref[pl.ds(r0, REC_SUB), cols].astype(F32)
        o_ref[pl.ds(r0, REC_SUB), cols] = (o * _silu(gj)).astype(o_ref.dtype)

    @pl.when(decay_bounded)
    def _():
        qd_ref[...] = (qs_ref[...] * jnp.exp(b_ref[...])).astype(BF16)
        kt_ref[...] = (k_ref[...] * jnp.exp(-b_ref[...])).astype(BF16)
        ri = lax.broadcasted_iota(I32, (REC_TILE, REC_TILE), 0)
        ci = lax.broadcasted_iota(I32, (REC_TILE, REC_TILE), 1)
        keep = (ri >= ci) & (ri // REC_BLOCK == ci // REC_BLOCK)
        for h in range(REC_HEADS):
            cols = slice(h * REC_DIM, (h + 1) * REC_DIM)
            att = lax.dot_general(qd_ref[:, cols], kt_ref[:, cols], nt, preferred_element_type=F32)
            att = jnp.where(keep, att, 0.0).astype(BF16)
            intra_ref[:, cols] = jnp.dot(att, i_ref[:, cols], preferred_element_type=F32)

        for j in range(REC_TILE // REC_BLOCK):
            blk = slice(j * REC_BLOCK, (j + 1) * REC_BLOCK)
            for h in range(REC_HEADS):
                cols = slice(h * REC_DIM, (h + 1) * REC_DIM)
                first, last = j * REC_BLOCK, (j + 1) * REC_BLOCK - 1
                e1 = jnp.exp(lf_ref[first:first + 1, cols] - b_ref[first:first + 1, cols])
                e2 = jnp.exp(b_ref[last:last + 1, cols])
                st_mid = st_ref[h][...] * e1
                inter_ref[blk, cols] = lax.dot_general(
                    qd_ref[blk, cols], st_mid.astype(BF16), nt, preferred_element_type=F32)
                kv_t = lax.dot_general(i_ref[blk, cols], kt_ref[blk, cols], tn, preferred_element_type=F32)
                st_ref[h][...] = (st_mid + kv_t) * e2

        for h in range(REC_HEADS):
            cols = slice(h * REC_DIM, (h + 1) * REC_DIM)
            o = intra_ref[:, cols] + inter_ref[:, cols]
            ms = jnp.mean(o * o, axis=-1, keepdims=True)
            o = o * lax.rsqrt(ms + RMS_EPS) * ng_ref[...]
            o_ref[:, cols] = (o * _silu(g_ref[:, cols].astype(F32))).astype(o_ref.dtype)

    @pl.when(jnp.logical_not(decay_bounded))
    def _():
        b_ref[...] = prefix_sum(lf_ref[...], REC_SUB)
        ones = jnp.ones((REC_DIM, REC_DIM), BF16)

        def step(j, carry):
            r0 = pl.multiple_of(j * REC_SUB, REC_SUB)
            for h in range(REC_HEADS):
                cols = slice(h * REC_DIM, (h + 1) * REC_DIM)
                bj = b_ref[pl.ds(r0, REC_SUB), cols]
                kj = k_ref[pl.ds(r0, REC_SUB), cols]
                qj = qs_ref[pl.ds(r0, REC_SUB), cols]
                vj = i_ref[pl.ds(r0, REC_SUB), cols].astype(F32)
                st = st_ref[h][...]
                qd = (qj * jnp.exp(bj)).astype(BF16)
                o = lax.dot_general(qd, st.astype(BF16), nt, preferred_element_type=F32)
                parts = []
                for s in range(REC_SUB):
                    dec = jnp.exp(jnp.where(t_iota >= s, bj - bj[s:s + 1, :], NEG_INF))
                    parts.append((qj * dec * kj[s:s + 1, :]).astype(BF16))
                pstack = jnp.concatenate(parts, axis=0)
                rsum = jnp.dot(pstack, ones, preferred_element_type=F32)
                for s in range(REC_SUB):
                    o = o + rsum[s * REC_SUB:(s + 1) * REC_SUB, :] * vj[s:s + 1, :]
                b_end = bj[REC_SUB - 1:REC_SUB, :]
                kd = (kj * jnp.exp(b_end - bj)).astype(BF16)
                kv_t = lax.dot_general(vj.astype(BF16), kd, tn, preferred_element_type=F32)
                st_ref[h][...] = st * jnp.exp(b_end) + kv_t
                finish(o, r0, cols)
            return carry

        lax.fori_loop(0, REC_TILE // REC_SUB, step, 0)


def hgrn2(proj, proj_f, log_lb, log1m_lb, norm_g, batch, seq, q_col, i_col, g_col):
    n = batch * seq
    nb = seq // REC_TILE
    d = D_MODEL

    def blk(col):
        return pl.BlockSpec((REC_TILE, d), lambda b, i: (b * nb + i, col // d))

    return pl.pallas_call(
        _hgrn_kernel,
        out_shape=jax.ShapeDtypeStruct((n, d), BF16),
        grid=(batch, nb),
        in_specs=[blk(q_col), blk(i_col), blk(g_col),
                  pl.BlockSpec((REC_TILE, d), lambda b, i: (b * nb + i, 0)),
                  pl.BlockSpec((1, d), lambda b, i: (0, 0)),
                  pl.BlockSpec((1, d), lambda b, i: (0, 0)),
                  pl.BlockSpec((1, REC_DIM), lambda b, i: (0, 0))],
        out_specs=pl.BlockSpec((REC_TILE, d), lambda b, i: (b * nb + i, 0)),
        scratch_shapes=[pltpu.VMEM((REC_TILE, d), F32) for _ in range(6)]
                       + [pltpu.VMEM((REC_TILE, d), BF16),
                        pltpu.VMEM((REC_TILE, d), BF16)]
                       + [pltpu.VMEM((REC_DIM, REC_DIM), F32) for _ in range(REC_HEADS)],
        compiler_params=_params("parallel", "arbitrary"),
        name="hgrn2",
    )(proj, proj, proj, proj_f, log_lb.reshape(1, d), log1m_lb.reshape(1, d), norm_g.reshape(1, REC_DIM))


def _merge_kernel(h_ref, attn_ref, rec_ref, ga_ref, gr_ref, wpa_ref, wpr_ref, wo_ref, g_ref, b_ref,
                  o_ref, op_ref):
    tm = h_ref.shape[0]
    pieces = [slice(p * MERGE_PIECE, (p + 1) * MERGE_PIECE) for p in range(tm // MERGE_PIECE)]
    branches = [(jnp.dot(attn_ref[rows, :], wpa_ref[...], preferred_element_type=F32),
                 jnp.dot(rec_ref[rows, :], wpr_ref[...], preferred_element_type=F32)) for rows in pieces]
    merged = [(_sigmoid(ga_ref[rows, :].astype(F32)) * a + _sigmoid(gr_ref[rows, :].astype(F32)) * r).astype(BF16)
              for rows, (a, r) in zip(pieces, branches)]
    ys = [jnp.dot(m, wo_ref[...], preferred_element_type=F32) for m in merged]
    for rows, y in zip(pieces, ys):
        h1 = _layer_norm_rows(DEEPNORM_ALPHA * h_ref[rows, :] + y, g_ref[...], b_ref[...])
        o_ref[rows, :] = h1
        op_ref[0, rows, :] = _pack_pair(h1[:, 0 * QUARTER:1 * QUARTER], h1[:, 1 * QUARTER:2 * QUARTER])
        op_ref[1, rows, :] = _pack_pair(h1[:, 2 * QUARTER:3 * QUARTER], h1[:, 3 * QUARTER:4 * QUARTER])


def merge_outproj_ln(h, attn, rec, proj, ga_col, gr_col, wpa, wpr, wo, g, b):
    n, d = h.shape
    tm = MERGE_ROWS
    row = lambda i: (i, 0)
    const = lambda i: (0, 0)
    return pl.pallas_call(
        _merge_kernel,
        out_shape=(jax.ShapeDtypeStruct((n, d), F32), jax.ShapeDtypeStruct((2, n, QUARTER), U32)),
        grid=(n // tm,),
        in_specs=[pl.BlockSpec((tm, d), row), pl.BlockSpec((tm, d), row), pl.BlockSpec((tm, d), row),
                  pl.BlockSpec((tm, d), lambda i: (i, ga_col // d)),
                  pl.BlockSpec((tm, d), lambda i: (i, gr_col // d)),
                  pl.BlockSpec((d, d), const), pl.BlockSpec((d, d), const), pl.BlockSpec((d, d), const),
                  pl.BlockSpec((1, d), const), pl.BlockSpec((1, d), const)],
        out_specs=(pl.BlockSpec((tm, d), row), pl.BlockSpec((2, tm, QUARTER), lambda i: (0, i, 0))),
        compiler_params=_params("parallel"),
        name="merge_outproj_ln",
    )(h, attn, rec, proj, proj, wpa, wpr, wo, g.reshape(1, d), b.reshape(1, d))


def _router_kernel(h_ref, whi_ref, wlo_ref, bias_ref, idx_ref, gate_ref, rank_ref, cnt_ref, carry_ref):
    @pl.when(pl.program_id(0) == 0)
    def _():
        carry_ref[...] = jnp.zeros_like(carry_ref)

    tm = h_ref.shape[0]
    h = h_ref[...]
    h_hi = h.astype(BF16)
    h_lo = (h - h_hi.astype(F32)).astype(BF16)
    nt = (((1,), (1,)), ((), ()))
    logits = (lax.dot_general(whi_ref[...], h_hi, nt, preferred_element_type=F32)
              + lax.dot_general(whi_ref[...], h_lo, nt, preferred_element_type=F32)
              + lax.dot_general(wlo_ref[...], h_hi, nt, preferred_element_type=F32))
    scores = _sigmoid(logits)
    sel = scores + bias_ref[...]
    e_iota = lax.broadcasted_iota(I32, (N_EXPERTS, tm), 0)

    g_iota = lax.broadcasted_iota(I32, (N_GROUPS, tm), 0)
    l_iota = lax.broadcasted_iota(I32, (GROUP_SIZE, tm), 0)
    grp = jnp.zeros((N_GROUPS, tm), F32)
    for g in range(N_GROUPS):
        sg = sel[g * GROUP_SIZE:(g + 1) * GROUP_SIZE, :]
        m1 = jnp.max(sg, axis=0, keepdims=True)
        i1 = jnp.min(jnp.where(sg == m1, l_iota, GROUP_SIZE), axis=0, keepdims=True)
        m2 = jnp.max(jnp.where(l_iota == i1, NEG_INF, sg), axis=0, keepdims=True)
        grp = jnp.where(g_iota == g, m1 + m2, grp)
    gsel = jnp.zeros((N_GROUPS, tm), I32)
    for _ in range(TOPK_GROUPS):
        m = jnp.max(grp, axis=0, keepdims=True)
        gi = jnp.min(jnp.where(grp == m, g_iota, N_GROUPS), axis=0, keepdims=True)
        hit = g_iota == gi
        gsel = jnp.where(hit, 1, gsel)
        grp = jnp.where(hit, NEG_INF, grp)
    masked = []
    for g in range(N_GROUPS):
        sg = sel[g * GROUP_SIZE:(g + 1) * GROUP_SIZE, :]
        masked.append(jnp.where(gsel[g:g + 1, :] > 0, sg, NEG_INF))
    selm = jnp.concatenate(masked, axis=0)

    k_iota = lax.broadcasted_iota(I32, (TOP_K, tm), 0)
    idx = jnp.zeros((TOP_K, tm), I32)
    gate = jnp.zeros((TOP_K, tm), F32)
    member = jnp.zeros((N_EXPERTS, tm), F32)
    for k in range(TOP_K):
        m = jnp.max(selm, axis=0, keepdims=True)
        ei = jnp.min(jnp.where(selm == m, e_iota, N_EXPERTS), axis=0, keepdims=True)
        hit = e_iota == ei
        gk = jnp.sum(jnp.where(hit, scores, 0.0), axis=0, keepdims=True)
        idx = jnp.where(k_iota == k, ei, idx)
        gate = jnp.where(k_iota == k, gk, gate)
        member = jnp.where(hit, 1.0, member)
        selm = jnp.where(hit, NEG_INF, selm)
    gate = gate / jnp.sum(gate, axis=0, keepdims=True) * ROUTED_SCALE

    upper = lax.broadcasted_iota(I32, (tm, tm), 0) < lax.broadcasted_iota(I32, (tm, tm), 1)
    before = jnp.dot(member.astype(BF16), upper.astype(BF16), preferred_element_type=F32) + carry_ref[...]
    rank = jnp.zeros((TOP_K, tm), F32)
    for k in range(TOP_K):
        rk = jnp.sum(jnp.where(e_iota == idx[k:k + 1, :], before, 0.0), axis=0, keepdims=True)
        rank = jnp.where(k_iota == k, rk, rank)
    carry_ref[...] = carry_ref[...] + jnp.sum(member, axis=1, keepdims=True)

    idx_ref[...] = idx
    gate_ref[...] = gate
    rank_ref[...] = rank.astype(I32)
    cnt_ref[...] = jnp.broadcast_to(carry_ref[...], cnt_ref.shape).astype(I32)


def router(h, w_t_hi, w_t_lo, bias):
    n, d = h.shape
    tm = ROUTER_ROWS
    tok = lambda i: (0, i)
    const = lambda i: (0, 0)
    return pl.pallas_call(
        _router_kernel,
        out_shape=(jax.ShapeDtypeStruct((TOP_K, n), I32),
                   jax.ShapeDtypeStruct((TOP_K, n), F32),
                   jax.ShapeDtypeStruct((TOP_K, n), I32),
                   jax.ShapeDtypeStruct((N_EXPERTS, 128), I32)),
        grid=(n // tm,),
        in_specs=[pl.BlockSpec((tm, d), lambda i: (i, 0)),
                  pl.BlockSpec((N_EXPERTS, d), const),
                  pl.BlockSpec((N_EXPERTS, d), const),
                  pl.BlockSpec((N_EXPERTS, 1), const)],
        out_specs=(pl.BlockSpec((TOP_K, tm), tok), pl.BlockSpec((TOP_K, tm), tok),
                   pl.BlockSpec((TOP_K, tm), tok), pl.BlockSpec((N_EXPERTS, 128), const)),
        scratch_shapes=[pltpu.VMEM((N_EXPERTS, 1), F32)],
        compiler_params=_params("arbitrary"),
        name="router",
    )(h, w_t_hi, w_t_lo, bias.reshape(N_EXPERTS, 1))


def _slot_pos_kernel(idx_ref, rank_ref, start_ref, pos_ref):
    tm = idx_ref.shape[1]
    e_iota = lax.broadcasted_iota(I32, (N_EXPERTS, tm), 0)
    k_iota = lax.broadcasted_iota(I32, (TOP_K, tm), 0)
    idx = idx_ref[...]
    start = start_ref[...]
    base = jnp.zeros((TOP_K, tm), F32)
    for k in range(TOP_K):
        bk = jnp.sum(jnp.where(e_iota == idx[k:k + 1, :], start, 0.0), axis=0, keepdims=True)
        base = jnp.where(k_iota == k, bk, base)
    pos_ref[...] = base.astype(I32) + rank_ref[...]


def slot_positions(idx_t, rank_t, pad_start):
    n = idx_t.shape[1]
    tm = 1024
    tok = lambda i: (0, i)
    return pl.pallas_call(
        _slot_pos_kernel,
        out_shape=jax.ShapeDtypeStruct((TOP_K, n), I32),
        grid=(n // tm,),
        in_specs=[pl.BlockSpec((TOP_K, tm), tok), pl.BlockSpec((TOP_K, tm), tok),
                  pl.BlockSpec((N_EXPERTS, 1), lambda i: (0, 0))],
        out_specs=pl.BlockSpec((TOP_K, tm), tok),
        compiler_params=_params("parallel"),
        name="slot_positions",
    )(idx_t, rank_t, pad_start.astype(F32).reshape(N_EXPERTS, 1))


SC_WINDOW = 128
SC_WORDS = QUARTER


def _sc_mesh():
    return plsc.VectorSubcoreMesh(core_axis_name="core", subcore_axis_name="subcore")


def sc_scatter_rows(src, pos_t, rows):
    n = src.shape[1]
    src2 = src.reshape(2 * n, SC_WORDS)
    idx2 = jnp.concatenate([pos_t, pos_t + rows], axis=1)

    @pl.kernel(out_type=jax.ShapeDtypeStruct((2 * rows, SC_WORDS), src.dtype), mesh=_sc_mesh(), scratch_types=[])
    def scatter_kernel(x_hbm, i_hbm, o_hbm):
        def body(x_vmem, i_vmem):
            pltpu.sync_copy(x_vmem, o_hbm.at[i_vmem.at[0]])

        pltpu.emit_pipeline(
            body,
            grid=(2 * n // SC_WINDOW, TOP_K),
            in_specs=[pl.BlockSpec((SC_WINDOW, SC_WORDS), index_map=lambda i, k: (i, 0)),
                      pl.BlockSpec((1, SC_WINDOW), index_map=lambda i, k: (k, i))],
            out_specs=[],
            core_axis_name=("core", "subcore"),
            dimension_semantics=(pltpu.PARALLEL, pltpu.ARBITRARY),
        )(x_hbm, i_hbm)

    return scatter_kernel(src2, idx2).reshape(2, rows, SC_WORDS)


def sc_gather_rows(src, pos):
    r = src.shape[1]
    m = pos.shape[0]
    src2 = src.reshape(2 * r, SC_WORDS)
    idx2 = jnp.concatenate([pos, pos + r]).reshape(1, 2 * m)

    @pl.kernel(out_type=jax.ShapeDtypeStruct((2 * m, SC_WORDS), src.dtype), mesh=_sc_mesh(), scratch_types=[])
    def gather_kernel(x_hbm, i_hbm, o_hbm):
        def body(i_vmem, o_vmem):
            pltpu.sync_copy(x_hbm.at[i_vmem.at[0]], o_vmem)

        pltpu.emit_pipeline(
            body,
            grid=(2 * m // SC_WINDOW,),
            in_specs=[pl.BlockSpec((1, SC_WINDOW), index_map=lambda i: (0, i))],
            out_specs=[pl.BlockSpec((SC_WINDOW, SC_WORDS), index_map=lambda i: (i, 0))],
            core_axis_name=("core", "subcore"),
            dimension_semantics=(pltpu.PARALLEL,),
        )(i_hbm, o_hbm)

    return gather_kernel(src2, idx2).reshape(2, m, SC_WORDS)


def _expert_kernel(blk_expert_ref, blk_valid_ref, blk_first_ref, blk_slot_ref, blk_next_ref, n_used_ref,
                   x_ref, wgu_hbm, wd_hbm, y_ref, wgu_buf, wd_buf, sem, *, layer):
    j = pl.program_id(0)
    used = j < n_used_ref[0]

    def weight_copies(e, slot):
        return (pltpu.make_async_copy(wgu_hbm.at[layer, e], wgu_buf.at[slot], sem.at[0, slot]),
                pltpu.make_async_copy(wd_hbm.at[layer, e], wd_buf.at[slot], sem.at[1, slot]))

    @pl.when(used)
    def _():
        slot = blk_slot_ref[j]

        @pl.when(blk_first_ref[j] == 1)
        def _():
            @pl.when(j == 0)
            def _():
                for copy in weight_copies(blk_expert_ref[0], slot):
                    copy.start()

            for copy in weight_copies(blk_expert_ref[j], slot):
                copy.wait()

            @pl.when(blk_next_ref[j] >= 0)
            def _():
                for copy in weight_copies(blk_next_ref[j], 1 - slot):
                    copy.start()

        valid = lax.broadcasted_iota(I32, x_ref.shape[1:], 0) < blk_valid_ref[j]
        quarters = _load_quarters(jnp.where(valid, x_ref[0], U32(0)), jnp.where(valid, x_ref[1], U32(0)))
        gu = sum(jnp.dot(xq.astype(BF16), wgu_buf[slot, c * QUARTER:(c + 1) * QUARTER, :].astype(BF16),
                         preferred_element_type=F32)
                 for c, xq in enumerate(quarters))
        act = (_silu(gu[:, :EXPERT_FF]) * gu[:, EXPERT_FF:]).astype(BF16)
        _store_planes(y_ref, jnp.dot(act, wd_buf[slot].astype(BF16), preferred_element_type=F32))

    @pl.when(jnp.logical_not(used))
    def _():
        y_ref[...] = jnp.zeros_like(y_ref)


def expert_plan(counts, pad_start, pad_end, n_blocks):
    blk_row = jnp.arange(n_blocks, dtype=I32) * ROW_BLOCK
    blk_expert = jnp.minimum(jnp.sum(pad_end[None, :] <= blk_row[:, None], axis=1), N_EXPERTS - 1).astype(I32)
    blk_valid = jnp.clip(counts[blk_expert] - (blk_row - pad_start[blk_expert]), 0, ROW_BLOCK).astype(I32)
    used = blk_row < pad_end[-1]
    prev_expert = jnp.concatenate([jnp.full((1,), -1, I32), blk_expert[:-1]])
    blk_first = (used & (blk_expert != prev_expert)).astype(I32)
    blk_slot = ((jnp.cumsum(blk_first) - 1) % 2).astype(I32)
    expert_ids = jnp.arange(N_EXPERTS, dtype=I32)
    nonempty_at_or_after = lax.cummin(jnp.where(counts > 0, expert_ids, N_EXPERTS), reverse=True)
    nonempty_after = jnp.concatenate([nonempty_at_or_after[1:], jnp.full((1,), N_EXPERTS, I32)])
    nxt = nonempty_after[blk_expert]
    blk_next = jnp.where(nxt < N_EXPERTS, nxt, -1).astype(I32)
    n_used = (pad_end[-1:] // ROW_BLOCK).astype(I32)
    return blk_expert, blk_valid, blk_first, blk_slot, blk_next, n_used


def expert_ffn(xs, plan, w_gu, w_down, layer):
    _, rows, w = xs.shape
    d = D_MODEL
    n_blocks = rows // ROW_BLOCK

    def row_map(j, be, bv, bf, bs, bn, nu):
        return (0, jnp.minimum(j, nu[0] - 1), 0)

    return pl.pallas_call(
        functools.partial(_expert_kernel, layer=layer),
        out_shape=jax.ShapeDtypeStruct(xs.shape, U32),
        grid_spec=pltpu.PrefetchScalarGridSpec(
            num_scalar_prefetch=6,
            grid=(n_blocks,),
            in_specs=[pl.BlockSpec((2, ROW_BLOCK, w), row_map),
                      pl.BlockSpec(memory_space=pl.ANY),
                      pl.BlockSpec(memory_space=pl.ANY)],
            out_specs=pl.BlockSpec((2, ROW_BLOCK, w), lambda j, be, bv, bf, bs, bn, nu: (0, j, 0)),
            scratch_shapes=[pltpu.VMEM((2, d, 2 * EXPERT_FF), F32),
                            pltpu.VMEM((2, EXPERT_FF, d), F32),
                            pltpu.SemaphoreType.DMA((2, 2))]),
        compiler_params=_params("arbitrary"),
        name="expert_ffn",
    )(*plan, xs, w_gu, w_down)


def _combine_kernel(h_ref, gate_ref, ys_ref, sgu_ref, sd_ref, g_ref, b_ref, *refs):
    o_ref, ob_ref = refs[-2:]
    h = h_ref[...]
    gu = jnp.dot(h.astype(BF16), sgu_ref[...], preferred_element_type=F32)
    act = _silu(gu[:, :EXPERT_FF]) * gu[:, EXPERT_FF:]
    y = jnp.dot(act.astype(BF16), sd_ref[...], preferred_element_type=F32)
    gate = gate_ref[...]
    acc = [y[:, c * QUARTER:(c + 1) * QUARTER] for c in range(4)]
    for k in range(TOP_K):
        gk = gate[:, k:k + 1]
        acc = [a + gk * q for a, q in zip(acc, _load_quarters(ys_ref[0, k], ys_ref[1, k]))]
    out = _layer_norm_rows(DEEPNORM_ALPHA * h + jnp.concatenate(acc, axis=-1), g_ref[...], b_ref[...])
    o_ref[...] = out
    ob_ref[...] = out.astype(BF16)


def combine_shared_ln(h, gate, y_slots, chunk, prev_out, sh_gu, sh_down, g, b):
    n, d = h.shape
    tm = MOE_ROWS
    steps = y_slots.shape[2] // tm
    row = lambda i: (i + chunk * steps, 0)
    const = lambda i: (0, 0)
    passthrough = () if prev_out is None else tuple(prev_out)
    n_in = 7
    return pl.pallas_call(
        _combine_kernel,
        out_shape=(jax.ShapeDtypeStruct((n, d), F32), jax.ShapeDtypeStruct((n, d), BF16)),
        grid=(steps,),
        in_specs=[pl.BlockSpec((tm, d), row),
                  pl.BlockSpec((tm, TOP_K), row),
                  pl.BlockSpec((2, TOP_K, tm, QUARTER), lambda i: (0, 0, i, 0)),
                  pl.BlockSpec((d, 2 * EXPERT_FF), const),
                  pl.BlockSpec((EXPERT_FF, d), const),
                  pl.BlockSpec((1, d), const), pl.BlockSpec((1, d), const)]
                 + [pl.BlockSpec(memory_space=pl.ANY) for _ in passthrough],
        out_specs=(pl.BlockSpec((tm, d), row), pl.BlockSpec((tm, d), row)),
        input_output_aliases={n_in + i: i for i in range(len(passthrough))},
        compiler_params=_params("parallel"),
        name="combine_shared_ln",
    )(h, gate, y_slots, sh_gu, sh_down, g.reshape(1, d), b.reshape(1, d), *passthrough)


_MAIN_BLOCKS = (0, 1, 3, 4, 7, 8, 9, 10, 11, 12, 13, 14, 2)
_FORGET_BLOCKS = (5, 6)
_Q_A, _Q_R, _I_R, _G_R, _GATE_A, _GATE_R = (i * D_MODEL for i in range(6))
_K_A = 6 * D_MODEL
_V_A = _K_A + KV_WIDTH


def kernel(x, ln_in_g, ln_in_b, lb_logits, w_in, b_in, attn_sinks, rec_norm_g, w_proj_attn, w_proj_rec, w_out,
           ln1_g, ln1_b, router_w, router_bias, expert_w_gu, expert_w_down, shared_w_gu, shared_w_down,
           ln2_g, ln2_b):
    batch, seq, d = x.shape
    n = batch * seq
    depth = w_in.shape[0]
    n_blocks = n * TOP_K // ROW_BLOCK + N_EXPERTS
    rows = n_blocks * ROW_BLOCK

    p = jax.nn.softmax(lb_logits.astype(F32), axis=0)
    cum = jnp.cumsum(p, axis=0)
    lower = cum - cum[0:1]
    log_lb = jnp.log(lower)
    log1m_lb = jnp.log1p(-lower)

    w_in_bf = w_in.astype(BF16)
    h, hb = layer_norm_in(x.reshape(n, d), ln_in_g, ln_in_b)
    for l in range(depth):
        proj = in_proj(hb, w_in_bf, b_in, l, _MAIN_BLOCKS, BF16, "in_proj_main")
        proj_f = in_proj(hb, w_in_bf, b_in, l, _FORGET_BLOCKS, F32, "in_proj_forget")
        attn = swa_attention(proj, attn_sinks[l], batch, seq, _Q_A, _K_A, _V_A)
        rec = hgrn2(proj, proj_f, log_lb[l], log1m_lb[l], rec_norm_g[l], batch, seq, _Q_R, _I_R, _G_R)
        h, hp = merge_outproj_ln(h, attn, rec, proj, _GATE_A, _GATE_R,
                                 w_proj_attn[l].astype(BF16), w_proj_rec[l].astype(BF16), w_out[l].astype(BF16),
                                 ln1_g[l], ln1_b[l])

        rw_t = router_w[l].T
        rw_hi = rw_t.astype(BF16)
        rw_lo = (rw_t - rw_hi.astype(F32)).astype(BF16)
        idx_t, gate_t, rank_t, cnt = router(h, rw_hi, rw_lo, router_bias[l])
        counts = cnt[:, 0]
        padded = (counts + ROW_BLOCK - 1) // ROW_BLOCK * ROW_BLOCK
        pad_end = jnp.cumsum(padded)
        pad_start = pad_end - padded
        pos_t = slot_positions(idx_t, rank_t, pad_start)
        plan = expert_plan(counts, pad_start, pad_end, n_blocks)

        xs = sc_scatter_rows(hp, pos_t, rows)
        ys = expert_ffn(xs, plan, expert_w_gu, expert_w_down, l)
        nc = n // MOE_CHUNKS
        gate, out = gate_t.T, None
        for c in range(MOE_CHUNKS):
            pos_c = lax.slice_in_dim(pos_t, c * nc, (c + 1) * nc, axis=1).reshape(TOP_K * nc)
            y_slots = sc_gather_rows(ys, pos_c).reshape(2, TOP_K, nc, QUARTER)
            out = combine_shared_ln(h, gate, y_slots, c, out, shared_w_gu[l].astype(BF16),
                                    shared_w_down[l].astype(BF16), ln2_g[l], ln2_b[l])
        h, hb = out
    return h.reshape(batch, seq, d)
```

```python
import functools

import jax
import jax.numpy as jnp
from jax import lax
from jax.experimental import pallas as pl
from jax.experimental.pallas import tpu as pltpu
from jax.experimental.pallas import tpu_sc as plsc

F32 = jnp.float32
BF16 = jnp.bfloat16
U32 = jnp.uint32
I32 = jnp.int32

D_MODEL = 1024
QUARTER = D_MODEL // 4
CHUNK = 64
ATTN_HEADS = 16
ATTN_KV_HEADS = 4
ATTN_HEAD_DIM = 64
ATTN_GROUP = ATTN_HEADS // ATTN_KV_HEADS
WIN_CHUNKS = 2
KV_WIDTH = ATTN_KV_HEADS * ATTN_HEAD_DIM
REC_HEADS = 8
REC_DIM = 128
N_EXPERTS = 256
TOP_K = 8
N_GROUPS = 8
GROUP_SIZE = N_EXPERTS // N_GROUPS
TOPK_GROUPS = 4
EXPERT_FF = 256
ROUTED_SCALE = 2.5
DEPTH = 2
DEEPNORM_ALPHA = (2 * DEPTH) ** 0.25
LN_EPS = 1e-5
RMS_EPS = 1e-5
NEG_INF = float("-inf")

SUBLANES = 8
LANES = 128
VMEM_LIMIT_BYTES = 48 * 1024 * 1024

LN_ROWS = 512
PROJ_ROWS = 4096
PROJ_COLS = 512
ATTN_Q_BLOCK = 256
ATTN_STAGE_LAG = 3
REC_TILE = 256
REC_BLOCK = 64
REC_SUB = 16
MERGE_ROWS = 512
MERGE_PIECE = 256
ROUTER_ROWS = 256
SLOT_POS_COLS = 1024
MOE_ROWS = 512
MOE_CHUNKS = 4
ROW_BLOCK = 512

DECAY_LIMIT = 60.0


def _params(*sem):
    return pltpu.CompilerParams(dimension_semantics=sem, vmem_limit_bytes=VMEM_LIMIT_BYTES)


def _layer_norm_rows(x, g, b):
    mu = jnp.mean(x, axis=-1, keepdims=True)
    xc = x - mu
    var = jnp.mean(xc * xc, axis=-1, keepdims=True)
    return xc * lax.rsqrt(var + LN_EPS) * g + b


def _sigmoid(x):
    return 1.0 / (1.0 + jnp.exp(-x))


def _silu(x):
    return x * _sigmoid(x)


def _pack_pair(lo, hi):
    lo_bits = pltpu.bitcast(lo.astype(BF16).astype(F32), U32)
    hi_bits = pltpu.bitcast(hi.astype(BF16).astype(F32), U32)
    return lax.shift_right_logical(lo_bits, U32(16)) | (hi_bits & U32(0xFFFF0000))


def _unpack_pair(w):
    lo = pltpu.bitcast(lax.shift_left(w, U32(16)), F32)
    hi = pltpu.bitcast(w & U32(0xFFFF0000), F32)
    return lo, hi


def _store_planes(ref, x):
    q = QUARTER
    ref[0] = _pack_pair(x[:, 0 * q:1 * q], x[:, 1 * q:2 * q])
    ref[1] = _pack_pair(x[:, 2 * q:3 * q], x[:, 3 * q:4 * q])


def _load_quarters(plane0, plane1):
    return _unpack_pair(plane0) + _unpack_pair(plane1)


def _ln_in_kernel(x_ref, g_ref, b_ref, h_ref, hb_ref):
    h = _layer_norm_rows(x_ref[...], g_ref[...], b_ref[...])
    h_ref[...] = h
    hb_ref[...] = h.astype(BF16)


def layer_norm_in(x, g, b):
    n, d = x.shape
    row = lambda i: (i, 0)
    const = lambda i: (0, 0)
    return pl.pallas_call(
        _ln_in_kernel,
        out_shape=(jax.ShapeDtypeStruct((n, d), F32), jax.ShapeDtypeStruct((n, d), BF16)),
        grid=(n // LN_ROWS,),
        in_specs=[pl.BlockSpec((LN_ROWS, d), row), pl.BlockSpec((1, d), const), pl.BlockSpec((1, d), const)],
        out_specs=(pl.BlockSpec((LN_ROWS, d), row), pl.BlockSpec((LN_ROWS, d), row)),
        compiler_params=_params("parallel"),
        name="ln_in",
    )(x, g.reshape(1, d), b.reshape(1, d))


def _in_proj_kernel(perm_ref, x_ref, w_ref, b_ref, o_ref):
    del perm_ref
    acc = jnp.dot(x_ref[...], w_ref[0], preferred_element_type=F32)
    o_ref[...] = (acc + b_ref[0]).astype(o_ref.dtype)


def in_proj(xb, w, b, layer, col_blocks, out_dtype, name):
    n, k = xb.shape
    perm = jnp.asarray(col_blocks, I32)
    nblk = len(col_blocks)
    tm = min(PROJ_ROWS, n)
    return pl.pallas_call(
        _in_proj_kernel,
        out_shape=jax.ShapeDtypeStruct((n, nblk * PROJ_COLS), out_dtype),
        grid_spec=pltpu.PrefetchScalarGridSpec(
            num_scalar_prefetch=1,
            grid=(n // tm, nblk),
            in_specs=[pl.BlockSpec((tm, k), lambda i, j, p: (i, 0)),
                      pl.BlockSpec((1, k, PROJ_COLS), lambda i, j, p: (layer, 0, p[j])),
                      pl.BlockSpec((1, 1, PROJ_COLS), lambda i, j, p: (layer, 0, p[j]))],
            out_specs=pl.BlockSpec((tm, PROJ_COLS), lambda i, j, p: (i, j))),
        compiler_params=_params("parallel", "arbitrary"),
        name=name,
    )(perm, xb, w, b.reshape(b.shape[0], 1, -1))


def _attn_kernel(sink_ref, q_ref, kp_ref, kc_ref, vp_ref, vc_ref, o_ref, s_ref, e_ref):
    i = pl.program_id(1)
    half = ATTN_Q_BLOCK // 2
    qc = lax.broadcasted_iota(I32, (half, ATTN_Q_BLOCK), 0) // CHUNK
    kc = lax.broadcasted_iota(I32, (half, ATTN_Q_BLOCK), 1) // CHUNK
    valid = (kc >= qc) & (kc <= qc + WIN_CHUNKS)
    first = jnp.where(i == 0, WIN_CHUNKS, 0)
    scale = ATTN_HEAD_DIM ** -0.5
    windows = ((jnp.concatenate([kp_ref[half:, :], kc_ref[:half, :]], axis=0),
                jnp.concatenate([vp_ref[half:, :], vc_ref[:half, :]], axis=0),
                jnp.where(valid & (kc >= first), 0.0, NEG_INF)),
               (kc_ref[...], vc_ref[...], jnp.where(valid, 0.0, NEG_INF)))
    for part, (k, _, mask_bias) in enumerate(windows):
        rows = slice(part * half, (part + 1) * half)
        for h in range(ATTN_HEADS):
            kv = h // ATTN_GROUP
            qh = q_ref[rows, h * ATTN_HEAD_DIM:(h + 1) * ATTN_HEAD_DIM] * scale
            kh = k[:, kv * ATTN_HEAD_DIM:(kv + 1) * ATTN_HEAD_DIM]
            s_ref[part * ATTN_HEADS + h] = (
                lax.dot_general(qh, kh, (((1,), (1,)), ((), ())), preferred_element_type=F32) + mask_bias)
    units = [(part, h) for part in range(2) for h in range(ATTN_HEADS)]

    def row_max(part, h):
        return jnp.maximum(jnp.max(s_ref[part * ATTN_HEADS + h], axis=-1, keepdims=True), sink_ref[h])

    def exponentials(part, h, m):
        e_ref[part * ATTN_HEADS + h] = jnp.exp(s_ref[part * ATTN_HEADS + h] - m).astype(BF16)
        return jnp.exp(sink_ref[h] - m)

    ones = jnp.ones((ATTN_Q_BLOCK, ATTN_HEAD_DIM), BF16)

    def values(part, h, sink_term):
        kv = h // ATTN_GROUP
        vh = windows[part][1][:, kv * ATTN_HEAD_DIM:(kv + 1) * ATTN_HEAD_DIM]
        e = e_ref[part * ATTN_HEADS + h]
        denom = jnp.dot(e, ones, preferred_element_type=F32) + sink_term
        oh = jnp.dot(e, vh, preferred_element_type=F32) / denom
        rows = slice(part * half, (part + 1) * half)
        o_ref[rows, h * ATTN_HEAD_DIM:(h + 1) * ATTN_HEAD_DIM] = oh.astype(o_ref.dtype)

    maxes, rdens = {}, {}
    lag = ATTN_STAGE_LAG
    for t in range(len(units) + 2 * lag):
        if t < len(units):
            maxes[t] = row_max(*units[t])
        if 0 <= t - lag < len(units):
            rdens[t - lag] = exponentials(*units[t - lag], maxes.pop(t - lag))
        if 0 <= t - 2 * lag < len(units):
            values(*units[t - 2 * lag], rdens.pop(t - 2 * lag))


def swa_attention(proj, sinks, batch, seq, q_col, k_col, v_col):
    n = batch * seq
    nb = seq // ATTN_Q_BLOCK
    qb, kb, vb = q_col // D_MODEL, k_col // KV_WIDTH, v_col // KV_WIDTH

    def cur(col):
        return lambda b, i, s: (b * nb + i, col)

    def prev(col):
        return lambda b, i, s: (b * nb + jnp.maximum(i - 1, 0), col)

    return pl.pallas_call(
        _attn_kernel,
        out_shape=jax.ShapeDtypeStruct((n, D_MODEL), BF16),
        grid_spec=pltpu.PrefetchScalarGridSpec(
            num_scalar_prefetch=1,
            grid=(batch, nb),
            in_specs=[pl.BlockSpec((ATTN_Q_BLOCK, D_MODEL), cur(qb)),
                      pl.BlockSpec((ATTN_Q_BLOCK, KV_WIDTH), prev(kb)),
                      pl.BlockSpec((ATTN_Q_BLOCK, KV_WIDTH), cur(kb)),
                      pl.BlockSpec((ATTN_Q_BLOCK, KV_WIDTH), prev(vb)),
                      pl.BlockSpec((ATTN_Q_BLOCK, KV_WIDTH), cur(vb))],
            out_specs=pl.BlockSpec((ATTN_Q_BLOCK, D_MODEL), lambda b, i, s: (b * nb + i, 0)),
            scratch_shapes=[pltpu.VMEM((2 * ATTN_HEADS, ATTN_Q_BLOCK // 2, ATTN_Q_BLOCK), F32),
                            pltpu.VMEM((2 * ATTN_HEADS, ATTN_Q_BLOCK // 2, ATTN_Q_BLOCK), BF16)]),
        compiler_params=_params("parallel", "parallel"),
        name="swa_attention",
    )(sinks.astype(F32), proj, proj, proj, proj, proj)


def _hgrn_kernel(q_ref, i_ref, g_ref, f_ref, loglb_ref, log1mlb_ref, ng_ref, o_ref,
                 b_ref, k_ref, qs_ref, lf_ref, intra_ref, inter_ref, qd_ref, kt_ref, *st_ref):
    @pl.when(pl.program_id(1) == 0)
    def _():
        for s_ref in st_ref:
            s_ref[...] = jnp.zeros_like(s_ref)

    fl = f_ref[...]
    log_sig = jnp.minimum(fl, 0.0) - jnp.log(1.0 + jnp.exp(-jnp.abs(fl)))
    a = loglb_ref[...]
    c = log1mlb_ref[...] + log_sig
    log_f = jnp.maximum(a, c) + jnp.log(1.0 + jnp.exp(-jnp.abs(a - c)))
    lf_ref[...] = log_f
    k_ref[...] = 1.0 - jnp.exp(log_f)
    qs_ref[...] = _silu(q_ref[...].astype(F32))

    def prefix_sum(x, width):
        pos = lax.broadcasted_iota(I32, x.shape, 0) % width
        shift = 1
        while shift < width:
            x = x + jnp.where(pos >= shift, pltpu.roll(x, shift, axis=0), 0.0)
            shift *= 2
        return x

    half = REC_BLOCK // 2
    ti = lax.broadcasted_iota(I32, (REC_TILE, REC_TILE), 0)
    si = lax.broadcasted_iota(I32, (REC_TILE, REC_TILE), 1)
    tm, sm = ti % REC_BLOCK, si % REC_BLOCK
    same = ti // REC_BLOCK == si // REC_BLOCK
    plus = same & (tm >= half) & (sm >= half) & (sm <= tm)
    minus = same & (tm < half) & (sm > tm) & (sm < half)
    signed = jnp.where(plus, 1.0, jnp.where(minus, -1.0, 0.0)).astype(BF16)
    hi = log_f.astype(BF16)
    rest = log_f - hi.astype(F32)
    mid = rest.astype(BF16)
    lo = (rest - mid.astype(F32)).astype(BF16)
    d = (jnp.dot(signed, hi, preferred_element_type=F32) + jnp.dot(signed, mid, preferred_element_type=F32)
         + jnp.dot(signed, lo, preferred_element_type=F32))
    b_ref[...] = d
    decay_bounded = jnp.max(jnp.abs(d)) <= DECAY_LIMIT

    t_iota = lax.broadcasted_iota(I32, (REC_SUB, REC_DIM), 0)
    nt = (((1,), (1,)), ((), ()))
    tn = (((0,), (0,)), ((), ()))

    def finish(o, r0, cols):
        ms = jnp.mean(o * o, axis=-1, keepdims=True)
        o = o * lax.rsqrt(ms + RMS_EPS) * ng_ref[...]
        gj = g_ref[pl.ds(r0, REC_SUB), cols].astype(F32)
        o_ref[pl.ds(r0, REC_SUB), cols] = (o * _silu(gj)).astype(o_ref.dtype)

    @pl.when(decay_bounded)
    def _():
        qd_ref[...] = (qs_ref[...] * jnp.exp(b_ref[...])).astype(BF16)
        kt_ref[...] = (k_ref[...] * jnp.exp(-b_ref[...])).astype(BF16)
        ri = lax.broadcasted_iota(I32, (REC_TILE, REC_TILE), 0)
        ci = lax.broadcasted_iota(I32, (REC_TILE, REC_TILE), 1)
        keep = (ri >= ci) & (ri // REC_BLOCK == ci // REC_BLOCK)
        for h in range(REC_HEADS):
            cols = slice(h * REC_DIM, (h + 1) * REC_DIM)
            att = lax.dot_general(qd_ref[:, cols], kt_ref[:, cols], nt, preferred_element_type=F32)
            att = jnp.where(keep, att, 0.0).astype(BF16)
            intra_ref[:, cols] = jnp.dot(att, i_ref[:, cols], preferred_element_type=F32)

        for j in range(REC_TILE // REC_BLOCK):
            blk = slice(j * REC_BLOCK, (j + 1) * REC_BLOCK)
            for h in range(REC_HEADS):
                cols = slice(h * REC_DIM, (h + 1) * REC_DIM)
                first, last = j * REC_BLOCK, (j + 1) * REC_BLOCK - 1
                e1 = jnp.exp(lf_ref[first:first + 1, cols] - b_ref[first:first + 1, cols])
                e2 = jnp.exp(b_ref[last:last + 1, cols])
                st_mid = st_ref[h][...] * e1
                inter_ref[blk, cols] = lax.dot_general(
                    qd_ref[blk, cols], st_mid.astype(BF16), nt, preferred_element_type=F32)
                kv_t = lax.dot_general(i_ref[blk, cols], kt_ref[blk, cols], tn, preferred_element_type=F32)
                st_ref[h][...] = (st_mid + kv_t) * e2

        for h in range(REC_HEADS):
            cols = slice(h * REC_DIM, (h + 1) * REC_DIM)
            o = intra_ref[:, cols] + inter_ref[:, cols]
            ms = jnp.mean(o * o, axis=-1, keepdims=True)
            o = o * lax.rsqrt(ms + RMS_EPS) * ng_ref[...]
            o_ref[:, cols] = (o * _silu(g_ref[:, cols].astype(F32))).astype(o_ref.dtype)

    @pl.when(jnp.logical_not(decay_bounded))
    def _():
        b_ref[...] = prefix_sum(lf_ref[...], REC_SUB)
        ones = jnp.ones((REC_DIM, REC_DIM), BF16)

        def step(j, carry):
            r0 = pl.multiple_of(j * REC_SUB, REC_SUB)
            for h in range(REC_HEADS):
                cols = slice(h * REC_DIM, (h + 1) * REC_DIM)
                bj = b_ref[pl.ds(r0, REC_SUB), cols]
                kj = k_ref[pl.ds(r0, REC_SUB), cols]
                qj = qs_ref[pl.ds(r0, REC_SUB), cols]
                vj = i_ref[pl.ds(r0, REC_SUB), cols].astype(F32)
                st = st_ref[h][...]
                qd = (qj * jnp.exp(bj)).astype(BF16)
                o = lax.dot_general(qd, st.astype(BF16), nt, preferred_element_type=F32)
                parts = []
                for s in range(REC_SUB):
                    dec = jnp.exp(jnp.where(t_iota >= s, bj - bj[s:s + 1, :], NEG_INF))
                    parts.append((qj * dec * kj[s:s + 1, :]).astype(BF16))
                pstack = jnp.concatenate(parts, axis=0)
                rsum = jnp.dot(pstack, ones, preferred_element_type=F32)
                for s in range(REC_SUB):
                    o = o + rsum[s * REC_SUB:(s + 1) * REC_SUB, :] * vj[s:s + 1, :]
                b_end = bj[REC_SUB - 1:REC_SUB, :]
                kd = (kj * jnp.exp(b_end - bj)).astype(BF16)
                kv_t = lax.dot_general(vj.astype(BF16), kd, tn, preferred_element_type=F32)
                st_ref[h][...] = st * jnp.exp(b_end) + kv_t
                finish(o, r0, cols)
            return carry

        lax.fori_loop(0, REC_TILE // REC_SUB, step, 0)


def hgrn2(proj, proj_f, log_lb, log1m_lb, norm_g, batch, seq, q_col, i_col, g_col):
    n = batch * seq
    nb = seq // REC_TILE
    d = D_MODEL

    def blk(col):
        return pl.BlockSpec((REC_TILE, d), lambda b, i: (b * nb + i, col // d))

    return pl.pallas_call(
        _hgrn_kernel,
        out_shape=jax.ShapeDtypeStruct((n, d), BF16),
        grid=(batch, nb),
        in_specs=[blk(q_col), blk(i_col), blk(g_col),
                  pl.BlockSpec((REC_TILE, d), lambda b, i: (b * nb + i, 0)),
                  pl.BlockSpec((1, d), lambda b, i: (0, 0)),
                  pl.BlockSpec((1, d), lambda b, i: (0, 0)),
                  pl.BlockSpec((1, REC_DIM), lambda b, i: (0, 0))],
        out_specs=pl.BlockSpec((REC_TILE, d), lambda b, i: (b * nb + i, 0)),
        scratch_shapes=[pltpu.VMEM((REC_TILE, d), F32) for _ in range(6)]
                       + [pltpu.VMEM((REC_TILE, d), BF16),
                        pltpu.VMEM((REC_TILE, d), BF16)]
                       + [pltpu.VMEM((REC_DIM, REC_DIM), F32) for _ in range(REC_HEADS)],
        compiler_params=_params("parallel", "arbitrary"),
        name="hgrn2",
    )(proj, proj, proj, proj_f, log_lb.reshape(1, d), log1m_lb.reshape(1, d), norm_g.reshape(1, REC_DIM))


def _merge_kernel(h_ref, attn_ref, rec_ref, ga_ref, gr_ref, wpa_ref, wpr_ref, wo_ref, g_ref, b_ref,
                  o_ref, op_ref):
    tm = h_ref.shape[0]
    pieces = [slice(p * MERGE_PIECE, (p + 1) * MERGE_PIECE) for p in range(tm // MERGE_PIECE)]
    branches = [(jnp.dot(attn_ref[rows, :], wpa_ref[...], preferred_element_type=F32),
                 jnp.dot(rec_ref[rows, :], wpr_ref[...], preferred_element_type=F32)) for rows in pieces]
    merged = [(_sigmoid(ga_ref[rows, :].astype(F32)) * a + _sigmoid(gr_ref[rows, :].astype(F32)) * r).astype(BF16)
              for rows, (a, r) in zip(pieces, branches)]
    ys = [jnp.dot(m, wo_ref[...], preferred_element_type=F32) for m in merged]
    for rows, y in zip(pieces, ys):
        h1 = _layer_norm_rows(DEEPNORM_ALPHA * h_ref[rows, :] + y, g_ref[...], b_ref[...])
        o_ref[rows, :] = h1
        op_ref[0, rows, :] = _pack_pair(h1[:, 0 * QUARTER:1 * QUARTER], h1[:, 1 * QUARTER:2 * QUARTER])
        op_ref[1, rows, :] = _pack_pair(h1[:, 2 * QUARTER:3 * QUARTER], h1[:, 3 * QUARTER:4 * QUARTER])


def merge_outproj_ln(h, attn, rec, proj, ga_col, gr_col, wpa, wpr, wo, g, b):
    n, d = h.shape
    tm = MERGE_ROWS
    row = lambda i: (i, 0)
    const = lambda i: (0, 0)
    return pl.pallas_call(
        _merge_kernel,
        out_shape=(jax.ShapeDtypeStruct((n, d), F32), jax.ShapeDtypeStruct((2, n, QUARTER), U32)),
        grid=(n // tm,),
        in_specs=[pl.BlockSpec((tm, d), row), pl.BlockSpec((tm, d), row), pl.BlockSpec((tm, d), row),
                  pl.BlockSpec((tm, d), lambda i: (i, ga_col // d)),
                  pl.BlockSpec((tm, d), lambda i: (i, gr_col // d)),
                  pl.BlockSpec((d, d), const), pl.BlockSpec((d, d), const), pl.BlockSpec((d, d), const),
                  pl.BlockSpec((1, d), const), pl.BlockSpec((1, d), const)],
        out_specs=(pl.BlockSpec((tm, d), row), pl.BlockSpec((2, tm, QUARTER), lambda i: (0, i, 0))),
        compiler_params=_params("parallel"),
        name="merge_outproj_ln",
    )(h, attn, rec, proj, proj, wpa, wpr, wo, g.reshape(1, d), b.reshape(1, d))


def _router_kernel(h_ref, whi_ref, wlo_ref, bias_ref, idx_ref, gate_ref, rank_ref, cnt_ref, carry_ref):
    @pl.when(pl.program_id(0) == 0)
    def _():
        carry_ref[...] = jnp.zeros_like(carry_ref)

    tm = h_ref.shape[0]
    h = h_ref[...]
    h_hi = h.astype(BF16)
    h_lo = (h - h_hi.astype(F32)).astype(BF16)
    nt = (((1,), (1,)), ((), ()))
    logits = (lax.dot_general(whi_ref[...], h_hi, nt, preferred_element_type=F32)
              + lax.dot_general(whi_ref[...], h_lo, nt, preferred_element_type=F32)
              + lax.dot_general(wlo_ref[...], h_hi, nt, preferred_element_type=F32))
    scores = _sigmoid(logits)
    sel = scores + bias_ref[...]
    e_iota = lax.broadcasted_iota(I32, (N_EXPERTS, tm), 0)

    g_iota = lax.broadcasted_iota(I32, (N_GROUPS, tm), 0)
    l_iota = lax.broadcasted_iota(I32, (GROUP_SIZE, tm), 0)
    grp = jnp.zeros((N_GROUPS, tm), F32)
    for g in range(N_GROUPS):
        sg = sel[g * GROUP_SIZE:(g + 1) * GROUP_SIZE, :]
        m1 = jnp.max(sg, axis=0, keepdims=True)
        i1 = jnp.min(jnp.where(sg == m1, l_iota, GROUP_SIZE), axis=0, keepdims=True)
        m2 = jnp.max(jnp.where(l_iota == i1, NEG_INF, sg), axis=0, keepdims=True)
        grp = jnp.where(g_iota == g, m1 + m2, grp)
    gsel = jnp.zeros((N_GROUPS, tm), I32)
    for _ in range(TOPK_GROUPS):
        m = jnp.max(grp, axis=0, keepdims=True)
        gi = jnp.min(jnp.where(grp == m, g_iota, N_GROUPS), axis=0, keepdims=True)
        hit = g_iota == gi
        gsel = jnp.where(hit, 1, gsel)
        grp = jnp.where(hit, NEG_INF, grp)
    masked = []
    for g in range(N_GROUPS):
        sg = sel[g * GROUP_SIZE:(g + 1) * GROUP_SIZE, :]
        masked.append(jnp.where(gsel[g:g + 1, :] > 0, sg, NEG_INF))
    selm = jnp.concatenate(masked, axis=0)

    k_iota = lax.broadcasted_iota(I32, (TOP_K, tm), 0)
    idx = jnp.zeros((TOP_K, tm), I32)
    gate = jnp.zeros((TOP_K, tm), F32)
    member = jnp.zeros((N_EXPERTS, tm), F32)
    for k in range(TOP_K):
        m = jnp.max(selm, axis=0, keepdims=True)
        ei = jnp.min(jnp.where(selm == m, e_iota, N_EXPERTS), axis=0, keepdims=True)
        hit = e_iota == ei
        gk = jnp.sum(jnp.where(hit, scores, 0.0), axis=0, keepdims=True)
        idx = jnp.where(k_iota == k, ei, idx)
        gate = jnp.where(k_iota == k, gk, gate)
        member = jnp.where(hit, 1.0, member)
        selm = jnp.where(hit, NEG_INF, selm)
    gate = gate / jnp.sum(gate, axis=0, keepdims=True) * ROUTED_SCALE

    upper = lax.broadcasted_iota(I32, (tm, tm), 0) < lax.broadcasted_iota(I32, (tm, tm), 1)
    before = jnp.dot(member.astype(BF16), upper.astype(BF16), preferred_element_type=F32) + carry_ref[...]
    rank = jnp.zeros((TOP_K, tm), F32)
    for k in range(TOP_K):
        rk = jnp.sum(jnp.where(e_iota == idx[k:k + 1, :], before, 0.0), axis=0, keepdims=True)
        rank = jnp.where(k_iota == k, rk, rank)
    carry_ref[...] = carry_ref[...] + jnp.sum(member, axis=1, keepdims=True)

    idx_ref[...] = idx
    gate_ref[...] = gate
    rank_ref[...] = rank.astype(I32)
    cnt_ref[...] = jnp.broadcast_to(carry_ref[...], cnt_ref.shape).astype(I32)


def router(h, w_t_hi, w_t_lo, bias):
    n, d = h.shape
    tm = ROUTER_ROWS
    tok = lambda i: (0, i)
    const = lambda i: (0, 0)
    return pl.pallas_call(
        _router_kernel,
        out_shape=(jax.ShapeDtypeStruct((TOP_K, n), I32),
                   jax.ShapeDtypeStruct((TOP_K, n), F32),
                   jax.ShapeDtypeStruct((TOP_K, n), I32),
                   jax.ShapeDtypeStruct((N_EXPERTS, LANES), I32)),
        grid=(n // tm,),
        in_specs=[pl.BlockSpec((tm, d), lambda i: (i, 0)),
                  pl.BlockSpec((N_EXPERTS, d), const),
                  pl.BlockSpec((N_EXPERTS, d), const),
                  pl.BlockSpec((N_EXPERTS, 1), const)],
        out_specs=(pl.BlockSpec((TOP_K, tm), tok), pl.BlockSpec((TOP_K, tm), tok),
                   pl.BlockSpec((TOP_K, tm), tok), pl.BlockSpec((N_EXPERTS, LANES), const)),
        scratch_shapes=[pltpu.VMEM((N_EXPERTS, 1), F32)],
        compiler_params=_params("arbitrary"),
        name="router",
    )(h, w_t_hi, w_t_lo, bias.reshape(N_EXPERTS, 1))


def _slot_pos_kernel(idx_ref, rank_ref, start_ref, pos_ref):
    tm = idx_ref.shape[1]
    e_iota = lax.broadcasted_iota(I32, (N_EXPERTS, tm), 0)
    k_iota = lax.broadcasted_iota(I32, (TOP_K, tm), 0)
    idx = idx_ref[...]
    start = start_ref[...]
    base = jnp.zeros((TOP_K, tm), F32)
    for k in range(TOP_K):
        bk = jnp.sum(jnp.where(e_iota == idx[k:k + 1, :], start, 0.0), axis=0, keepdims=True)
        base = jnp.where(k_iota == k, bk, base)
    pos_ref[...] = base.astype(I32) + rank_ref[...]


def slot_positions(idx_t, rank_t, pad_start):
    n = idx_t.shape[1]
    tm = SLOT_POS_COLS
    tok = lambda i: (0, i)
    return pl.pallas_call(
        _slot_pos_kernel,
        out_shape=jax.ShapeDtypeStruct((TOP_K, n), I32),
        grid=(n // tm,),
        in_specs=[pl.BlockSpec((TOP_K, tm), tok), pl.BlockSpec((TOP_K, tm), tok),
                  pl.BlockSpec((N_EXPERTS, 1), lambda i: (0, 0))],
        out_specs=pl.BlockSpec((TOP_K, tm), tok),
        compiler_params=_params("parallel"),
        name="slot_positions",
    )(idx_t, rank_t, pad_start.astype(F32).reshape(N_EXPERTS, 1))


SC_WINDOW = 128
SC_WORDS = QUARTER


def _sc_mesh():
    return plsc.VectorSubcoreMesh(core_axis_name="core", subcore_axis_name="subcore")


def sc_scatter_rows(src, pos_t, rows):
    n = src.shape[1]
    src2 = src.reshape(2 * n, SC_WORDS)
    idx2 = jnp.concatenate([pos_t, pos_t + rows], axis=1)

    @pl.kernel(out_type=jax.ShapeDtypeStruct((2 * rows, SC_WORDS), src.dtype), mesh=_sc_mesh(), scratch_types=[])
    def scatter_kernel(x_hbm, i_hbm, o_hbm):
        def body(x_vmem, i_vmem):
            pltpu.sync_copy(x_vmem, o_hbm.at[i_vmem.at[0]])

        pltpu.emit_pipeline(
            body,
            grid=(2 * n // SC_WINDOW, TOP_K),
            in_specs=[pl.BlockSpec((SC_WINDOW, SC_WORDS), index_map=lambda i, k: (i, 0)),
                      pl.BlockSpec((1, SC_WINDOW), index_map=lambda i, k: (k, i))],
            out_specs=[],
            core_axis_name=("core", "subcore"),
            dimension_semantics=(pltpu.PARALLEL, pltpu.ARBITRARY),
        )(x_hbm, i_hbm)

    return scatter_kernel(src2, idx2).reshape(2, rows, SC_WORDS)


def sc_gather_rows(src, pos):
    r = src.shape[1]
    m = pos.shape[0]
    src2 = src.reshape(2 * r, SC_WORDS)
    idx2 = jnp.concatenate([pos, pos + r]).reshape(1, 2 * m)

    @pl.kernel(out_type=jax.ShapeDtypeStruct((2 * m, SC_WORDS), src.dtype), mesh=_sc_mesh(), scratch_types=[])
    def gather_kernel(x_hbm, i_hbm, o_hbm):
        def body(i_vmem, o_vmem):
            pltpu.sync_copy(x_hbm.at[i_vmem.at[0]], o_vmem)

        pltpu.emit_pipeline(
            body,
            grid=(2 * m // SC_WINDOW,),
            in_specs=[pl.BlockSpec((1, SC_WINDOW), index_map=lambda i: (0, i))],
            out_specs=[pl.BlockSpec((SC_WINDOW, SC_WORDS), index_map=lambda i: (i, 0))],
            core_axis_name=("core", "subcore"),
            dimension_semantics=(pltpu.PARALLEL,),
        )(i_hbm, o_hbm)

    return gather_kernel(src2, idx2).reshape(2, m, SC_WORDS)


def _expert_kernel(blk_expert_ref, blk_valid_ref, blk_first_ref, blk_slot_ref, blk_next_ref, n_used_ref,
                   x_ref, wgu_hbm, wd_hbm, y_ref, wgu_buf, wd_buf, sem, *, layer):
    j = pl.program_id(0)
    used = j < n_used_ref[0]

    def weight_copies(e, slot):
        return (pltpu.make_async_copy(wgu_hbm.at[layer, e], wgu_buf.at[slot], sem.at[0, slot]),
                pltpu.make_async_copy(wd_hbm.at[layer, e], wd_buf.at[slot], sem.at[1, slot]))

    @pl.when(used)
    def _():
        slot = blk_slot_ref[j]

        @pl.when(blk_first_ref[j] == 1)
        def _():
            @pl.when(j == 0)
            def _():
                for copy in weight_copies(blk_expert_ref[0], slot):
                    copy.start()

            for copy in weight_copies(blk_expert_ref[j], slot):
                copy.wait()

            @pl.when(blk_next_ref[j] >= 0)
            def _():
                for copy in weight_copies(blk_next_ref[j], 1 - slot):
                    copy.start()

        valid = lax.broadcasted_iota(I32, x_ref.shape[1:], 0) < blk_valid_ref[j]
        quarters = _load_quarters(jnp.where(valid, x_ref[0], U32(0)), jnp.where(valid, x_ref[1], U32(0)))
        gu = sum(jnp.dot(xq.astype(BF16), wgu_buf[slot, c * QUARTER:(c + 1) * QUARTER, :].astype(BF16),
                         preferred_element_type=F32)
                 for c, xq in enumerate(quarters))
        act = (_silu(gu[:, :EXPERT_FF]) * gu[:, EXPERT_FF:]).astype(BF16)
        _store_planes(y_ref, jnp.dot(act, wd_buf[slot].astype(BF16), preferred_element_type=F32))

    @pl.when(jnp.logical_not(used))
    def _():
        y_ref[...] = jnp.zeros_like(y_ref)


def expert_plan(counts, pad_start, pad_end, n_blocks):
    blk_row = jnp.arange(n_blocks, dtype=I32) * ROW_BLOCK
    blk_expert = jnp.minimum(jnp.sum(pad_end[None, :] <= blk_row[:, None], axis=1), N_EXPERTS - 1).astype(I32)
    blk_valid = jnp.clip(counts[blk_expert] - (blk_row - pad_start[blk_expert]), 0, ROW_BLOCK).astype(I32)
    used = blk_row < pad_end[-1]
    prev_expert = jnp.concatenate([jnp.full((1,), -1, I32), blk_expert[:-1]])
    blk_first = (used & (blk_expert != prev_expert)).astype(I32)
    blk_slot = ((jnp.cumsum(blk_first) - 1) % 2).astype(I32)
    expert_ids = jnp.arange(N_EXPERTS, dtype=I32)
    nonempty_at_or_after = lax.cummin(jnp.where(counts > 0, expert_ids, N_EXPERTS), reverse=True)
    nonempty_after = jnp.concatenate([nonempty_at_or_after[1:], jnp.full((1,), N_EXPERTS, I32)])
    nxt = nonempty_after[blk_expert]
    blk_next = jnp.where(nxt < N_EXPERTS, nxt, -1).astype(I32)
    n_used = (pad_end[-1:] // ROW_BLOCK).astype(I32)
    return blk_expert, blk_valid, blk_first, blk_slot, blk_next, n_used


def expert_ffn(xs, plan, w_gu, w_down, layer):
    _, rows, w = xs.shape
    d = D_MODEL
    n_blocks = rows // ROW_BLOCK

    def row_map(j, be, bv, bf, bs, bn, nu):
        return (0, jnp.minimum(j, nu[0] - 1), 0)

    return pl.pallas_call(
        functools.partial(_expert_kernel, layer=layer),
        out_shape=jax.ShapeDtypeStruct(xs.shape, U32),
        grid_spec=pltpu.PrefetchScalarGridSpec(
            num_scalar_prefetch=6,
            grid=(n_blocks,),
            in_specs=[pl.BlockSpec((2, ROW_BLOCK, w), row_map),
                      pl.BlockSpec(memory_space=pl.ANY),
                      pl.BlockSpec(memory_space=pl.ANY)],
            out_specs=pl.BlockSpec((2, ROW_BLOCK, w), lambda j, be, bv, bf, bs, bn, nu: (0, j, 0)),
            scratch_shapes=[pltpu.VMEM((2, d, 2 * EXPERT_FF), F32),
                            pltpu.VMEM((2, EXPERT_FF, d), F32),
                            pltpu.SemaphoreType.DMA((2, 2))]),
        compiler_params=_params("arbitrary"),
        name="expert_ffn",
    )(*plan, xs, w_gu, w_down)


def _combine_kernel(h_ref, gate_ref, ys_ref, sgu_ref, sd_ref, g_ref, b_ref, *refs):
    o_ref, ob_ref = refs[-2:]
    h = h_ref[...]
    gu = jnp.dot(h.astype(BF16), sgu_ref[...], preferred_element_type=F32)
    act = _silu(gu[:, :EXPERT_FF]) * gu[:, EXPERT_FF:]
    y = jnp.dot(act.astype(BF16), sd_ref[...], preferred_element_type=F32)
    gate = gate_ref[...]
    acc = [y[:, c * QUARTER:(c + 1) * QUARTER] for c in range(4)]
    for k in range(TOP_K):
        gk = gate[:, k:k + 1]
        acc = [a + gk * q for a, q in zip(acc, _load_quarters(ys_ref[0, k], ys_ref[1, k]))]
    out = _layer_norm_rows(DEEPNORM_ALPHA * h + jnp.concatenate(acc, axis=-1), g_ref[...], b_ref[...])
    o_ref[...] = out
    ob_ref[...] = out.astype(BF16)


def combine_shared_ln(h, gate, y_slots, chunk, prev_out, sh_gu, sh_down, g, b):
    n, d = h.shape
    tm = MOE_ROWS
    steps = y_slots.shape[2] // tm
    row = lambda i: (i + chunk * steps, 0)
    const = lambda i: (0, 0)
    passthrough = () if prev_out is None else tuple(prev_out)
    n_in = 7
    return pl.pallas_call(
        _combine_kernel,
        out_shape=(jax.ShapeDtypeStruct((n, d), F32), jax.ShapeDtypeStruct((n, d), BF16)),
        grid=(steps,),
        in_specs=[pl.BlockSpec((tm, d), row),
                  pl.BlockSpec((tm, TOP_K), row),
                  pl.BlockSpec((2, TOP_K, tm, QUARTER), lambda i: (0, 0, i, 0)),
                  pl.BlockSpec((d, 2 * EXPERT_FF), const),
                  pl.BlockSpec((EXPERT_FF, d), const),
                  pl.BlockSpec((1, d), const), pl.BlockSpec((1, d), const)]
                 + [pl.BlockSpec(memory_space=pl.ANY) for _ in passthrough],
        out_specs=(pl.BlockSpec((tm, d), row), pl.BlockSpec((tm, d), row)),
        input_output_aliases={n_in + i: i for i in range(len(passthrough))},
        compiler_params=_params("parallel"),
        name="combine_shared_ln",
    )(h, gate, y_slots, sh_gu, sh_down, g.reshape(1, d), b.reshape(1, d), *passthrough)


_MAIN_BLOCKS = (0, 1, 3, 4, 7, 8, 9, 10, 11, 12, 13, 14, 2)
_FORGET_BLOCKS = (5, 6)
_Q_A, _Q_R, _I_R, _G_R, _GATE_A, _GATE_R = (i * D_MODEL for i in range(6))
_K_A = 6 * D_MODEL
_V_A = _K_A + KV_WIDTH


def kernel(x, ln_in_g, ln_in_b, lb_logits, w_in, b_in, attn_sinks, rec_norm_g, w_proj_attn, w_proj_rec, w_out,
           ln1_g, ln1_b, router_w, router_bias, expert_w_gu, expert_w_down, shared_w_gu, shared_w_down,
           ln2_g, ln2_b):
    batch, seq, d = x.shape
    n = batch * seq
    depth = w_in.shape[0]
    n_blocks = n * TOP_K // ROW_BLOCK + N_EXPERTS
    rows = n_blocks * ROW_BLOCK

    p = jax.nn.softmax(lb_logits.astype(F32), axis=0)
    cum = jnp.cumsum(p, axis=0)
    lower = cum - cum[0:1]
    log_lb = jnp.log(lower)
    log1m_lb = jnp.log1p(-lower)

    w_in_bf = w_in.astype(BF16)
    h, hb = layer_norm_in(x.reshape(n, d), ln_in_g, ln_in_b)
    for l in range(depth):
        proj = in_proj(hb, w_in_bf, b_in, l, _MAIN_BLOCKS, BF16, "in_proj_main")
        proj_f = in_proj(hb, w_in_bf, b_in, l, _FORGET_BLOCKS, F32, "in_proj_forget")
        attn = swa_attention(proj, attn_sinks[l], batch, seq, _Q_A, _K_A, _V_A)
        rec = hgrn2(proj, proj_f, log_lb[l], log1m_lb[l], rec_norm_g[l], batch, seq, _Q_R, _I_R, _G_R)
        h, hp = merge_outproj_ln(h, attn, rec, proj, _GATE_A, _GATE_R,
                                 w_proj_attn[l].astype(BF16), w_proj_rec[l].astype(BF16), w_out[l].astype(BF16),
                                 ln1_g[l], ln1_b[l])

        rw_t = router_w[l].T
        rw_hi = rw_t.astype(BF16)
        rw_lo = (rw_t - rw_hi.astype(F32)).astype(BF16)
        idx_t, gate_t, rank_t, cnt = router(h, rw_hi, rw_lo, router_bias[l])
        counts = cnt[:, 0]
        padded = (counts + ROW_BLOCK - 1) // ROW_BLOCK * ROW_BLOCK
        pad_end = jnp.cumsum(padded)
        pad_start = pad_end - padded
        pos_t = slot_positions(idx_t, rank_t, pad_start)
        plan = expert_plan(counts, pad_start, pad_end, n_blocks)

        xs = sc_scatter_rows(hp, pos_t, rows)
        ys = expert_ffn(xs, plan, expert_w_gu, expert_w_down, l)
        nc = n // MOE_CHUNKS
        gate, out = gate_t.T, None
        for c in range(MOE_CHUNKS):
            pos_c = lax.slice_in_dim(pos_t, c * nc, (c + 1) * nc, axis=1).reshape(TOP_K * nc)
            y_slots = sc_gather_rows(ys, pos_c).reshape(2, TOP_K, nc, QUARTER)
            out = combine_shared_ln(h, gate, y_slots, c, out, shared_w_gu[l].astype(BF16),
                                    shared_w_down[l].astype(BF16), ln2_g[l], ln2_b[l])
        h, hb = out
    return h.reshape(batch, seq, d)
```

```python
import functools

import jax
import jax.numpy as jnp
from jax import lax
from jax.experimental import pallas as pl
from jax.experimental.pallas import tpu as pltpu
from jax.experimental.pallas import tpu_sc as plsc

F32 = jnp.float32
BF16 = jnp.bfloat16
U32 = jnp.uint32
I32 = jnp.int32

D_MODEL = 1024
QUARTER = D_MODEL // 4
CHUNK = 64
ATTN_HEADS = 16
ATTN_KV_HEADS = 4
ATTN_HEAD_DIM = 64
ATTN_GROUP = ATTN_HEADS // ATTN_KV_HEADS
WIN_CHUNKS = 2
KV_WIDTH = ATTN_KV_HEADS * ATTN_HEAD_DIM
REC_HEADS = 8
REC_DIM = 128
N_EXPERTS = 256
TOP_K = 8
N_GROUPS = 8
GROUP_SIZE = N_EXPERTS // N_GROUPS
TOPK_GROUPS = 4
EXPERT_FF = 256
ROUTED_SCALE = 2.5
DEPTH = 2
DEEPNORM_ALPHA = (2 * DEPTH) ** 0.25
LN_EPS = 1e-5
RMS_EPS = 1e-5
NEG_INF = float("-inf")

SUBLANES = 8
LANES = 128
VMEM_LIMIT_BYTES = 48 * 1024 * 1024

LN_ROWS = 512
PROJ_ROWS = 4096
PROJ_COLS = 512
ATTN_Q_BLOCK = 256
ATTN_STAGE_LAG = 3
REC_TILE = 256
REC_BLOCK = 64
REC_SUB = 16
MERGE_ROWS = 512
MERGE_PIECE = 256
ROUTER_ROWS = 512
SLOT_POS_COLS = 1024
MOE_ROWS = 512
MOE_CHUNKS = 4
ROW_BLOCK = 512

DECAY_LIMIT = 60.0


def _params(*sem):
    return pltpu.CompilerParams(dimension_semantics=sem, vmem_limit_bytes=VMEM_LIMIT_BYTES)


def _layer_norm_rows(x, g, b):
    mu = jnp.mean(x, axis=-1, keepdims=True)
    xc = x - mu
    var = jnp.mean(xc * xc, axis=-1, keepdims=True)
    return xc * lax.rsqrt(var + LN_EPS) * g + b


def _sigmoid(x):
    return 1.0 / (1.0 + jnp.exp(-x))


def _silu(x):
    return x * _sigmoid(x)


def _pack_pair(lo, hi):
    lo_bits = pltpu.bitcast(lo.astype(BF16).astype(F32), U32)
    hi_bits = pltpu.bitcast(hi.astype(BF16).astype(F32), U32)
    return lax.shift_right_logical(lo_bits, U32(16)) | (hi_bits & U32(0xFFFF0000))


def _unpack_pair(w):
    lo = pltpu.bitcast(lax.shift_left(w, U32(16)), F32)
    hi = pltpu.bitcast(w & U32(0xFFFF0000), F32)
    return lo, hi


def _store_planes(ref, x):
    q = QUARTER
    ref[0] = _pack_pair(x[:, 0 * q:1 * q], x[:, 1 * q:2 * q])
    ref[1] = _pack_pair(x[:, 2 * q:3 * q], x[:, 3 * q:4 * q])


def _load_quarters(plane0, plane1):
    return _unpack_pair(plane0) + _unpack_pair(plane1)


def _ln_in_kernel(x_ref, g_ref, b_ref, h_ref, hb_ref):
    h = _layer_norm_rows(x_ref[...], g_ref[...], b_ref[...])
    h_ref[...] = h
    hb_ref[...] = h.astype(BF16)


def layer_norm_in(x, g, b):
    n, d = x.shape
    row = lambda i: (i, 0)
    const = lambda i: (0, 0)
    return pl.pallas_call(
        _ln_in_kernel,
        out_shape=(jax.ShapeDtypeStruct((n, d), F32), jax.ShapeDtypeStruct((n, d), BF16)),
        grid=(n // LN_ROWS,),
        in_specs=[pl.BlockSpec((LN_ROWS, d), row), pl.BlockSpec((1, d), const), pl.BlockSpec((1, d), const)],
        out_specs=(pl.BlockSpec((LN_ROWS, d), row), pl.BlockSpec((LN_ROWS, d), row)),
        compiler_params=_params("parallel"),
        name="ln_in",
    )(x, g.reshape(1, d), b.reshape(1, d))


def _in_proj_kernel(perm_ref, x_ref, w_ref, b_ref, o_ref):
    del perm_ref
    acc = jnp.dot(x_ref[...], w_ref[0], preferred_element_type=F32)
    o_ref[...] = (acc + b_ref[0]).astype(o_ref.dtype)


def in_proj(xb, w, b, layer, col_blocks, out_dtype, name):
    n, k = xb.shape
    perm = jnp.asarray(col_blocks, I32)
    nblk = len(col_blocks)
    tm = min(PROJ_ROWS, n)
    return pl.pallas_call(
        _in_proj_kernel,
        out_shape=jax.ShapeDtypeStruct((n, nblk * PROJ_COLS), out_dtype),
        grid_spec=pltpu.PrefetchScalarGridSpec(
            num_scalar_prefetch=1,
            grid=(n // tm, nblk),
            in_specs=[pl.BlockSpec((tm, k), lambda i, j, p: (i, 0)),
                      pl.BlockSpec((1, k, PROJ_COLS), lambda i, j, p: (layer, 0, p[j])),
                      pl.BlockSpec((1, 1, PROJ_COLS), lambda i, j, p: (layer, 0, p[j]))],
            out_specs=pl.BlockSpec((tm, PROJ_COLS), lambda i, j, p: (i, j))),
        compiler_params=_params("parallel", "arbitrary"),
        name=name,
    )(perm, xb, w, b.reshape(b.shape[0], 1, -1))


def _attn_kernel(sink_ref, q_ref, kp_ref, kc_ref, vp_ref, vc_ref, o_ref, s_ref, e_ref):
    i = pl.program_id(1)
    half = ATTN_Q_BLOCK // 2
    qc = lax.broadcasted_iota(I32, (half, ATTN_Q_BLOCK), 0) // CHUNK
    kc = lax.broadcasted_iota(I32, (half, ATTN_Q_BLOCK), 1) // CHUNK
    valid = (kc >= qc) & (kc <= qc + WIN_CHUNKS)
    first = jnp.where(i == 0, WIN_CHUNKS, 0)
    scale = ATTN_HEAD_DIM ** -0.5
    windows = ((jnp.concatenate([kp_ref[half:, :], kc_ref[:half, :]], axis=0),
                jnp.concatenate([vp_ref[half:, :], vc_ref[:half, :]], axis=0),
                jnp.where(valid & (kc >= first), 0.0, NEG_INF)),
               (kc_ref[...], vc_ref[...], jnp.where(valid, 0.0, NEG_INF)))
    for part, (k, _, mask_bias) in enumerate(windows):
        rows = slice(part * half, (part + 1) * half)
        for h in range(ATTN_HEADS):
            kv = h // ATTN_GROUP
            qh = q_ref[rows, h * ATTN_HEAD_DIM:(h + 1) * ATTN_HEAD_DIM] * scale
            kh = k[:, kv * ATTN_HEAD_DIM:(kv + 1) * ATTN_HEAD_DIM]
            s_ref[part * ATTN_HEADS + h] = (
                lax.dot_general(qh, kh, (((1,), (1,)), ((), ())), preferred_element_type=F32) + mask_bias)
    units = [(part, h) for part in range(2) for h in range(ATTN_HEADS)]

    def row_max(part, h):
        return jnp.maximum(jnp.max(s_ref[part * ATTN_HEADS + h], axis=-1, keepdims=True), sink_ref[h])

    def exponentials(part, h, m):
        e_ref[part * ATTN_HEADS + h] = jnp.exp(s_ref[part * ATTN_HEADS + h] - m).astype(BF16)
        return jnp.exp(sink_ref[h] - m)

    ones = jnp.ones((ATTN_Q_BLOCK, ATTN_HEAD_DIM), BF16)

    def values(part, h, sink_term):
        kv = h // ATTN_GROUP
        vh = windows[part][1][:, kv * ATTN_HEAD_DIM:(kv + 1) * ATTN_HEAD_DIM]
        e = e_ref[part * ATTN_HEADS + h]
        denom = jnp.dot(e, ones, preferred_element_type=F32) + sink_term
        oh = jnp.dot(e, vh, preferred_element_type=F32) / denom
        rows = slice(part * half, (part + 1) * half)
        o_ref[rows, h * ATTN_HEAD_DIM:(h + 1) * ATTN_HEAD_DIM] = oh.astype(o_ref.dtype)

    maxes, rdens = {}, {}
    lag = ATTN_STAGE_LAG
    for t in range(len(units) + 2 * lag):
        if t < len(units):
            maxes[t] = row_max(*units[t])
        if 0 <= t - lag < len(units):
            rdens[t - lag] = exponentials(*units[t - lag], maxes.pop(t - lag))
        if 0 <= t - 2 * lag < len(units):
            values(*units[t - 2 * lag], rdens.pop(t - 2 * lag))


def swa_attention(proj, sinks, batch, seq, q_col, k_col, v_col):
    n = batch * seq
    nb = seq // ATTN_Q_BLOCK
    qb, kb, vb = q_col // D_MODEL, k_col // KV_WIDTH, v_col // KV_WIDTH

    def cur(col):
        return lambda b, i, s: (b * nb + i, col)

    def prev(col):
        return lambda b, i, s: (b * nb + jnp.maximum(i - 1, 0), col)

    return pl.pallas_call(
        _attn_kernel,
        out_shape=jax.ShapeDtypeStruct((n, D_MODEL), BF16),
        grid_spec=pltpu.PrefetchScalarGridSpec(
            num_scalar_prefetch=1,
            grid=(batch, nb),
            in_specs=[pl.BlockSpec((ATTN_Q_BLOCK, D_MODEL), cur(qb)),
                      pl.BlockSpec((ATTN_Q_BLOCK, KV_WIDTH), prev(kb)),
                      pl.BlockSpec((ATTN_Q_BLOCK, KV_WIDTH), cur(kb)),
                      pl.BlockSpec((ATTN_Q_BLOCK, KV_WIDTH), prev(vb)),
                      pl.BlockSpec((ATTN_Q_BLOCK, KV_WIDTH), cur(vb))],
            out_specs=pl.BlockSpec((ATTN_Q_BLOCK, D_MODEL), lambda b, i, s: (b * nb + i, 0)),
            scratch_shapes=[pltpu.VMEM((2 * ATTN_HEADS, ATTN_Q_BLOCK // 2, ATTN_Q_BLOCK), F32),
                            pltpu.VMEM((2 * ATTN_HEADS, ATTN_Q_BLOCK // 2, ATTN_Q_BLOCK), BF16)]),
        compiler_params=_params("parallel", "parallel"),
        name="swa_attention",
    )(sinks.astype(F32), proj, proj, proj, proj, proj)


def _hgrn_kernel(q_ref, i_ref, g_ref, f_ref, loglb_ref, log1mlb_ref, ng_ref, o_ref,
                 b_ref, k_ref, qs_ref, lf_ref, intra_ref, inter_ref, qd_ref, kt_ref, *st_ref):
    @pl.when(pl.program_id(1) == 0)
    def _():
        for s_ref in st_ref:
            s_ref[...] = jnp.zeros_like(s_ref)

    fl = f_ref[...]
    log_sig = jnp.minimum(fl, 0.0) - jnp.log(1.0 + jnp.exp(-jnp.abs(fl)))
    a = loglb_ref[...]
    c = log1mlb_ref[...] + log_sig
    log_f = jnp.maximum(a, c) + jnp.log(1.0 + jnp.exp(-jnp.abs(a - c)))
    lf_ref[...] = log_f
    k_ref[...] = 1.0 - jnp.exp(log_f)
    qs_ref[...] = _silu(q_ref[...].astype(F32))

    def prefix_sum(x, width):
        pos = lax.broadcasted_iota(I32, x.shape, 0) % width
        shift = 1
        while shift < width:
            x = x + jnp.where(pos >= shift, pltpu.roll(x, shift, axis=0), 0.0)
            shift *= 2
        return x

    half = REC_BLOCK // 2
    ti = lax.broadcasted_iota(I32, (REC_TILE, REC_TILE), 0)
    si = lax.broadcasted_iota(I32, (REC_TILE, REC_TILE), 1)
    tm, sm = ti % REC_BLOCK, si % REC_BLOCK
    same = ti // REC_BLOCK == si // REC_BLOCK
    plus = same & (tm >= half) & (sm >= half) & (sm <= tm)
    minus = same & (tm < half) & (sm > tm) & (sm < half)
    signed = jnp.where(plus, 1.0, jnp.where(minus, -1.0, 0.0)).astype(BF16)
    hi = log_f.astype(BF16)
    rest = log_f - hi.astype(F32)
    mid = rest.astype(BF16)
    lo = (rest - mid.astype(F32)).astype(BF16)
    d = (jnp.dot(signed, hi, preferred_element_type=F32) + jnp.dot(signed, mid, preferred_element_type=F32)
         + jnp.dot(signed, lo, preferred_element_type=F32))
    b_ref[...] = d
    decay_bounded = jnp.max(jnp.abs(d)) <= DECAY_LIMIT

    t_iota = lax.broadcasted_iota(I32, (REC_SUB, REC_DIM), 0)
    nt = (((1,), (1,)), ((), ()))
    tn = (((0,), (0,)), ((), ()))

    def finish(o, r0, cols):
        ms = jnp.mean(o * o, axis=-1, keepdims=True)
        o = o * lax.rsqrt(ms + RMS_EPS) * ng_ref[...]
        gj = g_ref[pl.ds(r0, REC_SUB), cols].astype(F32)
        o_ref[pl.ds(r0, REC_SUB), cols] = (o * _silu(gj)).astype(o_ref.dtype)

    @pl.when(decay_bounded)
    def _():
        qd_ref[...] = (qs_ref[...] * jnp.exp(b_ref[...])).astype(BF16)
        kt_ref[...] = (k_ref[...] * jnp.exp(-b_ref[...])).astype(BF16)
        ri = lax.broadcasted_iota(I32, (REC_TILE, REC_TILE), 0)
        ci = lax.broadcasted_iota(I32, (REC_TILE, REC_TILE), 1)
        keep = (ri >= ci) & (ri // REC_BLOCK == ci // REC_BLOCK)
        for h in range(REC_HEADS):
            cols = slice(h * REC_DIM, (h + 1) * REC_DIM)
            att = lax.dot_general(qd_ref[:, cols], kt_ref[:, cols], nt, preferred_element_type=F32)
            att = jnp.where(keep, att, 0.0).astype(BF16)
            intra_ref[:, cols] = jnp.dot(att, i_ref[:, cols], preferred_element_type=F32)

        for j in range(REC_TILE // REC_BLOCK):
            blk = slice(j * REC_BLOCK, (j + 1) * REC_BLOCK)
            for h in range(REC_HEADS):
                cols = slice(h * REC_DIM, (h + 1) * REC_DIM)
                first, last = j * REC_BLOCK, (j + 1) * REC_BLOCK - 1
                e1 = jnp.exp(lf_ref[first:first + 1, cols] - b_ref[first:first + 1, cols])
                e2 = jnp.exp(b_ref[last:last + 1, cols])
                st_mid = st_ref[h][...] * e1
                inter_ref[blk, cols] = lax.dot_general(
                    qd_ref[blk, cols], st_mid.astype(BF16), nt, preferred_element_type=F32)
                kv_t = lax.dot_general(i_ref[blk, cols], kt_ref[blk, cols], tn, preferred_element_type=F32)
                st_ref[h][...] = (st_mid + kv_t) * e2

        for h in range(REC_HEADS):
            cols = slice(h * REC_DIM, (h + 1) * REC_DIM)
            o = intra_ref[:, cols] + inter_ref[:, cols]
            ms = jnp.mean(o * o, axis=-1, keepdims=True)
            o = o * lax.rsqrt(ms + RMS_EPS) * ng_ref[...]
            o_ref[:, cols] = (o * _silu(g_ref[:, cols].astype(F32))).astype(o_ref.dtype)

    @pl.when(jnp.logical_not(decay_bounded))
    def _():
        b_ref[...] = prefix_sum(lf_ref[...], REC_SUB)
        ones = jnp.ones((REC_DIM, REC_DIM), BF16)

        def step(j, carry):
            r0 = pl.multiple_of(j * REC_SUB, REC_SUB)
            for h in range(REC_HEADS):
                cols = slice(h * REC_DIM, (h + 1) * REC_DIM)
                bj = b_ref[pl.ds(r0, REC_SUB), cols]
                kj = k_ref[pl.ds(r0, REC_SUB), cols]
                qj = qs_ref[pl.ds(r0, REC_SUB), cols]
                vj = i_ref[pl.ds(r0, REC_SUB), cols].astype(F32)
                st = st_ref[h][...]
                qd = (qj * jnp.exp(bj)).astype(BF16)
                o = lax.dot_general(qd, st.astype(BF16), nt, preferred_element_type=F32)
                parts = []
                for s in range(REC_SUB):
                    dec = jnp.exp(jnp.where(t_iota >= s, bj - bj[s:s + 1, :], NEG_INF))
                    parts.append((qj * dec * kj[s:s + 1, :]).astype(BF16))
                pstack = jnp.concatenate(parts, axis=0)
                rsum = jnp.dot(pstack, ones, preferred_element_type=F32)
                for s in range(REC_SUB):
                    o = o + rsum[s * REC_SUB:(s + 1) * REC_SUB, :] * vj[s:s + 1, :]
                b_end = bj[REC_SUB - 1:REC_SUB, :]
                kd = (kj * jnp.exp(b_end - bj)).astype(BF16)
                kv_t = lax.dot_general(vj.astype(BF16), kd, tn, preferred_element_type=F32)
                st_ref[h][...] = st * jnp.exp(b_end) + kv_t
                finish(o, r0, cols)
            return carry

        lax.fori_loop(0, REC_TILE // REC_SUB, step, 0)


def hgrn2(proj, proj_f, log_lb, log1m_lb, norm_g, batch, seq, q_col, i_col, g_col):
    n = batch * seq
    nb = seq // REC_TILE
    d = D_MODEL

    def blk(col):
        return pl.BlockSpec((REC_TILE, d), lambda b, i: (b * nb + i, col // d))

    return pl.pallas_call(
        _hgrn_kernel,
        out_shape=jax.ShapeDtypeStruct((n, d), BF16),
        grid=(batch, nb),
        in_specs=[blk(q_col), blk(i_col), blk(g_col),
                  pl.BlockSpec((REC_TILE, d), lambda b, i: (b * nb + i, 0)),
                  pl.BlockSpec((1, d), lambda b, i: (0, 0)),
                  pl.BlockSpec((1, d), lambda b, i: (0, 0)),
                  pl.BlockSpec((1, REC_DIM), lambda b, i: (0, 0))],
        out_specs=pl.BlockSpec((REC_TILE, d), lambda b, i: (b * nb + i, 0)),
        scratch_shapes=[pltpu.VMEM((REC_TILE, d), F32) for _ in range(6)]
                       + [pltpu.VMEM((REC_TILE, d), BF16),
                        pltpu.VMEM((REC_TILE, d), BF16)]
                       + [pltpu.VMEM((REC_DIM, REC_DIM), F32) for _ in range(REC_HEADS)],
        compiler_params=_params("parallel", "arbitrary"),
        name="hgrn2",
    )(proj, proj, proj, proj_f, log_lb.reshape(1, d), log1m_lb.reshape(1, d), norm_g.reshape(1, REC_DIM))


def _merge_kernel(h_ref, attn_ref, rec_ref, ga_ref, gr_ref, wpa_ref, wpr_ref, wo_ref, g_ref, b_ref,
                  o_ref, op_ref):
    tm = h_ref.shape[0]
    pieces = [slice(p * MERGE_PIECE, (p + 1) * MERGE_PIECE) for p in range(tm // MERGE_PIECE)]
    branches = [(jnp.dot(attn_ref[rows, :], wpa_ref[...], preferred_element_type=F32),
                 jnp.dot(rec_ref[rows, :], wpr_ref[...], preferred_element_type=F32)) for rows in pieces]
    merged = [(_sigmoid(ga_ref[rows, :].astype(F32)) * a + _sigmoid(gr_ref[rows, :].astype(F32)) * r).astype(BF16)
              for rows, (a, r) in zip(pieces, branches)]
    ys = [jnp.dot(m, wo_ref[...], preferred_element_type=F32) for m in merged]
    for rows, y in zip(pieces, ys):
        h1 = _layer_norm_rows(DEEPNORM_ALPHA * h_ref[rows, :] + y, g_ref[...], b_ref[...])
        o_ref[rows, :] = h1
        op_ref[0, rows, :] = _pack_pair(h1[:, 0 * QUARTER:1 * QUARTER], h1[:, 1 * QUARTER:2 * QUARTER])
        op_ref[1, rows, :] = _pack_pair(h1[:, 2 * QUARTER:3 * QUARTER], h1[:, 3 * QUARTER:4 * QUARTER])


def merge_outproj_ln(h, attn, rec, proj, ga_col, gr_col, wpa, wpr, wo, g, b):
    n, d = h.shape
    tm = MERGE_ROWS
    row = lambda i: (i, 0)
    const = lambda i: (0, 0)
    return pl.pallas_call(
        _merge_kernel,
        out_shape=(jax.ShapeDtypeStruct((n, d), F32), jax.ShapeDtypeStruct((2, n, QUARTER), U32)),
        grid=(n // tm,),
        in_specs=[pl.BlockSpec((tm, d), row), pl.BlockSpec((tm, d), row), pl.BlockSpec((tm, d), row),
                  pl.BlockSpec((tm, d), lambda i: (i, ga_col // d)),
                  pl.BlockSpec((tm, d), lambda i: (i, gr_col // d)),
                  pl.BlockSpec((d, d), const), pl.BlockSpec((d, d), const), pl.BlockSpec((d, d), const),
                  pl.BlockSpec((1, d), const), pl.BlockSpec((1, d), const)],
        out_specs=(pl.BlockSpec((tm, d), row), pl.BlockSpec((2, tm, QUARTER), lambda i: (0, i, 0))),
        compiler_params=_params("parallel"),
        name="merge_outproj_ln",
    )(h, attn, rec, proj, proj, wpa, wpr, wo, g.reshape(1, d), b.reshape(1, d))


def _router_kernel(h_ref, whi_ref, wlo_ref, bias_ref, idx_ref, gate_ref, rank_ref, cnt_ref, carry_ref):
    @pl.when(pl.program_id(0) == 0)
    def _():
        carry_ref[...] = jnp.zeros_like(carry_ref)

    tm = h_ref.shape[0]
    h = h_ref[...]
    h_hi = h.astype(BF16)
    h_lo = (h - h_hi.astype(F32)).astype(BF16)
    nt = (((1,), (1,)), ((), ()))
    logits = (lax.dot_general(whi_ref[...], h_hi, nt, preferred_element_type=F32)
              + lax.dot_general(whi_ref[...], h_lo, nt, preferred_element_type=F32)
              + lax.dot_general(wlo_ref[...], h_hi, nt, preferred_element_type=F32))
    scores = _sigmoid(logits)
    sel = scores + bias_ref[...]
    e_iota = lax.broadcasted_iota(I32, (N_EXPERTS, tm), 0)

    g_iota = lax.broadcasted_iota(I32, (N_GROUPS, tm), 0)
    l_iota = lax.broadcasted_iota(I32, (GROUP_SIZE, tm), 0)
    grp = jnp.zeros((N_GROUPS, tm), F32)
    for g in range(N_GROUPS):
        sg = sel[g * GROUP_SIZE:(g + 1) * GROUP_SIZE, :]
        m1 = jnp.max(sg, axis=0, keepdims=True)
        i1 = jnp.min(jnp.where(sg == m1, l_iota, GROUP_SIZE), axis=0, keepdims=True)
        m2 = jnp.max(jnp.where(l_iota == i1, NEG_INF, sg), axis=0, keepdims=True)
        grp = jnp.where(g_iota == g, m1 + m2, grp)
    gsel = jnp.zeros((N_GROUPS, tm), I32)
    for _ in range(TOPK_GROUPS):
        m = jnp.max(grp, axis=0, keepdims=True)
        gi = jnp.min(jnp.where(grp == m, g_iota, N_GROUPS), axis=0, keepdims=True)
        hit = g_iota == gi
        gsel = jnp.where(hit, 1, gsel)
        grp = jnp.where(hit, NEG_INF, grp)
    masked = []
    for g in range(N_GROUPS):
        sg = sel[g * GROUP_SIZE:(g + 1) * GROUP_SIZE, :]
        masked.append(jnp.where(gsel[g:g + 1, :] > 0, sg, NEG_INF))
    selm = jnp.concatenate(masked, axis=0)

    k_iota = lax.broadcasted_iota(I32, (TOP_K, tm), 0)
    idx = jnp.zeros((TOP_K, tm), I32)
    gate = jnp.zeros((TOP_K, tm), F32)
    member = jnp.zeros((N_EXPERTS, tm), F32)
    for k in range(TOP_K):
        m = jnp.max(selm, axis=0, keepdims=True)
        ei = jnp.min(jnp.where(selm == m, e_iota, N_EXPERTS), axis=0, keepdims=True)
        hit = e_iota == ei
        gk = jnp.sum(jnp.where(hit, scores, 0.0), axis=0, keepdims=True)
        idx = jnp.where(k_iota == k, ei, idx)
        gate = jnp.where(k_iota == k, gk, gate)
        member = jnp.where(hit, 1.0, member)
        selm = jnp.where(hit, NEG_INF, selm)
    gate = gate / jnp.sum(gate, axis=0, keepdims=True) * ROUTED_SCALE

    upper = lax.broadcasted_iota(I32, (tm, tm), 0) < lax.broadcasted_iota(I32, (tm, tm), 1)
    before = jnp.dot(member.astype(BF16), upper.astype(BF16), preferred_element_type=F32) + carry_ref[...]
    rank = jnp.zeros((TOP_K, tm), F32)
    for k in range(TOP_K):
        rk = jnp.sum(jnp.where(e_iota == idx[k:k + 1, :], before, 0.0), axis=0, keepdims=True)
        rank = jnp.where(k_iota == k, rk, rank)
    carry_ref[...] = carry_ref[...] + jnp.sum(member, axis=1, keepdims=True)

    idx_ref[...] = idx
    gate_ref[...] = gate
    rank_ref[...] = rank.astype(I32)
    cnt_ref[...] = jnp.broadcast_to(carry_ref[...], cnt_ref.shape).astype(I32)


def router(h, w_t_hi, w_t_lo, bias):
    n, d = h.shape
    tm = ROUTER_ROWS
    tok = lambda i: (0, i)
    const = lambda i: (0, 0)
    return pl.pallas_call(
        _router_kernel,
        out_shape=(jax.ShapeDtypeStruct((TOP_K, n), I32),
                   jax.ShapeDtypeStruct((TOP_K, n), F32),
                   jax.ShapeDtypeStruct((TOP_K, n), I32),
                   jax.ShapeDtypeStruct((N_EXPERTS, LANES), I32)),
        grid=(n // tm,),
        in_specs=[pl.BlockSpec((tm, d), lambda i: (i, 0)),
                  pl.BlockSpec((N_EXPERTS, d), const),
                  pl.BlockSpec((N_EXPERTS, d), const),
                  pl.BlockSpec((N_EXPERTS, 1), const)],
        out_specs=(pl.BlockSpec((TOP_K, tm), tok), pl.BlockSpec((TOP_K, tm), tok),
                   pl.BlockSpec((TOP_K, tm), tok), pl.BlockSpec((N_EXPERTS, LANES), const)),
        scratch_shapes=[pltpu.VMEM((N_EXPERTS, 1), F32)],
        compiler_params=_params("arbitrary"),
        name="router",
    )(h, w_t_hi, w_t_lo, bias.reshape(N_EXPERTS, 1))


def _slot_pos_kernel(idx_ref, rank_ref, start_ref, pos_ref):
    tm = idx_ref.shape[1]
    e_iota = lax.broadcasted_iota(I32, (N_EXPERTS, tm), 0)
    k_iota = lax.broadcasted_iota(I32, (TOP_K, tm), 0)
    idx = idx_ref[...]
    start = start_ref[...]
    base = jnp.zeros((TOP_K, tm), F32)
    for k in range(TOP_K):
        bk = jnp.sum(jnp.where(e_iota == idx[k:k + 1, :], start, 0.0), axis=0, keepdims=True)
        base = jnp.where(k_iota == k, bk, base)
    pos_ref[...] = base.astype(I32) + rank_ref[...]


def slot_positions(idx_t, rank_t, pad_start):
    n = idx_t.shape[1]
    tm = SLOT_POS_COLS
    tok = lambda i: (0, i)
    return pl.pallas_call(
        _slot_pos_kernel,
        out_shape=jax.ShapeDtypeStruct((TOP_K, n), I32),
        grid=(n // tm,),
        in_specs=[pl.BlockSpec((TOP_K, tm), tok), pl.BlockSpec((TOP_K, tm), tok),
                  pl.BlockSpec((N_EXPERTS, 1), lambda i: (0, 0))],
        out_specs=pl.BlockSpec((TOP_K, tm), tok),
        compiler_params=_params("parallel"),
        name="slot_positions",
    )(idx_t, rank_t, pad_start.astype(F32).reshape(N_EXPERTS, 1))


SC_WINDOW = 128
SC_WORDS = QUARTER


def _sc_mesh():
    return plsc.VectorSubcoreMesh(core_axis_name="core", subcore_axis_name="subcore")


def sc_scatter_rows(src, pos_t, rows):
    n = src.shape[1]
    src2 = src.reshape(2 * n, SC_WORDS)
    idx2 = jnp.concatenate([pos_t, pos_t + rows], axis=1)

    @pl.kernel(out_type=jax.ShapeDtypeStruct((2 * rows, SC_WORDS), src.dtype), mesh=_sc_mesh(), scratch_types=[])
    def scatter_kernel(x_hbm, i_hbm, o_hbm):
        def body(x_vmem, i_vmem):
            pltpu.sync_copy(x_vmem, o_hbm.at[i_vmem.at[0]])

        pltpu.emit_pipeline(
            body,
            grid=(2 * n // SC_WINDOW, TOP_K),
            in_specs=[pl.BlockSpec((SC_WINDOW, SC_WORDS), index_map=lambda i, k: (i, 0)),
                      pl.BlockSpec((1, SC_WINDOW), index_map=lambda i, k: (k, i))],
            out_specs=[],
            core_axis_name=("core", "subcore"),
            dimension_semantics=(pltpu.PARALLEL, pltpu.ARBITRARY),
        )(x_hbm, i_hbm)

    return scatter_kernel(src2, idx2).reshape(2, rows, SC_WORDS)


def sc_gather_rows(src, pos):
    r = src.shape[1]
    m = pos.shape[0]
    src2 = src.reshape(2 * r, SC_WORDS)
    idx2 = jnp.concatenate([pos, pos + r]).reshape(1, 2 * m)

    @pl.kernel(out_type=jax.ShapeDtypeStruct((2 * m, SC_WORDS), src.dtype), mesh=_sc_mesh(), scratch_types=[])
    def gather_kernel(x_hbm, i_hbm, o_hbm):
        def body(i_vmem, o_vmem):
            pltpu.sync_copy(x_hbm.at[i_vmem.at[0]], o_vmem)

        pltpu.emit_pipeline(
            body,
            grid=(2 * m // SC_WINDOW,),
            in_specs=[pl.BlockSpec((1, SC_WINDOW), index_map=lambda i: (0, i))],
            out_specs=[pl.BlockSpec((SC_WINDOW, SC_WORDS), index_map=lambda i: (i, 0))],
            core_axis_name=("core", "subcore"),
            dimension_semantics=(pltpu.PARALLEL,),
        )(i_hbm, o_hbm)

    return gather_kernel(src2, idx2).reshape(2, m, SC_WORDS)


def _expert_kernel(blk_expert_ref, blk_valid_ref, blk_first_ref, blk_slot_ref, blk_next_ref, n_used_ref,
                   x_ref, wgu_hbm, wd_hbm, y_ref, wgu_buf, wd_buf, sem, *, layer):
    j = pl.program_id(0)
    used = j < n_used_ref[0]

    def weight_copies(e, slot):
        return (pltpu.make_async_copy(wgu_hbm.at[layer, e], wgu_buf.at[slot], sem.at[0, slot]),
                pltpu.make_async_copy(wd_hbm.at[layer, e], wd_buf.at[slot], sem.at[1, slot]))

    @pl.when(used)
    def _():
        slot = blk_slot_ref[j]

        @pl.when(blk_first_ref[j] == 1)
        def _():
            @pl.when(j == 0)
            def _():
                for copy in weight_copies(blk_expert_ref[0], slot):
                    copy.start()

            for copy in weight_copies(blk_expert_ref[j], slot):
                copy.wait()

            @pl.when(blk_next_ref[j] >= 0)
            def _():
                for copy in weight_copies(blk_next_ref[j], 1 - slot):
                    copy.start()

        valid = lax.broadcasted_iota(I32, x_ref.shape[1:], 0) < blk_valid_ref[j]
        quarters = _load_quarters(jnp.where(valid, x_ref[0], U32(0)), jnp.where(valid, x_ref[1], U32(0)))
        gu = sum(jnp.dot(xq.astype(BF16), wgu_buf[slot, c * QUARTER:(c + 1) * QUARTER, :].astype(BF16),
                         preferred_element_type=F32)
                 for c, xq in enumerate(quarters))
        act = (_silu(gu[:, :EXPERT_FF]) * gu[:, EXPERT_FF:]).astype(BF16)
        _store_planes(y_ref, jnp.dot(act, wd_buf[slot].astype(BF16), preferred_element_type=F32))

    @pl.when(jnp.logical_not(used))
    def _():
        y_ref[...] = jnp.zeros_like(y_ref)


def expert_plan(counts, pad_start, pad_end, n_blocks):
    blk_row = jnp.arange(n_blocks, dtype=I32) * ROW_BLOCK
    blk_expert = jnp.minimum(jnp.sum(pad_end[None, :] <= blk_row[:, None], axis=1), N_EXPERTS - 1).astype(I32)
    blk_valid = jnp.clip(counts[blk_expert] - (blk_row - pad_start[blk_expert]), 0, ROW_BLOCK).astype(I32)
    used = blk_row < pad_end[-1]
    prev_expert = jnp.concatenate([jnp.full((1,), -1, I32), blk_expert[:-1]])
    blk_first = (used & (blk_expert != prev_expert)).astype(I32)
    blk_slot = ((jnp.cumsum(blk_first) - 1) % 2).astype(I32)
    expert_ids = jnp.arange(N_EXPERTS, dtype=I32)
    nonempty_at_or_after = lax.cummin(jnp.where(counts > 0, expert_ids, N_EXPERTS), reverse=True)
    nonempty_after = jnp.concatenate([nonempty_at_or_after[1:], jnp.full((1,), N_EXPERTS, I32)])
    nxt = nonempty_after[blk_expert]
    blk_next = jnp.where(nxt < N_EXPERTS, nxt, -1).astype(I32)
    n_used = (pad_end[-1:] // ROW_BLOCK).astype(I32)
    return blk_expert, blk_valid, blk_first, blk_slot, blk_next, n_used


def expert_ffn(xs, plan, w_gu, w_down, layer):
    _, rows, w = xs.shape
    d = D_MODEL
    n_blocks = rows // ROW_BLOCK

    def row_map(j, be, bv, bf, bs, bn, nu):
        return (0, jnp.minimum(j, nu[0] - 1), 0)

    return pl.pallas_call(
        functools.partial(_expert_kernel, layer=layer),
        out_shape=jax.ShapeDtypeStruct(xs.shape, U32),
        grid_spec=pltpu.PrefetchScalarGridSpec(
            num_scalar_prefetch=6,
            grid=(n_blocks,),
            in_specs=[pl.BlockSpec((2, ROW_BLOCK, w), row_map),
                      pl.BlockSpec(memory_space=pl.ANY),
                      pl.BlockSpec(memory_space=pl.ANY)],
            out_specs=pl.BlockSpec((2, ROW_BLOCK, w), lambda j, be, bv, bf, bs, bn, nu: (0, j, 0)),
            scratch_shapes=[pltpu.VMEM((2, d, 2 * EXPERT_FF), F32),
                            pltpu.VMEM((2, EXPERT_FF, d), F32),
                            pltpu.SemaphoreType.DMA((2, 2))]),
        compiler_params=_params("arbitrary"),
        name="expert_ffn",
    )(*plan, xs, w_gu, w_down)


def _combine_kernel(h_ref, gate_ref, ys_ref, sgu_ref, sd_ref, g_ref, b_ref, *refs):
    o_ref, ob_ref = refs[-2:]
    h = h_ref[...]
    gu = jnp.dot(h.astype(BF16), sgu_ref[...], preferred_element_type=F32)
    act = _silu(gu[:, :EXPERT_FF]) * gu[:, EXPERT_FF:]
    y = jnp.dot(act.astype(BF16), sd_ref[...], preferred_element_type=F32)
    gate = gate_ref[...]
    acc = [y[:, c * QUARTER:(c + 1) * QUARTER] for c in range(4)]
    for k in range(TOP_K):
        gk = gate[:, k:k + 1]
        acc = [a + gk * q for a, q in zip(acc, _load_quarters(ys_ref[0, k], ys_ref[1, k]))]
    out = _layer_norm_rows(DEEPNORM_ALPHA * h + jnp.concatenate(acc, axis=-1), g_ref[...], b_ref[...])
    o_ref[...] = out
    ob_ref[...] = out.astype(BF16)


def combine_shared_ln(h, gate, y_slots, chunk, prev_out, sh_gu, sh_down, g, b):
    n, d = h.shape
    tm = MOE_ROWS
    steps = y_slots.shape[2] // tm
    row = lambda i: (i + chunk * steps, 0)
    const = lambda i: (0, 0)
    passthrough = () if prev_out is None else tuple(prev_out)
    n_in = 7
    return pl.pallas_call(
        _combine_kernel,
        out_shape=(jax.ShapeDtypeStruct((n, d), F32), jax.ShapeDtypeStruct((n, d), BF16)),
        grid=(steps,),
        in_specs=[pl.BlockSpec((tm, d), row),
                  pl.BlockSpec((tm, TOP_K), row),
                  pl.BlockSpec((2, TOP_K, tm, QUARTER), lambda i: (0, 0, i, 0)),
                  pl.BlockSpec((d, 2 * EXPERT_FF), const),
                  pl.BlockSpec((EXPERT_FF, d), const),
                  pl.BlockSpec((1, d), const), pl.BlockSpec((1, d), const)]
                 + [pl.BlockSpec(memory_space=pl.ANY) for _ in passthrough],
        out_specs=(pl.BlockSpec((tm, d), row), pl.BlockSpec((tm, d), row)),
        input_output_aliases={n_in + i: i for i in range(len(passthrough))},
        compiler_params=_params("parallel"),
        name="combine_shared_ln",
    )(h, gate, y_slots, sh_gu, sh_down, g.reshape(1, d), b.reshape(1, d), *passthrough)


_MAIN_BLOCKS = (0, 1, 3, 4, 7, 8, 9, 10, 11, 12, 13, 14, 2)
_FORGET_BLOCKS = (5, 6)
_Q_A, _Q_R, _I_R, _G_R, _GATE_A, _GATE_R = (i * D_MODEL for i in range(6))
_K_A = 6 * D_MODEL
_V_A = _K_A + KV_WIDTH


def kernel(x, ln_in_g, ln_in_b, lb_logits, w_in, b_in, attn_sinks, rec_norm_g, w_proj_attn, w_proj_rec, w_out,
           ln1_g, ln1_b, router_w, router_bias, expert_w_gu, expert_w_down, shared_w_gu, shared_w_down,
           ln2_g, ln2_b):
    batch, seq, d = x.shape
    n = batch * seq
    depth = w_in.shape[0]
    n_blocks = n * TOP_K // ROW_BLOCK + N_EXPERTS
    rows = n_blocks * ROW_BLOCK

    p = jax.nn.softmax(lb_logits.astype(F32), axis=0)
    cum = jnp.cumsum(p, axis=0)
    lower = cum - cum[0:1]
    log_lb = jnp.log(lower)
    log1m_lb = jnp.log1p(-lower)

    w_in_bf = w_in.astype(BF16)
    h, hb = layer_norm_in(x.reshape(n, d), ln_in_g, ln_in_b)
    for l in range(depth):
        proj = in_proj(hb, w_in_bf, b_in, l, _MAIN_BLOCKS, BF16, "in_proj_main")
        proj_f = in_proj(hb, w_in_bf, b_in, l, _FORGET_BLOCKS, F32, "in_proj_forget")
        attn = swa_attention(proj, attn_sinks[l], batch, seq, _Q_A, _K_A, _V_A)
        rec = hgrn2(proj, proj_f, log_lb[l], log1m_lb[l], rec_norm_g[l], batch, seq, _Q_R, _I_R, _G_R)
        h, hp = merge_outproj_ln(h, attn, rec, proj, _GATE_A, _GATE_R,
                                 w_proj_attn[l].astype(BF16), w_proj_rec[l].astype(BF16), w_out[l].astype(BF16),
                                 ln1_g[l], ln1_b[l])

        rw_t = router_w[l].T
        rw_hi = rw_t.astype(BF16)
        rw_lo = (rw_t - rw_hi.astype(F32)).astype(BF16)
        idx_t, gate_t, rank_t, cnt = router(h, rw_hi, rw_lo, router_bias[l])
        counts = cnt[:, 0]
        padded = (counts + ROW_BLOCK - 1) // ROW_BLOCK * ROW_BLOCK
        pad_end = jnp.cumsum(padded)
        pad_start = pad_end - padded
        pos_t = slot_positions(idx_t, rank_t, pad_start)
        plan = expert_plan(counts, pad_start, pad_end, n_blocks)

        xs = sc_scatter_rows(hp, pos_t, rows)
        ys = expert_ffn(xs, plan, expert_w_gu, expert_w_down, l)
        nc = n // MOE_CHUNKS
        gate, out = gate_t.T, None
        for c in range(MOE_CHUNKS):
            pos_c = lax.slice_in_dim(pos_t, c * nc, (c + 1) * nc, axis=1).reshape(TOP_K * nc)
            y_slots = sc_gather_rows(ys, pos_c).reshape(2, TOP_K, nc, QUARTER)
            out = combine_shared_ln(h, gate, y_slots, c, out, shared_w_gu[l].astype(BF16),
                                    shared_w_down[l].astype(BF16), ln2_g[l], ln2_b[l])
        h, hb = out
    return h.reshape(batch, seq, d)
```

```python
import functools

import jax
import jax.numpy as jnp
from jax import lax
from jax.experimental import pallas as pl
from jax.experimental.pallas import tpu as pltpu
from jax.experimental.pallas import tpu_sc as plsc

F32 = jnp.float32
BF16 = jnp.bfloat16
U32 = jnp.uint32
I32 = jnp.int32

D_MODEL = 1024
QUARTER = D_MODEL // 4
CHUNK = 64
ATTN_HEADS = 16
ATTN_KV_HEADS = 4
ATTN_HEAD_DIM = 64
ATTN_GROUP = ATTN_HEADS // ATTN_KV_HEADS
WIN_CHUNKS = 2
KV_WIDTH = ATTN_KV_HEADS * ATTN_HEAD_DIM
REC_HEADS = 8
REC_DIM = 128
N_EXPERTS = 256
TOP_K = 8
N_GROUPS = 8
GROUP_SIZE = N_EXPERTS // N_GROUPS
TOPK_GROUPS = 4
EXPERT_FF = 256
ROUTED_SCALE = 2.5
DEPTH = 2
DEEPNORM_ALPHA = (2 * DEPTH) ** 0.25
LN_EPS = 1e-5
RMS_EPS = 1e-5
NEG_INF = float("-inf")

SUBLANES = 8
LANES = 128
VMEM_LIMIT_BYTES = 48 * 1024 * 1024

LN_ROWS = 512
PROJ_ROWS = 4096
PROJ_COLS = 512
ATTN_Q_BLOCK = 256
ATTN_STAGE_LAG = 3
REC_TILE = 256
REC_BLOCK = 64
REC_SUB = 16
MERGE_ROWS = 512
MERGE_PIECE = 256
ROUTER_ROWS = 512
SLOT_POS_COLS = 1024
MOE_ROWS = 512
MOE_CHUNKS = 4
ROW_BLOCK = 512

DECAY_LIMIT = 60.0


def _params(*sem):
    return pltpu.CompilerParams(dimension_semantics=sem, vmem_limit_bytes=VMEM_LIMIT_BYTES)


def _layer_norm_rows(x, g, b):
    mu = jnp.mean(x, axis=-1, keepdims=True)
    xc = x - mu
    var = jnp.mean(xc * xc, axis=-1, keepdims=True)
    return xc * lax.rsqrt(var + LN_EPS) * g + b


def _sigmoid(x):
    return 1.0 / (1.0 + jnp.exp(-x))


def _silu(x):
    return x * _sigmoid(x)


def _pack_pair(lo, hi):
    lo_bits = pltpu.bitcast(lo.astype(BF16).astype(F32), U32)
    hi_bits = pltpu.bitcast(hi.astype(BF16).astype(F32), U32)
    return lax.shift_right_logical(lo_bits, U32(16)) | (hi_bits & U32(0xFFFF0000))


def _unpack_pair(w):
    lo = pltpu.bitcast(lax.shift_left(w, U32(16)), F32)
    hi = pltpu.bitcast(w & U32(0xFFFF0000), F32)
    return lo, hi


def _store_planes(ref, x):
    q = QUARTER
    ref[0] = _pack_pair(x[:, 0 * q:1 * q], x[:, 1 * q:2 * q])
    ref[1] = _pack_pair(x[:, 2 * q:3 * q], x[:, 3 * q:4 * q])


def _load_quarters(plane0, plane1):
    return _unpack_pair(plane0) + _unpack_pair(plane1)


def _ln_in_kernel(x_ref, g_ref, b_ref, h_ref, hb_ref):
    h = _layer_norm_rows(x_ref[...], g_ref[...], b_ref[...])
    h_ref[...] = h
    hb_ref[...] = h.astype(BF16)


def layer_norm_in(x, g, b):
    n, d = x.shape
    row = lambda i: (i, 0)
    const = lambda i: (0, 0)
    return pl.pallas_call(
        _ln_in_kernel,
        out_shape=(jax.ShapeDtypeStruct((n, d), F32), jax.ShapeDtypeStruct((n, d), BF16)),
        grid=(n // LN_ROWS,),
        in_specs=[pl.BlockSpec((LN_ROWS, d), row), pl.BlockSpec((1, d), const), pl.BlockSpec((1, d), const)],
        out_specs=(pl.BlockSpec((LN_ROWS, d), row), pl.BlockSpec((LN_ROWS, d), row)),
        compiler_params=_params("parallel"),
        name="ln_in",
    )(x, g.reshape(1, d), b.reshape(1, d))


def _in_proj_kernel(perm_ref, x_ref, w_ref, b_ref, o_ref):
    del perm_ref
    acc = jnp.dot(x_ref[...], w_ref[0], preferred_element_type=F32)
    o_ref[...] = (acc + b_ref[0]).astype(o_ref.dtype)


def in_proj(xb, w, b, layer, col_blocks, out_dtype, name):
    n, k = xb.shape
    perm = jnp.asarray(col_blocks, I32)
    nblk = len(col_blocks)
    tm = min(PROJ_ROWS, n)
    return pl.pallas_call(
        _in_proj_kernel,
        out_shape=jax.ShapeDtypeStruct((n, nblk * PROJ_COLS), out_dtype),
        grid_spec=pltpu.PrefetchScalarGridSpec(
            num_scalar_prefetch=1,
            grid=(n // tm, nblk),
            in_specs=[pl.BlockSpec((tm, k), lambda i, j, p: (i, 0)),
                      pl.BlockSpec((1, k, PROJ_COLS), lambda i, j, p: (layer, 0, p[j])),
                      pl.BlockSpec((1, 1, PROJ_COLS), lambda i, j, p: (layer, 0, p[j]))],
            out_specs=pl.BlockSpec((tm, PROJ_COLS), lambda i, j, p: (i, j))),
        compiler_params=_params("parallel", "arbitrary"),
        name=name,
    )(perm, xb, w, b.reshape(b.shape[0], 1, -1))


def _attn_kernel(sink_ref, q_ref, kp_ref, kc_ref, vp_ref, vc_ref, o_ref, s_ref, e_ref):
    i = pl.program_id(1)
    half = ATTN_Q_BLOCK // 2
    qc = lax.broadcasted_iota(I32, (half, ATTN_Q_BLOCK), 0) // CHUNK
    kc = lax.broadcasted_iota(I32, (half, ATTN_Q_BLOCK), 1) // CHUNK
    valid = (kc >= qc) & (kc <= qc + WIN_CHUNKS)
    first = jnp.where(i == 0, WIN_CHUNKS, 0)
    scale = ATTN_HEAD_DIM ** -0.5
    windows = ((jnp.concatenate([kp_ref[half:, :], kc_ref[:half, :]], axis=0),
                jnp.concatenate([vp_ref[half:, :], vc_ref[:half, :]], axis=0),
                jnp.where(valid & (kc >= first), 0.0, NEG_INF)),
               (kc_ref[...], vc_ref[...], jnp.where(valid, 0.0, NEG_INF)))
    for part, (k, _, mask_bias) in enumerate(windows):
        rows = slice(part * half, (part + 1) * half)
        for h in range(ATTN_HEADS):
            kv = h // ATTN_GROUP
            qh = q_ref[rows, h * ATTN_HEAD_DIM:(h + 1) * ATTN_HEAD_DIM] * scale
            kh = k[:, kv * ATTN_HEAD_DIM:(kv + 1) * ATTN_HEAD_DIM]
            s_ref[part * ATTN_HEADS + h] = (
                lax.dot_general(qh, kh, (((1,), (1,)), ((), ())), preferred_element_type=F32) + mask_bias)
    units = [(part, h) for part in range(2) for h in range(ATTN_HEADS)]

    def row_max(part, h):
        return jnp.maximum(jnp.max(s_ref[part * ATTN_HEADS + h], axis=-1, keepdims=True), sink_ref[h])

    def exponentials(part, h, m):
        e_ref[part * ATTN_HEADS + h] = jnp.exp(s_ref[part * ATTN_HEADS + h] - m).astype(BF16)
        return jnp.exp(sink_ref[h] - m)

    ones = jnp.ones((ATTN_Q_BLOCK, ATTN_HEAD_DIM), BF16)

    def values(part, h, sink_term):
        kv = h // ATTN_GROUP
        vh = windows[part][1][:, kv * ATTN_HEAD_DIM:(kv + 1) * ATTN_HEAD_DIM]
        e = e_ref[part * ATTN_HEADS + h]
        denom = jnp.dot(e, ones, preferred_element_type=F32) + sink_term
        oh = jnp.dot(e, vh, preferred_element_type=F32) / denom
        rows = slice(part * half, (part + 1) * half)
        o_ref[rows, h * ATTN_HEAD_DIM:(h + 1) * ATTN_HEAD_DIM] = oh.astype(o_ref.dtype)

    maxes, rdens = {}, {}
    lag = ATTN_STAGE_LAG
    for t in range(len(units) + 2 * lag):
        if t < len(units):
            maxes[t] = row_max(*units[t])
        if 0 <= t - lag < len(units):
            rdens[t - lag] = exponentials(*units[t - lag], maxes.pop(t - lag))
        if 0 <= t - 2 * lag < len(units):
            values(*units[t - 2 * lag], rdens.pop(t - 2 * lag))


def swa_attention(proj, sinks, batch, seq, q_col, k_col, v_col):
    n = batch * seq
    nb = seq // ATTN_Q_BLOCK
    qb, kb, vb = q_col // D_MODEL, k_col // KV_WIDTH, v_col // KV_WIDTH

    def cur(col):
        return lambda b, i, s: (b * nb + i, col)

    def prev(col):
        return lambda b, i, s: (b * nb + jnp.maximum(i - 1, 0), col)

    return pl.pallas_call(
        _attn_kernel,
        out_shape=jax.ShapeDtypeStruct((n, D_MODEL), BF16),
        grid_spec=pltpu.PrefetchScalarGridSpec(
            num_scalar_prefetch=1,
            grid=(batch, nb),
            in_specs=[pl.BlockSpec((ATTN_Q_BLOCK, D_MODEL), cur(qb)),
                      pl.BlockSpec((ATTN_Q_BLOCK, KV_WIDTH), prev(kb)),
                      pl.BlockSpec((ATTN_Q_BLOCK, KV_WIDTH), cur(kb)),
                      pl.BlockSpec((ATTN_Q_BLOCK, KV_WIDTH), prev(vb)),
                      pl.BlockSpec((ATTN_Q_BLOCK, KV_WIDTH), cur(vb))],
            out_specs=pl.BlockSpec((ATTN_Q_BLOCK, D_MODEL), lambda b, i, s: (b * nb + i, 0)),
            scratch_shapes=[pltpu.VMEM((2 * ATTN_HEADS, ATTN_Q_BLOCK // 2, ATTN_Q_BLOCK), F32),
                            pltpu.VMEM((2 * ATTN_HEADS, ATTN_Q_BLOCK // 2, ATTN_Q_BLOCK), BF16)]),
        compiler_params=_params("parallel", "parallel"),
        name="swa_attention",
    )(sinks.astype(F32), proj, proj, proj, proj, proj)


def _hgrn_kernel(q_ref, i_ref, g_ref, f_ref, loglb_ref, log1mlb_ref, ng_ref, o_ref,
                 b_ref, k_ref, qs_ref, lf_ref, intra_ref, inter_ref, qd_ref, kt_ref, *st_ref):
    @pl.when(pl.program_id(1) == 0)
    def _():
        for s_ref in st_ref:
            s_ref[...] = jnp.zeros_like(s_ref)

    fl = f_ref[...]
    log_sig = jnp.minimum(fl, 0.0) - jnp.log(1.0 + jnp.exp(-jnp.abs(fl)))
    a = loglb_ref[...]
    c = log1mlb_ref[...] + log_sig
    log_f = jnp.maximum(a, c) + jnp.log(1.0 + jnp.exp(-jnp.abs(a - c)))
    lf_ref[...] = log_f
    k_ref[...] = 1.0 - jnp.exp(log_f)
    qs_ref[...] = _silu(q_ref[...].astype(F32))

    def prefix_sum(x, width):
        pos = lax.broadcasted_iota(I32, x.shape, 0) % width
        shift = 1
        while shift < width:
            x = x + jnp.where(pos >= shift, pltpu.roll(x, shift, axis=0), 0.0)
            shift *= 2
        return x

    half = REC_BLOCK // 2
    ti = lax.broadcasted_iota(I32, (REC_TILE, REC_TILE), 0)
    si = lax.broadcasted_iota(I32, (REC_TILE, REC_TILE), 1)
    tm, sm = ti % REC_BLOCK, si % REC_BLOCK
    same = ti // REC_BLOCK == si // REC_BLOCK
    plus = same & (tm >= half) & (sm >= half) & (sm <= tm)
    minus = same & (tm < half) & (sm > tm) & (sm < half)
    signed = jnp.where(plus, 1.0, jnp.where(minus, -1.0, 0.0)).astype(BF16)
    hi = log_f.astype(BF16)
    rest = log_f - hi.astype(F32)
    mid = rest.astype(BF16)
    lo = (rest - mid.astype(F32)).astype(BF16)
    d = (jnp.dot(signed, hi, preferred_element_type=F32) + jnp.dot(signed, mid, preferred_element_type=F32)
         + jnp.dot(signed, lo, preferred_element_type=F32))
    b_ref[...] = d
    decay_bounded = jnp.max(jnp.abs(d)) <= DECAY_LIMIT

    t_iota = lax.broadcasted_iota(I32, (REC_SUB, REC_DIM), 0)
    nt = (((1,), (1,)), ((), ()))
    tn = (((0,), (0,)), ((), ()))

    def finish(o, r0, cols):
        ms = jnp.mean(o * o, axis=-1, keepdims=True)
        o = o * lax.rsqrt(ms + RMS_EPS) * ng_ref[...]
        gj = g_ref[pl.ds(r0, REC_SUB), cols].astype(F32)
        o_ref[pl.ds(r0, REC_SUB), cols] = (o * _silu(gj)).astype(o_ref.dtype)

    @pl.when(decay_bounded)
    def _():
        qd_ref[...] = (qs_ref[...] * jnp.exp(b_ref[...])).astype(BF16)
        kt_ref[...] = (k_ref[...] * jnp.exp(-b_ref[...])).astype(BF16)
        ri = lax.broadcasted_iota(I32, (REC_TILE, REC_TILE), 0)
        ci = lax.broadcasted_iota(I32, (REC_TILE, REC_TILE), 1)
        keep = (ri >= ci) & (ri // REC_BLOCK == ci // REC_BLOCK)
        for h in range(REC_HEADS):
            cols = slice(h * REC_DIM, (h + 1) * REC_DIM)
            att = lax.dot_general(qd_ref[:, cols], kt_ref[:, cols], nt, preferred_element_type=F32)
            att = jnp.where(keep, att, 0.0).astype(BF16)
            intra_ref[:, cols] = jnp.dot(att, i_ref[:, cols], preferred_element_type=F32)

        for j in range(REC_TILE // REC_BLOCK):
            blk = slice(j * REC_BLOCK, (j + 1) * REC_BLOCK)
            for h in range(REC_HEADS):
                cols = slice(h * REC_DIM, (h + 1) * REC_DIM)
                first, last = j * REC_BLOCK, (j + 1) * REC_BLOCK - 1
                e1 = jnp.exp(lf_ref[first:first + 1, cols] - b_ref[first:first + 1, cols])
                e2 = jnp.exp(b_ref[last:last + 1, cols])
                st_mid = st_ref[h][...] * e1
                inter_ref[blk, cols] = lax.dot_general(
                    qd_ref[blk, cols], st_mid.astype(BF16), nt, preferred_element_type=F32)
                kv_t = lax.dot_general(i_ref[blk, cols], kt_ref[blk, cols], tn, preferred_element_type=F32)
                st_ref[h][...] = (st_mid + kv_t) * e2

        for h in range(REC_HEADS):
            cols = slice(h * REC_DIM, (h + 1) * REC_DIM)
            o = intra_ref[:, cols] + inter_ref[:, cols]
            ms = jnp.mean(o * o, axis=-1, keepdims=True)
            o = o * lax.rsqrt(ms + RMS_EPS) * ng_ref[...]
            o_ref[:, cols] = (o * _silu(g_ref[:, cols].astype(F32))).astype(o_ref.dtype)

    @pl.when(jnp.logical_not(decay_bounded))
    def _():
        b_ref[...] = prefix_sum(lf_ref[...], REC_SUB)
        ones = jnp.ones((REC_DIM, REC_DIM), BF16)

        def step(j, carry):
            r0 = pl.multiple_of(j * REC_SUB, REC_SUB)
            for h in range(REC_HEADS):
                cols = slice(h * REC_DIM, (h + 1) * REC_DIM)
                bj = b_ref[pl.ds(r0, REC_SUB), cols]
                kj = k_ref[pl.ds(r0, REC_SUB), cols]
                qj = qs_ref[pl.ds(r0, REC_SUB), cols]
                vj = i_ref[pl.ds(r0, REC_SUB), cols].astype(F32)
                st = st_ref[h][...]
                qd = (qj * jnp.exp(bj)).astype(BF16)
                o = lax.dot_general(qd, st.astype(BF16), nt, preferred_element_type=F32)
                parts = []
                for s in range(REC_SUB):
                    dec = jnp.exp(jnp.where(t_iota >= s, bj - bj[s:s + 1, :], NEG_INF))
                    parts.append((qj * dec * kj[s:s + 1, :]).astype(BF16))
                pstack = jnp.concatenate(parts, axis=0)
                rsum = jnp.dot(pstack, ones, preferred_element_type=F32)
                for s in range(REC_SUB):
                    o = o + rsum[s * REC_SUB:(s + 1) * REC_SUB, :] * vj[s:s + 1, :]
                b_end = bj[REC_SUB - 1:REC_SUB, :]
                kd = (kj * jnp.exp(b_end - bj)).astype(BF16)
                kv_t = lax.dot_general(vj.astype(BF16), kd, tn, preferred_element_type=F32)
                st_ref[h][...] = st * jnp.exp(b_end) + kv_t
                finish(o, r0, cols)
            return carry

        lax.fori_loop(0, REC_TILE // REC_SUB, step, 0)


def hgrn2(proj, proj_f, log_lb, log1m_lb, norm_g, batch, seq, q_col, i_col, g_col):
    n = batch * seq
    nb = seq // REC_TILE
    d = D_MODEL

    def blk(col):
        return pl.BlockSpec((REC_TILE, d), lambda b, i: (b * nb + i, col // d))

    return pl.pallas_call(
        _hgrn_kernel,
        out_shape=jax.ShapeDtypeStruct((n, d), BF16),
        grid=(batch, nb),
        in_specs=[blk(q_col), blk(i_col), blk(g_col),
                  pl.BlockSpec((REC_TILE, d), lambda b, i: (b * nb + i, 0)),
                  pl.BlockSpec((1, d), lambda b, i: (0, 0)),
                  pl.BlockSpec((1, d), lambda b, i: (0, 0)),
                  pl.BlockSpec((1, REC_DIM), lambda b, i: (0, 0))],
        out_specs=pl.BlockSpec((REC_TILE, d), lambda b, i: (b * nb + i, 0)),
        scratch_shapes=[pltpu.VMEM((REC_TILE, d), F32) for _ in range(6)]
                       + [pltpu.VMEM((REC_TILE, d), BF16),
                        pltpu.VMEM((REC_TILE, d), BF16)]
                       + [pltpu.VMEM((REC_DIM, REC_DIM), F32) for _ in range(REC_HEADS)],
        compiler_params=_params("parallel", "arbitrary"),
        name="hgrn2",
    )(proj, proj, proj, proj_f, log_lb.reshape(1, d), log1m_lb.reshape(1, d), norm_g.reshape(1, REC_DIM))


def _merge_kernel(h_ref, attn_ref, rec_ref, ga_ref, gr_ref, wpa_ref, wpr_ref, wo_ref, g_ref, b_ref,
                  o_ref, op_ref):
    tm = h_ref.shape[0]
    pieces = [slice(p * MERGE_PIECE, (p + 1) * MERGE_PIECE) for p in range(tm // MERGE_PIECE)]
    branches = [(jnp.dot(attn_ref[rows, :], wpa_ref[...], preferred_element_type=F32),
                 jnp.dot(rec_ref[rows, :], wpr_ref[...], preferred_element_type=F32)) for rows in pieces]
    merged = [(_sigmoid(ga_ref[rows, :].astype(F32)) * a + _sigmoid(gr_ref[rows, :].astype(F32)) * r).astype(BF16)
              for rows, (a, r) in zip(pieces, branches)]
    ys = [jnp.dot(m, wo_ref[...], preferred_element_type=F32) for m in merged]
    for rows, y in zip(pieces, ys):
        h1 = _layer_norm_rows(DEEPNORM_ALPHA * h_ref[rows, :] + y, g_ref[...], b_ref[...])
        o_ref[rows, :] = h1
        op_ref[0, rows, :] = _pack_pair(h1[:, 0 * QUARTER:1 * QUARTER], h1[:, 1 * QUARTER:2 * QUARTER])
        op_ref[1, rows, :] = _pack_pair(h1[:, 2 * QUARTER:3 * QUARTER], h1[:, 3 * QUARTER:4 * QUARTER])


def merge_outproj_ln(h, attn, rec, proj, ga_col, gr_col, wpa, wpr, wo, g, b):
    n, d = h.shape
    tm = MERGE_ROWS
    row = lambda i: (i, 0)
    const = lambda i: (0, 0)
    return pl.pallas_call(
        _merge_kernel,
        out_shape=(jax.ShapeDtypeStruct((n, d), F32), jax.ShapeDtypeStruct((2, n, QUARTER), U32)),
        grid=(n // tm,),
        in_specs=[pl.BlockSpec((tm, d), row), pl.BlockSpec((tm, d), row), pl.BlockSpec((tm, d), row),
                  pl.BlockSpec((tm, d), lambda i: (i, ga_col // d)),
                  pl.BlockSpec((tm, d), lambda i: (i, gr_col // d)),
                  pl.BlockSpec((d, d), const), pl.BlockSpec((d, d), const), pl.BlockSpec((d, d), const),
                  pl.BlockSpec((1, d), const), pl.BlockSpec((1, d), const)],
        out_specs=(pl.BlockSpec((tm, d), row), pl.BlockSpec((2, tm, QUARTER), lambda i: (0, i, 0))),
        compiler_params=_params("parallel"),
        name="merge_outproj_ln",
    )(h, attn, rec, proj, proj, wpa, wpr, wo, g.reshape(1, d), b.reshape(1, d))


def _router_kernel(h_ref, whi_ref, wlo_ref, bias_ref, idx_ref, gate_ref, rank_ref, cnt_ref, carry_ref):
    @pl.when(pl.program_id(0) == 0)
    def _():
        carry_ref[...] = jnp.zeros_like(carry_ref)

    tm = h_ref.shape[0]
    h = h_ref[...]
    h_hi = h.astype(BF16)
    h_lo = (h - h_hi.astype(F32)).astype(BF16)
    nt = (((1,), (1,)), ((), ()))
    logits = (lax.dot_general(whi_ref[...], h_hi, nt, preferred_element_type=F32)
              + lax.dot_general(whi_ref[...], h_lo, nt, preferred_element_type=F32)
              + lax.dot_general(wlo_ref[...], h_hi, nt, preferred_element_type=F32))
    scores = _sigmoid(logits)
    sel = scores + bias_ref[...]
    e_iota = lax.broadcasted_iota(I32, (N_EXPERTS, tm), 0)

    g_iota = lax.broadcasted_iota(I32, (N_GROUPS, tm), 0)
    l_iota = lax.broadcasted_iota(I32, (GROUP_SIZE, tm), 0)
    grp = jnp.zeros((N_GROUPS, tm), F32)
    for g in range(N_GROUPS):
        sg = sel[g * GROUP_SIZE:(g + 1) * GROUP_SIZE, :]
        m1 = jnp.max(sg, axis=0, keepdims=True)
        i1 = jnp.min(jnp.where(sg == m1, l_iota, GROUP_SIZE), axis=0, keepdims=True)
        m2 = jnp.max(jnp.where(l_iota == i1, NEG_INF, sg), axis=0, keepdims=True)
        grp = jnp.where(g_iota == g, m1 + m2, grp)
    gsel = jnp.zeros((N_GROUPS, tm), I32)
    for _ in range(TOPK_GROUPS):
        m = jnp.max(grp, axis=0, keepdims=True)
        gi = jnp.min(jnp.where(grp == m, g_iota, N_GROUPS), axis=0, keepdims=True)
        hit = g_iota == gi
        gsel = jnp.where(hit, 1, gsel)
        grp = jnp.where(hit, NEG_INF, grp)
    masked = []
    for g in range(N_GROUPS):
        sg = sel[g * GROUP_SIZE:(g + 1) * GROUP_SIZE, :]
        masked.append(jnp.where(gsel[g:g + 1, :] > 0, sg, NEG_INF))
    selm = jnp.concatenate(masked, axis=0)

    k_iota = lax.broadcasted_iota(I32, (TOP_K, tm), 0)
    idx = jnp.zeros((TOP_K, tm), I32)
    gate = jnp.zeros((TOP_K, tm), F32)
    member = jnp.zeros((N_EXPERTS, tm), F32)
    for k in range(TOP_K):
        m = jnp.max(selm, axis=0, keepdims=True)
        ei = jnp.min(jnp.where(selm == m, e_iota, N_EXPERTS), axis=0, keepdims=True)
        hit = e_iota == ei
        gk = jnp.sum(jnp.where(hit, scores, 0.0), axis=0, keepdims=True)
        idx = jnp.where(k_iota == k, ei, idx)
        gate = jnp.where(k_iota == k, gk, gate)
        member = jnp.where(hit, 1.0, member)
        selm = jnp.where(hit, NEG_INF, selm)
    gate = gate / jnp.sum(gate, axis=0, keepdims=True) * ROUTED_SCALE

    upper = lax.broadcasted_iota(I32, (tm, tm), 0) < lax.broadcasted_iota(I32, (tm, tm), 1)
    before = jnp.dot(member.astype(BF16), upper.astype(BF16), preferred_element_type=F32) + carry_ref[...]
    rank = jnp.zeros((TOP_K, tm), F32)
    for k in range(TOP_K):
        rk = jnp.sum(jnp.where(e_iota == idx[k:k + 1, :], before, 0.0), axis=0, keepdims=True)
        rank = jnp.where(k_iota == k, rk, rank)
    carry_ref[...] = carry_ref[...] + jnp.sum(member, axis=1, keepdims=True)

    idx_ref[...] = idx
    gate_ref[...] = gate
    rank_ref[...] = rank.astype(I32)
    cnt_ref[...] = jnp.broadcast_to(carry_ref[...], cnt_ref.shape).astype(I32)


def router(h, w_t_hi, w_t_lo, bias):
    n, d = h.shape
    tm = ROUTER_ROWS
    tok = lambda i: (0, i)
    const = lambda i: (0, 0)
    return pl.pallas_call(
        _router_kernel,
        out_shape=(jax.ShapeDtypeStruct((TOP_K, n), I32),
                   jax.ShapeDtypeStruct((TOP_K, n), F32),
                   jax.ShapeDtypeStruct((TOP_K, n), I32),
                   jax.ShapeDtypeStruct((N_EXPERTS, LANES), I32)),
        grid=(n // tm,),
        in_specs=[pl.BlockSpec((tm, d), lambda i: (i, 0)),
                  pl.BlockSpec((N_EXPERTS, d), const),
                  pl.BlockSpec((N_EXPERTS, d), const),
                  pl.BlockSpec((N_EXPERTS, 1), const)],
        out_specs=(pl.BlockSpec((TOP_K, tm), tok), pl.BlockSpec((TOP_K, tm), tok),
                   pl.BlockSpec((TOP_K, tm), tok), pl.BlockSpec((N_EXPERTS, LANES), const)),
        scratch_shapes=[pltpu.VMEM((N_EXPERTS, 1), F32)],
        compiler_params=_params("arbitrary"),
        name="router",
    )(h, w_t_hi, w_t_lo, bias.reshape(N_EXPERTS, 1))


def _slot_pos_kernel(idx_ref, rank_ref, start_ref, pos_ref):
    tm = idx_ref.shape[1]
    e_iota = lax.broadcasted_iota(I32, (N_EXPERTS, tm), 0)
    k_iota = lax.broadcasted_iota(I32, (TOP_K, tm), 0)
    idx = idx_ref[...]
    start = start_ref[...]
    base = jnp.zeros((TOP_K, tm), F32)
    for k in range(TOP_K):
        bk = jnp.sum(jnp.where(e_iota == idx[k:k + 1, :], start, 0.0), axis=0, keepdims=True)
        base = jnp.where(k_iota == k, bk, base)
    pos_ref[...] = base.astype(I32) + rank_ref[...]


def slot_positions(idx_t, rank_t, pad_start):
    n = idx_t.shape[1]
    tm = SLOT_POS_COLS
    tok = lambda i: (0, i)
    return pl.pallas_call(
        _slot_pos_kernel,
        out_shape=jax.ShapeDtypeStruct((TOP_K, n), I32),
        grid=(n // tm,),
        in_specs=[pl.BlockSpec((TOP_K, tm), tok), pl.BlockSpec((TOP_K, tm), tok),
                  pl.BlockSpec((N_EXPERTS, 1), lambda i: (0, 0))],
        out_specs=pl.BlockSpec((TOP_K, tm), tok),
        compiler_params=_params("parallel"),
        name="slot_positions",
    )(idx_t, rank_t, pad_start.astype(F32).reshape(N_EXPERTS, 1))


SC_WINDOW = 128
SC_WORDS = QUARTER


def _sc_mesh():
    return plsc.VectorSubcoreMesh(core_axis_name="core", subcore_axis_name="subcore")


def sc_scatter_rows(src, pos_t, rows):
    n = src.shape[1]
    src2 = src.reshape(2 * n, SC_WORDS)
    idx2 = jnp.concatenate([pos_t, pos_t + rows], axis=1)

    @pl.kernel(out_type=jax.ShapeDtypeStruct((2 * rows, SC_WORDS), src.dtype), mesh=_sc_mesh(), scratch_types=[])
    def scatter_kernel(x_hbm, i_hbm, o_hbm):
        def body(x_vmem, i_vmem):
            pltpu.sync_copy(x_vmem, o_hbm.at[i_vmem.at[0]])

        pltpu.emit_pipeline(
            body,
            grid=(2 * n // SC_WINDOW, TOP_K),
            in_specs=[pl.BlockSpec((SC_WINDOW, SC_WORDS), index_map=lambda i, k: (i, 0)),
                      pl.BlockSpec((1, SC_WINDOW), index_map=lambda i, k: (k, i))],
            out_specs=[],
            core_axis_name=("core", "subcore"),
            dimension_semantics=(pltpu.PARALLEL, pltpu.ARBITRARY),
        )(x_hbm, i_hbm)

    return scatter_kernel(src2, idx2).reshape(2, rows, SC_WORDS)


def sc_gather_rows(src, pos):
    r = src.shape[1]
    m = pos.shape[0]
    src2 = src.reshape(2 * r, SC_WORDS)
    idx2 = jnp.concatenate([pos, pos + r]).reshape(1, 2 * m)

    @pl.kernel(out_type=jax.ShapeDtypeStruct((2 * m, SC_WORDS), src.dtype), mesh=_sc_mesh(), scratch_types=[])
    def gather_kernel(x_hbm, i_hbm, o_hbm):
        def body(i_vmem, o_vmem):
            pltpu.sync_copy(x_hbm.at[i_vmem.at[0]], o_vmem)

        pltpu.emit_pipeline(
            body,
            grid=(2 * m // SC_WINDOW,),
            in_specs=[pl.BlockSpec((1, SC_WINDOW), index_map=lambda i: (0, i))],
            out_specs=[pl.BlockSpec((SC_WINDOW, SC_WORDS), index_map=lambda i: (i, 0))],
            core_axis_name=("core", "subcore"),
            dimension_semantics=(pltpu.PARALLEL,),
        )(i_hbm, o_hbm)

    return gather_kernel(src2, idx2).reshape(2, m, SC_WORDS)


def _expert_kernel(blk_expert_ref, blk_valid_ref, blk_first_ref, blk_slot_ref, blk_next_ref, n_used_ref,
                   x_ref, wgu_hbm, wd_hbm, y_ref, wgu_buf, wd_buf, sem, *, layer):
    j = pl.program_id(0)
    used = j < n_used_ref[0]

    def weight_copies(e, slot):
        return (pltpu.make_async_copy(wgu_hbm.at[layer, e], wgu_buf.at[slot], sem.at[0, slot]),
                pltpu.make_async_copy(wd_hbm.at[layer, e], wd_buf.at[slot], sem.at[1, slot]))

    @pl.when(used)
    def _():
        slot = blk_slot_ref[j]

        @pl.when(blk_first_ref[j] == 1)
        def _():
            @pl.when(j == 0)
            def _():
                for copy in weight_copies(blk_expert_ref[0], slot):
                    copy.start()

            for copy in weight_copies(blk_expert_ref[j], slot):
                copy.wait()

            @pl.when(blk_next_ref[j] >= 0)
            def _():
                for copy in weight_copies(blk_next_ref[j], 1 - slot):
                    copy.start()

        valid = lax.broadcasted_iota(I32, x_ref.shape[1:], 0) < blk_valid_ref[j]
        quarters = _load_quarters(jnp.where(valid, x_ref[0], U32(0)), jnp.where(valid, x_ref[1], U32(0)))
        gu = sum(jnp.dot(xq.astype(BF16), wgu_buf[slot, c * QUARTER:(c + 1) * QUARTER, :].astype(BF16),
                         preferred_element_type=F32)
                 for c, xq in enumerate(quarters))
        act = (_silu(gu[:, :EXPERT_FF]) * gu[:, EXPERT_FF:]).astype(BF16)
        _store_planes(y_ref, jnp.dot(act, wd_buf[slot].astype(BF16), preferred_element_type=F32))

    @pl.when(jnp.logical_not(used))
    def _():
        y_ref[...] = jnp.zeros_like(y_ref)


def expert_plan(counts, pad_start, pad_end, n_blocks):
    blk_row = jnp.arange(n_blocks, dtype=I32) * ROW_BLOCK
    blk_expert = jnp.minimum(jnp.sum(pad_end[None, :] <= blk_row[:, None], axis=1), N_EXPERTS - 1).astype(I32)
    blk_valid = jnp.clip(counts[blk_expert] - (blk_row - pad_start[blk_expert]), 0, ROW_BLOCK).astype(I32)
    used = blk_row < pad_end[-1]
    prev_expert = jnp.concatenate([jnp.full((1,), -1, I32), blk_expert[:-1]])
    blk_first = (used & (blk_expert != prev_expert)).astype(I32)
    blk_slot = ((jnp.cumsum(blk_first) - 1) % 2).astype(I32)
    expert_ids = jnp.arange(N_EXPERTS, dtype=I32)
    nonempty_at_or_after = lax.cummin(jnp.where(counts > 0, expert_ids, N_EXPERTS), reverse=True)
    nonempty_after = jnp.concatenate([nonempty_at_or_after[1:], jnp.full((1,), N_EXPERTS, I32)])
    nxt = nonempty_after[blk_expert]
    blk_next = jnp.where(nxt < N_EXPERTS, nxt, -1).astype(I32)
    n_used = (pad_end[-1:] // ROW_BLOCK).astype(I32)
    return blk_expert, blk_valid, blk_first, blk_slot, blk_next, n_used


def expert_ffn(xs, plan, w_gu, w_down, layer):
    _, rows, w = xs.shape
    d = D_MODEL
    n_blocks = rows // ROW_BLOCK

    def row_map(j, be, bv, bf, bs, bn, nu):
        return (0, jnp.minimum(j, nu[0] - 1), 0)

    return pl.pallas_call(
        functools.partial(_expert_kernel, layer=layer),
        out_shape=jax.ShapeDtypeStruct(xs.shape, U32),
        grid_spec=pltpu.PrefetchScalarGridSpec(
            num_scalar_prefetch=6,
            grid=(n_blocks,),
            in_specs=[pl.BlockSpec((2, ROW_BLOCK, w), row_map),
                      pl.BlockSpec(memory_space=pl.ANY),
                      pl.BlockSpec(memory_space=pl.ANY)],
            out_specs=pl.BlockSpec((2, ROW_BLOCK, w), lambda j, be, bv, bf, bs, bn, nu: (0, j, 0)),
            scratch_shapes=[pltpu.VMEM((2, d, 2 * EXPERT_FF), F32),
                            pltpu.VMEM((2, EXPERT_FF, d), F32),
                            pltpu.SemaphoreType.DMA((2, 2))]),
        compiler_params=_params("arbitrary"),
        name="expert_ffn",
    )(*plan, xs, w_gu, w_down)


def _shared_ffn_kernel(h_ref, sgu_ref, sd_ref, o_ref):
    gu = jnp.dot(h_ref[...].astype(BF16), sgu_ref[...], preferred_element_type=F32)
    act = _silu(gu[:, :EXPERT_FF]) * gu[:, EXPERT_FF:]
    o_ref[...] = jnp.dot(act.astype(BF16), sd_ref[...], preferred_element_type=F32).astype(o_ref.dtype)


def shared_ffn(h, sh_gu, sh_down):
    n, d = h.shape
    tm = MOE_ROWS
    row = lambda i: (i, 0)
    const = lambda i: (0, 0)
    return pl.pallas_call(
        _shared_ffn_kernel,
        out_shape=jax.ShapeDtypeStruct((n, d), BF16),
        grid=(n // tm,),
        in_specs=[pl.BlockSpec((tm, d), row), pl.BlockSpec((d, 2 * EXPERT_FF), const),
                  pl.BlockSpec((EXPERT_FF, d), const)],
        out_specs=pl.BlockSpec((tm, d), row),
        compiler_params=_params("parallel"),
        name="shared_ffn",
    )(h, sh_gu, sh_down)


def _combine_kernel(h_ref, gate_ref, ys_ref, shared_ref, g_ref, b_ref, *refs):
    o_ref, ob_ref = refs[-2:]
    h = h_ref[...]
    y = shared_ref[...].astype(F32)
    gate = gate_ref[...]
    acc = [y[:, c * QUARTER:(c + 1) * QUARTER] for c in range(4)]
    for k in range(TOP_K):
        gk = gate[:, k:k + 1]
        acc = [a + gk * q for a, q in zip(acc, _load_quarters(ys_ref[0, k], ys_ref[1, k]))]
    out = _layer_norm_rows(DEEPNORM_ALPHA * h + jnp.concatenate(acc, axis=-1), g_ref[...], b_ref[...])
    o_ref[...] = out
    ob_ref[...] = out.astype(BF16)


def combine_shared_ln(h, gate, y_slots, chunk, prev_out, shared, g, b):
    n, d = h.shape
    tm = MOE_ROWS
    steps = y_slots.shape[2] // tm
    row = lambda i: (i + chunk * steps, 0)
    const = lambda i: (0, 0)
    passthrough = () if prev_out is None else tuple(prev_out)
    n_in = 6
    return pl.pallas_call(
        _combine_kernel,
        out_shape=(jax.ShapeDtypeStruct((n, d), F32), jax.ShapeDtypeStruct((n, d), BF16)),
        grid=(steps,),
        in_specs=[pl.BlockSpec((tm, d), row),
                  pl.BlockSpec((tm, TOP_K), row),
                  pl.BlockSpec((2, TOP_K, tm, QUARTER), lambda i: (0, 0, i, 0)),
                  pl.BlockSpec((tm, d), row),
                  pl.BlockSpec((1, d), const), pl.BlockSpec((1, d), const)]
                 + [pl.BlockSpec(memory_space=pl.ANY) for _ in passthrough],
        out_specs=(pl.BlockSpec((tm, d), row), pl.BlockSpec((tm, d), row)),
        input_output_aliases={n_in + i: i for i in range(len(passthrough))},
        compiler_params=_params("parallel"),
        name="combine_shared_ln",
    )(h, gate, y_slots, shared, g.reshape(1, d), b.reshape(1, d), *passthrough)


_MAIN_BLOCKS = (0, 1, 3, 4, 7, 8, 9, 10, 11, 12, 13, 14, 2)
_FORGET_BLOCKS = (5, 6)
_Q_A, _Q_R, _I_R, _G_R, _GATE_A, _GATE_R = (i * D_MODEL for i in range(6))
_K_A = 6 * D_MODEL
_V_A = _K_A + KV_WIDTH


def kernel(x, ln_in_g, ln_in_b, lb_logits, w_in, b_in, attn_sinks, rec_norm_g, w_proj_attn, w_proj_rec, w_out,
           ln1_g, ln1_b, router_w, router_bias, expert_w_gu, expert_w_down, shared_w_gu, shared_w_down,
           ln2_g, ln2_b):
    batch, seq, d = x.shape
    n = batch * seq
    depth = w_in.shape[0]
    n_blocks = n * TOP_K // ROW_BLOCK + N_EXPERTS
    rows = n_blocks * ROW_BLOCK

    p = jax.nn.softmax(lb_logits.astype(F32), axis=0)
    cum = jnp.cumsum(p, axis=0)
    lower = cum - cum[0:1]
    log_lb = jnp.log(lower)
    log1m_lb = jnp.log1p(-lower)

    w_in_bf = w_in.astype(BF16)
    h, hb = layer_norm_in(x.reshape(n, d), ln_in_g, ln_in_b)
    for l in range(depth):
        proj = in_proj(hb, w_in_bf, b_in, l, _MAIN_BLOCKS, BF16, "in_proj_main")
        proj_f = in_proj(hb, w_in_bf, b_in, l, _FORGET_BLOCKS, F32, "in_proj_forget")
        attn = swa_attention(proj, attn_sinks[l], batch, seq, _Q_A, _K_A, _V_A)
        rec = hgrn2(proj, proj_f, log_lb[l], log1m_lb[l], rec_norm_g[l], batch, seq, _Q_R, _I_R, _G_R)
        h, hp = merge_outproj_ln(h, attn, rec, proj, _GATE_A, _GATE_R,
                                 w_proj_attn[l].astype(BF16), w_proj_rec[l].astype(BF16), w_out[l].astype(BF16),
                                 ln1_g[l], ln1_b[l])

        rw_t = router_w[l].T
        rw_hi = rw_t.astype(BF16)
        rw_lo = (rw_t - rw_hi.astype(F32)).astype(BF16)
        idx_t, gate_t, rank_t, cnt = router(h, rw_hi, rw_lo, router_bias[l])
        counts = cnt[:, 0]
        padded = (counts + ROW_BLOCK - 1) // ROW_BLOCK * ROW_BLOCK
        pad_end = jnp.cumsum(padded)
        pad_start = pad_end - padded
        pos_t = slot_positions(idx_t, rank_t, pad_start)
        plan = expert_plan(counts, pad_start, pad_end, n_blocks)

        xs = sc_scatter_rows(hp, pos_t, rows)
        shared = shared_ffn(h, shared_w_gu[l].astype(BF16), shared_w_down[l].astype(BF16))
        ys = expert_ffn(xs, plan, expert_w_gu, expert_w_down, l)
        nc = n // MOE_CHUNKS
        gate, out = gate_t.T, None
        for c in range(MOE_CHUNKS):
            pos_c = lax.slice_in_dim(pos_t, c * nc, (c + 1) * nc, axis=1).reshape(TOP_K * nc)
            y_slots = sc_gather_rows(ys, pos_c).reshape(2, TOP_K, nc, QUARTER)
            out = combine_shared_ln(h, gate, y_slots, c, out, shared, ln2_g[l], ln2_b[l])
        h, hb = out
    return h.reshape(batch, seq, d)
```
